```python
import math
import jax
import jax.numpy as jnp
from jax import lax
import numpy as np

D_MODEL = 2048
BATCH = 2
SEQ = 8192
DEPTH = 1

D_ATTN = D_MODEL // 2
HEAD_DIM_A = 64
N_HEADS_A = D_ATTN // HEAD_DIM_A
DILATED_CONFIGS = ((128, 1), (512, 4), (2048, 16))
ATTN_BLOCK = 128
NUM_BUCKETS = 32
MAX_DISTANCE = 2048
D_MLSTM = D_MODEL - D_ATTN
N_HEADS_M = 4
HEAD_DIM_M = D_MLSTM // N_HEADS_M
CONV_K = 4
MLSTM_CHUNK = 128
IN_COLS = 3 * D_ATTN + 4 * D_MLSTM + 2 * N_HEADS_M
SPLITS = (D_ATTN, 2 * D_ATTN, 3 * D_ATTN, 3 * D_ATTN + 2 * D_MLSTM, 3 * D_ATTN + 3 * D_MLSTM,
          3 * D_ATTN + 4 * D_MLSTM, 3 * D_ATTN + 4 * D_MLSTM + N_HEADS_M)
N_EXPERTS = 32
TOP_K = 4
D_FF = D_MODEL
SWIGLU_LIMIT = 7.0
SWIGLU_ALPHA = 1.702
EXPERT_BLOCK = 128
DN_ALPHA = (2 * DEPTH) ** 0.25
DN_BETA = (8 * DEPTH) ** -0.25
LN_EPS = 1e-5
HEAD_NORM_EPS = 1e-6
NEG_INF = -1e30

kernel_name = "hymba_dilated_mlstm_moe_deepnorm"


def layer_norm(x, g, b):
    xf = x.astype(jnp.float32)
    mu = jnp.mean(xf, axis=-1, keepdims=True)
    var = jnp.mean(jnp.square(xf - mu), axis=-1, keepdims=True)
    return ((xf - mu) * lax.rsqrt(var + LN_EPS) * g + b).astype(x.dtype)


def head_rms(t):
    tf = t.astype(jnp.float32)
    tf = tf * lax.rsqrt(jnp.mean(jnp.square(tf), axis=-1, keepdims=True) + HEAD_NORM_EPS)
    return tf.reshape(t.shape[0], t.shape[1], -1)


def t5_bucket(dist):
    max_exact = NUM_BUCKETS // 2
    d_f = jnp.maximum(dist, 1).astype(jnp.float32)
    large = max_exact + (jnp.log(d_f / max_exact) / math.log(MAX_DISTANCE / max_exact)
                         * (NUM_BUCKETS - max_exact)).astype(jnp.int32)
    large = jnp.minimum(large, NUM_BUCKETS - 1)
    return jnp.where(dist < max_exact, dist, large)


def dilated_branch(q, k, v, rel_bias, window, dil):
    B, H, S, dh = q.shape
    n_back = window // dil
    L = S // dil
    Lp = -(-L // ATTN_BLOCK) * ATTN_BLOCK
    nb = Lp // ATTN_BLOCK

    def to_blocks(t):
        t = t.reshape(B, H, L, dil, dh).transpose(0, 1, 3, 2, 4)
        t = jnp.pad(t, ((0, 0), (0, 0), (0, 0), (0, Lp - L), (0, 0)))
        return t.reshape(B, H, dil, nb, ATTN_BLOCK, dh)

    def with_prev(t):
        prev = jnp.concatenate([jnp.zeros_like(t[:, :, :, :1]), t[:, :, :, :-1]], axis=3)
        return jnp.concatenate([prev, t], axis=4)

    qb = to_blocks(q)
    kc = with_prev(to_blocks(k))
    vc = with_prev(to_blocks(v)).astype(jnp.float32)
    s = jnp.einsum('bhrnqc,bhrnkc->bhrnqk', qb, kc).astype(jnp.float32)

    j = (jnp.arange(ATTN_BLOCK)[:, None] + ATTN_BLOCK) - jnp.arange(2 * ATTN_BLOCK)[None, :]
    band = (j >= 0) & (j <= n_back)
    has_prev = (jnp.arange(nb)[:, None, None] > 0) | (jnp.arange(2 * ATTN_BLOCK)[None, None, :] >= ATTN_BLOCK)
    mask = band[None] & has_prev
    bias = rel_bias[t5_bucket(jnp.maximum(j, 0) * dil)].transpose(2, 0, 1)
    s = s + bias[None, :, None, None].astype(jnp.float32)
    s = jnp.where(mask, s, NEG_INF)

    m = jnp.max(s, axis=-1, keepdims=True)
    p = jnp.exp(s - m)
    den = jnp.sum(p, axis=-1, keepdims=True)
    o = jnp.einsum('bhrnqk,bhrnkc->bhrnqc', p, vc) / den
    lse = (m + jnp.log(den))[..., 0]

    o = o.reshape(B, H, dil, Lp, dh)[:, :, :, :L].transpose(0, 1, 3, 2, 4).reshape(B, H, S, dh)
    lse = lse.reshape(B, H, dil, Lp)[:, :, :, :L].transpose(0, 1, 3, 2).reshape(B, H, S)
    return o, lse


def causal_depthwise_conv(t, w, b):
    C = t.shape[-1]
    out = lax.conv_general_dilated(t, w[:, None, :].astype(t.dtype), window_strides=(1,),
                                   padding=[(CONV_K - 1, 0)], dimension_numbers=('NWC', 'WIO', 'NWC'),
                                   feature_group_count=C)
    return out + b


def mlstm(q, k, v, i_pre, f_pre):
    B, S, H, dh = q.shape
    N = S // MLSTM_CHUNK
    f32 = jnp.float32

    def chunks(t):
        return t.astype(f32).reshape(B, N, MLSTM_CHUNK, H, dh).transpose(1, 0, 3, 2, 4)

    def gchunks(t):
        return t.astype(f32).reshape(B, N, MLSTM_CHUNK, H).transpose(1, 0, 3, 2)

    qc, kc, vc = chunks(q), chunks(k) * (dh ** -0.5), chunks(v)
    ic = gchunks(i_pre)
    lfc = jax.nn.log_sigmoid(gchunks(f_pre))
    causal = jnp.tril(jnp.ones((MLSTM_CHUNK, MLSTM_CHUNK), dtype=bool))

    def step(carry, inp):
        C, n, m = carry
        qt, kt, vt, it, lf = inp
        b = jnp.cumsum(lf, axis=-1)
        D = jnp.where(causal, b[..., :, None] - b[..., None, :] + it[..., None, :], -jnp.inf)
        m_inter = b + m[..., None]
        m_t = jnp.maximum(m_inter, jnp.max(D, axis=-1))
        W = jnp.exp(D - m_t[..., None]) * jnp.einsum('bhtd,bhsd->bhts', qt, kt)
        decay = jnp.exp(m_inter - m_t)
        num = jnp.einsum('bhts,bhsd->bhtd', W, vt) + decay[..., None] * jnp.einsum('bhvk,bhtk->bhtv', C, qt)
        den = jnp.sum(W, axis=-1) + decay * jnp.einsum('bhk,bhtk->bht', n, qt)
        h = num / jnp.maximum(jnp.abs(den), jnp.exp(-m_t))[..., None]
        g = b[..., -1]
        a = g[..., None] - b + it
        m_new = jnp.maximum(g + m, jnp.max(a, axis=-1))
        carry_decay = jnp.exp(g + m - m_new)
        wa = jnp.exp(a - m_new[..., None])
        C = carry_decay[..., None, None] * C + jnp.einsum('bhsv,bhsk->bhvk', wa[..., None] * vt, kt)
        n = carry_decay[..., None] * n + jnp.einsum('bhs,bhsk->bhk', wa, kt)
        return (C, n, m_new), h

    init = (jnp.zeros((B, H, dh, dh), f32), jnp.zeros((B, H, dh), f32), jnp.zeros((B, H), f32))
    _, hs = lax.scan(step, init, (qc, kc, vc, ic, lfc))
    return hs.transpose(1, 0, 3, 2, 4).reshape(B, S, H, dh)


def hybrid_mixer(h, w_in, b_igate, b_fgate, conv_w, conv_b, rel_bias, beta_attn, beta_mlstm, w_out):
    B, S, _ = h.shape
    proj = h @ w_in
    qa, ka, va, qk_m, vm, om, ig, fg = jnp.split(proj, SPLITS, axis=-1)

    def heads_a(t):
        return t.reshape(B, S, N_HEADS_A, HEAD_DIM_A).transpose(0, 2, 1, 3)
    qa = heads_a(qa) * (HEAD_DIM_A ** -0.5)
    ka, va = heads_a(ka), heads_a(va)
    outs, lses = [], []
    for window, dil in DILATED_CONFIGS:
        o, l = dilated_branch(qa, ka, va, rel_bias, window, dil)
        outs.append(o)
        lses.append(l)
    wts = jax.nn.softmax(jnp.stack(lses), axis=0)
    o_attn = jnp.einsum('gbhs,gbhsd->bshd', wts, jnp.stack(outs))

    qk_m = jax.nn.silu(causal_depthwise_conv(qk_m, conv_w, conv_b))
    qm, km = jnp.split(qk_m, 2, axis=-1)
    def heads_m(t):
        return t.reshape(B, S, N_HEADS_M, HEAD_DIM_M)
    hm = mlstm(heads_m(qm), heads_m(km), heads_m(vm), ig + b_igate, fg + b_fgate)
    hm = jax.nn.sigmoid(heads_m(om).astype(jnp.float32)) * hm

    y = jnp.concatenate([head_rms(o_attn) * beta_attn, head_rms(hm) * beta_mlstm], axis=-1)
    return y.astype(h.dtype) @ w_out


def moe_ffn(xf, w_router, b_router, w_up, b_up, w_down, b_down):
    N, D = xf.shape
    logits = (xf @ w_router + b_router).astype(jnp.float32)
    top_vals, top_idx = lax.top_k(logits, TOP_K)
    gates = jax.nn.softmax(top_vals, axis=-1)
    A = N * TOP_K
    e_flat = top_idx.reshape(-1).astype(jnp.int32)
    tok_flat = jnp.arange(A, dtype=jnp.int32) // TOP_K
    g_flat = gates.reshape(-1)

    order = jnp.argsort(e_flat, stable=True)
    e_sorted = e_flat[order]
    counts = jnp.zeros((N_EXPERTS,), jnp.int32).at[e_flat].add(1)
    padded = (counts + EXPERT_BLOCK - 1) // EXPERT_BLOCK * EXPERT_BLOCK
    pad_end = jnp.cumsum(padded)
    pad_start = pad_end - padded
    start = jnp.cumsum(counts) - counts
    dest = pad_start[e_sorted] + (jnp.arange(A, dtype=jnp.int32) - start[e_sorted])
    n_blocks = -(-A // EXPERT_BLOCK) + N_EXPERTS
    P = n_blocks * EXPERT_BLOCK
    slot_tok = jnp.zeros((P,), jnp.int32).at[dest].set(tok_flat[order])
    slot_gate = jnp.zeros((P,), jnp.float32).at[dest].set(g_flat[order])
    blk_exp = jnp.minimum(jnp.searchsorted(pad_end, jnp.arange(n_blocks, dtype=jnp.int32) * EXPERT_BLOCK,
                                           side='right'), N_EXPERTS - 1).astype(jnp.int32)
    xs = xf[slot_tok].reshape(n_blocks, EXPERT_BLOCK, D)

    def expert_block(args):
        e, xb = args
        hg = xb @ w_up[e] + b_up[e]
        gate = jnp.minimum(hg[:, :D_FF], SWIGLU_LIMIT)
        up = jnp.clip(hg[:, D_FF:], -SWIGLU_LIMIT, SWIGLU_LIMIT)
        act = (up + 1.0) * (gate * jax.nn.sigmoid(SWIGLU_ALPHA * gate))
        return act @ w_down[e] + b_down[e]

    ys = lax.map(expert_block, (blk_exp, xs)).reshape(P, D)
    return jnp.zeros_like(xf).at[slot_tok].add(ys * slot_gate[:, None].astype(ys.dtype))


def setup_inputs(seed: int = 0) -> dict:
    key = jax.random.key(seed)
    ks = jax.random.split(key, 20)
    f32 = jnp.float32
    nrm = lambda k, shape: jax.random.normal(k, shape, f32)
    return {
        "x": nrm(ks[0], (BATCH, SEQ, D_MODEL)),
        "w_in": nrm(ks[1], (DEPTH, D_MODEL, IN_COLS)) * D_MODEL ** -0.5,
        "b_igate": 0.1 * nrm(ks[2], (DEPTH, N_HEADS_M)),
        "b_fgate": jnp.linspace(3.0, 6.0, N_HEADS_M, dtype=f32)[None] + 0.1 * nrm(ks[3], (DEPTH, N_HEADS_M)),
        "conv_w": nrm(ks[4], (DEPTH, CONV_K, 2 * D_MLSTM)) * CONV_K ** -0.5,
        "conv_b": 0.01 * nrm(ks[5], (DEPTH, 2 * D_MLSTM)),
        "rel_bias": 0.5 * nrm(ks[6], (NUM_BUCKETS, N_HEADS_A)),
        "beta_attn": 1.0 + 0.02 * nrm(ks[7], (DEPTH, D_ATTN)),
        "beta_mlstm": 1.0 + 0.02 * nrm(ks[8], (DEPTH, D_MLSTM)),
        "w_out": nrm(ks[9], (DEPTH, D_MODEL, D_MODEL)) * (D_MODEL ** -0.5 * DN_BETA),
        "ln1_g": 1.0 + 0.02 * nrm(ks[10], (DEPTH, D_MODEL)),
        "ln1_b": 0.02 * nrm(ks[11], (DEPTH, D_MODEL)),
        "w_router": nrm(ks[12], (DEPTH, D_MODEL, N_EXPERTS)) * D_MODEL ** -0.5,
        "b_router": 0.01 * nrm(ks[13], (DEPTH, N_EXPERTS)),
        "w_up": nrm(ks[14], (DEPTH, N_EXPERTS, D_MODEL, 2 * D_FF)) * D_MODEL ** -0.5,
        "b_up": 0.01 * nrm(ks[15], (DEPTH, N_EXPERTS, 2 * D_FF)),
        "w_down": nrm(ks[16], (DEPTH, N_EXPERTS, D_FF, D_MODEL)) * (D_FF ** -0.5 * DN_BETA),
        "b_down": 0.01 * nrm(ks[17], (DEPTH, N_EXPERTS, D_MODEL)),
        "ln2_g": 1.0 + 0.02 * nrm(ks[18], (DEPTH, D_MODEL)),
        "ln2_b": 0.02 * nrm(ks[19], (DEPTH, D_MODEL)),
    }


def reference(x, w_in, b_igate, b_fgate, conv_w, conv_b, rel_bias, beta_attn, beta_mlstm, w_out,
              ln1_g, ln1_b, w_router, b_router, w_up, b_up, w_down, b_down, ln2_g, ln2_b):
    h = x
    B, S, D = h.shape
    for l in range(DEPTH):
        mix = hybrid_mixer(h, w_in[l], b_igate[l], b_fgate[l], conv_w[l], conv_b[l], rel_bias,
                           beta_attn[l], beta_mlstm[l], w_out[l])
        h = layer_norm(DN_ALPHA * h + mix, ln1_g[l], ln1_b[l])
        ffn = moe_ffn(h.reshape(B * S, D), w_router[l], b_router[l], w_up[l], b_up[l],
                      w_down[l], b_down[l]).reshape(B, S, D)
        h = layer_norm(DN_ALPHA * h + ffn, ln2_g[l], ln2_b[l])
    return h
```

```python
import functools
import math

import numpy as np
import jax
import jax.numpy as jnp
from jax import lax
from jax.experimental import pallas as pl
from jax.experimental.pallas import tpu as pltpu

F32 = jnp.float32
BF16 = jnp.bfloat16

D_MODEL = 2048
D_ATTN = 1024
HEAD_DIM_A = 64
N_HEADS_A = 16
DILATED_CONFIGS = ((128, 1), (512, 4), (2048, 16))
ATTN_BLOCK = 128
NUM_BUCKETS = 32
MAX_DISTANCE = 2048
D_MLSTM = 1024
N_HEADS_M = 4
HEAD_DIM_M = 256
CONV_K = 4
MLSTM_CHUNK = 128
MAIN_COLS = 3 * D_ATTN + 4 * D_MLSTM
N_EXPERTS = 32
TOP_K = 4
D_FF = 2048
SWIGLU_LIMIT = 7.0
SWIGLU_ALPHA = 1.702
DEPTH = 1
DN_ALPHA = (2 * DEPTH) ** 0.25
LN_EPS = 1e-5
HEAD_NORM_EPS = 1e-6
NEG_INF = -1e30

LANES = 128
VMEM_LIMIT = 48 * 1024 * 1024

PROJ_TM = 1024
PROJ_TN = 1024
COMBINE_TM = 512
OUTPROJ_TM = 512
ROUTER_TM = 512
SCATTER_TM = 512
FFN_ROWS = 512
FFN_TF = 512
GATHER_TM = 128


def _cparams(sem, vmem=VMEM_LIMIT):
    return pltpu.CompilerParams(dimension_semantics=sem, vmem_limit_bytes=vmem)


def _dot(a, b):
    return jnp.dot(a, b, preferred_element_type=F32)


def _dot_f32_rhs(a_bf16, b_f32):
    hi = b_f32.astype(BF16)
    lo = (b_f32 - hi.astype(F32)).astype(BF16)
    return _dot(a_bf16, hi) + _dot(a_bf16, lo)


def _dot_f32_lhs(a_f32, b_bf16):
    hi = a_f32.astype(BF16)
    lo = (a_f32 - hi.astype(F32)).astype(BF16)
    return _dot(hi, b_bf16) + _dot(lo, b_bf16)


def _sigmoid(x):
    return 1.0 / (1.0 + jnp.exp(-x))


def _log_sigmoid(x):
    return jnp.minimum(x, 0.0) - jnp.log(1.0 + jnp.exp(-jnp.abs(x)))


def _proj_kernel(x_ref, w_ref, o_ref):
    o_ref[...] = _dot(x_ref[...].astype(BF16), w_ref[...]).astype(o_ref.dtype)


def _project(x, w, out_dtype, tm, tn):
    m, k = x.shape
    n = w.shape[1]
    return pl.pallas_call(
        _proj_kernel,
        grid=(m // tm, n // tn),
        in_specs=[pl.BlockSpec((tm, k), lambda i, j: (i, 0)),
                  pl.BlockSpec((k, tn), lambda i, j: (0, j))],
        out_specs=pl.BlockSpec((tm, tn), lambda i, j: (i, j)),
        out_shape=jax.ShapeDtypeStruct((m, n), out_dtype),
        compiler_params=_cparams(("parallel", "parallel")),
        name="in_proj",
    )(x, w)


def _attn_kernel(q_ref, kp_ref, kc_ref, vp_ref, vc_ref, bias_ref, o_ref, lse_ref):
    n = pl.program_id(2)
    tab = jnp.minimum(n, 1)
    lse_ref[...] = jnp.zeros(lse_ref.shape, F32)
    for h in range(N_HEADS_A):
        sl = slice(h * HEAD_DIM_A, (h + 1) * HEAD_DIM_A)
        q = q_ref[0, :, sl] * (HEAD_DIM_A ** -0.5)
        kk = jnp.concatenate([kp_ref[0, :, sl], kc_ref[0, :, sl]], axis=0)
        vv = jnp.concatenate([vp_ref[0, :, sl], vc_ref[0, :, sl]], axis=0)
        s = lax.dot_general(q, kk, (((1,), (1,)), ((), ())), preferred_element_type=F32)
        s = s + bias_ref[tab, h]
        m = jnp.max(s, axis=-1, keepdims=True)
        p = jnp.exp(s - m)
        den = jnp.sum(p, axis=-1, keepdims=True)
        o = _dot(p.astype(BF16), vv) / den
        o_ref[0, :, sl] = o.astype(o_ref.dtype)
        lse_ref[0, :, h:h + 1] = m + jnp.log(den)


def _attn_bias_tables(rel_bias, dil):
    blk = ATTN_BLOCK
    j = (np.arange(blk)[:, None] + blk) - np.arange(2 * blk)[None, :]
    band = (j >= 0) & (j <= blk)
    dist = np.maximum(j, 0) * dil
    max_exact = NUM_BUCKETS // 2
    d_f = np.maximum(dist, 1).astype(np.float32)
    large = max_exact + (np.log(d_f / np.float32(max_exact)) / np.float32(math.log(MAX_DISTANCE / max_exact))
                         * np.float32(NUM_BUCKETS - max_exact)).astype(np.int32)
    large = np.minimum(large, NUM_BUCKETS - 1)
    bucket = np.where(dist < max_exact, dist, large).astype(np.int32)
    bias = rel_bias[jnp.asarray(bucket)].transpose(2, 0, 1).astype(F32)
    has_prev = np.arange(2 * blk)[None, :] >= blk
    t1 = jnp.where(jnp.asarray(band)[None], bias, NEG_INF)
    t0 = jnp.where(jnp.asarray(band & has_prev)[None], bias, NEG_INF)
    return jnp.stack([t0, t1])


def _dilated_attention(proj, bias_tab, batch, seq, dil):
    blk = ATTN_BLOCK
    l = seq // dil
    nb = l // blk
    ncb = MAIN_COLS // D_ATTN
    pv = proj.reshape(batch, l, dil * MAIN_COLS)

    def cur(c):
        return pl.BlockSpec((1, blk, D_ATTN), lambda b, r, n: (b, n, r * ncb + c))

    def prev(c):
        return pl.BlockSpec((1, blk, D_ATTN), lambda b, r, n: (b, jnp.maximum(n - 1, 0), r * ncb + c))

    o, lse = pl.pallas_call(
        _attn_kernel,
        grid=(batch, dil, nb),
        in_specs=[cur(0), prev(1), cur(1), prev(2), cur(2),
                  pl.BlockSpec((2, N_HEADS_A, blk, 2 * blk), lambda b, r, n: (0, 0, 0, 0))],
        out_specs=[pl.BlockSpec((1, blk, D_ATTN), lambda b, r, n: (b, n, r)),
                   pl.BlockSpec((1, blk, LANES), lambda b, r, n: (b, n, r))],
        out_shape=[jax.ShapeDtypeStruct((batch, l, dil * D_ATTN), BF16),
                   jax.ShapeDtypeStruct((batch, l, dil * LANES), F32)],
        compiler_params=_cparams(("parallel", "parallel", "arbitrary")),
        name=f"dilated_attn_d{dil}",
    )(pv, pv, pv, pv, pv, bias_tab)
    return o.reshape(batch * seq, D_ATTN), lse.reshape(batch * seq, LANES)


def _attn_combine_kernel(o1_ref, o2_ref, o3_ref, l1_ref, l2_ref, l3_ref, e_ref, et_ref, beta_ref, y_ref):
    lses = [l1_ref[...], l2_ref[...], l3_ref[...]]
    outs = [o1_ref, o2_ref, o3_ref]
    mx = jnp.maximum(jnp.maximum(lses[0], lses[1]), lses[2])
    ws = [jnp.exp(l - mx) for l in lses]
    tot = ws[0] + ws[1] + ws[2]
    e = e_ref[...]
    acc = None
    for w, o_ref in zip(ws, outs):
        term = _dot_f32_lhs(w / tot, e) * o_ref[...].astype(F32)
        acc = term if acc is None else acc + term
    ss = _dot_f32_lhs(acc * acc, et_ref[...])
    inv = lax.rsqrt(ss * (1.0 / HEAD_DIM_A) + HEAD_NORM_EPS)
    y_ref[...] = (acc * _dot_f32_lhs(inv, e) * beta_ref[...]).astype(y_ref.dtype)


def _attn_combine(os_, lses, beta_attn):
    n = os_[0].shape[0]
    tm = min(COMBINE_TM, n)
    head_of_lane = np.arange(D_ATTN) // HEAD_DIM_A
    e = (np.arange(LANES)[:, None] == head_of_lane[None, :]).astype(np.float32)
    e_j = jnp.asarray(e, BF16)
    et_j = jnp.asarray(e.T, BF16)
    row = lambda w: pl.BlockSpec((tm, w), lambda i: (i, 0))
    full = lambda a, b: pl.BlockSpec((a, b), lambda i: (0, 0))
    return pl.pallas_call(
        _attn_combine_kernel,
        grid=(n // tm,),
        in_specs=[row(D_ATTN)] * 3 + [row(LANES)] * 3 + [full(LANES, D_ATTN), full(D_ATTN, LANES), full(1, D_ATTN)],
        out_specs=row(D_ATTN),
        out_shape=jax.ShapeDtypeStruct((n, D_ATTN), BF16),
        compiler_params=_cparams(("parallel",)),
        name="attn_combine",
    )(*os_, *lses, e_j, et_j, beta_attn.reshape(1, D_ATTN).astype(F32))


def _mlstm_kernel(qp_ref, kp_ref, qprev_ref, kprev_ref, v_ref, og_ref, g_ref, gb_ref, cw_ref, cb_ref,
                  beta_ref, y_ref, c_ref, n_ref, m_ref):
    step = pl.program_id(1)
    ch = MLSTM_CHUNK
    dh = HEAD_DIM_M

    @pl.when(step == 0)
    def _():
        c_ref[...] = jnp.zeros(c_ref.shape, F32)
        n_ref[...] = jnp.zeros(n_ref.shape, F32)
        m_ref[...] = jnp.zeros(m_ref.shape, F32)

    def conv_silu(x_ref, prev_ref, coff):
        x = x_ref[0].astype(F32)
        p = jnp.where(step > 0, prev_ref[0].astype(F32), 0.0)
        xe = jnp.concatenate([p, x], axis=0)
        npad = p.shape[0]
        cols = slice(coff, coff + D_MLSTM)
        acc = cb_ref[:, cols] + cw_ref[CONV_K - 1:CONV_K, cols] * x
        for s in range(1, CONV_K):
            shifted = pltpu.roll(xe, s, 0)[npad:]
            acc = acc + cw_ref[CONV_K - 1 - s:CONV_K - s, cols] * shifted
        return acc * _sigmoid(acc)

    qf = conv_silu(qp_ref, qprev_ref, 0)
    kf = conv_silu(kp_ref, kprev_ref, D_MLSTM) * (dh ** -0.5)
    qb = qf.astype(BF16)
    kb = kf.astype(BF16)

    g = g_ref[0] + gb_ref[...]
    gt = g.T
    row_i = lax.broadcasted_iota(jnp.int32, (ch, ch), 0)
    col_i = lax.broadcasted_iota(jnp.int32, (ch, ch), 1)
    causal = row_i >= col_i
    tri = jnp.where(causal, 1.0, 0.0).astype(BF16)
    upp = jnp.where(row_i <= col_i, 1.0, 0.0).astype(BF16)
    b_cols = _dot_f32_rhs(tri, _log_sigmoid(g))
    b_rows = _dot_f32_lhs(_log_sigmoid(gt), upp)

    for h in range(N_HEADS_M):
        hs = slice(h * dh, (h + 1) * dh)
        fi = N_HEADS_M + h
        i_row = gt[h:h + 1, :]
        i_col = g[:, h:h + 1]
        b_row = b_rows[fi:fi + 1, :]
        b_col = b_cols[:, fi:fi + 1]
        m_prev = m_ref[h][:, 0:1]
        q_h, k_h = qb[:, hs], kb[:, hs]
        v_h = v_ref[0, :, hs]

        dmat = jnp.where(causal, b_col - b_row + i_row, NEG_INF)
        m_inter = b_col + m_prev
        m_t = jnp.maximum(m_inter, jnp.max(dmat, axis=-1, keepdims=True))
        w = jnp.exp(dmat - m_t) * lax.dot_general(q_h, k_h, (((1,), (1,)), ((), ())),
                                                  preferred_element_type=F32)
        decay = jnp.exp(m_inter - m_t)
        c_old = c_ref[h]
        inter = lax.dot_general(q_h, c_old.astype(BF16), (((1,), (1,)), ((), ())), preferred_element_type=F32)
        num = _dot(w.astype(BF16), v_h) + decay * inter
        n_old = n_ref[h]
        den = jnp.sum(w, axis=-1, keepdims=True) + decay * jnp.sum(qf[:, hs] * n_old, axis=-1, keepdims=True)
        hh = num / jnp.maximum(jnp.abs(den), jnp.exp(-m_t))

        g_last = b_col[ch - 1:ch, :]
        a_row = g_last - b_row + i_row
        a_col = g_last - b_col + i_col
        m_new = jnp.maximum(g_last + m_prev, jnp.max(a_row, axis=-1, keepdims=True))
        carry = jnp.exp(g_last + m_prev - m_new)
        wa_col = jnp.exp(a_col - m_new)
        wv = (wa_col * v_h.astype(F32)).astype(BF16)
        c_ref[h] = carry * c_old + lax.dot_general(wv, k_h, (((0,), (0,)), ((), ())), preferred_element_type=F32)
        n_ref[h] = carry * n_old + jnp.sum(wa_col * kf[:, hs], axis=0, keepdims=True)
        m_ref[h] = jnp.broadcast_to(m_new, (1, LANES))

        gated = _sigmoid(og_ref[0, :, hs].astype(F32)) * hh
        ms = jnp.sum(gated * gated, axis=-1, keepdims=True) * (1.0 / dh)
        y_ref[0, :, hs] = (gated * lax.rsqrt(ms + HEAD_NORM_EPS) * beta_ref[:, hs]).astype(y_ref.dtype)


def _mlstm(proj, gates, gate_bias, conv_w, conv_b, beta_mlstm, batch, seq):
    ch = MLSTM_CHUNK
    nchunk = seq // ch
    pv = proj.reshape(batch, seq, MAIN_COLS)
    gv = gates.reshape(batch, seq, LANES)
    prev_rows = 16
    cb0 = 3 * D_ATTN // D_MLSTM

    def cur(c):
        return pl.BlockSpec((1, ch, D_MLSTM), lambda b, n: (b, n, c))

    def prev(c):
        per = ch // prev_rows
        return pl.BlockSpec((1, prev_rows, D_MLSTM), lambda b, n: (b, jnp.maximum(n * per - 1, 0), c))

    const = lambda a, b_: pl.BlockSpec((a, b_), lambda b, n: (0, 0))
    y = pl.pallas_call(
        _mlstm_kernel,
        grid=(batch, nchunk),
        in_specs=[cur(cb0), cur(cb0 + 1), prev(cb0), prev(cb0 + 1), cur(cb0 + 2), cur(cb0 + 3),
                  pl.BlockSpec((1, ch, LANES), lambda b, n: (b, n, 0)),
                  const(1, LANES), const(CONV_K, 2 * D_MLSTM), const(1, 2 * D_MLSTM), const(1, D_MLSTM)],
        out_specs=pl.BlockSpec((1, ch, D_MLSTM), lambda b, n: (b, n, 0)),
        out_shape=jax.ShapeDtypeStruct((batch, seq, D_MLSTM), BF16),
        scratch_shapes=[pltpu.VMEM((N_HEADS_M, HEAD_DIM_M, HEAD_DIM_M), F32),
                        pltpu.VMEM((N_HEADS_M, 1, HEAD_DIM_M), F32),
                        pltpu.VMEM((N_HEADS_M, 1, LANES), F32)],
        compiler_params=_cparams(("parallel", "arbitrary")),
        name="mlstm",
    )(pv, pv, pv, pv, pv, pv, gv, gate_bias, conv_w.astype(F32), conv_b.reshape(1, -1).astype(F32),
      beta_mlstm.reshape(1, D_MLSTM).astype(F32))
    return y.reshape(batch * seq, D_MLSTM)


def _layer_norm(z, g, b):
    mu = jnp.mean(z, axis=-1, keepdims=True)
    zc = z - mu
    var = jnp.mean(zc * zc, axis=-1, keepdims=True)
    return zc * lax.rsqrt(var + LN_EPS) * g + b


def _outproj_kernel(ya_ref, ym_ref, x_ref, w_ref, g_ref, b_ref, h_ref):
    y = _dot(ya_ref[...], w_ref[0:D_ATTN, :]) + _dot(ym_ref[...], w_ref[D_ATTN:D_MODEL, :])
    h_ref[...] = _layer_norm(DN_ALPHA * x_ref[...] + y, g_ref[...], b_ref[...])


def _outproj_ln(ya, ym, x, w_out, g, b):
    n = x.shape[0]
    tm = min(OUTPROJ_TM, n)
    row = lambda w: pl.BlockSpec((tm, w), lambda i: (i, 0))
    full = lambda a, b_: pl.BlockSpec((a, b_), lambda i: (0, 0))
    return pl.pallas_call(
        _outproj_kernel,
        grid=(n // tm,),
        in_specs=[row(D_ATTN), row(D_MLSTM), row(D_MODEL), full(D_MODEL, D_MODEL), full(1, D_MODEL), full(1, D_MODEL)],
        out_specs=row(D_MODEL),
        out_shape=jax.ShapeDtypeStruct((n, D_MODEL), F32),
        compiler_params=_cparams(("parallel",)),
        name="out_proj_ln",
    )(ya, ym, x, w_out, g.reshape(1, -1), b.reshape(1, -1))


def _router_kernel(h_ref, whi_ref, wlo_ref, b_ref, tri_ref, idx_ref, gate_ref, rank_ref, cnt_ref, carry_ref):
    i = pl.program_id(0)

    @pl.when(i == 0)
    def _():
        carry_ref[...] = jnp.zeros(carry_ref.shape, F32)

    x = h_ref[...]
    xhi = x.astype(BF16)
    xlo = (x - xhi.astype(F32)).astype(BF16)
    logits = _dot(xhi, whi_ref[...]) + _dot(xhi, wlo_ref[...]) + _dot(xlo, whi_ref[...]) + b_ref[...]
    tm = logits.shape[0]
    lane = lax.broadcasted_iota(jnp.int32, (tm, LANES), 1)
    lane_f = lane.astype(F32)
    vals = jnp.where(lane < N_EXPERTS, logits, NEG_INF)

    sels, tops = [], []
    for _ in range(TOP_K):
        mx = jnp.max(vals, axis=-1, keepdims=True)
        first = jnp.min(jnp.where(vals == mx, lane_f, float(LANES)), axis=-1, keepdims=True)
        sel = lane_f == first
        sels.append(sel)
        tops.append((mx, first))
        vals = jnp.where(sel, 2.0 * NEG_INF, vals)

    exps = [jnp.exp(mx - tops[0][0]) for mx, _ in tops]
    tot = exps[0] + exps[1] + exps[2] + exps[3]

    onehot = jnp.zeros((tm, LANES), F32)
    for sel in sels:
        onehot = jnp.where(sel, 1.0, onehot)
    before = _dot(tri_ref[...], onehot.astype(BF16)) + carry_ref[...]

    idx_out = jnp.zeros((tm, LANES), F32)
    gate_out = jnp.zeros((tm, LANES), F32)
    rank_out = jnp.zeros((tm, LANES), F32)
    for k in range(TOP_K):
        rank_k = jnp.sum(jnp.where(sels[k], before, 0.0), axis=-1, keepdims=True)
        idx_out = jnp.where(lane == k, tops[k][1], idx_out)
        gate_out = jnp.where(lane == k, exps[k] / tot, gate_out)
        rank_out = jnp.where(lane == k, rank_k, rank_out)
    idx_ref[...] = idx_out.astype(jnp.int32)
    gate_ref[...] = gate_out
    rank_ref[...] = rank_out.astype(jnp.int32)

    carry = carry_ref[...] + jnp.sum(onehot, axis=0, keepdims=True)
    carry_ref[...] = carry
    cnt_ref[...] = carry.astype(jnp.int32)


def _router(h1, w_router, b_router):
    n = h1.shape[0]
    tm = min(ROUTER_TM, n)
    wpad = jnp.zeros((D_MODEL, LANES), F32).at[:, :N_EXPERTS].set(w_router)
    whi = wpad.astype(BF16)
    wlo = (wpad - whi.astype(F32)).astype(BF16)
    bpad = jnp.zeros((1, LANES), F32).at[0, :N_EXPERTS].set(b_router)
    tri = jnp.asarray(np.tril(np.ones((tm, tm), np.float32), -1), BF16)
    row = lambda w: pl.BlockSpec((tm, w), lambda i: (i, 0))
    full = lambda a, b_: pl.BlockSpec((a, b_), lambda i: (0, 0))
    return pl.pallas_call(
        _router_kernel,
        grid=(n // tm,),
        in_specs=[row(D_MODEL), full(D_MODEL, LANES), full(D_MODEL, LANES), full(1, LANES), full(tm, tm)],
        out_specs=[row(LANES), row(LANES), row(LANES), full(1, LANES)],
        out_shape=[jax.ShapeDtypeStruct((n, LANES), jnp.int32), jax.ShapeDtypeStruct((n, LANES), F32),
                   jax.ShapeDtypeStruct((n, LANES), jnp.int32), jax.ShapeDtypeStruct((1, LANES), jnp.int32)],
        scratch_shapes=[pltpu.VMEM((1, LANES), F32)],
        compiler_params=_cparams(("arbitrary",)),
        name="router",
    )(h1, whi, wlo, bpad, tri)


def _scatter_rows_kernel(dest_ref, x_ref, buf_in_ref, buf_ref, sem):
    del buf_in_ref
    tm = dest_ref.shape[0] // TOP_K
    base = pl.program_id(0) * tm

    def issue(t, carry):
        for k in range(TOP_K):
            pltpu.make_async_copy(x_ref.at[pl.ds(base + t, 1)], buf_ref.at[pl.ds(dest_ref[t * TOP_K + k], 1)],
                                  sem).start()
        return carry

    lax.fori_loop(0, tm, issue, 0)
    pltpu.make_async_copy(x_ref.at[pl.ds(0, tm * TOP_K)], buf_ref.at[pl.ds(0, tm * TOP_K)], sem).wait()


def _scatter_rows(h1, dest_flat, n_rows):
    n = h1.shape[0]
    tm = min(SCATTER_TM, n)
    zeros = jnp.zeros((n_rows, D_MODEL), F32)
    return pl.pallas_call(
        _scatter_rows_kernel,
        grid=(n // tm,),
        in_specs=[pl.BlockSpec((tm * TOP_K,), lambda i: (i,), memory_space=pltpu.SMEM),
                  pl.BlockSpec(memory_space=pl.ANY), pl.BlockSpec(memory_space=pl.ANY)],
        out_specs=pl.BlockSpec(memory_space=pl.ANY),
        out_shape=jax.ShapeDtypeStruct((n_rows, D_MODEL), F32),
        scratch_shapes=[pltpu.SemaphoreType.DMA],
        input_output_aliases={2: 0},
        compiler_params=pltpu.CompilerParams(dimension_semantics=("arbitrary",), has_side_effects=True),
        name="scatter_rows",
    )(dest_flat, h1, zeros)


def _ffn_kernel(bexp_ref, nused_ref, x_ref, wg_ref, wu_ref, wd_ref, bg_ref, bu_ref, bd_ref, o_ref):
    del bexp_ref
    i = pl.program_id(0)
    f = pl.program_id(1)
    used = i < nused_ref[0]

    @pl.when(used)
    def _():
        x = x_ref[...].astype(BF16)
        hg = _dot(x, wg_ref[0]) + bg_ref[0]
        hu = _dot(x, wu_ref[0]) + bu_ref[0]
        gate = jnp.minimum(hg, SWIGLU_LIMIT)
        up = jnp.clip(hu, -SWIGLU_LIMIT, SWIGLU_LIMIT)
        act = (up + 1.0) * (gate * _sigmoid(SWIGLU_ALPHA * gate))
        y = _dot(act.astype(BF16), wd_ref[0])

        @pl.when(f == 0)
        def _():
            o_ref[...] = y + bd_ref[0]

        @pl.when(f > 0)
        def _():
            o_ref[...] += y

    @pl.when(jnp.logical_and(jnp.logical_not(used), f == 0))
    def _():
        o_ref[...] = jnp.zeros(o_ref.shape, F32)


def _expert_ffn(xs, blk_exp, n_used, w_up, b_up, w_down, b_down):
    n_rows = xs.shape[0]
    rb, tf = FFN_ROWS, FFN_TF
    nblk = n_rows // rb
    nf = D_FF // tf
    last_f = nf - 1

    def clamp(i, nu):
        return jnp.minimum(i, nu[0] - 1)

    def fsel(i, f, nu):
        return jnp.where(i < nu[0], f, last_f)

    grid_spec = pltpu.PrefetchScalarGridSpec(
        num_scalar_prefetch=2,
        grid=(nblk, nf),
        in_specs=[
            pl.BlockSpec((rb, D_MODEL), lambda i, f, be, nu: (clamp(i, nu), 0)),
            pl.BlockSpec((1, D_MODEL, tf), lambda i, f, be, nu: (be[clamp(i, nu)], 0, fsel(i, f, nu))),
            pl.BlockSpec((1, D_MODEL, tf), lambda i, f, be, nu: (be[clamp(i, nu)], 0, nf + fsel(i, f, nu))),
            pl.BlockSpec((1, tf, D_MODEL), lambda i, f, be, nu: (be[clamp(i, nu)], fsel(i, f, nu), 0)),
            pl.BlockSpec((1, 1, tf), lambda i, f, be, nu: (be[clamp(i, nu)], 0, fsel(i, f, nu))),
            pl.BlockSpec((1, 1, tf), lambda i, f, be, nu: (be[clamp(i, nu)], 0, nf + fsel(i, f, nu))),
            pl.BlockSpec((1, 1, D_MODEL), lambda i, f, be, nu: (be[clamp(i, nu)], 0, 0)),
        ],
        out_specs=pl.BlockSpec((rb, D_MODEL), lambda i, f, be, nu: (i, 0)),
    )
    return pl.pallas_call(
        _ffn_kernel,
        grid_spec=grid_spec,
        out_shape=jax.ShapeDtypeStruct((n_rows, D_MODEL), F32),
        compiler_params=_cparams(("arbitrary", "arbitrary")),
        name="expert_ffn",
    )(blk_exp, n_used, xs, w_up, w_up, w_down, b_up.reshape(N_EXPERTS, 1, 2 * D_FF),
      b_up.reshape(N_EXPERTS, 1, 2 * D_FF), b_down.reshape(N_EXPERTS, 1, D_MODEL))


def _combine_kernel(dest_ref, gate_ref, h_ref, g_ref, b_ref, ys_ref, o_ref, buf_ref, sem):
    tm = h_ref.shape[0]

    def issue(t, carry):
        for k in range(TOP_K):
            pltpu.make_async_copy(ys_ref.at[pl.ds(dest_ref[t * TOP_K + k], 1)], buf_ref.at[pl.ds(k * tm + t, 1)],
                                  sem).start()
        return carry

    lax.fori_loop(0, tm, issue, 0)
    pltpu.make_async_copy(ys_ref.at[pl.ds(0, tm * TOP_K)], buf_ref, sem).wait()
    gates = gate_ref[...]
    z = DN_ALPHA * h_ref[...]
    for k in range(TOP_K):
        z = z + gates[:, k:k + 1] * buf_ref[k * tm:(k + 1) * tm, :]
    o_ref[...] = _layer_norm(z, g_ref[...], b_ref[...])


def _combine_ln(dest_flat, gates, h1, ys, g, b):
    n = h1.shape[0]
    tm = min(GATHER_TM, n)
    row = lambda w: pl.BlockSpec((tm, w), lambda i: (i, 0))
    full = lambda a, b_: pl.BlockSpec((a, b_), lambda i: (0, 0))
    return pl.pallas_call(
        _combine_kernel,
        grid=(n // tm,),
        in_specs=[pl.BlockSpec((tm * TOP_K,), lambda i: (i,), memory_space=pltpu.SMEM),
                  row(LANES), row(D_MODEL), full(1, D_MODEL), full(1, D_MODEL),
                  pl.BlockSpec(memory_space=pl.ANY)],
        out_specs=row(D_MODEL),
        out_shape=jax.ShapeDtypeStruct((n, D_MODEL), F32),
        scratch_shapes=[pltpu.VMEM((tm * TOP_K, D_MODEL), F32), pltpu.SemaphoreType.DMA],
        compiler_params=_cparams(("arbitrary",)),
        name="combine_ln",
    )(dest_flat, gates, h1, g.reshape(1, -1), b.reshape(1, -1), ys)


def _layer(h, w_in, b_igate, b_fgate, conv_w, conv_b, rel_bias, beta_attn, beta_mlstm, w_out,
           ln1_g, ln1_b, w_router, b_router, w_up, b_up, w_down, b_down, ln2_g, ln2_b, batch, seq):
    n = batch * seq
    w_main = w_in[:, :MAIN_COLS].astype(BF16)
    w_gate = jnp.zeros((D_MODEL, LANES), BF16).at[:, :2 * N_HEADS_M].set(w_in[:, MAIN_COLS:].astype(BF16))
    proj = _project(h, w_main, BF16, min(PROJ_TM, n), PROJ_TN)
    gates = _project(h, w_gate, F32, min(PROJ_TM, n), LANES)
    gate_bias = jnp.zeros((1, LANES), F32).at[0, :2 * N_HEADS_M].set(jnp.concatenate([b_igate, b_fgate]))

    outs, lses = [], []
    for _, dil in DILATED_CONFIGS:
        o, l = _dilated_attention(proj, _attn_bias_tables(rel_bias, dil), batch, seq, dil)
        outs.append(o)
        lses.append(l)
    y_attn = _attn_combine(outs, lses, beta_attn)
    y_mlstm = _mlstm(proj, gates, gate_bias, conv_w, conv_b, beta_mlstm, batch, seq)

    h1 = _outproj_ln(y_attn, y_mlstm, h, w_out.astype(BF16), ln1_g, ln1_b)

    top_idx, top_gate, rank, counts = _router(h1, w_router, b_router)
    counts = counts[0, :N_EXPERTS]
    padded = (counts + FFN_ROWS - 1) // FFN_ROWS * FFN_ROWS
    pad_end = jnp.cumsum(padded)
    pad_start = pad_end - padded
    dest = (pad_start[top_idx[:, :TOP_K]] + rank[:, :TOP_K]).reshape(-1).astype(jnp.int32)
    nblk = n * TOP_K // FFN_ROWS + N_EXPERTS
    blk_exp = jnp.minimum(jnp.searchsorted(pad_end, jnp.arange(nblk, dtype=jnp.int32) * FFN_ROWS, side='right'),
                          N_EXPERTS - 1).astype(jnp.int32)
    n_used = (pad_end[-1:] // FFN_ROWS).astype(jnp.int32)

    xs = _scatter_rows(h1, dest, nblk * FFN_ROWS)
    ys = _expert_ffn(xs, blk_exp, n_used, w_up.astype(BF16), b_up, w_down.astype(BF16), b_down)
    return _combine_ln(dest, top_gate, h1, ys, ln2_g, ln2_b)


def kernel(x, w_in, b_igate, b_fgate, conv_w, conv_b, rel_bias, beta_attn, beta_mlstm, w_out, ln1_g, ln1_b,
           w_router, b_router, w_up, b_up, w_down, b_down, ln2_g, ln2_b):
    batch, seq, d = x.shape
    h = x.reshape(batch * seq, d)
    for l in range(DEPTH):
        h = _layer(h, w_in[l], b_igate[l], b_fgate[l], conv_w[l], conv_b[l], rel_bias, beta_attn[l], beta_mlstm[l],
                   w_out[l], ln1_g[l], ln1_b[l], w_router[l], b_router[l], w_up[l], b_up[l], w_down[l], b_down[l],
                   ln2_g[l], ln2_b[l], batch, seq)
    return h.reshape(batch, seq, d)
```

```python
import functools
import math

import numpy as np
import jax
import jax.numpy as jnp
from jax import lax
from jax.experimental import pallas as pl
from jax.experimental.pallas import tpu as pltpu

F32 = jnp.float32
BF16 = jnp.bfloat16

D_MODEL = 2048
D_ATTN = 1024
HEAD_DIM_A = 64
N_HEADS_A = 16
DILATED_CONFIGS = ((128, 1), (512, 4), (2048, 16))
ATTN_BLOCK = 128
NUM_BUCKETS = 32
MAX_DISTANCE = 2048
D_MLSTM = 1024
N_HEADS_M = 4
HEAD_DIM_M = 256
CONV_K = 4
MLSTM_CHUNK = 128
MAIN_COLS = 3 * D_ATTN + 4 * D_MLSTM
N_EXPERTS = 32
TOP_K = 4
D_FF = 2048
SWIGLU_LIMIT = 7.0
SWIGLU_ALPHA = 1.702
DEPTH = 1
DN_ALPHA = (2 * DEPTH) ** 0.25
LN_EPS = 1e-5
HEAD_NORM_EPS = 1e-6
NEG_INF = -1e30

LANES = 128
VMEM_LIMIT = 48 * 1024 * 1024

PROJ_TM = 1024
PROJ_TN = 1024
COMBINE_TM = 512
OUTPROJ_TM = 512
ROUTER_TM = 512
QKV_TM = 512
FFN_ROWS = 512
FFN_TF = 512
GATHER_ROWS = FFN_ROWS
GATHER_TM = 128


def _cparams(sem, vmem=VMEM_LIMIT):
    return pltpu.CompilerParams(dimension_semantics=sem, vmem_limit_bytes=vmem)


def _dot(a, b):
    return jnp.dot(a, b, preferred_element_type=F32)


def _dot_f32_rhs(a_bf16, b_f32):
    hi = b_f32.astype(BF16)
    lo = (b_f32 - hi.astype(F32)).astype(BF16)
    return _dot(a_bf16, hi) + _dot(a_bf16, lo)


def _dot_f32_lhs(a_f32, b_bf16):
    hi = a_f32.astype(BF16)
    lo = (a_f32 - hi.astype(F32)).astype(BF16)
    return _dot(hi, b_bf16) + _dot(lo, b_bf16)


def _sigmoid(x):
    return 1.0 / (1.0 + jnp.exp(-x))


def _log_sigmoid(x):
    return jnp.minimum(x, 0.0) - jnp.log(1.0 + jnp.exp(-jnp.abs(x)))


def _proj_kernel(x_ref, w_ref, o_ref):
    o_ref[...] = _dot(x_ref[...].astype(BF16), w_ref[...]).astype(o_ref.dtype)


def _project(x, w, out_dtype, tm, tn):
    m, k = x.shape
    n = w.shape[1]
    return pl.pallas_call(
        _proj_kernel,
        grid=(m // tm, n // tn),
        in_specs=[pl.BlockSpec((tm, k), lambda i, j: (i, 0)),
                  pl.BlockSpec((k, tn), lambda i, j: (0, j))],
        out_specs=pl.BlockSpec((tm, tn), lambda i, j: (i, j)),
        out_shape=jax.ShapeDtypeStruct((m, n), out_dtype),
        compiler_params=_cparams(("parallel", "parallel")),
        name="in_proj",
    )(x, w)


def _qkv_proj_kernel(x_ref, w_ref, *refs):
    o_refs, r_ref = refs[:-1], refs[-1]
    res = _dot(x_ref[...].astype(BF16), w_ref[...])
    ntile, tm, _ = r_ref.shape
    wid = ntile * LANES
    for c in range(ntile):
        r_ref[c] = res[:, c * LANES:(c + 1) * LANES]
    for (_, dil), o_ref in zip(DILATED_CONFIGS, o_refs):
        if dil == 1:
            o_ref[...] = res.astype(o_ref.dtype)
        else:
            for r in range(dil):
                for c in range(ntile):
                    col = r * wid + c * LANES
                    o_ref[:, col:col + LANES] = r_ref[c, pl.ds(r, tm // dil, stride=dil), :].astype(o_ref.dtype)


def _qkv_project(x, w, tm):
    m, k = x.shape
    wid = w.shape[1]
    dils = [d for _, d in DILATED_CONFIGS]
    return pl.pallas_call(
        _qkv_proj_kernel,
        grid=(m // tm,),
        in_specs=[pl.BlockSpec((tm, k), lambda i: (i, 0)),
                  pl.BlockSpec((k, wid), lambda i: (0, 0), pipeline_mode=pl.Buffered(1))],
        out_specs=[pl.BlockSpec((tm // d, d * wid), lambda i: (i, 0)) for d in dils],
        out_shape=[jax.ShapeDtypeStruct((m // d, d * wid), BF16) for d in dils],
        scratch_shapes=[pltpu.VMEM((wid // LANES, tm, LANES), F32)],
        compiler_params=_cparams(("parallel",), 56 * 1024 * 1024),
        name="qkv_proj",
    )(x, w)


def _attn_kernel(q_ref, kp_ref, kc_ref, vp_ref, vc_ref, bias_ref, o_ref, lse_ref):
    n = pl.program_id(2)
    tab = jnp.minimum(n, 1)
    lse_ref[...] = jnp.zeros(lse_ref.shape, F32)
    for h in range(N_HEADS_A):
        sl = slice(h * HEAD_DIM_A, (h + 1) * HEAD_DIM_A)
        q = q_ref[0, :, sl] * (HEAD_DIM_A ** -0.5)
        kk = jnp.concatenate([kp_ref[0, :, sl], kc_ref[0, :, sl]], axis=0)
        vv = jnp.concatenate([vp_ref[0, :, sl], vc_ref[0, :, sl]], axis=0)
        s = lax.dot_general(q, kk, (((1,), (1,)), ((), ())), preferred_element_type=F32)
        s = s + bias_ref[tab, h]
        m = jnp.max(s, axis=-1, keepdims=True)
        p = jnp.exp(s - m)
        den = jnp.sum(p, axis=-1, keepdims=True)
        o = _dot(p.astype(BF16), vv) / den
        o_ref[0, :, sl] = o.astype(o_ref.dtype)
        lse_ref[0, :, h:h + 1] = m + jnp.log(den)


def _attn_bias_tables(rel_bias, dil):
    blk = ATTN_BLOCK
    period = 3 * blk
    k = np.arange(period)
    valid = k <= blk
    dist = np.where(valid, blk - k, 0) * dil
    max_exact = NUM_BUCKETS // 2
    d_f = np.maximum(dist, 1).astype(np.float32)
    large = max_exact + (np.log(d_f / np.float32(max_exact)) / np.float32(math.log(MAX_DISTANCE / max_exact))
                         * np.float32(NUM_BUCKETS - max_exact)).astype(np.int32)
    large = np.minimum(large, NUM_BUCKETS - 1)
    bucket = np.where(dist < max_exact, dist, large).astype(np.int32)
    w = jnp.where(jnp.asarray(valid)[None, :], rel_bias[jnp.asarray(bucket)].T.astype(F32), NEG_INF)
    t1 = jnp.tile(w, (1, blk))[:, :blk * (period - 1)].reshape(N_HEADS_A, blk, period - 1)[:, :, :2 * blk]
    has_prev = np.arange(2 * blk)[None, None, :] >= blk
    t0 = jnp.where(jnp.asarray(has_prev), t1, NEG_INF)
    return jnp.stack([t0, t1])


def _dilated_attention(qkv, bias_tab, batch, seq, dil):
    blk = ATTN_BLOCK
    l = seq // dil
    nb = l // blk
    ncb = 3
    pv = qkv.reshape(batch, l, dil * ncb * D_ATTN)

    def cur(c):
        return pl.BlockSpec((1, blk, D_ATTN), lambda b, r, n: (b, n, r * ncb + c))

    def prev(c):
        return pl.BlockSpec((1, blk, D_ATTN), lambda b, r, n: (b, jnp.maximum(n - 1, 0), r * ncb + c))

    o, lse = pl.pallas_call(
        _attn_kernel,
        grid=(batch, dil, nb),
        in_specs=[cur(0), prev(1), cur(1), prev(2), cur(2),
                  pl.BlockSpec((2, N_HEADS_A, blk, 2 * blk), lambda b, r, n: (0, 0, 0, 0))],
        out_specs=[pl.BlockSpec((1, blk, D_ATTN), lambda b, r, n: (b, n, r)),
                   pl.BlockSpec((1, blk, LANES), lambda b, r, n: (b, n, r))],
        out_shape=[jax.ShapeDtypeStruct((batch, l, dil * D_ATTN), BF16),
                   jax.ShapeDtypeStruct((batch, l, dil * LANES), F32)],
        compiler_params=_cparams(("parallel", "parallel", "arbitrary")),
        name=f"dilated_attn_d{dil}",
    )(pv, pv, pv, pv, pv, bias_tab)
    return o.reshape(batch * seq, D_ATTN), lse.reshape(batch * seq, LANES)


def _attn_combine_kernel(o1_ref, o2_ref, o3_ref, l1_ref, l2_ref, l3_ref, e_ref, et_ref, beta_ref, y_ref):
    lses = [l1_ref[...], l2_ref[...], l3_ref[...]]
    outs = [o1_ref, o2_ref, o3_ref]
    mx = jnp.maximum(jnp.maximum(lses[0], lses[1]), lses[2])
    ws = [jnp.exp(l - mx) for l in lses]
    tot = ws[0] + ws[1] + ws[2]
    e = e_ref[...]
    acc = None
    for w, o_ref in zip(ws, outs):
        term = _dot_f32_lhs(w / tot, e) * o_ref[...].astype(F32)
        acc = term if acc is None else acc + term
    ss = _dot_f32_lhs(acc * acc, et_ref[...])
    inv = lax.rsqrt(ss * (1.0 / HEAD_DIM_A) + HEAD_NORM_EPS)
    y_ref[...] = (acc * _dot_f32_lhs(inv, e) * beta_ref[...]).astype(y_ref.dtype)


def _attn_combine(os_, lses, beta_attn):
    n = os_[0].shape[0]
    tm = min(COMBINE_TM, n)
    head_of_lane = np.arange(D_ATTN) // HEAD_DIM_A
    e = (np.arange(LANES)[:, None] == head_of_lane[None, :]).astype(np.float32)
    e_j = jnp.asarray(e, BF16)
    et_j = jnp.asarray(e.T, BF16)
    row = lambda w: pl.BlockSpec((tm, w), lambda i: (i, 0))
    full = lambda a, b: pl.BlockSpec((a, b), lambda i: (0, 0))
    return pl.pallas_call(
        _attn_combine_kernel,
        grid=(n // tm,),
        in_specs=[row(D_ATTN)] * 3 + [row(LANES)] * 3 + [full(LANES, D_ATTN), full(D_ATTN, LANES), full(1, D_ATTN)],
        out_specs=row(D_ATTN),
        out_shape=jax.ShapeDtypeStruct((n, D_ATTN), BF16),
        compiler_params=_cparams(("parallel",)),
        name="attn_combine",
    )(*os_, *lses, e_j, et_j, beta_attn.reshape(1, D_ATTN).astype(F32))


def _mlstm_kernel(qp_ref, kp_ref, qprev_ref, kprev_ref, v_ref, og_ref, g_ref, gb_ref, cw_ref, cb_ref,
                  beta_ref, y_ref, c_ref, n_ref, m_ref):
    step = pl.program_id(1)
    ch = MLSTM_CHUNK
    dh = HEAD_DIM_M

    @pl.when(step == 0)
    def _():
        c_ref[...] = jnp.zeros(c_ref.shape, F32)
        n_ref[...] = jnp.zeros(n_ref.shape, F32)
        m_ref[...] = jnp.zeros(m_ref.shape, F32)

    def conv_silu(x_ref, prev_ref, coff):
        x = x_ref[0].astype(F32)
        p = jnp.where(step > 0, prev_ref[0].astype(F32), 0.0)
        xe = jnp.concatenate([p, x], axis=0)
        npad = p.shape[0]
        cols = slice(coff, coff + D_MLSTM)
        acc = cb_ref[:, cols] + cw_ref[CONV_K - 1:CONV_K, cols] * x
        for s in range(1, CONV_K):
            shifted = pltpu.roll(xe, s, 0)[npad:]
            acc = acc + cw_ref[CONV_K - 1 - s:CONV_K - s, cols] * shifted
        return acc * _sigmoid(acc)

    qf = conv_silu(qp_ref, qprev_ref, 0)
    kf = conv_silu(kp_ref, kprev_ref, D_MLSTM) * (dh ** -0.5)
    qb = qf.astype(BF16)
    kb = kf.astype(BF16)

    g = g_ref[0] + gb_ref[...]
    gt = g.T
    row_i = lax.broadcasted_iota(jnp.int32, (ch, ch), 0)
    col_i = lax.broadcasted_iota(jnp.int32, (ch, ch), 1)
    causal = row_i >= col_i
    tri = jnp.where(causal, 1.0, 0.0).astype(BF16)
    upp = jnp.where(row_i <= col_i, 1.0, 0.0).astype(BF16)
    b_cols = _dot_f32_rhs(tri, _log_sigmoid(g))
    b_rows = _dot_f32_lhs(_log_sigmoid(gt), upp)

    for h in range(N_HEADS_M):
        hs = slice(h * dh, (h + 1) * dh)
        fi = N_HEADS_M + h
        i_row = gt[h:h + 1, :]
        i_col = g[:, h:h + 1]
        b_row = b_rows[fi:fi + 1, :]
        b_col = b_cols[:, fi:fi + 1]
        m_prev = m_ref[h][:, 0:1]
        q_h, k_h = qb[:, hs], kb[:, hs]
        v_h = v_ref[0, :, hs]

        dmat = jnp.where(causal, b_col - b_row + i_row, NEG_INF)
        m_inter = b_col + m_prev
        m_t = jnp.maximum(m_inter, jnp.max(dmat, axis=-1, keepdims=True))
        w = jnp.exp(dmat - m_t) * lax.dot_general(q_h, k_h, (((1,), (1,)), ((), ())),
                                                  preferred_element_type=F32)
        decay = jnp.exp(m_inter - m_t)
        c_old = c_ref[h]
        inter = lax.dot_general(q_h, c_old.astype(BF16), (((1,), (1,)), ((), ())), preferred_element_type=F32)
        num = _dot(w.astype(BF16), v_h) + decay * inter
        n_old = n_ref[h]
        den = jnp.sum(w, axis=-1, keepdims=True) + decay * jnp.sum(qf[:, hs] * n_old, axis=-1, keepdims=True)
        hh = num / jnp.maximum(jnp.abs(den), jnp.exp(-m_t))

        g_last = b_col[ch - 1:ch, :]
        a_row = g_last - b_row + i_row
        a_col = g_last - b_col + i_col
        m_new = jnp.maximum(g_last + m_prev, jnp.max(a_row, axis=-1, keepdims=True))
        carry = jnp.exp(g_last + m_prev - m_new)
        wa_col = jnp.exp(a_col - m_new)
        wv = (wa_col * v_h.astype(F32)).astype(BF16)
        c_ref[h] = carry * c_old + lax.dot_general(wv, k_h, (((0,), (0,)), ((), ())), preferred_element_type=F32)
        n_ref[h] = carry * n_old + jnp.sum(wa_col * kf[:, hs], axis=0, keepdims=True)
        m_ref[h] = jnp.broadcast_to(m_new, (1, LANES))

        gated = _sigmoid(og_ref[0, :, hs].astype(F32)) * hh
        ms = jnp.sum(gated * gated, axis=-1, keepdims=True) * (1.0 / dh)
        y_ref[0, :, hs] = (gated * lax.rsqrt(ms + HEAD_NORM_EPS) * beta_ref[:, hs]).astype(y_ref.dtype)


def _mlstm(proj, gates, gate_bias, conv_w, conv_b, beta_mlstm, batch, seq):
    ch = MLSTM_CHUNK
    nchunk = seq // ch
    pv = proj.reshape(batch, seq, 4 * D_MLSTM)
    gv = gates.reshape(batch, seq, LANES)
    prev_rows = 16
    cb0 = 0

    def cur(c):
        return pl.BlockSpec((1, ch, D_MLSTM), lambda b, n: (b, n, c))

    def prev(c):
        per = ch // prev_rows
        return pl.BlockSpec((1, prev_rows, D_MLSTM), lambda b, n: (b, jnp.maximum(n * per - 1, 0), c))

    const = lambda a, b_: pl.BlockSpec((a, b_), lambda b, n: (0, 0))
    y = pl.pallas_call(
        _mlstm_kernel,
        grid=(batch, nchunk),
        in_specs=[cur(cb0), cur(cb0 + 1), prev(cb0), prev(cb0 + 1), cur(cb0 + 2), cur(cb0 + 3),
                  pl.BlockSpec((1, ch, LANES), lambda b, n: (b, n, 0)),
                  const(1, LANES), const(CONV_K, 2 * D_MLSTM), const(1, 2 * D_MLSTM), const(1, D_MLSTM)],
        out_specs=pl.BlockSpec((1, ch, D_MLSTM), lambda b, n: (b, n, 0)),
        out_shape=jax.ShapeDtypeStruct((batch, seq, D_MLSTM), BF16),
        scratch_shapes=[pltpu.VMEM((N_HEADS_M, HEAD_DIM_M, HEAD_DIM_M), F32),
                        pltpu.VMEM((N_HEADS_M, 1, HEAD_DIM_M), F32),
                        pltpu.VMEM((N_HEADS_M, 1, LANES), F32)],
        compiler_params=_cparams(("parallel", "arbitrary")),
        name="mlstm",
    )(pv, pv, pv, pv, pv, pv, gv, gate_bias, conv_w.astype(F32), conv_b.reshape(1, -1).astype(F32),
      beta_mlstm.reshape(1, D_MLSTM).astype(F32))
    return y.reshape(batch * seq, D_MLSTM)


def _layer_norm(z, g, b):
    mu = jnp.mean(z, axis=-1, keepdims=True)
    zc = z - mu
    var = jnp.mean(zc * zc, axis=-1, keepdims=True)
    return zc * lax.rsqrt(var + LN_EPS) * g + b


def _outproj_kernel(ya_ref, ym_ref, x_ref, w_ref, g_ref, b_ref, h_ref):
    y = _dot(ya_ref[...], w_ref[0:D_ATTN, :]) + _dot(ym_ref[...], w_ref[D_ATTN:D_MODEL, :])
    h_ref[...] = _layer_norm(DN_ALPHA * x_ref[...] + y, g_ref[...], b_ref[...])


def _outproj_ln(ya, ym, x, w_out, g, b):
    n = x.shape[0]
    tm = min(OUTPROJ_TM, n)
    row = lambda w: pl.BlockSpec((tm, w), lambda i: (i, 0))
    full = lambda a, b_: pl.BlockSpec((a, b_), lambda i: (0, 0))
    return pl.pallas_call(
        _outproj_kernel,
        grid=(n // tm,),
        in_specs=[row(D_ATTN), row(D_MLSTM), row(D_MODEL), full(D_MODEL, D_MODEL), full(1, D_MODEL), full(1, D_MODEL)],
        out_specs=row(D_MODEL),
        out_shape=jax.ShapeDtypeStruct((n, D_MODEL), F32),
        compiler_params=_cparams(("parallel",)),
        name="out_proj_ln",
    )(ya, ym, x, w_out, g.reshape(1, -1), b.reshape(1, -1))


def _router_kernel(h_ref, whi_ref, wlo_ref, b_ref, tri_ref, idx_ref, gate_ref, rank_ref, cnt_ref, carry_ref):
    i = pl.program_id(0)

    @pl.when(i == 0)
    def _():
        carry_ref[...] = jnp.zeros(carry_ref.shape, F32)

    x = h_ref[...]
    xhi = x.astype(BF16)
    xlo = (x - xhi.astype(F32)).astype(BF16)
    logits = _dot(xhi, whi_ref[...]) + _dot(xhi, wlo_ref[...]) + _dot(xlo, whi_ref[...]) + b_ref[...]
    tm = logits.shape[0]
    lane = lax.broadcasted_iota(jnp.int32, (tm, LANES), 1)
    lane_f = lane.astype(F32)
    vals = jnp.where(lane < N_EXPERTS, logits, NEG_INF)

    sels, tops = [], []
    for _ in range(TOP_K):
        mx = jnp.max(vals, axis=-1, keepdims=True)
        first = jnp.min(jnp.where(vals == mx, lane_f, float(LANES)), axis=-1, keepdims=True)
        sel = lane_f == first
        sels.append(sel)
        tops.append((mx, first))
        vals = jnp.where(sel, 2.0 * NEG_INF, vals)

    exps = [jnp.exp(mx - tops[0][0]) for mx, _ in tops]
    tot = exps[0] + exps[1] + exps[2] + exps[3]

    onehot = jnp.zeros((tm, LANES), F32)
    for sel in sels:
        onehot = jnp.where(sel, 1.0, onehot)
    before = _dot(tri_ref[...], onehot.astype(BF16)) + carry_ref[...]

    idx_out = jnp.zeros((tm, LANES), F32)
    gate_out = jnp.zeros((tm, LANES), F32)
    rank_out = jnp.zeros((tm, LANES), F32)
    for k in range(TOP_K):
        rank_k = jnp.sum(jnp.where(sels[k], before, 0.0), axis=-1, keepdims=True)
        idx_out = jnp.where(lane == k, tops[k][1], idx_out)
        gate_out = jnp.where(lane == k, exps[k] / tot, gate_out)
        rank_out = jnp.where(lane == k, rank_k, rank_out)
    idx_ref[...] = idx_out.astype(jnp.int32)
    gate_ref[...] = gate_out
    rank_ref[...] = rank_out.astype(jnp.int32)

    carry = carry_ref[...] + jnp.sum(onehot, axis=0, keepdims=True)
    carry_ref[...] = carry
    cnt_ref[...] = carry.astype(jnp.int32)


def _router(h1, w_router, b_router):
    n = h1.shape[0]
    tm = min(ROUTER_TM, n)
    wpad = jnp.zeros((D_MODEL, LANES), F32).at[:, :N_EXPERTS].set(w_router)
    whi = wpad.astype(BF16)
    wlo = (wpad - whi.astype(F32)).astype(BF16)
    bpad = jnp.zeros((1, LANES), F32).at[0, :N_EXPERTS].set(b_router)
    tri = jnp.asarray(np.tril(np.ones((tm, tm), np.float32), -1), BF16)
    row = lambda w: pl.BlockSpec((tm, w), lambda i: (i, 0))
    full = lambda a, b_: pl.BlockSpec((a, b_), lambda i: (0, 0))
    return pl.pallas_call(
        _router_kernel,
        grid=(n // tm,),
        in_specs=[row(D_MODEL), full(D_MODEL, LANES), full(D_MODEL, LANES), full(1, LANES), full(tm, tm)],
        out_specs=[row(LANES), row(LANES), row(LANES), full(1, LANES)],
        out_shape=[jax.ShapeDtypeStruct((n, LANES), jnp.int32), jax.ShapeDtypeStruct((n, LANES), F32),
                   jax.ShapeDtypeStruct((n, LANES), jnp.int32), jax.ShapeDtypeStruct((1, LANES), jnp.int32)],
        scratch_shapes=[pltpu.VMEM((1, LANES), F32)],
        compiler_params=_cparams(("arbitrary",)),
        name="router",
    )(h1, whi, wlo, bpad, tri)


def _gather_rows_kernel(nused_ref, tok_ref, x_ref, o_ref, buf_ref, sem):
    rows = o_ref.shape[0]
    used = pl.program_id(0) < nused_ref[0]

    @pl.when(used)
    def _():
        def issue(j, carry):
            pltpu.make_async_copy(x_ref.at[pl.ds(tok_ref[j], 1)], buf_ref.at[pl.ds(j, 1)], sem).start()
            return carry

        lax.fori_loop(0, rows, issue, 0)
        pltpu.make_async_copy(x_ref.at[pl.ds(0, rows)], buf_ref, sem).wait()
        o_ref[...] = buf_ref[...].astype(o_ref.dtype)

    @pl.when(jnp.logical_not(used))
    def _():
        o_ref[...] = jnp.zeros(o_ref.shape, o_ref.dtype)


def _gather_rows(h1, slot_tok, n_used):
    n_rows = slot_tok.shape[0]
    rows = GATHER_ROWS
    grid_spec = pltpu.PrefetchScalarGridSpec(
        num_scalar_prefetch=1,
        grid=(n_rows // rows,),
        in_specs=[pl.BlockSpec((rows,), lambda i, nu: (i,), memory_space=pltpu.SMEM),
                  pl.BlockSpec(memory_space=pl.ANY)],
        out_specs=pl.BlockSpec((rows, D_MODEL), lambda i, nu: (i, 0)),
        scratch_shapes=[pltpu.VMEM((rows, D_MODEL), F32), pltpu.SemaphoreType.DMA],
    )
    return pl.pallas_call(
        _gather_rows_kernel,
        grid_spec=grid_spec,
        out_shape=jax.ShapeDtypeStruct((n_rows, D_MODEL), BF16),
        compiler_params=pltpu.CompilerParams(dimension_semantics=("arbitrary",), vmem_limit_bytes=VMEM_LIMIT,
                                             disable_bounds_checks=True),
        name="gather_rows",
    )(n_used, slot_tok, h1)


def _ffn_kernel(bexp_ref, nused_ref, x_ref, wg_ref, wu_ref, wd_ref, bg_ref, bu_ref, bd_ref, o_ref):
    del bexp_ref
    i = pl.program_id(0)
    f = pl.program_id(1)
    used = i < nused_ref[0]

    @pl.when(used)
    def _():
        x = x_ref[...]
        hg = _dot(x, wg_ref[0]) + bg_ref[0]
        hu = _dot(x, wu_ref[0]) + bu_ref[0]
        gate = jnp.minimum(hg, SWIGLU_LIMIT)
        up = jnp.clip(hu, -SWIGLU_LIMIT, SWIGLU_LIMIT)
        act = (up + 1.0) * (gate * _sigmoid(SWIGLU_ALPHA * gate))
        y = _dot(act.astype(BF16), wd_ref[0])

        @pl.when(f == 0)
        def _():
            o_ref[...] = y + bd_ref[0]

        @pl.when(f > 0)
        def _():
            o_ref[...] += y

    @pl.when(jnp.logical_and(jnp.logical_not(used), f == 0))
    def _():
        o_ref[...] = jnp.zeros(o_ref.shape, F32)


def _expert_ffn(xs, blk_exp, n_used, w_up, b_up, w_down, b_down):
    n_rows = xs.shape[0]
    rb, tf = FFN_ROWS, FFN_TF
    nblk = n_rows // rb
    nf = D_FF // tf
    last_f = nf - 1

    def clamp(i, nu):
        return jnp.minimum(i, nu[0] - 1)

    def fsel(i, f, nu):
        return jnp.where(i < nu[0], f, last_f)

    grid_spec = pltpu.PrefetchScalarGridSpec(
        num_scalar_prefetch=2,
        grid=(nblk, nf),
        in_specs=[
            pl.BlockSpec((rb, D_MODEL), lambda i, f, be, nu: (clamp(i, nu), 0)),
            pl.BlockSpec((1, D_MODEL, tf), lambda i, f, be, nu: (be[clamp(i, nu)], 0, fsel(i, f, nu))),
            pl.BlockSpec((1, D_MODEL, tf), lambda i, f, be, nu: (be[clamp(i, nu)], 0, nf + fsel(i, f, nu))),
            pl.BlockSpec((1, tf, D_MODEL), lambda i, f, be, nu: (be[clamp(i, nu)], fsel(i, f, nu), 0)),
            pl.BlockSpec((1, 1, tf), lambda i, f, be, nu: (be[clamp(i, nu)], 0, fsel(i, f, nu))),
            pl.BlockSpec((1, 1, tf), lambda i, f, be, nu: (be[clamp(i, nu)], 0, nf + fsel(i, f, nu))),
            pl.BlockSpec((1, 1, D_MODEL), lambda i, f, be, nu: (be[clamp(i, nu)], 0, 0)),
        ],
        out_specs=pl.BlockSpec((rb, D_MODEL), lambda i, f, be, nu: (i, 0)),
    )
    return pl.pallas_call(
        _ffn_kernel,
        grid_spec=grid_spec,
        out_shape=jax.ShapeDtypeStruct((n_rows, D_MODEL), F32),
        compiler_params=_cparams(("arbitrary", "arbitrary")),
        name="expert_ffn",
    )(blk_exp, n_used, xs, w_up, w_up, w_down, b_up.reshape(N_EXPERTS, 1, 2 * D_FF),
      b_up.reshape(N_EXPERTS, 1, 2 * D_FF), b_down.reshape(N_EXPERTS, 1, D_MODEL))


def _combine_kernel(dest_ref, gate_ref, h_ref, g_ref, b_ref, ys_ref, o_ref, buf_ref, sem):
    tm = h_ref.shape[0]

    def issue(t, carry):
        for k in range(TOP_K):
            pltpu.make_async_copy(ys_ref.at[pl.ds(dest_ref[t * TOP_K + k], 1)], buf_ref.at[pl.ds(k * tm + t, 1)],
                                  sem).start()
        return carry

    lax.fori_loop(0, tm, issue, 0)
    pltpu.make_async_copy(ys_ref.at[pl.ds(0, tm * TOP_K)], buf_ref, sem).wait()
    gates = gate_ref[...]
    z = DN_ALPHA * h_ref[...]
    for k in range(TOP_K):
        z = z + gates[:, k:k + 1] * buf_ref[k * tm:(k + 1) * tm, :]
    o_ref[...] = _layer_norm(z, g_ref[...], b_ref[...])


def _combine_ln(dest_flat, gates, h1, ys, g, b):
    n = h1.shape[0]
    tm = min(GATHER_TM, n)
    row = lambda w: pl.BlockSpec((tm, w), lambda i: (i, 0))
    full = lambda a, b_: pl.BlockSpec((a, b_), lambda i: (0, 0))
    return pl.pallas_call(
        _combine_kernel,
        grid=(n // tm,),
        in_specs=[pl.BlockSpec((tm * TOP_K,), lambda i: (i,), memory_space=pltpu.SMEM),
                  row(LANES), row(D_MODEL), full(1, D_MODEL), full(1, D_MODEL),
                  pl.BlockSpec(memory_space=pl.ANY)],
        out_specs=row(D_MODEL),
        out_shape=jax.ShapeDtypeStruct((n, D_MODEL), F32),
        scratch_shapes=[pltpu.VMEM((tm * TOP_K, D_MODEL), F32), pltpu.SemaphoreType.DMA],
        compiler_params=pltpu.CompilerParams(dimension_semantics=("arbitrary",), vmem_limit_bytes=VMEM_LIMIT,
                                             disable_bounds_checks=True),
        name="combine_ln",
    )(dest_flat, gates, h1, g.reshape(1, -1), b.reshape(1, -1), ys)


def _layer(h, w_in, b_igate, b_fgate, conv_w, conv_b, rel_bias, beta_attn, beta_mlstm, w_out,
           ln1_g, ln1_b, w_router, b_router, w_up, b_up, w_down, b_down, ln2_g, ln2_b, batch, seq):
    n = batch * seq
    w_qkv = w_in[:, :3 * D_ATTN].astype(BF16)
    w_mix = w_in[:, 3 * D_ATTN:MAIN_COLS].astype(BF16)
    w_gate = jnp.zeros((D_MODEL, LANES), BF16).at[:, :2 * N_HEADS_M].set(w_in[:, MAIN_COLS:].astype(BF16))
    qkvs = _qkv_project(h, w_qkv, min(QKV_TM, n))
    proj_m = _project(h, w_mix, BF16, min(PROJ_TM, n), PROJ_TN)
    gates = _project(h, w_gate, F32, min(PROJ_TM, n), LANES)
    gate_bias = jnp.zeros((1, LANES), F32).at[0, :2 * N_HEADS_M].set(jnp.concatenate([b_igate, b_fgate]))

    outs, lses = [], []
    for (_, dil), qkv in zip(DILATED_CONFIGS, qkvs):
        o, l = _dilated_attention(qkv, _attn_bias_tables(rel_bias, dil), batch, seq, dil)
        outs.append(o)
        lses.append(l)
    y_attn = _attn_combine(outs, lses, beta_attn)
    y_mlstm = _mlstm(proj_m, gates, gate_bias, conv_w, conv_b, beta_mlstm, batch, seq)

    h1 = _outproj_ln(y_attn, y_mlstm, h, w_out.astype(BF16), ln1_g, ln1_b)

    top_idx, top_gate, rank, counts = _router(h1, w_router, b_router)
    counts = counts[0, :N_EXPERTS]
    padded = (counts + FFN_ROWS - 1) // FFN_ROWS * FFN_ROWS
    pad_end = jnp.cumsum(padded)
    pad_start = pad_end - padded
    dest = (pad_start[top_idx[:, :TOP_K]] + rank[:, :TOP_K]).reshape(-1).astype(jnp.int32)
    nblk = n * TOP_K // FFN_ROWS + N_EXPERTS
    blk_row0 = jnp.arange(nblk, dtype=jnp.int32) * FFN_ROWS
    blk_exp = jnp.minimum(jnp.sum(pad_end[None, :] <= blk_row0[:, None], axis=1), N_EXPERTS - 1).astype(jnp.int32)
    n_used = (pad_end[-1:] // FFN_ROWS).astype(jnp.int32)
    tok_of_assignment = jnp.arange(n * TOP_K, dtype=jnp.int32) // TOP_K
    slot_tok = jnp.zeros((nblk * FFN_ROWS,), jnp.int32).at[dest].set(tok_of_assignment, unique_indices=True)

    xs = _gather_rows(h1, slot_tok, n_used)
    ys = _expert_ffn(xs, blk_exp, n_used, w_up.astype(BF16), b_up, w_down.astype(BF16), b_down)
    return _combine_ln(dest, top_gate, h1, ys, ln2_g, ln2_b)


def kernel(x, w_in, b_igate, b_fgate, conv_w, conv_b, rel_bias, beta_attn, beta_mlstm, w_out, ln1_g, ln1_b,
           w_router, b_router, w_up, b_up, w_down, b_down, ln2_g, ln2_b):
    batch, seq, d = x.shape
    h = x.reshape(batch * seq, d)
    for l in range(DEPTH):
        h = _layer(h, w_in[l], b_igate[l], b_fgate[l], conv_w[l], conv_b[l], rel_bias, beta_attn[l], beta_mlstm[l],
                   w_out[l], ln1_g[l], ln1_b[l], w_router[l], b_router[l], w_up[l], b_up[l], w_down[l], b_down[l],
                   ln2_g[l], ln2_b[l], batch, seq)
    return h.reshape(batch, seq, d)
```

```python
import functools
import math

import numpy as np
import jax
import jax.numpy as jnp
from jax import lax
from jax.experimental import pallas as pl
from jax.experimental.pallas import tpu as pltpu

F32 = jnp.float32
BF16 = jnp.bfloat16

D_MODEL = 2048
D_ATTN = 1024
HEAD_DIM_A = 64
N_HEADS_A = 16
DILATED_CONFIGS = ((128, 1), (512, 4), (2048, 16))
ATTN_BLOCK = 128
NUM_BUCKETS = 32
MAX_DISTANCE = 2048
D_MLSTM = 1024
N_HEADS_M = 4
HEAD_DIM_M = 256
CONV_K = 4
MLSTM_CHUNK = 128
MAIN_COLS = 3 * D_ATTN + 4 * D_MLSTM
N_EXPERTS = 32
TOP_K = 4
D_FF = 2048
SWIGLU_LIMIT = 7.0
SWIGLU_ALPHA = 1.702
DEPTH = 1
DN_ALPHA = (2 * DEPTH) ** 0.25
LN_EPS = 1e-5
HEAD_NORM_EPS = 1e-6
NEG_INF = -1e30

LANES = 128
VMEM_LIMIT = 48 * 1024 * 1024

PROJ_TM = 1024
PROJ_TN = 1024
COMBINE_TM = 512
OUTPROJ_TM = 512
ROUTER_TM = 512
QKV_TM = 512
ATTN_GROUP = 2
FFN_ROWS = 512
FFN_TF = 512
GATHER_ROWS = FFN_ROWS
GATHER_TM = 128


def _cparams(sem, vmem=VMEM_LIMIT):
    return pltpu.CompilerParams(dimension_semantics=sem, vmem_limit_bytes=vmem)


def _dot(a, b):
    return jnp.dot(a, b, preferred_element_type=F32)


def _dot_f32_rhs(a_bf16, b_f32):
    hi = b_f32.astype(BF16)
    lo = (b_f32 - hi.astype(F32)).astype(BF16)
    return _dot(a_bf16, hi) + _dot(a_bf16, lo)


def _dot_f32_lhs(a_f32, b_bf16):
    hi = a_f32.astype(BF16)
    lo = (a_f32 - hi.astype(F32)).astype(BF16)
    return _dot(hi, b_bf16) + _dot(lo, b_bf16)


def _sigmoid(x):
    return 1.0 / (1.0 + jnp.exp(-x))


def _log_sigmoid(x):
    return jnp.minimum(x, 0.0) - jnp.log(1.0 + jnp.exp(-jnp.abs(x)))


def _proj_kernel(x_ref, w_ref, o_ref):
    o_ref[...] = _dot(x_ref[...].astype(BF16), w_ref[...]).astype(o_ref.dtype)


def _project(x, w, out_dtype, tm, tn):
    m, k = x.shape
    n = w.shape[1]
    return pl.pallas_call(
        _proj_kernel,
        grid=(m // tm, n // tn),
        in_specs=[pl.BlockSpec((tm, k), lambda i, j: (i, 0)),
                  pl.BlockSpec((k, tn), lambda i, j: (0, j))],
        out_specs=pl.BlockSpec((tm, tn), lambda i, j: (i, j)),
        out_shape=jax.ShapeDtypeStruct((m, n), out_dtype),
        compiler_params=_cparams(("parallel", "parallel")),
        name="in_proj",
    )(x, w)


def _qkv_proj_kernel(x_ref, w_ref, *refs):
    o_refs, r_ref = refs[:-1], refs[-1]
    res = _dot(x_ref[...].astype(BF16), w_ref[...])
    ntile, tm, _ = r_ref.shape
    wid = ntile * LANES
    for c in range(ntile):
        r_ref[c] = res[:, c * LANES:(c + 1) * LANES]
    for (_, dil), o_ref in zip(DILATED_CONFIGS, o_refs):
        if dil == 1:
            o_ref[...] = res.astype(o_ref.dtype)
        else:
            for r in range(dil):
                for c in range(ntile):
                    col = r * wid + c * LANES
                    o_ref[:, col:col + LANES] = r_ref[c, pl.ds(r, tm // dil, stride=dil), :].astype(o_ref.dtype)


def _qkv_project(x, w, tm):
    m, k = x.shape
    wid = w.shape[1]
    dils = [d for _, d in DILATED_CONFIGS]
    return pl.pallas_call(
        _qkv_proj_kernel,
        grid=(m // tm,),
        in_specs=[pl.BlockSpec((tm, k), lambda i: (i, 0)),
                  pl.BlockSpec((k, wid), lambda i: (0, 0), pipeline_mode=pl.Buffered(1))],
        out_specs=[pl.BlockSpec((tm // d, d * wid), lambda i: (i, 0)) for d in dils],
        out_shape=[jax.ShapeDtypeStruct((m // d, d * wid), BF16) for d in dils],
        scratch_shapes=[pltpu.VMEM((wid // LANES, tm, LANES), F32)],
        compiler_params=_cparams(("parallel",), 56 * 1024 * 1024),
        name="qkv_proj",
    )(x, w)


def _attn_kernel(q_ref, kp_ref, kc_ref, vp_ref, vc_ref, bias_ref, o_ref, lse_ref):
    n = pl.program_id(2)
    tab = jnp.minimum(n, 1)
    lse_ref[...] = jnp.zeros(lse_ref.shape, F32)
    grp, dh, nk = ATTN_GROUP, HEAD_DIM_A, 2 * ATTN_BLOCK
    wid = grp * dh
    lane_head = lax.broadcasted_iota(jnp.int32, (nk, wid), 1) // dh
    zero = jnp.zeros((nk, wid), BF16)
    ones_bd = jnp.concatenate([jnp.where(lane_head == j, 1.0, 0.0).astype(BF16) for j in range(grp)], axis=0)
    for g in range(N_HEADS_A // grp):
        cols = slice(g * wid, (g + 1) * wid)
        q = q_ref[0, :, cols] * (dh ** -0.5)
        kslab = jnp.concatenate([kp_ref[0, :, cols], kc_ref[0, :, cols]], axis=0)
        vslab = jnp.concatenate([vp_ref[0, :, cols], vc_ref[0, :, cols]], axis=0)
        k_bd = jnp.concatenate([jnp.where(lane_head == j, kslab, zero) for j in range(grp)], axis=0)
        v_bd = jnp.concatenate([jnp.where(lane_head == j, vslab, zero) for j in range(grp)], axis=0)
        s_all = lax.dot_general(q, k_bd, (((1,), (1,)), ((), ())), preferred_element_type=F32)
        ps, ms = [], []
        for j in range(grp):
            s = s_all[:, j * nk:(j + 1) * nk] + bias_ref[tab, g * grp + j]
            m = jnp.max(s, axis=-1, keepdims=True)
            ps.append(jnp.exp(s - m).astype(BF16))
            ms.append(m)
        p_all = jnp.concatenate(ps, axis=-1)
        res = _dot(p_all, jnp.concatenate([v_bd, ones_bd], axis=-1))
        den = res[:, wid:]
        o_ref[0, :, cols] = (res[:, :wid] / den).astype(o_ref.dtype)
        for j in range(grp):
            h = g * grp + j
            lse_ref[0, :, h:h + 1] = ms[j] + jnp.log(den[:, j * dh:j * dh + 1])


def _attn_bias_tables(rel_bias, dil):
    blk = ATTN_BLOCK
    period = 3 * blk
    k = np.arange(period)
    valid = k <= blk
    dist = np.where(valid, blk - k, 0) * dil
    max_exact = NUM_BUCKETS // 2
    d_f = np.maximum(dist, 1).astype(np.float32)
    large = max_exact + (np.log(d_f / np.float32(max_exact)) / np.float32(math.log(MAX_DISTANCE / max_exact))
                         * np.float32(NUM_BUCKETS - max_exact)).astype(np.int32)
    large = np.minimum(large, NUM_BUCKETS - 1)
    bucket = np.where(dist < max_exact, dist, large).astype(np.int32)
    w = jnp.where(jnp.asarray(valid)[None, :], rel_bias[jnp.asarray(bucket)].T.astype(F32), NEG_INF)
    t1 = jnp.tile(w, (1, blk))[:, :blk * (period - 1)].reshape(N_HEADS_A, blk, period - 1)[:, :, :2 * blk]
    has_prev = np.arange(2 * blk)[None, None, :] >= blk
    t0 = jnp.where(jnp.asarray(has_prev), t1, NEG_INF)
    return jnp.stack([t0, t1])


def _dilated_attention(qkv, bias_tab, batch, seq, dil):
    blk = ATTN_BLOCK
    l = seq // dil
    nb = l // blk
    ncb = 3
    pv = qkv.reshape(batch, l, dil * ncb * D_ATTN)

    def cur(c):
        return pl.BlockSpec((1, blk, D_ATTN), lambda b, r, n: (b, n, r * ncb + c))

    def prev(c):
        return pl.BlockSpec((1, blk, D_ATTN), lambda b, r, n: (b, jnp.maximum(n - 1, 0), r * ncb + c))

    o, lse = pl.pallas_call(
        _attn_kernel,
        grid=(batch, dil, nb),
        in_specs=[cur(0), prev(1), cur(1), prev(2), cur(2),
                  pl.BlockSpec((2, N_HEADS_A, blk, 2 * blk), lambda b, r, n: (0, 0, 0, 0))],
        out_specs=[pl.BlockSpec((1, blk, D_ATTN), lambda b, r, n: (b, n, r)),
                   pl.BlockSpec((1, blk, LANES), lambda b, r, n: (b, n, r))],
        out_shape=[jax.ShapeDtypeStruct((batch, l, dil * D_ATTN), BF16),
                   jax.ShapeDtypeStruct((batch, l, dil * LANES), F32)],
        compiler_params=_cparams(("parallel", "parallel", "arbitrary")),
        name=f"dilated_attn_d{dil}",
    )(pv, pv, pv, pv, pv, bias_tab)
    return o.reshape(batch * seq, D_ATTN), lse.reshape(batch * seq, LANES)


def _attn_combine_kernel(o1_ref, o2_ref, o3_ref, l1_ref, l2_ref, l3_ref, e_ref, et_ref, beta_ref, y_ref):
    lses = [l1_ref[...], l2_ref[...], l3_ref[...]]
    outs = [o1_ref, o2_ref, o3_ref]
    mx = jnp.maximum(jnp.maximum(lses[0], lses[1]), lses[2])
    ws = [jnp.exp(l - mx) for l in lses]
    tot = ws[0] + ws[1] + ws[2]
    e = e_ref[...]
    acc = None
    for w, o_ref in zip(ws, outs):
        term = _dot_f32_lhs(w / tot, e) * o_ref[...].astype(F32)
        acc = term if acc is None else acc + term
    ss = _dot_f32_lhs(acc * acc, et_ref[...])
    inv = lax.rsqrt(ss * (1.0 / HEAD_DIM_A) + HEAD_NORM_EPS)
    y_ref[...] = (acc * _dot_f32_lhs(inv, e) * beta_ref[...]).astype(y_ref.dtype)


def _attn_combine(os_, lses, beta_attn):
    n = os_[0].shape[0]
    tm = min(COMBINE_TM, n)
    head_of_lane = np.arange(D_ATTN) // HEAD_DIM_A
    e = (np.arange(LANES)[:, None] == head_of_lane[None, :]).astype(np.float32)
    e_j = jnp.asarray(e, BF16)
    et_j = jnp.asarray(e.T, BF16)
    row = lambda w: pl.BlockSpec((tm, w), lambda i: (i, 0))
    full = lambda a, b: pl.BlockSpec((a, b), lambda i: (0, 0))
    return pl.pallas_call(
        _attn_combine_kernel,
        grid=(n // tm,),
        in_specs=[row(D_ATTN)] * 3 + [row(LANES)] * 3 + [full(LANES, D_ATTN), full(D_ATTN, LANES), full(1, D_ATTN)],
        out_specs=row(D_ATTN),
        out_shape=jax.ShapeDtypeStruct((n, D_ATTN), BF16),
        compiler_params=_cparams(("parallel",)),
        name="attn_combine",
    )(*os_, *lses, e_j, et_j, beta_attn.reshape(1, D_ATTN).astype(F32))


def _mlstm_kernel(qp_ref, kp_ref, qprev_ref, kprev_ref, v_ref, og_ref, g_ref, gb_ref, cw_ref, cb_ref,
                  beta_ref, y_ref, c_ref, n_ref, m_ref):
    step = pl.program_id(1)
    ch = MLSTM_CHUNK
    dh = HEAD_DIM_M

    @pl.when(step == 0)
    def _():
        c_ref[...] = jnp.zeros(c_ref.shape, F32)
        n_ref[...] = jnp.zeros(n_ref.shape, F32)
        m_ref[...] = jnp.zeros(m_ref.shape, F32)

    def conv_silu(x_ref, prev_ref, coff):
        x = x_ref[0].astype(F32)
        p = jnp.where(step > 0, prev_ref[0].astype(F32), 0.0)
        xe = jnp.concatenate([p, x], axis=0)
        npad = p.shape[0]
        cols = slice(coff, coff + D_MLSTM)
        acc = cb_ref[:, cols] + cw_ref[CONV_K - 1:CONV_K, cols] * x
        for s in range(1, CONV_K):
            shifted = pltpu.roll(xe, s, 0)[npad:]
            acc = acc + cw_ref[CONV_K - 1 - s:CONV_K - s, cols] * shifted
        return acc * _sigmoid(acc)

    qf = conv_silu(qp_ref, qprev_ref, 0)
    kf = conv_silu(kp_ref, kprev_ref, D_MLSTM) * (dh ** -0.5)
    qb = qf.astype(BF16)
    kb = kf.astype(BF16)

    g = g_ref[0] + gb_ref[...]
    gt = g.T
    row_i = lax.broadcasted_iota(jnp.int32, (ch, ch), 0)
    col_i = lax.broadcasted_iota(jnp.int32, (ch, ch), 1)
    causal = row_i >= col_i
    tri = jnp.where(causal, 1.0, 0.0).astype(BF16)
    upp = jnp.where(row_i <= col_i, 1.0, 0.0).astype(BF16)
    b_cols = _dot_f32_rhs(tri, _log_sigmoid(g))
    b_rows = _dot_f32_lhs(_log_sigmoid(gt), upp)

    for h in range(N_HEADS_M):
        hs = slice(h * dh, (h + 1) * dh)
        fi = N_HEADS_M + h
        i_row = gt[h:h + 1, :]
        i_col = g[:, h:h + 1]
        b_row = b_rows[fi:fi + 1, :]
        b_col = b_cols[:, fi:fi + 1]
        m_prev = m_ref[h][:, 0:1]
        q_h, k_h = qb[:, hs], kb[:, hs]
        v_h = v_ref[0, :, hs]

        dmat = jnp.where(causal, b_col - b_row + i_row, NEG_INF)
        m_inter = b_col + m_prev
        m_t = jnp.maximum(m_inter, jnp.max(dmat, axis=-1, keepdims=True))
        w = jnp.exp(dmat - m_t) * lax.dot_general(q_h, k_h, (((1,), (1,)), ((), ())),
                                                  preferred_element_type=F32)
        decay = jnp.exp(m_inter - m_t)
        c_old = c_ref[h]
        inter = lax.dot_general(q_h, c_old.astype(BF16), (((1,), (1,)), ((), ())), preferred_element_type=F32)
        num = _dot(w.astype(BF16), v_h) + decay * inter
        n_old = n_ref[h]
        den = jnp.sum(w, axis=-1, keepdims=True) + decay * jnp.sum(qf[:, hs] * n_old, axis=-1, keepdims=True)
        hh = num / jnp.maximum(jnp.abs(den), jnp.exp(-m_t))

        g_last = b_col[ch - 1:ch, :]
        a_row = g_last - b_row + i_row
        a_col = g_last - b_col + i_col
        m_new = jnp.maximum(g_last + m_prev, jnp.max(a_row, axis=-1, keepdims=True))
        carry = jnp.exp(g_last + m_prev - m_new)
        wa_col = jnp.exp(a_col - m_new)
        wv = (wa_col * v_h.astype(F32)).astype(BF16)
        c_ref[h] = carry * c_old + lax.dot_general(wv, k_h, (((0,), (0,)), ((), ())), preferred_element_type=F32)
        n_ref[h] = carry * n_old + jnp.sum(wa_col * kf[:, hs], axis=0, keepdims=True)
        m_ref[h] = jnp.broadcast_to(m_new, (1, LANES))

        gated = _sigmoid(og_ref[0, :, hs].astype(F32)) * hh
        ms = jnp.sum(gated * gated, axis=-1, keepdims=True) * (1.0 / dh)
        y_ref[0, :, hs] = (gated * lax.rsqrt(ms + HEAD_NORM_EPS) * beta_ref[:, hs]).astype(y_ref.dtype)


def _mlstm(proj, gates, gate_bias, conv_w, conv_b, beta_mlstm, batch, seq):
    ch = MLSTM_CHUNK
    nchunk = seq // ch
    pv = proj.reshape(batch, seq, 4 * D_MLSTM)
    gv = gates.reshape(batch, seq, LANES)
    prev_rows = 16
    cb0 = 0

    def cur(c):
        return pl.BlockSpec((1, ch, D_MLSTM), lambda b, n: (b, n, c))

    def prev(c):
        per = ch // prev_rows
        return pl.BlockSpec((1, prev_rows, D_MLSTM), lambda b, n: (b, jnp.maximum(n * per - 1, 0), c))

    const = lambda a, b_: pl.BlockSpec((a, b_), lambda b, n: (0, 0))
    y = pl.pallas_call(
        _mlstm_kernel,
        grid=(batch, nchunk),
        in_specs=[cur(cb0), cur(cb0 + 1), prev(cb0), prev(cb0 + 1), cur(cb0 + 2), cur(cb0 + 3),
                  pl.BlockSpec((1, ch, LANES), lambda b, n: (b, n, 0)),
                  const(1, LANES), const(CONV_K, 2 * D_MLSTM), const(1, 2 * D_MLSTM), const(1, D_MLSTM)],
        out_specs=pl.BlockSpec((1, ch, D_MLSTM), lambda b, n: (b, n, 0)),
        out_shape=jax.ShapeDtypeStruct((batch, seq, D_MLSTM), BF16),
        scratch_shapes=[pltpu.VMEM((N_HEADS_M, HEAD_DIM_M, HEAD_DIM_M), F32),
                        pltpu.VMEM((N_HEADS_M, 1, HEAD_DIM_M), F32),
                        pltpu.VMEM((N_HEADS_M, 1, LANES), F32)],
        compiler_params=_cparams(("parallel", "arbitrary")),
        name="mlstm",
    )(pv, pv, pv, pv, pv, pv, gv, gate_bias, conv_w.astype(F32), conv_b.reshape(1, -1).astype(F32),
      beta_mlstm.reshape(1, D_MLSTM).astype(F32))
    return y.reshape(batch * seq, D_MLSTM)


def _layer_norm(z, g, b):
    mu = jnp.mean(z, axis=-1, keepdims=True)
    zc = z - mu
    var = jnp.mean(zc * zc, axis=-1, keepdims=True)
    return zc * lax.rsqrt(var + LN_EPS) * g + b


PACK_ROWS = D_MODEL // (2 * LANES)


def _store_packed_rows(dst_ref, x):
    rows = x.shape[0]
    half = D_MODEL // 2
    for s in range(PACK_ROWS):
        lo = x[:, s * LANES:(s + 1) * LANES].astype(BF16).astype(F32)
        hi = x[:, half + s * LANES:half + (s + 1) * LANES].astype(BF16).astype(F32)
        word = pltpu.bitcast(hi, jnp.int32) | lax.shift_right_logical(pltpu.bitcast(lo, jnp.int32), 16)
        dst_ref[pl.ds(s, rows, stride=PACK_ROWS), :] = word


def _load_packed_rows(src_ref, first, rows):
    los, his = [], []
    for s in range(PACK_ROWS):
        word = src_ref[pl.ds(first * PACK_ROWS + s, rows, stride=PACK_ROWS), :]
        los.append(pltpu.bitcast(lax.shift_left(word, 16), F32))
        his.append(pltpu.bitcast(word & jnp.int32(-65536), F32))
    return los, his


def _outproj_kernel(ya_ref, ym_ref, x_ref, w_ref, g_ref, b_ref, h_ref, hp_ref):
    y = _dot(ya_ref[...], w_ref[0:D_ATTN, :]) + _dot(ym_ref[...], w_ref[D_ATTN:D_MODEL, :])
    h = _layer_norm(DN_ALPHA * x_ref[...] + y, g_ref[...], b_ref[...])
    h_ref[...] = h
    _store_packed_rows(hp_ref, h)


def _outproj_ln(ya, ym, x, w_out, g, b):
    n = x.shape[0]
    tm = min(OUTPROJ_TM, n)
    row = lambda w: pl.BlockSpec((tm, w), lambda i: (i, 0))
    full = lambda a, b_: pl.BlockSpec((a, b_), lambda i: (0, 0))
    return pl.pallas_call(
        _outproj_kernel,
        grid=(n // tm,),
        in_specs=[row(D_ATTN), row(D_MLSTM), row(D_MODEL), full(D_MODEL, D_MODEL), full(1, D_MODEL), full(1, D_MODEL)],
        out_specs=[row(D_MODEL), pl.BlockSpec((tm * PACK_ROWS, LANES), lambda i: (i, 0))],
        out_shape=[jax.ShapeDtypeStruct((n, D_MODEL), F32),
                   jax.ShapeDtypeStruct((n * PACK_ROWS, LANES), jnp.int32)],
        compiler_params=_cparams(("parallel",)),
        name="out_proj_ln",
    )(ya, ym, x, w_out, g.reshape(1, -1), b.reshape(1, -1))


def _router_kernel(h_ref, whi_ref, wlo_ref, b_ref, tri_ref, idx_ref, gate_ref, rank_ref, cnt_ref, carry_ref):
    i = pl.program_id(0)

    @pl.when(i == 0)
    def _():
        carry_ref[...] = jnp.zeros(carry_ref.shape, F32)

    x = h_ref[...]
    xhi = x.astype(BF16)
    xlo = (x - xhi.astype(F32)).astype(BF16)
    logits = _dot(xhi, whi_ref[...]) + _dot(xhi, wlo_ref[...]) + _dot(xlo, whi_ref[...]) + b_ref[...]
    tm = logits.shape[0]
    lane = lax.broadcasted_iota(jnp.int32, (tm, LANES), 1)
    lane_f = lane.astype(F32)
    vals = jnp.where(lane < N_EXPERTS, logits, NEG_INF)

    sels, tops = [], []
    for _ in range(TOP_K):
        mx = jnp.max(vals, axis=-1, keepdims=True)
        first = jnp.min(jnp.where(vals == mx, lane_f, float(LANES)), axis=-1, keepdims=True)
        sel = lane_f == first
        sels.append(sel)
        tops.append((mx, first))
        vals = jnp.where(sel, 2.0 * NEG_INF, vals)

    exps = [jnp.exp(mx - tops[0][0]) for mx, _ in tops]
    tot = exps[0] + exps[1] + exps[2] + exps[3]

    onehot = jnp.zeros((tm, LANES), F32)
    for sel in sels:
        onehot = jnp.where(sel, 1.0, onehot)
    before = _dot(tri_ref[...], onehot.astype(BF16)) + carry_ref[...]

    idx_out = jnp.zeros((tm, LANES), F32)
    gate_out = jnp.zeros((tm, LANES), F32)
    rank_out = jnp.zeros((tm, LANES), F32)
    for k in range(TOP_K):
        rank_k = jnp.sum(jnp.where(sels[k], before, 0.0), axis=-1, keepdims=True)
        idx_out = jnp.where(lane == k, tops[k][1], idx_out)
        gate_out = jnp.where(lane == k, exps[k] / tot, gate_out)
        rank_out = jnp.where(lane == k, rank_k, rank_out)
    idx_ref[...] = idx_out.astype(jnp.int32)
    gate_ref[...] = gate_out
    rank_ref[...] = rank_out.astype(jnp.int32)

    carry = carry_ref[...] + jnp.sum(onehot, axis=0, keepdims=True)
    carry_ref[...] = carry
    cnt_ref[...] = carry.astype(jnp.int32)


def _router(h1, w_router, b_router):
    n = h1.shape[0]
    tm = min(ROUTER_TM, n)
    wpad = jnp.zeros((D_MODEL, LANES), F32).at[:, :N_EXPERTS].set(w_router)
    whi = wpad.astype(BF16)
    wlo = (wpad - whi.astype(F32)).astype(BF16)
    bpad = jnp.zeros((1, LANES), F32).at[0, :N_EXPERTS].set(b_router)
    tri = jnp.asarray(np.tril(np.ones((tm, tm), np.float32), -1), BF16)
    row = lambda w: pl.BlockSpec((tm, w), lambda i: (i, 0))
    full = lambda a, b_: pl.BlockSpec((a, b_), lambda i: (0, 0))
    return pl.pallas_call(
        _router_kernel,
        grid=(n // tm,),
        in_specs=[row(D_MODEL), full(D_MODEL, LANES), full(D_MODEL, LANES), full(1, LANES), full(tm, tm)],
        out_specs=[row(LANES), row(LANES), row(LANES), full(1, LANES)],
        out_shape=[jax.ShapeDtypeStruct((n, LANES), jnp.int32), jax.ShapeDtypeStruct((n, LANES), F32),
                   jax.ShapeDtypeStruct((n, LANES), jnp.int32), jax.ShapeDtypeStruct((1, LANES), jnp.int32)],
        scratch_shapes=[pltpu.VMEM((1, LANES), F32)],
        compiler_params=_cparams(("arbitrary",)),
        name="router",
    )(h1, whi, wlo, bpad, tri)


def _gather_rows_kernel(nused_ref, tok_ref, x_ref, o_ref, buf_ref, sem):
    rows = o_ref.shape[0]
    used = pl.program_id(0) < nused_ref[0]

    @pl.when(used)
    def _():
        def issue(j, carry):
            src = pl.multiple_of(tok_ref[j] * PACK_ROWS, PACK_ROWS)
            dst = pl.multiple_of(j * PACK_ROWS, PACK_ROWS)
            pltpu.make_async_copy(x_ref.at[pl.ds(src, PACK_ROWS)], buf_ref.at[pl.ds(dst, PACK_ROWS)], sem).start()
            return carry

        lax.fori_loop(0, rows, issue, 0)
        pltpu.make_async_copy(x_ref.at[pl.ds(0, rows * PACK_ROWS)], buf_ref, sem).wait()
        los, his = _load_packed_rows(buf_ref, 0, rows)
        for s in range(PACK_ROWS):
            o_ref[:, s * LANES:(s + 1) * LANES] = los[s].astype(o_ref.dtype)
            o_ref[:, D_MODEL // 2 + s * LANES:D_MODEL // 2 + (s + 1) * LANES] = his[s].astype(o_ref.dtype)

    @pl.when(jnp.logical_not(used))
    def _():
        o_ref[...] = jnp.zeros(o_ref.shape, o_ref.dtype)


def _gather_rows(h1, slot_tok, n_used):
    n_rows = slot_tok.shape[0]
    rows = GATHER_ROWS
    grid_spec = pltpu.PrefetchScalarGridSpec(
        num_scalar_prefetch=1,
        grid=(n_rows // rows,),
        in_specs=[pl.BlockSpec((rows,), lambda i, nu: (i,), memory_space=pltpu.SMEM),
                  pl.BlockSpec(memory_space=pl.ANY)],
        out_specs=pl.BlockSpec((rows, D_MODEL), lambda i, nu: (i, 0)),
        scratch_shapes=[pltpu.VMEM((rows * PACK_ROWS, LANES), jnp.int32), pltpu.SemaphoreType.DMA],
    )
    return pl.pallas_call(
        _gather_rows_kernel,
        grid_spec=grid_spec,
        out_shape=jax.ShapeDtypeStruct((n_rows, D_MODEL), BF16),
        compiler_params=pltpu.CompilerParams(dimension_semantics=("arbitrary",), vmem_limit_bytes=VMEM_LIMIT,
                                             disable_bounds_checks=True),
        name="gather_rows",
    )(n_used, slot_tok, h1)


def _ffn_kernel(bexp_ref, nused_ref, x_ref, wg_ref, wu_ref, wd_ref, bg_ref, bu_ref, bd_ref, o_ref, acc_ref):
    del bexp_ref
    i = pl.program_id(0)
    f = pl.program_id(1)
    last_f = pl.num_programs(1) - 1
    used = i < nused_ref[0]

    @pl.when(used)
    def _():
        x = x_ref[...]
        hg = _dot(x, wg_ref[0]) + bg_ref[0]
        hu = _dot(x, wu_ref[0]) + bu_ref[0]
        gate = jnp.minimum(hg, SWIGLU_LIMIT)
        up = jnp.clip(hu, -SWIGLU_LIMIT, SWIGLU_LIMIT)
        act = (up + 1.0) * (gate * _sigmoid(SWIGLU_ALPHA * gate))
        y = _dot(act.astype(BF16), wd_ref[0])

        @pl.when(f == 0)
        def _():
            acc_ref[...] = y + bd_ref[0]

        @pl.when(jnp.logical_and(f > 0, f < last_f))
        def _():
            acc_ref[...] += y

        @pl.when(f == last_f)
        def _():
            _store_packed_rows(o_ref, acc_ref[...] + y)

    @pl.when(jnp.logical_and(jnp.logical_not(used), f == 0))
    def _():
        o_ref[...] = jnp.zeros(o_ref.shape, o_ref.dtype)


def _expert_ffn(xs, blk_exp, n_used, w_up, b_up, w_down, b_down):
    n_rows = xs.shape[0]
    rb, tf = FFN_ROWS, FFN_TF
    nblk = n_rows // rb
    nf = D_FF // tf
    last_f = nf - 1

    def clamp(i, nu):
        return jnp.minimum(i, nu[0] - 1)

    def fsel(i, f, nu):
        return jnp.where(i < nu[0], f, last_f)

    grid_spec = pltpu.PrefetchScalarGridSpec(
        num_scalar_prefetch=2,
        grid=(nblk, nf),
        in_specs=[
            pl.BlockSpec((rb, D_MODEL), lambda i, f, be, nu: (clamp(i, nu), 0)),
            pl.BlockSpec((1, D_MODEL, tf), lambda i, f, be, nu: (be[clamp(i, nu)], 0, fsel(i, f, nu))),
            pl.BlockSpec((1, D_MODEL, tf), lambda i, f, be, nu: (be[clamp(i, nu)], 0, nf + fsel(i, f, nu))),
            pl.BlockSpec((1, tf, D_MODEL), lambda i, f, be, nu: (be[clamp(i, nu)], fsel(i, f, nu), 0)),
            pl.BlockSpec((1, 1, tf), lambda i, f, be, nu: (be[clamp(i, nu)], 0, fsel(i, f, nu))),
            pl.BlockSpec((1, 1, tf), lambda i, f, be, nu: (be[clamp(i, nu)], 0, nf + fsel(i, f, nu))),
            pl.BlockSpec((1, 1, D_MODEL), lambda i, f, be, nu: (be[clamp(i, nu)], 0, 0)),
        ],
        out_specs=pl.BlockSpec((rb * PACK_ROWS, LANES), lambda i, f, be, nu: (i, 0)),
        scratch_shapes=[pltpu.VMEM((rb, D_MODEL), F32)],
    )
    assert nf >= 2, "the kernel keeps first / middle / last d_ff chunk steps apart"
    return pl.pallas_call(
        _ffn_kernel,
        grid_spec=grid_spec,
        out_shape=jax.ShapeDtypeStruct((n_rows * PACK_ROWS, LANES), jnp.int32),
        compiler_params=_cparams(("arbitrary", "arbitrary")),
        name="expert_ffn",
    )(blk_exp, n_used, xs, w_up, w_up, w_down, b_up.reshape(N_EXPERTS, 1, 2 * D_FF),
      b_up.reshape(N_EXPERTS, 1, 2 * D_FF), b_down.reshape(N_EXPERTS, 1, D_MODEL))


def _combine_kernel(dest_ref, gate_ref, h_ref, g_ref, b_ref, ys_ref, o_ref, buf_ref, sem):
    tm = h_ref.shape[0]

    def issue(t, carry):
        for k in range(TOP_K):
            src = pl.multiple_of(dest_ref[t * TOP_K + k] * PACK_ROWS, PACK_ROWS)
            dst = pl.multiple_of((k * tm + t) * PACK_ROWS, PACK_ROWS)
            pltpu.make_async_copy(ys_ref.at[pl.ds(src, PACK_ROWS)], buf_ref.at[pl.ds(dst, PACK_ROWS)], sem).start()
        return carry

    lax.fori_loop(0, tm, issue, 0)
    pltpu.make_async_copy(ys_ref.at[pl.ds(0, tm * TOP_K * PACK_ROWS)], buf_ref, sem).wait()
    gates = gate_ref[...]
    cols = [None] * (2 * PACK_ROWS)
    for k in range(TOP_K):
        gk = gates[:, k:k + 1]
        los, his = _load_packed_rows(buf_ref, k * tm, tm)
        for c, blk in enumerate(los + his):
            cols[c] = gk * blk if cols[c] is None else cols[c] + gk * blk
    z = DN_ALPHA * h_ref[...] + jnp.concatenate(cols, axis=-1)
    o_ref[...] = _layer_norm(z, g_ref[...], b_ref[...])


def _combine_ln(dest_flat, gates, h1, ys, g, b):
    n = h1.shape[0]
    tm = min(GATHER_TM, n)
    row = lambda w: pl.BlockSpec((tm, w), lambda i: (i, 0))
    full = lambda a, b_: pl.BlockSpec((a, b_), lambda i: (0, 0))
    return pl.pallas_call(
        _combine_kernel,
        grid=(n // tm,),
        in_specs=[pl.BlockSpec((tm * TOP_K,), lambda i: (i,), memory_space=pltpu.SMEM),
                  row(LANES), row(D_MODEL), full(1, D_MODEL), full(1, D_MODEL),
                  pl.BlockSpec(memory_space=pl.ANY)],
        out_specs=row(D_MODEL),
        out_shape=jax.ShapeDtypeStruct((n, D_MODEL), F32),
        scratch_shapes=[pltpu.VMEM((tm * TOP_K * PACK_ROWS, LANES), jnp.int32), pltpu.SemaphoreType.DMA],
        compiler_params=pltpu.CompilerParams(dimension_semantics=("arbitrary",), vmem_limit_bytes=VMEM_LIMIT,
                                             disable_bounds_checks=True),
        name="combine_ln",
    )(dest_flat, gates, h1, g.reshape(1, -1), b.reshape(1, -1), ys)


def _layer(h, w_in, b_igate, b_fgate, conv_w, conv_b, rel_bias, beta_attn, beta_mlstm, w_out,
           ln1_g, ln1_b, w_router, b_router, w_up, b_up, w_down, b_down, ln2_g, ln2_b, batch, seq):
    n = batch * seq
    w_qkv = w_in[:, :3 * D_ATTN].astype(BF16)
    w_mix = w_in[:, 3 * D_ATTN:MAIN_COLS].astype(BF16)
    w_gate = jnp.zeros((D_MODEL, LANES), BF16).at[:, :2 * N_HEADS_M].set(w_in[:, MAIN_COLS:].astype(BF16))
    qkvs = _qkv_project(h, w_qkv, min(QKV_TM, n))
    proj_m = _project(h, w_mix, BF16, min(PROJ_TM, n), PROJ_TN)
    gates = _project(h, w_gate, F32, min(PROJ_TM, n), LANES)
    gate_bias = jnp.zeros((1, LANES), F32).at[0, :2 * N_HEADS_M].set(jnp.concatenate([b_igate, b_fgate]))

    outs, lses = [], []
    for (_, dil), qkv in zip(DILATED_CONFIGS, qkvs):
        o, l = _dilated_attention(qkv, _attn_bias_tables(rel_bias, dil), batch, seq, dil)
        outs.append(o)
        lses.append(l)
    y_attn = _attn_combine(outs, lses, beta_attn)
    y_mlstm = _mlstm(proj_m, gates, gate_bias, conv_w, conv_b, beta_mlstm, batch, seq)

    h1, h1_packed = _outproj_ln(y_attn, y_mlstm, h, w_out.astype(BF16), ln1_g, ln1_b)

    top_idx, top_gate, rank, counts = _router(h1, w_router, b_router)
    counts = counts[0, :N_EXPERTS]
    padded = (counts + FFN_ROWS - 1) // FFN_ROWS * FFN_ROWS
    pad_end = jnp.cumsum(padded)
    pad_start = pad_end - padded
    dest = (pad_start[top_idx[:, :TOP_K]] + rank[:, :TOP_K]).reshape(-1).astype(jnp.int32)
    nblk = n * TOP_K // FFN_ROWS + N_EXPERTS
    blk_row0 = jnp.arange(nblk, dtype=jnp.int32) * FFN_ROWS
    blk_exp = jnp.minimum(jnp.sum(pad_end[None, :] <= blk_row0[:, None], axis=1), N_EXPERTS - 1).astype(jnp.int32)
    n_used = (pad_end[-1:] // FFN_ROWS).astype(jnp.int32)
    tok_of_assignment = jnp.arange(n * TOP_K, dtype=jnp.int32) // TOP_K
    slot_tok = jnp.zeros((nblk * FFN_ROWS,), jnp.int32).at[dest].set(tok_of_assignment, unique_indices=True)

    xs = _gather_rows(h1_packed, slot_tok, n_used)
    ys = _expert_ffn(xs, blk_exp, n_used, w_up.astype(BF16), b_up, w_down.astype(BF16), b_down)
    return _combine_ln(dest, top_gate, h1, ys, ln2_g, ln2_b)


def kernel(x, w_in, b_igate, b_fgate, conv_w, conv_b, rel_bias, beta_attn, beta_mlstm, w_out, ln1_g, ln1_b,
           w_router, b_router, w_up, b_up, w_down, b_down, ln2_g, ln2_b):
    batch, seq, d = x.shape
    h = x.reshape(batch * seq, d)
    for l in range(DEPTH):
        h = _layer(h, w_in[l], b_igate[l], b_fgate[l], conv_w[l], conv_b[l], rel_bias, beta_attn[l], beta_mlstm[l],
                   w_out[l], ln1_g[l], ln1_b[l], w_router[l], b_router[l], w_up[l], b_up[l], w_down[l], b_down[l],
                   ln2_g[l], ln2_b[l], batch, seq)
    return h.reshape(batch, seq, d)
```

```python
import functools
import math

import numpy as np
import jax
import jax.numpy as jnp
from jax import lax
from jax.experimental import pallas as pl
from jax.experimental.pallas import tpu as pltpu

F32 = jnp.float32
BF16 = jnp.bfloat16

D_MODEL = 2048
D_ATTN = 1024
HEAD_DIM_A = 64
N_HEADS_A = 16
DILATED_CONFIGS = ((128, 1), (512, 4), (2048, 16))
ATTN_BLOCK = 128
NUM_BUCKETS = 32
MAX_DISTANCE = 2048
D_MLSTM = 1024
N_HEADS_M = 4
HEAD_DIM_M = 256
CONV_K = 4
MLSTM_CHUNK = 128
MAIN_COLS = 3 * D_ATTN + 4 * D_MLSTM
N_EXPERTS = 32
TOP_K = 4
D_FF = 2048
SWIGLU_LIMIT = 7.0
SWIGLU_ALPHA = 1.702
DEPTH = 1
DN_ALPHA = (2 * DEPTH) ** 0.25
LN_EPS = 1e-5
HEAD_NORM_EPS = 1e-6
NEG_INF = -1e30

LANES = 128
VMEM_LIMIT = 48 * 1024 * 1024

PROJ_TM = 1024
PROJ_TN = 1024
COMBINE_TM = 512
OUTPROJ_TM = 512
ROUTER_TM = 512
QKV_TM = 512
ATTN_GROUP = 2
FFN_ROWS = 512
FFN_TF = 512
COMBINE_LN_TM = 256


def _cparams(sem, vmem=VMEM_LIMIT):
    return pltpu.CompilerParams(dimension_semantics=sem, vmem_limit_bytes=vmem)


def _dot(a, b):
    return jnp.dot(a, b, preferred_element_type=F32)


def _dot_f32_rhs(a_bf16, b_f32):
    hi = b_f32.astype(BF16)
    lo = (b_f32 - hi.astype(F32)).astype(BF16)
    return _dot(a_bf16, hi) + _dot(a_bf16, lo)


def _dot_f32_lhs(a_f32, b_bf16):
    hi = a_f32.astype(BF16)
    lo = (a_f32 - hi.astype(F32)).astype(BF16)
    return _dot(hi, b_bf16) + _dot(lo, b_bf16)


def _sigmoid(x):
    return 1.0 / (1.0 + jnp.exp(-x))


def _log_sigmoid(x):
    return jnp.minimum(x, 0.0) - jnp.log(1.0 + jnp.exp(-jnp.abs(x)))


def _proj_kernel(x_ref, w_ref, o_ref):
    o_ref[...] = _dot(x_ref[...].astype(BF16), w_ref[...]).astype(o_ref.dtype)


def _project(x, w, out_dtype, tm, tn):
    m, k = x.shape
    n = w.shape[1]
    return pl.pallas_call(
        _proj_kernel,
        grid=(m // tm, n // tn),
        in_specs=[pl.BlockSpec((tm, k), lambda i, j: (i, 0)),
                  pl.BlockSpec((k, tn), lambda i, j: (0, j))],
        out_specs=pl.BlockSpec((tm, tn), lambda i, j: (i, j)),
        out_shape=jax.ShapeDtypeStruct((m, n), out_dtype),
        compiler_params=_cparams(("parallel", "parallel")),
        name="in_proj",
    )(x, w)


def _qkv_proj_kernel(x_ref, w_ref, *refs):
    o_refs, r_ref = refs[:-1], refs[-1]
    res = _dot(x_ref[...].astype(BF16), w_ref[...])
    ntile, tm, _ = r_ref.shape
    wid = ntile * LANES
    for c in range(ntile):
        r_ref[c] = res[:, c * LANES:(c + 1) * LANES]
    for (_, dil), o_ref in zip(DILATED_CONFIGS, o_refs):
        if dil == 1:
            o_ref[...] = res.astype(o_ref.dtype)
        else:
            for r in range(dil):
                for c in range(ntile):
                    col = r * wid + c * LANES
                    o_ref[:, col:col + LANES] = r_ref[c, pl.ds(r, tm // dil, stride=dil), :].astype(o_ref.dtype)


def _qkv_project(x, w, tm):
    m, k = x.shape
    wid = w.shape[1]
    dils = [d for _, d in DILATED_CONFIGS]
    return pl.pallas_call(
        _qkv_proj_kernel,
        grid=(m // tm,),
        in_specs=[pl.BlockSpec((tm, k), lambda i: (i, 0)),
                  pl.BlockSpec((k, wid), lambda i: (0, 0), pipeline_mode=pl.Buffered(1))],
        out_specs=[pl.BlockSpec((tm // d, d * wid), lambda i: (i, 0)) for d in dils],
        out_shape=[jax.ShapeDtypeStruct((m // d, d * wid), BF16) for d in dils],
        scratch_shapes=[pltpu.VMEM((wid // LANES, tm, LANES), F32)],
        compiler_params=_cparams(("parallel",), 56 * 1024 * 1024),
        name="qkv_proj",
    )(x, w)


def _attn_kernel(q_ref, kp_ref, kc_ref, vp_ref, vc_ref, bias_ref, o_ref, lse_ref):
    n = pl.program_id(2)
    tab = jnp.minimum(n, 1)
    lse_ref[...] = jnp.zeros(lse_ref.shape, F32)
    grp, dh, nk = ATTN_GROUP, HEAD_DIM_A, 2 * ATTN_BLOCK
    wid = grp * dh
    lane_head = lax.broadcasted_iota(jnp.int32, (nk, wid), 1) // dh
    zero = jnp.zeros((nk, wid), BF16)
    ones_bd = jnp.concatenate([jnp.where(lane_head == j, 1.0, 0.0).astype(BF16) for j in range(grp)], axis=0)
    for g in range(N_HEADS_A // grp):
        cols = slice(g * wid, (g + 1) * wid)
        q = q_ref[0, :, cols] * (dh ** -0.5)
        kslab = jnp.concatenate([kp_ref[0, :, cols], kc_ref[0, :, cols]], axis=0)
        vslab = jnp.concatenate([vp_ref[0, :, cols], vc_ref[0, :, cols]], axis=0)
        k_bd = jnp.concatenate([jnp.where(lane_head == j, kslab, zero) for j in range(grp)], axis=0)
        v_bd = jnp.concatenate([jnp.where(lane_head == j, vslab, zero) for j in range(grp)], axis=0)
        s_all = lax.dot_general(q, k_bd, (((1,), (1,)), ((), ())), preferred_element_type=F32)
        ps, ms = [], []
        for j in range(grp):
            s = s_all[:, j * nk:(j + 1) * nk] + bias_ref[tab, g * grp + j]
            m = jnp.max(s, axis=-1, keepdims=True)
            ps.append(jnp.exp(s - m).astype(BF16))
            ms.append(m)
        p_all = jnp.concatenate(ps, axis=-1)
        res = _dot(p_all, jnp.concatenate([v_bd, ones_bd], axis=-1))
        den = res[:, wid:]
        o_ref[0, :, cols] = (res[:, :wid] / den).astype(o_ref.dtype)
        for j in range(grp):
            h = g * grp + j
            lse_ref[0, :, h:h + 1] = ms[j] + jnp.log(den[:, j * dh:j * dh + 1])


def _attn_bias_tables(rel_bias, dil):
    blk = ATTN_BLOCK
    period = 3 * blk
    k = np.arange(period)
    valid = k <= blk
    dist = np.where(valid, blk - k, 0) * dil
    max_exact = NUM_BUCKETS // 2
    d_f = np.maximum(dist, 1).astype(np.float32)
    large = max_exact + (np.log(d_f / np.float32(max_exact)) / np.float32(math.log(MAX_DISTANCE / max_exact))
                         * np.float32(NUM_BUCKETS - max_exact)).astype(np.int32)
    large = np.minimum(large, NUM_BUCKETS - 1)
    bucket = np.where(dist < max_exact, dist, large).astype(np.int32)
    w = jnp.where(jnp.asarray(valid)[None, :], rel_bias[jnp.asarray(bucket)].T.astype(F32), NEG_INF)
    t1 = jnp.tile(w, (1, blk))[:, :blk * (period - 1)].reshape(N_HEADS_A, blk, period - 1)[:, :, :2 * blk]
    has_prev = np.arange(2 * blk)[None, None, :] >= blk
    t0 = jnp.where(jnp.asarray(has_prev), t1, NEG_INF)
    return jnp.stack([t0, t1])


def _dilated_attention(qkv, bias_tab, batch, seq, dil):
    blk = ATTN_BLOCK
    l = seq // dil
    nb = l // blk
    ncb = 3
    pv = qkv.reshape(batch, l, dil * ncb * D_ATTN)

    def cur(c):
        return pl.BlockSpec((1, blk, D_ATTN), lambda b, r, n: (b, n, r * ncb + c))

    def prev(c):
        return pl.BlockSpec((1, blk, D_ATTN), lambda b, r, n: (b, jnp.maximum(n - 1, 0), r * ncb + c))

    o, lse = pl.pallas_call(
        _attn_kernel,
        grid=(batch, dil, nb),
        in_specs=[cur(0), prev(1), cur(1), prev(2), cur(2),
                  pl.BlockSpec((2, N_HEADS_A, blk, 2 * blk), lambda b, r, n: (0, 0, 0, 0))],
        out_specs=[pl.BlockSpec((1, blk, D_ATTN), lambda b, r, n: (b, n, r)),
                   pl.BlockSpec((1, blk, LANES), lambda b, r, n: (b, n, r))],
        out_shape=[jax.ShapeDtypeStruct((batch, l, dil * D_ATTN), BF16),
                   jax.ShapeDtypeStruct((batch, l, dil * LANES), F32)],
        compiler_params=_cparams(("parallel", "parallel", "arbitrary")),
        name=f"dilated_attn_d{dil}",
    )(pv, pv, pv, pv, pv, bias_tab)
    return o.reshape(batch * seq, D_ATTN), lse.reshape(batch * seq, LANES)


def _attn_combine_kernel(o1_ref, o2_ref, o3_ref, l1_ref, l2_ref, l3_ref, e_ref, et_ref, beta_ref, y_ref):
    lses = [l1_ref[...], l2_ref[...], l3_ref[...]]
    outs = [o1_ref, o2_ref, o3_ref]
    mx = jnp.maximum(jnp.maximum(lses[0], lses[1]), lses[2])
    ws = [jnp.exp(l - mx) for l in lses]
    tot = ws[0] + ws[1] + ws[2]
    e = e_ref[...]
    acc = None
    for w, o_ref in zip(ws, outs):
        term = _dot_f32_lhs(w / tot, e) * o_ref[...].astype(F32)
        acc = term if acc is None else acc + term
    ss = _dot_f32_lhs(acc * acc, et_ref[...])
    inv = lax.rsqrt(ss * (1.0 / HEAD_DIM_A) + HEAD_NORM_EPS)
    y_ref[...] = (acc * _dot_f32_lhs(inv, e) * beta_ref[...]).astype(y_ref.dtype)


def _attn_combine(os_, lses, beta_attn):
    n = os_[0].shape[0]
    tm = min(COMBINE_TM, n)
    head_of_lane = np.arange(D_ATTN) // HEAD_DIM_A
    e = (np.arange(LANES)[:, None] == head_of_lane[None, :]).astype(np.float32)
    e_j = jnp.asarray(e, BF16)
    et_j = jnp.asarray(e.T, BF16)
    row = lambda w: pl.BlockSpec((tm, w), lambda i: (i, 0))
    full = lambda a, b: pl.BlockSpec((a, b), lambda i: (0, 0))
    return pl.pallas_call(
        _attn_combine_kernel,
        grid=(n // tm,),
        in_specs=[row(D_ATTN)] * 3 + [row(LANES)] * 3 + [full(LANES, D_ATTN), full(D_ATTN, LANES), full(1, D_ATTN)],
        out_specs=row(D_ATTN),
        out_shape=jax.ShapeDtypeStruct((n, D_ATTN), BF16),
        compiler_params=_cparams(("parallel",)),
        name="attn_combine",
    )(*os_, *lses, e_j, et_j, beta_attn.reshape(1, D_ATTN).astype(F32))


def _mlstm_kernel(qp_ref, kp_ref, qprev_ref, kprev_ref, v_ref, og_ref, g_ref, gb_ref, cw_ref, cb_ref,
                  beta_ref, y_ref, c_ref, n_ref, m_ref):
    step = pl.program_id(1)
    ch = MLSTM_CHUNK
    dh = HEAD_DIM_M

    @pl.when(step == 0)
    def _():
        c_ref[...] = jnp.zeros(c_ref.shape, F32)
        n_ref[...] = jnp.zeros(n_ref.shape, F32)
        m_ref[...] = jnp.zeros(m_ref.shape, F32)

    def conv_silu(x_ref, prev_ref, coff):
        x = x_ref[0].astype(F32)
        p = jnp.where(step > 0, prev_ref[0].astype(F32), 0.0)
        xe = jnp.concatenate([p, x], axis=0)
        npad = p.shape[0]
        cols = slice(coff, coff + D_MLSTM)
        acc = cb_ref[:, cols] + cw_ref[CONV_K - 1:CONV_K, cols] * x
        for s in range(1, CONV_K):
            shifted = pltpu.roll(xe, s, 0)[npad:]
            acc = acc + cw_ref[CONV_K - 1 - s:CONV_K - s, cols] * shifted
        return acc * _sigmoid(acc)

    qf = conv_silu(qp_ref, qprev_ref, 0)
    kf = conv_silu(kp_ref, kprev_ref, D_MLSTM) * (dh ** -0.5)
    qb = qf.astype(BF16)
    kb = kf.astype(BF16)

    g = g_ref[0] + gb_ref[...]
    gt = g.T
    row_i = lax.broadcasted_iota(jnp.int32, (ch, ch), 0)
    col_i = lax.broadcasted_iota(jnp.int32, (ch, ch), 1)
    causal = row_i >= col_i
    tri = jnp.where(causal, 1.0, 0.0).astype(BF16)
    upp = jnp.where(row_i <= col_i, 1.0, 0.0).astype(BF16)
    b_cols = _dot_f32_rhs(tri, _log_sigmoid(g))
    b_rows = _dot_f32_lhs(_log_sigmoid(gt), upp)

    for h in range(N_HEADS_M):
        hs = slice(h * dh, (h + 1) * dh)
        fi = N_HEADS_M + h
        i_row = gt[h:h + 1, :]
        i_col = g[:, h:h + 1]
        b_row = b_rows[fi:fi + 1, :]
        b_col = b_cols[:, fi:fi + 1]
        m_prev = m_ref[h][:, 0:1]
        q_h, k_h = qb[:, hs], kb[:, hs]
        v_h = v_ref[0, :, hs]

        dmat = jnp.where(causal, b_col - b_row + i_row, NEG_INF)
        m_inter = b_col + m_prev
        m_t = jnp.maximum(m_inter, jnp.max(dmat, axis=-1, keepdims=True))
        w = jnp.exp(dmat - m_t) * lax.dot_general(q_h, k_h, (((1,), (1,)), ((), ())),
                                                  preferred_element_type=F32)
        decay = jnp.exp(m_inter - m_t)
        c_old = c_ref[h]
        inter = lax.dot_general(q_h, c_old.astype(BF16), (((1,), (1,)), ((), ())), preferred_element_type=F32)
        num = _dot(w.astype(BF16), v_h) + decay * inter
        n_old = n_ref[h]
        den = jnp.sum(w, axis=-1, keepdims=True) + decay * jnp.sum(qf[:, hs] * n_old, axis=-1, keepdims=True)
        hh = num / jnp.maximum(jnp.abs(den), jnp.exp(-m_t))

        g_last = b_col[ch - 1:ch, :]
        a_row = g_last - b_row + i_row
        a_col = g_last - b_col + i_col
        m_new = jnp.maximum(g_last + m_prev, jnp.max(a_row, axis=-1, keepdims=True))
        carry = jnp.exp(g_last + m_prev - m_new)
        wa_col = jnp.exp(a_col - m_new)
        wv = (wa_col * v_h.astype(F32)).astype(BF16)
        c_ref[h] = carry * c_old + lax.dot_general(wv, k_h, (((0,), (0,)), ((), ())), preferred_element_type=F32)
        n_ref[h] = carry * n_old + jnp.sum(wa_col * kf[:, hs], axis=0, keepdims=True)
        m_ref[h] = jnp.broadcast_to(m_new, (1, LANES))

        gated = _sigmoid(og_ref[0, :, hs].astype(F32)) * hh
        ms = jnp.sum(gated * gated, axis=-1, keepdims=True) * (1.0 / dh)
        y_ref[0, :, hs] = (gated * lax.rsqrt(ms + HEAD_NORM_EPS) * beta_ref[:, hs]).astype(y_ref.dtype)


def _mlstm(proj, gates, gate_bias, conv_w, conv_b, beta_mlstm, batch, seq):
    ch = MLSTM_CHUNK
    nchunk = seq // ch
    pv = proj.reshape(batch, seq, 4 * D_MLSTM)
    gv = gates.reshape(batch, seq, LANES)
    prev_rows = 16
    cb0 = 0

    def cur(c):
        return pl.BlockSpec((1, ch, D_MLSTM), lambda b, n: (b, n, c))

    def prev(c):
        per = ch // prev_rows
        return pl.BlockSpec((1, prev_rows, D_MLSTM), lambda b, n: (b, jnp.maximum(n * per - 1, 0), c))

    const = lambda a, b_: pl.BlockSpec((a, b_), lambda b, n: (0, 0))
    y = pl.pallas_call(
        _mlstm_kernel,
        grid=(batch, nchunk),
        in_specs=[cur(cb0), cur(cb0 + 1), prev(cb0), prev(cb0 + 1), cur(cb0 + 2), cur(cb0 + 3),
                  pl.BlockSpec((1, ch, LANES), lambda b, n: (b, n, 0)),
                  const(1, LANES), const(CONV_K, 2 * D_MLSTM), const(1, 2 * D_MLSTM), const(1, D_MLSTM)],
        out_specs=pl.BlockSpec((1, ch, D_MLSTM), lambda b, n: (b, n, 0)),
        out_shape=jax.ShapeDtypeStruct((batch, seq, D_MLSTM), BF16),
        scratch_shapes=[pltpu.VMEM((N_HEADS_M, HEAD_DIM_M, HEAD_DIM_M), F32),
                        pltpu.VMEM((N_HEADS_M, 1, HEAD_DIM_M), F32),
                        pltpu.VMEM((N_HEADS_M, 1, LANES), F32)],
        compiler_params=_cparams(("parallel", "arbitrary")),
        name="mlstm",
    )(pv, pv, pv, pv, pv, pv, gv, gate_bias, conv_w.astype(F32), conv_b.reshape(1, -1).astype(F32),
      beta_mlstm.reshape(1, D_MLSTM).astype(F32))
    return y.reshape(batch * seq, D_MLSTM)


def _layer_norm(z, g, b):
    mu = jnp.mean(z, axis=-1, keepdims=True)
    zc = z - mu
    var = jnp.mean(zc * zc, axis=-1, keepdims=True)
    return zc * lax.rsqrt(var + LN_EPS) * g + b


PACK_ROWS = D_MODEL // (2 * LANES)


def _store_packed_rows(dst_ref, x):
    rows = x.shape[0]
    half = D_MODEL // 2
    for s in range(PACK_ROWS):
        lo = x[:, s * LANES:(s + 1) * LANES].astype(BF16).astype(F32)
        hi = x[:, half + s * LANES:half + (s + 1) * LANES].astype(BF16).astype(F32)
        word = pltpu.bitcast(hi, jnp.int32) | lax.shift_right_logical(pltpu.bitcast(lo, jnp.int32), 16)
        dst_ref[pl.ds(s, rows, stride=PACK_ROWS), :] = word


def _load_packed_rows(src_ref, first, rows):
    los, his = [], []
    for s in range(PACK_ROWS):
        word = src_ref[pl.ds(first * PACK_ROWS + s, rows, stride=PACK_ROWS), :]
        los.append(pltpu.bitcast(lax.shift_left(word, 16), F32))
        his.append(pltpu.bitcast(word & jnp.int32(-65536), F32))
    return los, his


def _outproj_kernel(ya_ref, ym_ref, x_ref, w_ref, g_ref, b_ref, h_ref, hp_ref):
    y = _dot(ya_ref[...], w_ref[0:D_ATTN, :]) + _dot(ym_ref[...], w_ref[D_ATTN:D_MODEL, :])
    h = _layer_norm(DN_ALPHA * x_ref[...] + y, g_ref[...], b_ref[...])
    h_ref[...] = h
    _store_packed_rows(hp_ref, h)


def _outproj_ln(ya, ym, x, w_out, g, b):
    n = x.shape[0]
    tm = min(OUTPROJ_TM, n)
    row = lambda w: pl.BlockSpec((tm, w), lambda i: (i, 0))
    full = lambda a, b_: pl.BlockSpec((a, b_), lambda i: (0, 0))
    return pl.pallas_call(
        _outproj_kernel,
        grid=(n // tm,),
        in_specs=[row(D_ATTN), row(D_MLSTM), row(D_MODEL), full(D_MODEL, D_MODEL), full(1, D_MODEL), full(1, D_MODEL)],
        out_specs=[row(D_MODEL), pl.BlockSpec((tm * PACK_ROWS, LANES), lambda i: (i, 0))],
        out_shape=[jax.ShapeDtypeStruct((n, D_MODEL), F32),
                   jax.ShapeDtypeStruct((n * PACK_ROWS, LANES), jnp.int32)],
        compiler_params=_cparams(("parallel",)),
        name="out_proj_ln",
    )(ya, ym, x, w_out, g.reshape(1, -1), b.reshape(1, -1))


def _router_kernel(h_ref, whi_ref, wlo_ref, b_ref, tri_ref, idx_ref, gate_ref, rank_ref, cnt_ref, carry_ref):
    i = pl.program_id(0)

    @pl.when(i == 0)
    def _():
        carry_ref[...] = jnp.zeros(carry_ref.shape, F32)

    x = h_ref[...]
    xhi = x.astype(BF16)
    xlo = (x - xhi.astype(F32)).astype(BF16)
    logits = _dot(xhi, whi_ref[...]) + _dot(xhi, wlo_ref[...]) + _dot(xlo, whi_ref[...]) + b_ref[...]
    tm = logits.shape[0]
    lane = lax.broadcasted_iota(jnp.int32, (tm, LANES), 1)
    lane_f = lane.astype(F32)
    vals = jnp.where(lane < N_EXPERTS, logits, NEG_INF)

    sels, tops = [], []
    for _ in range(TOP_K):
        mx = jnp.max(vals, axis=-1, keepdims=True)
        first = jnp.min(jnp.where(vals == mx, lane_f, float(LANES)), axis=-1, keepdims=True)
        sel = lane_f == first
        sels.append(sel)
        tops.append((mx, first))
        vals = jnp.where(sel, 2.0 * NEG_INF, vals)

    exps = [jnp.exp(mx - tops[0][0]) for mx, _ in tops]
    tot = exps[0] + exps[1] + exps[2] + exps[3]

    onehot = jnp.zeros((tm, LANES), F32)
    for sel in sels:
        onehot = jnp.where(sel, 1.0, onehot)
    before = _dot(tri_ref[...], onehot.astype(BF16)) + carry_ref[...]

    idx_out = jnp.zeros((tm, LANES), F32)
    gate_out = jnp.zeros((tm, LANES), F32)
    rank_out = jnp.zeros((tm, LANES), F32)
    for k in range(TOP_K):
        rank_k = jnp.sum(jnp.where(sels[k], before, 0.0), axis=-1, keepdims=True)
        idx_out = jnp.where(lane == k, tops[k][1], idx_out)
        gate_out = jnp.where(lane == k, exps[k] / tot, gate_out)
        rank_out = jnp.where(lane == k, rank_k, rank_out)
    idx_ref[...] = idx_out.astype(jnp.int32)
    gate_ref[...] = gate_out
    rank_ref[...] = rank_out.astype(jnp.int32)

    carry = carry_ref[...] + jnp.sum(onehot, axis=0, keepdims=True)
    carry_ref[...] = carry
    cnt_ref[...] = carry.astype(jnp.int32)


def _router(h1, w_router, b_router):
    n = h1.shape[0]
    tm = min(ROUTER_TM, n)
    wpad = jnp.zeros((D_MODEL, LANES), F32).at[:, :N_EXPERTS].set(w_router)
    whi = wpad.astype(BF16)
    wlo = (wpad - whi.astype(F32)).astype(BF16)
    bpad = jnp.zeros((1, LANES), F32).at[0, :N_EXPERTS].set(b_router)
    tri = jnp.asarray(np.tril(np.ones((tm, tm), np.float32), -1), BF16)
    row = lambda w: pl.BlockSpec((tm, w), lambda i: (i, 0))
    full = lambda a, b_: pl.BlockSpec((a, b_), lambda i: (0, 0))
    return pl.pallas_call(
        _router_kernel,
        grid=(n // tm,),
        in_specs=[row(D_MODEL), full(D_MODEL, LANES), full(D_MODEL, LANES), full(1, LANES), full(tm, tm)],
        out_specs=[row(LANES), row(LANES), row(LANES), full(1, LANES)],
        out_shape=[jax.ShapeDtypeStruct((n, LANES), jnp.int32), jax.ShapeDtypeStruct((n, LANES), F32),
                   jax.ShapeDtypeStruct((n, LANES), jnp.int32), jax.ShapeDtypeStruct((1, LANES), jnp.int32)],
        scratch_shapes=[pltpu.VMEM((1, LANES), F32)],
        compiler_params=_cparams(("arbitrary",)),
        name="router",
    )(h1, whi, wlo, bpad, tri)


def _ffn_kernel(bexp_ref, nused_ref, nvalid_ref, slot_ref, src_cur_ref, src_nxt_ref, wg_ref, wu_ref, wd_ref, bg_ref,
                bu_ref, bd_ref, xp_ref, ys_ref, xbuf_ref, x_ref, acc_ref, obuf_ref, in_sem, out_sem):
    del bexp_ref
    i = pl.program_id(0)
    f = pl.program_id(1)
    last_f = pl.num_programs(1) - 1
    last_i = pl.num_programs(0) - 1
    n_used = nused_ref[0]
    used = i < n_used
    rb = x_ref.shape[0]
    unroll = 4

    def start_gather(src_ref, slot):
        def body(j, carry):
            for u in range(unroll):
                jj = j * unroll + u
                src = pl.multiple_of(src_ref[jj], PACK_ROWS)
                dst = pl.multiple_of(jj * PACK_ROWS, PACK_ROWS)
                pltpu.make_async_copy(xp_ref.at[pl.ds(src, PACK_ROWS)], xbuf_ref.at[slot, pl.ds(dst, PACK_ROWS)],
                                      in_sem.at[slot]).start(priority=u % 2)
            return carry

        lax.fori_loop(0, rb // unroll, body, 0)

    def start_scatter():
        def body(j, carry):
            for u in range(unroll):
                jj = j * unroll + u
                a = slot_ref[jj]

                @pl.when(a >= 0)
                def _():
                    src = pl.multiple_of(jj * PACK_ROWS, PACK_ROWS)
                    dst = pl.multiple_of(a * PACK_ROWS, PACK_ROWS)
                    pltpu.make_async_copy(obuf_ref.at[pl.ds(src, PACK_ROWS)], ys_ref.at[pl.ds(dst, PACK_ROWS)],
                                          out_sem).start(priority=u % 2)
            return carry

        lax.fori_loop(0, rb // unroll, body, 0)

    def wait_scatter(blk):
        nrow = pl.multiple_of(nvalid_ref[blk] * PACK_ROWS, PACK_ROWS)
        pltpu.make_async_copy(obuf_ref.at[pl.ds(0, nrow)], ys_ref.at[pl.ds(0, nrow)], out_sem).wait()

    def for_parity(value, fn):
        for par in range(2):
            @pl.when(lax.rem(value, 2) == par)
            def _():
                fn(par)

    @pl.when(jnp.logical_and(used, f == 0))
    def _():
        @pl.when(i == 0)
        def _():
            start_gather(src_cur_ref, 0)

        def take(slot):
            pltpu.make_async_copy(xp_ref.at[pl.ds(0, rb * PACK_ROWS)], xbuf_ref.at[slot], in_sem.at[slot]).wait()
            los, his = _load_packed_rows(xbuf_ref.at[slot], 0, rb)
            for s in range(PACK_ROWS):
                x_ref[:, s * LANES:(s + 1) * LANES] = los[s].astype(BF16)
                x_ref[:, D_MODEL // 2 + s * LANES:D_MODEL // 2 + (s + 1) * LANES] = his[s].astype(BF16)

        for_parity(i, take)

    @pl.when(jnp.logical_and(f == 1, i + 1 < n_used))
    def _():
        for_parity(i + 1, lambda slot: start_gather(src_nxt_ref, slot))

    @pl.when(used)
    def _():
        x = x_ref[...]
        hg = _dot(x, wg_ref[0]) + bg_ref[0]
        hu = _dot(x, wu_ref[0]) + bu_ref[0]
        gate = jnp.minimum(hg, SWIGLU_LIMIT)
        up = jnp.clip(hu, -SWIGLU_LIMIT, SWIGLU_LIMIT)
        act = (up + 1.0) * (gate * _sigmoid(SWIGLU_ALPHA * gate))
        y = _dot(act.astype(BF16), wd_ref[0])

        @pl.when(f == 0)
        def _():
            acc_ref[...] = y + bd_ref[0]

        @pl.when(jnp.logical_and(f > 0, f < last_f))
        def _():
            acc_ref[...] += y

        @pl.when(f == last_f)
        def _():
            @pl.when(i > 0)
            def _():
                wait_scatter(i - 1)

            _store_packed_rows(obuf_ref, acc_ref[...] + y)
            start_scatter()

    @pl.when(jnp.logical_and(i == last_i, f == last_f))
    def _():
        wait_scatter(jnp.minimum(i, n_used - 1))


def _expert_ffn(h_packed, slot_assign, blk_exp, blk_valid, n_used, w_up, b_up, w_down, b_down):
    rb, tf = FFN_ROWS, FFN_TF
    nblk = slot_assign.shape[0] // rb
    nf = D_FF // tf
    last_f = nf - 1
    n_assign = h_packed.shape[0] // PACK_ROWS * TOP_K
    slot_src = jnp.maximum(slot_assign, 0) // TOP_K * PACK_ROWS

    def clamp(i, nu):
        return jnp.minimum(i, nu[0] - 1)

    def fsel(i, f, nu):
        return jnp.where(i < nu[0], f, last_f)

    grid_spec = pltpu.PrefetchScalarGridSpec(
        num_scalar_prefetch=3,
        grid=(nblk, nf),
        in_specs=[
            pl.BlockSpec((rb,), lambda i, f, be, nu, *_: (clamp(i, nu),), memory_space=pltpu.SMEM),
            pl.BlockSpec((rb,), lambda i, f, be, nu, *_: (clamp(i, nu),), memory_space=pltpu.SMEM),
            pl.BlockSpec((rb,), lambda i, f, be, nu, *_: (clamp(i + 1, nu),), memory_space=pltpu.SMEM),
            pl.BlockSpec((1, D_MODEL, tf), lambda i, f, be, nu, *_: (be[clamp(i, nu)], 0, fsel(i, f, nu))),
            pl.BlockSpec((1, D_MODEL, tf), lambda i, f, be, nu, *_: (be[clamp(i, nu)], 0, nf + fsel(i, f, nu))),
            pl.BlockSpec((1, tf, D_MODEL), lambda i, f, be, nu, *_: (be[clamp(i, nu)], fsel(i, f, nu), 0)),
            pl.BlockSpec((1, 1, tf), lambda i, f, be, nu, *_: (be[clamp(i, nu)], 0, fsel(i, f, nu))),
            pl.BlockSpec((1, 1, tf), lambda i, f, be, nu, *_: (be[clamp(i, nu)], 0, nf + fsel(i, f, nu))),
            pl.BlockSpec((1, 1, D_MODEL), lambda i, f, be, nu, *_: (be[clamp(i, nu)], 0, 0)),
            pl.BlockSpec(memory_space=pl.ANY),
        ],
        out_specs=pl.BlockSpec(memory_space=pl.ANY),
        scratch_shapes=[pltpu.VMEM((2, rb * PACK_ROWS, LANES), jnp.int32),
                        pltpu.VMEM((rb, D_MODEL), BF16),
                        pltpu.VMEM((rb, D_MODEL), F32),
                        pltpu.VMEM((rb * PACK_ROWS, LANES), jnp.int32),
                        pltpu.SemaphoreType.DMA((2,)), pltpu.SemaphoreType.DMA],
    )
    assert nf >= 2, "the kernel keeps first / middle / last d_ff chunk steps apart"
    return pl.pallas_call(
        _ffn_kernel,
        grid_spec=grid_spec,
        out_shape=jax.ShapeDtypeStruct((n_assign * PACK_ROWS, LANES), jnp.int32),
        compiler_params=pltpu.CompilerParams(dimension_semantics=("arbitrary", "arbitrary"),
                                             vmem_limit_bytes=VMEM_LIMIT, disable_bounds_checks=True),
        name="expert_ffn",
    )(blk_exp, n_used, blk_valid, slot_assign, slot_src, slot_src, w_up, w_up, w_down,
      b_up.reshape(N_EXPERTS, 1, 2 * D_FF),
      b_up.reshape(N_EXPERTS, 1, 2 * D_FF), b_down.reshape(N_EXPERTS, 1, D_MODEL), h_packed)


def _combine_kernel(gate_ref, h_ref, g_ref, b_ref, ys_ref, o_ref):
    tm = h_ref.shape[0]
    gates = gate_ref[...]
    cols = [None] * (2 * PACK_ROWS)
    for k in range(TOP_K):
        gk = gates[:, k:k + 1]
        for s in range(PACK_ROWS):
            word = ys_ref[pl.ds(k * PACK_ROWS + s, tm, stride=TOP_K * PACK_ROWS), :]
            lo = pltpu.bitcast(lax.shift_left(word, 16), F32)
            hi = pltpu.bitcast(word & jnp.int32(-65536), F32)
            for c, blk in ((s, lo), (PACK_ROWS + s, hi)):
                cols[c] = gk * blk if cols[c] is None else cols[c] + gk * blk
    z = DN_ALPHA * h_ref[...] + jnp.concatenate(cols, axis=-1)
    o_ref[...] = _layer_norm(z, g_ref[...], b_ref[...])


def _combine_ln(gates, h1, ys, g, b):
    n = h1.shape[0]
    tm = min(COMBINE_LN_TM, n)
    row = lambda w: pl.BlockSpec((tm, w), lambda i: (i, 0))
    full = lambda a, b_: pl.BlockSpec((a, b_), lambda i: (0, 0))
    return pl.pallas_call(
        _combine_kernel,
        grid=(n // tm,),
        in_specs=[row(LANES), row(D_MODEL), full(1, D_MODEL), full(1, D_MODEL),
                  pl.BlockSpec((tm * TOP_K * PACK_ROWS, LANES), lambda i: (i, 0))],
        out_specs=row(D_MODEL),
        out_shape=jax.ShapeDtypeStruct((n, D_MODEL), F32),
        compiler_params=_cparams(("parallel",)),
        name="combine_ln",
    )(gates, h1, g.reshape(1, -1), b.reshape(1, -1), ys)


def _layer(h, w_in, b_igate, b_fgate, conv_w, conv_b, rel_bias, beta_attn, beta_mlstm, w_out,
           ln1_g, ln1_b, w_router, b_router, w_up, b_up, w_down, b_down, ln2_g, ln2_b, batch, seq):
    n = batch * seq
    w_qkv = w_in[:, :3 * D_ATTN].astype(BF16)
    w_mix = w_in[:, 3 * D_ATTN:MAIN_COLS].astype(BF16)
    w_gate = jnp.zeros((D_MODEL, LANES), BF16).at[:, :2 * N_HEADS_M].set(w_in[:, MAIN_COLS:].astype(BF16))
    qkvs = _qkv_project(h, w_qkv, min(QKV_TM, n))
    proj_m = _project(h, w_mix, BF16, min(PROJ_TM, n), PROJ_TN)
    gates = _project(h, w_gate, F32, min(PROJ_TM, n), LANES)
    gate_bias = jnp.zeros((1, LANES), F32).at[0, :2 * N_HEADS_M].set(jnp.concatenate([b_igate, b_fgate]))

    outs, lses = [], []
    for (_, dil), qkv in zip(DILATED_CONFIGS, qkvs):
        o, l = _dilated_attention(qkv, _attn_bias_tables(rel_bias, dil), batch, seq, dil)
        outs.append(o)
        lses.append(l)
    y_attn = _attn_combine(outs, lses, beta_attn)
    y_mlstm = _mlstm(proj_m, gates, gate_bias, conv_w, conv_b, beta_mlstm, batch, seq)

    h1, h1_packed = _outproj_ln(y_attn, y_mlstm, h, w_out.astype(BF16), ln1_g, ln1_b)

    top_idx, top_gate, rank, counts = _router(h1, w_router, b_router)
    counts = counts[0, :N_EXPERTS]
    padded = (counts + FFN_ROWS - 1) // FFN_ROWS * FFN_ROWS
    pad_end = jnp.cumsum(padded)
    pad_start = pad_end - padded
    dest = (pad_start[top_idx[:, :TOP_K]] + rank[:, :TOP_K]).reshape(-1).astype(jnp.int32)
    nblk = n * TOP_K // FFN_ROWS + N_EXPERTS
    blk_row0 = jnp.arange(nblk, dtype=jnp.int32) * FFN_ROWS
    blk_exp = jnp.minimum(jnp.sum(pad_end[None, :] <= blk_row0[:, None], axis=1), N_EXPERTS - 1).astype(jnp.int32)
    n_used = (pad_end[-1:] // FFN_ROWS).astype(jnp.int32)
    blk_valid = jnp.clip((pad_start + counts)[blk_exp] - blk_row0, 0, FFN_ROWS).astype(jnp.int32)
    slot_assign = jnp.full((nblk * FFN_ROWS,), -1, jnp.int32).at[dest].set(
        jnp.arange(n * TOP_K, dtype=jnp.int32), unique_indices=True)

    ys = _expert_ffn(h1_packed, slot_assign, blk_exp, blk_valid, n_used, w_up.astype(BF16), b_up,
                     w_down.astype(BF16), b_down)
    return _combine_ln(top_gate, h1, ys, ln2_g, ln2_b)


def kernel(x, w_in, b_igate, b_fgate, conv_w, conv_b, rel_bias, beta_attn, beta_mlstm, w_out, ln1_g, ln1_b,
           w_router, b_router, w_up, b_up, w_down, b_down, ln2_g, ln2_b):
    batch, seq, d = x.shape
    h = x.reshape(batch * seq, d)
    for l in range(DEPTH):
        h = _layer(h, w_in[l], b_igate[l], b_fgate[l], conv_w[l], conv_b[l], rel_bias, beta_attn[l], beta_mlstm[l],
                   w_out[l], ln1_g[l], ln1_b[l], w_router[l], b_router[l], w_up[l], b_up[l], w_down[l], b_down[l],
                   ln2_g[l], ln2_b[l], batch, seq)
    return h.reshape(batch, seq, d)
```

```python
import functools
import math

import numpy as np
import jax
import jax.numpy as jnp
from jax import lax
from jax.experimental import pallas as pl
from jax.experimental.pallas import tpu as pltpu

F32 = jnp.float32
BF16 = jnp.bfloat16

D_MODEL = 2048
D_ATTN = 1024
HEAD_DIM_A = 64
N_HEADS_A = 16
DILATED_CONFIGS = ((128, 1), (512, 4), (2048, 16))
ATTN_BLOCK = 128
NUM_BUCKETS = 32
MAX_DISTANCE = 2048
D_MLSTM = 1024
N_HEADS_M = 4
HEAD_DIM_M = 256
CONV_K = 4
MLSTM_CHUNK = 128
MAIN_COLS = 3 * D_ATTN + 4 * D_MLSTM
N_EXPERTS = 32
TOP_K = 4
D_FF = 2048
SWIGLU_LIMIT = 7.0
SWIGLU_ALPHA = 1.702
DEPTH = 1
DN_ALPHA = (2 * DEPTH) ** 0.25
LN_EPS = 1e-5
HEAD_NORM_EPS = 1e-6
NEG_INF = -1e30

LANES = 128
VMEM_LIMIT = 48 * 1024 * 1024

PROJ_TM = 1024
PROJ_TN = 1024
COMBINE_TM = 512
OUTPROJ_TM = 512
ROUTER_TM = 512
QKV_TM = 512
ATTN_GROUP = 2
FFN_ROWS = 1024
FFN_SUB = 256
FFN_TF = 512
FFN_TH = 256
FFN_VMEM_LIMIT = 56 * 1024 * 1024
COMBINE_LN_TM = 256


def _cparams(sem, vmem=VMEM_LIMIT):
    return pltpu.CompilerParams(dimension_semantics=sem, vmem_limit_bytes=vmem)


def _dot(a, b):
    return jnp.dot(a, b, preferred_element_type=F32)


def _dot_f32_rhs(a_bf16, b_f32):
    hi = b_f32.astype(BF16)
    lo = (b_f32 - hi.astype(F32)).astype(BF16)
    return _dot(a_bf16, hi) + _dot(a_bf16, lo)


def _dot_f32_lhs(a_f32, b_bf16):
    hi = a_f32.astype(BF16)
    lo = (a_f32 - hi.astype(F32)).astype(BF16)
    return _dot(hi, b_bf16) + _dot(lo, b_bf16)


def _sigmoid(x):
    return 1.0 / (1.0 + jnp.exp(-x))


def _log_sigmoid(x):
    return jnp.minimum(x, 0.0) - jnp.log(1.0 + jnp.exp(-jnp.abs(x)))


def _proj_kernel(x_ref, w_ref, o_ref):
    o_ref[...] = _dot(x_ref[...].astype(BF16), w_ref[...]).astype(o_ref.dtype)


def _project(x, w, out_dtype, tm, tn):
    m, k = x.shape
    n = w.shape[1]
    return pl.pallas_call(
        _proj_kernel,
        grid=(m // tm, n // tn),
        in_specs=[pl.BlockSpec((tm, k), lambda i, j: (i, 0)),
                  pl.BlockSpec((k, tn), lambda i, j: (0, j))],
        out_specs=pl.BlockSpec((tm, tn), lambda i, j: (i, j)),
        out_shape=jax.ShapeDtypeStruct((m, n), out_dtype),
        compiler_params=_cparams(("parallel", "parallel")),
        name="in_proj",
    )(x, w)


def _qkv_proj_kernel(x_ref, w_ref, *refs):
    o_refs, r_ref = refs[:-1], refs[-1]
    res = _dot(x_ref[...].astype(BF16), w_ref[...])
    ntile, tm, _ = r_ref.shape
    wid = ntile * LANES
    for c in range(ntile):
        r_ref[c] = res[:, c * LANES:(c + 1) * LANES]
    for (_, dil), o_ref in zip(DILATED_CONFIGS, o_refs):
        if dil == 1:
            o_ref[...] = res.astype(o_ref.dtype)
        else:
            for r in range(dil):
                for c in range(ntile):
                    col = r * wid + c * LANES
                    o_ref[:, col:col + LANES] = r_ref[c, pl.ds(r, tm // dil, stride=dil), :].astype(o_ref.dtype)


def _qkv_project(x, w, tm):
    m, k = x.shape
    wid = w.shape[1]
    dils = [d for _, d in DILATED_CONFIGS]
    return pl.pallas_call(
        _qkv_proj_kernel,
        grid=(m // tm,),
        in_specs=[pl.BlockSpec((tm, k), lambda i: (i, 0)),
                  pl.BlockSpec((k, wid), lambda i: (0, 0), pipeline_mode=pl.Buffered(1))],
        out_specs=[pl.BlockSpec((tm // d, d * wid), lambda i: (i, 0)) for d in dils],
        out_shape=[jax.ShapeDtypeStruct((m // d, d * wid), BF16) for d in dils],
        scratch_shapes=[pltpu.VMEM((wid // LANES, tm, LANES), F32)],
        compiler_params=_cparams(("parallel",), 56 * 1024 * 1024),
        name="qkv_proj",
    )(x, w)


def _attn_kernel(q_ref, kp_ref, kc_ref, vp_ref, vc_ref, bias_ref, o_ref, lse_ref):
    n = pl.program_id(2)
    tab = jnp.minimum(n, 1)
    lse_ref[...] = jnp.zeros(lse_ref.shape, F32)
    grp, dh, nk = ATTN_GROUP, HEAD_DIM_A, 2 * ATTN_BLOCK
    wid = grp * dh
    lane_head = lax.broadcasted_iota(jnp.int32, (nk, wid), 1) // dh
    zero = jnp.zeros((nk, wid), BF16)
    ones_bd = jnp.concatenate([jnp.where(lane_head == j, 1.0, 0.0).astype(BF16) for j in range(grp)], axis=0)
    for g in range(N_HEADS_A // grp):
        cols = slice(g * wid, (g + 1) * wid)
        q = q_ref[0, :, cols] * (dh ** -0.5)
        kslab = jnp.concatenate([kp_ref[0, :, cols], kc_ref[0, :, cols]], axis=0)
        vslab = jnp.concatenate([vp_ref[0, :, cols], vc_ref[0, :, cols]], axis=0)
        k_bd = jnp.concatenate([jnp.where(lane_head == j, kslab, zero) for j in range(grp)], axis=0)
        v_bd = jnp.concatenate([jnp.where(lane_head == j, vslab, zero) for j in range(grp)], axis=0)
        s_all = lax.dot_general(q, k_bd, (((1,), (1,)), ((), ())), preferred_element_type=F32)
        ps, ms = [], []
        for j in range(grp):
            s = s_all[:, j * nk:(j + 1) * nk] + bias_ref[tab, g * grp + j]
            m = jnp.max(s, axis=-1, keepdims=True)
            ps.append(jnp.exp(s - m).astype(BF16))
            ms.append(m)
        p_all = jnp.concatenate(ps, axis=-1)
        res = _dot(p_all, jnp.concatenate([v_bd, ones_bd], axis=-1))
        den = res[:, wid:]
        o_ref[0, :, cols] = (res[:, :wid] / den).astype(o_ref.dtype)
        for j in range(grp):
            h = g * grp + j
            lse_ref[0, :, h:h + 1] = ms[j] + jnp.log(den[:, j * dh:j * dh + 1])


def _attn_bias_tables(rel_bias, dil):
    blk = ATTN_BLOCK
    period = 3 * blk
    k = np.arange(period)
    valid = k <= blk
    dist = np.where(valid, blk - k, 0) * dil
    max_exact = NUM_BUCKETS // 2
    d_f = np.maximum(dist, 1).astype(np.float32)
    large = max_exact + (np.log(d_f / np.float32(max_exact)) / np.float32(math.log(MAX_DISTANCE / max_exact))
                         * np.float32(NUM_BUCKETS - max_exact)).astype(np.int32)
    large = np.minimum(large, NUM_BUCKETS - 1)
    bucket = np.where(dist < max_exact, dist, large).astype(np.int32)
    w = jnp.where(jnp.asarray(valid)[None, :], rel_bias[jnp.asarray(bucket)].T.astype(F32), NEG_INF)
    t1 = jnp.tile(w, (1, blk))[:, :blk * (period - 1)].reshape(N_HEADS_A, blk, period - 1)[:, :, :2 * blk]
    has_prev = np.arange(2 * blk)[None, None, :] >= blk
    t0 = jnp.where(jnp.asarray(has_prev), t1, NEG_INF)
    return jnp.stack([t0, t1])


def _dilated_attention(qkv, bias_tab, batch, seq, dil):
    blk = ATTN_BLOCK
    l = seq // dil
    nb = l // blk
    ncb = 3
    pv = qkv.reshape(batch, l, dil * ncb * D_ATTN)

    def cur(c):
        return pl.BlockSpec((1, blk, D_ATTN), lambda b, r, n: (b, n, r * ncb + c))

    def prev(c):
        return pl.BlockSpec((1, blk, D_ATTN), lambda b, r, n: (b, jnp.maximum(n - 1, 0), r * ncb + c))

    o, lse = pl.pallas_call(
        _attn_kernel,
        grid=(batch, dil, nb),
        in_specs=[cur(0), prev(1), cur(1), prev(2), cur(2),
                  pl.BlockSpec((2, N_HEADS_A, blk, 2 * blk), lambda b, r, n: (0, 0, 0, 0))],
        out_specs=[pl.BlockSpec((1, blk, D_ATTN), lambda b, r, n: (b, n, r)),
                   pl.BlockSpec((1, blk, LANES), lambda b, r, n: (b, n, r))],
        out_shape=[jax.ShapeDtypeStruct((batch, l, dil * D_ATTN), BF16),
                   jax.ShapeDtypeStruct((batch, l, dil * LANES), F32)],
        compiler_params=_cparams(("parallel", "parallel", "arbitrary")),
        name=f"dilated_attn_d{dil}",
    )(pv, pv, pv, pv, pv, bias_tab)
    return o.reshape(batch * seq, D_ATTN), lse.reshape(batch * seq, LANES)


def _attn_combine_kernel(o1_ref, o2_ref, o3_ref, l1_ref, l2_ref, l3_ref, e_ref, et_ref, beta_ref, y_ref):
    lses = [l1_ref[...], l2_ref[...], l3_ref[...]]
    outs = [o1_ref, o2_ref, o3_ref]
    mx = jnp.maximum(jnp.maximum(lses[0], lses[1]), lses[2])
    ws = [jnp.exp(l - mx) for l in lses]
    tot = ws[0] + ws[1] + ws[2]
    e = e_ref[...]
    acc = None
    for w, o_ref in zip(ws, outs):
        term = _dot_f32_lhs(w / tot, e) * o_ref[...].astype(F32)
        acc = term if acc is None else acc + term
    ss = _dot_f32_lhs(acc * acc, et_ref[...])
    inv = lax.rsqrt(ss * (1.0 / HEAD_DIM_A) + HEAD_NORM_EPS)
    y_ref[...] = (acc * _dot_f32_lhs(inv, e) * beta_ref[...]).astype(y_ref.dtype)


def _attn_combine(os_, lses, beta_attn):
    n = os_[0].shape[0]
    tm = min(COMBINE_TM, n)
    head_of_lane = np.arange(D_ATTN) // HEAD_DIM_A
    e = (np.arange(LANES)[:, None] == head_of_lane[None, :]).astype(np.float32)
    e_j = jnp.asarray(e, BF16)
    et_j = jnp.asarray(e.T, BF16)
    row = lambda w: pl.BlockSpec((tm, w), lambda i: (i, 0))
    full = lambda a, b: pl.BlockSpec((a, b), lambda i: (0, 0))
    return pl.pallas_call(
        _attn_combine_kernel,
        grid=(n // tm,),
        in_specs=[row(D_ATTN)] * 3 + [row(LANES)] * 3 + [full(LANES, D_ATTN), full(D_ATTN, LANES), full(1, D_ATTN)],
        out_specs=row(D_ATTN),
        out_shape=jax.ShapeDtypeStruct((n, D_ATTN), BF16),
        compiler_params=_cparams(("parallel",)),
        name="attn_combine",
    )(*os_, *lses, e_j, et_j, beta_attn.reshape(1, D_ATTN).astype(F32))


def _mlstm_kernel(qp_ref, kp_ref, qprev_ref, kprev_ref, v_ref, og_ref, g_ref, gb_ref, cw_ref, cb_ref,
                  beta_ref, y_ref, c_ref, n_ref, m_ref):
    step = pl.program_id(1)
    ch = MLSTM_CHUNK
    dh = HEAD_DIM_M

    @pl.when(step == 0)
    def _():
        c_ref[...] = jnp.zeros(c_ref.shape, F32)
        n_ref[...] = jnp.zeros(n_ref.shape, F32)
        m_ref[...] = jnp.zeros(m_ref.shape, F32)

    def conv_silu(x_ref, prev_ref, coff):
        x = x_ref[0].astype(F32)
        p = jnp.where(step > 0, prev_ref[0].astype(F32), 0.0)
        xe = jnp.concatenate([p, x], axis=0)
        npad = p.shape[0]
        cols = slice(coff, coff + D_MLSTM)
        acc = cb_ref[:, cols] + cw_ref[CONV_K - 1:CONV_K, cols] * x
        for s in range(1, CONV_K):
            shifted = pltpu.roll(xe, s, 0)[npad:]
            acc = acc + cw_ref[CONV_K - 1 - s:CONV_K - s, cols] * shifted
        return acc * _sigmoid(acc)

    qf = conv_silu(qp_ref, qprev_ref, 0)
    kf = conv_silu(kp_ref, kprev_ref, D_MLSTM) * (dh ** -0.5)
    qb = qf.astype(BF16)
    kb = kf.astype(BF16)

    g = g_ref[0] + gb_ref[...]
    gt = g.T
    row_i = lax.broadcasted_iota(jnp.int32, (ch, ch), 0)
    col_i = lax.broadcasted_iota(jnp.int32, (ch, ch), 1)
    causal = row_i >= col_i
    tri = jnp.where(causal, 1.0, 0.0).astype(BF16)
    upp = jnp.where(row_i <= col_i, 1.0, 0.0).astype(BF16)
    b_cols = _dot_f32_rhs(tri, _log_sigmoid(g))
    b_rows = _dot_f32_lhs(_log_sigmoid(gt), upp)

    for h in range(N_HEADS_M):
        hs = slice(h * dh, (h + 1) * dh)
        fi = N_HEADS_M + h
        i_row = gt[h:h + 1, :]
        i_col = g[:, h:h + 1]
        b_row = b_rows[fi:fi + 1, :]
        b_col = b_cols[:, fi:fi + 1]
        m_prev = m_ref[h][:, 0:1]
        q_h, k_h = qb[:, hs], kb[:, hs]
        v_h = v_ref[0, :, hs]

        dmat = jnp.where(causal, b_col - b_row + i_row, NEG_INF)
        m_inter = b_col + m_prev
        m_t = jnp.maximum(m_inter, jnp.max(dmat, axis=-1, keepdims=True))
        w = jnp.exp(dmat - m_t) * lax.dot_general(q_h, k_h, (((1,), (1,)), ((), ())),
                                                  preferred_element_type=F32)
        decay = jnp.exp(m_inter - m_t)
        c_old = c_ref[h]
        inter = lax.dot_general(q_h, c_old.astype(BF16), (((1,), (1,)), ((), ())), preferred_element_type=F32)
        num = _dot(w.astype(BF16), v_h) + decay * inter
        n_old = n_ref[h]
        den = jnp.sum(w, axis=-1, keepdims=True) + decay * jnp.sum(qf[:, hs] * n_old, axis=-1, keepdims=True)
        hh = num / jnp.maximum(jnp.abs(den), jnp.exp(-m_t))

        g_last = b_col[ch - 1:ch, :]
        a_row = g_last - b_row + i_row
        a_col = g_last - b_col + i_col
        m_new = jnp.maximum(g_last + m_prev, jnp.max(a_row, axis=-1, keepdims=True))
        carry = jnp.exp(g_last + m_prev - m_new)
        wa_col = jnp.exp(a_col - m_new)
        wv = (wa_col * v_h.astype(F32)).astype(BF16)
        c_ref[h] = carry * c_old + lax.dot_general(wv, k_h, (((0,), (0,)), ((), ())), preferred_element_type=F32)
        n_ref[h] = carry * n_old + jnp.sum(wa_col * kf[:, hs], axis=0, keepdims=True)
        m_ref[h] = jnp.broadcast_to(m_new, (1, LANES))

        gated = _sigmoid(og_ref[0, :, hs].astype(F32)) * hh
        ms = jnp.sum(gated * gated, axis=-1, keepdims=True) * (1.0 / dh)
        y_ref[0, :, hs] = (gated * lax.rsqrt(ms + HEAD_NORM_EPS) * beta_ref[:, hs]).astype(y_ref.dtype)


def _mlstm(proj, gates, gate_bias, conv_w, conv_b, beta_mlstm, batch, seq):
    ch = MLSTM_CHUNK
    nchunk = seq // ch
    pv = proj.reshape(batch, seq, 4 * D_MLSTM)
    gv = gates.reshape(batch, seq, LANES)
    prev_rows = 16
    cb0 = 0

    def cur(c):
        return pl.BlockSpec((1, ch, D_MLSTM), lambda b, n: (b, n, c))

    def prev(c):
        per = ch // prev_rows
        return pl.BlockSpec((1, prev_rows, D_MLSTM), lambda b, n: (b, jnp.maximum(n * per - 1, 0), c))

    const = lambda a, b_: pl.BlockSpec((a, b_), lambda b, n: (0, 0))
    y = pl.pallas_call(
        _mlstm_kernel,
        grid=(batch, nchunk),
        in_specs=[cur(cb0), cur(cb0 + 1), prev(cb0), prev(cb0 + 1), cur(cb0 + 2), cur(cb0 + 3),
                  pl.BlockSpec((1, ch, LANES), lambda b, n: (b, n, 0)),
                  const(1, LANES), const(CONV_K, 2 * D_MLSTM), const(1, 2 * D_MLSTM), const(1, D_MLSTM)],
        out_specs=pl.BlockSpec((1, ch, D_MLSTM), lambda b, n: (b, n, 0)),
        out_shape=jax.ShapeDtypeStruct((batch, seq, D_MLSTM), BF16),
        scratch_shapes=[pltpu.VMEM((N_HEADS_M, HEAD_DIM_M, HEAD_DIM_M), F32),
                        pltpu.VMEM((N_HEADS_M, 1, HEAD_DIM_M), F32),
                        pltpu.VMEM((N_HEADS_M, 1, LANES), F32)],
        compiler_params=_cparams(("parallel", "arbitrary")),
        name="mlstm",
    )(pv, pv, pv, pv, pv, pv, gv, gate_bias, conv_w.astype(F32), conv_b.reshape(1, -1).astype(F32),
      beta_mlstm.reshape(1, D_MLSTM).astype(F32))
    return y.reshape(batch * seq, D_MLSTM)


def _layer_norm(z, g, b):
    mu = jnp.mean(z, axis=-1, keepdims=True)
    zc = z - mu
    var = jnp.mean(zc * zc, axis=-1, keepdims=True)
    return zc * lax.rsqrt(var + LN_EPS) * g + b


PACK_ROWS = D_MODEL // (2 * LANES)


def _store_packed_rows(dst_ref, x, first=0):
    rows = x.shape[0]
    half = D_MODEL // 2
    for s in range(PACK_ROWS):
        lo = x[:, s * LANES:(s + 1) * LANES].astype(BF16).astype(F32)
        hi = x[:, half + s * LANES:half + (s + 1) * LANES].astype(BF16).astype(F32)
        word = pltpu.bitcast(hi, jnp.int32) | lax.shift_right_logical(pltpu.bitcast(lo, jnp.int32), 16)
        dst_ref[pl.ds(first * PACK_ROWS + s, rows, stride=PACK_ROWS), :] = word


def _load_packed_rows(src_ref, first, rows):
    los, his = [], []
    for s in range(PACK_ROWS):
        word = src_ref[pl.ds(first * PACK_ROWS + s, rows, stride=PACK_ROWS), :]
        los.append(pltpu.bitcast(lax.shift_left(word, 16), F32))
        his.append(pltpu.bitcast(word & jnp.int32(-65536), F32))
    return los, his


def _outproj_kernel(ya_ref, ym_ref, x_ref, w_ref, g_ref, b_ref, h_ref, hp_ref):
    y = _dot(ya_ref[...], w_ref[0:D_ATTN, :]) + _dot(ym_ref[...], w_ref[D_ATTN:D_MODEL, :])
    h = _layer_norm(DN_ALPHA * x_ref[...] + y, g_ref[...], b_ref[...])
    h_ref[...] = h
    _store_packed_rows(hp_ref, h)


def _outproj_ln(ya, ym, x, w_out, g, b):
    n = x.shape[0]
    tm = min(OUTPROJ_TM, n)
    row = lambda w: pl.BlockSpec((tm, w), lambda i: (i, 0))
    full = lambda a, b_: pl.BlockSpec((a, b_), lambda i: (0, 0))
    return pl.pallas_call(
        _outproj_kernel,
        grid=(n // tm,),
        in_specs=[row(D_ATTN), row(D_MLSTM), row(D_MODEL), full(D_MODEL, D_MODEL), full(1, D_MODEL), full(1, D_MODEL)],
        out_specs=[row(D_MODEL), pl.BlockSpec((tm * PACK_ROWS, LANES), lambda i: (i, 0))],
        out_shape=[jax.ShapeDtypeStruct((n, D_MODEL), F32),
                   jax.ShapeDtypeStruct((n * PACK_ROWS, LANES), jnp.int32)],
        compiler_params=_cparams(("parallel",)),
        name="out_proj_ln",
    )(ya, ym, x, w_out, g.reshape(1, -1), b.reshape(1, -1))


def _router_kernel(h_ref, whi_ref, wlo_ref, b_ref, tri_ref, idx_ref, gate_ref, rank_ref, cnt_ref, carry_ref):
    i = pl.program_id(0)

    @pl.when(i == 0)
    def _():
        carry_ref[...] = jnp.zeros(carry_ref.shape, F32)

    x = h_ref[...]
    xhi = x.astype(BF16)
    xlo = (x - xhi.astype(F32)).astype(BF16)
    logits = _dot(xhi, whi_ref[...]) + _dot(xhi, wlo_ref[...]) + _dot(xlo, whi_ref[...]) + b_ref[...]
    tm = logits.shape[0]
    lane = lax.broadcasted_iota(jnp.int32, (tm, LANES), 1)
    lane_f = lane.astype(F32)
    vals = jnp.where(lane < N_EXPERTS, logits, NEG_INF)

    sels, tops = [], []
    for _ in range(TOP_K):
        mx = jnp.max(vals, axis=-1, keepdims=True)
        first = jnp.min(jnp.where(vals == mx, lane_f, float(LANES)), axis=-1, keepdims=True)
        sel = lane_f == first
        sels.append(sel)
        tops.append((mx, first))
        vals = jnp.where(sel, 2.0 * NEG_INF, vals)

    exps = [jnp.exp(mx - tops[0][0]) for mx, _ in tops]
    tot = exps[0] + exps[1] + exps[2] + exps[3]

    onehot = jnp.zeros((tm, LANES), F32)
    for sel in sels:
        onehot = jnp.where(sel, 1.0, onehot)
    before = _dot(tri_ref[...], onehot.astype(BF16)) + carry_ref[...]

    idx_out = jnp.zeros((tm, LANES), F32)
    gate_out = jnp.zeros((tm, LANES), F32)
    rank_out = jnp.zeros((tm, LANES), F32)
    for k in range(TOP_K):
        rank_k = jnp.sum(jnp.where(sels[k], before, 0.0), axis=-1, keepdims=True)
        idx_out = jnp.where(lane == k, tops[k][1], idx_out)
        gate_out = jnp.where(lane == k, exps[k] / tot, gate_out)
        rank_out = jnp.where(lane == k, rank_k, rank_out)
    idx_ref[...] = idx_out.astype(jnp.int32)
    gate_ref[...] = gate_out
    rank_ref[...] = rank_out.astype(jnp.int32)

    carry = carry_ref[...] + jnp.sum(onehot, axis=0, keepdims=True)
    carry_ref[...] = carry
    cnt_ref[...] = carry.astype(jnp.int32)


def _router(h1, w_router, b_router):
    n = h1.shape[0]
    tm = min(ROUTER_TM, n)
    wpad = jnp.zeros((D_MODEL, LANES), F32).at[:, :N_EXPERTS].set(w_router)
    whi = wpad.astype(BF16)
    wlo = (wpad - whi.astype(F32)).astype(BF16)
    bpad = jnp.zeros((1, LANES), F32).at[0, :N_EXPERTS].set(b_router)
    tri = jnp.asarray(np.tril(np.ones((tm, tm), np.float32), -1), BF16)
    row = lambda w: pl.BlockSpec((tm, w), lambda i: (i, 0))
    full = lambda a, b_: pl.BlockSpec((a, b_), lambda i: (0, 0))
    return pl.pallas_call(
        _router_kernel,
        grid=(n // tm,),
        in_specs=[row(D_MODEL), full(D_MODEL, LANES), full(D_MODEL, LANES), full(1, LANES), full(tm, tm)],
        out_specs=[row(LANES), row(LANES), row(LANES), full(1, LANES)],
        out_shape=[jax.ShapeDtypeStruct((n, LANES), jnp.int32), jax.ShapeDtypeStruct((n, LANES), F32),
                   jax.ShapeDtypeStruct((n, LANES), jnp.int32), jax.ShapeDtypeStruct((1, LANES), jnp.int32)],
        scratch_shapes=[pltpu.VMEM((1, LANES), F32)],
        compiler_params=_cparams(("arbitrary",)),
        name="router",
    )(h1, whi, wlo, bpad, tri)


def _ffn_kernel(bexp_ref, nused_ref, nvalid_ref, slot_ref, src_cur_ref, src_nxt_ref, wg_ref, wu_ref, wdl_ref, wdh_ref,
                bg_ref, bu_ref, bdl_ref, bdh_ref, xp_ref, ys_ref, xbuf_ref, x_ref, act_ref, obuf_ref, wgb_ref, wub_ref,
                wdlb_ref, wdhb_ref, in_sem, out_sem):
    del bexp_ref
    i = pl.program_id(0)
    g = pl.program_id(1)
    n_up = act_ref.shape[0]
    last_g = pl.num_programs(1) - 1
    last_i = pl.num_programs(0) - 1
    n_used = nused_ref[0]
    used = i < n_used
    rb = x_ref.shape[0]
    sub = FFN_SUB
    th = wdlb_ref.shape[1]
    unroll = 4

    def sub_blocks(blk):
        return (nvalid_ref[blk] + (sub - 1)) // sub

    nsub = sub_blocks(i)

    def start_gather(src_ref, nrows):
        def body(j, carry):
            for u in range(unroll):
                jj = j * unroll + u
                src = pl.multiple_of(src_ref[jj], PACK_ROWS)
                dst = pl.multiple_of(jj * PACK_ROWS, PACK_ROWS)
                pltpu.make_async_copy(xp_ref.at[pl.ds(src, PACK_ROWS)], xbuf_ref.at[pl.ds(dst, PACK_ROWS)],
                                      in_sem).start(priority=u % 2)
            return carry

        lax.fori_loop(0, nrows // unroll, body, 0)

    def wait_gather(nrows):
        npk = pl.multiple_of(nrows * PACK_ROWS, PACK_ROWS)
        pltpu.make_async_copy(xp_ref.at[pl.ds(0, npk)], xbuf_ref.at[pl.ds(0, npk)], in_sem).wait()

    def start_scatter(nrows):
        def body(j, carry):
            for u in range(unroll):
                jj = j * unroll + u
                a = slot_ref[jj]

                @pl.when(a >= 0)
                def _():
                    src = pl.multiple_of(jj * PACK_ROWS, PACK_ROWS)
                    dst = pl.multiple_of(a * PACK_ROWS, PACK_ROWS)
                    pltpu.make_async_copy(obuf_ref.at[pl.ds(src, PACK_ROWS)], ys_ref.at[pl.ds(dst, PACK_ROWS)],
                                          out_sem).start(priority=u % 2)
            return carry

        lax.fori_loop(0, nrows // unroll, body, 0)

    def wait_scatter(blk):
        npk = pl.multiple_of(nvalid_ref[blk] * PACK_ROWS, PACK_ROWS)
        pltpu.make_async_copy(obuf_ref.at[pl.ds(0, npk)], ys_ref.at[pl.ds(0, npk)], out_sem).wait()

    @pl.when(jnp.logical_and(used, g == 0))
    def _():
        @pl.when(i == 0)
        def _():
            start_gather(src_cur_ref, nsub * sub)

        wait_gather(nsub * sub)
        for j in range(rb // sub):
            @pl.when(j < nsub)
            def _():
                los, his = _load_packed_rows(xbuf_ref, j * sub, sub)
                rows = slice(j * sub, (j + 1) * sub)
                for s in range(PACK_ROWS):
                    x_ref[rows, s * LANES:(s + 1) * LANES] = los[s].astype(BF16)
                    x_ref[rows, D_MODEL // 2 + s * LANES:D_MODEL // 2 + (s + 1) * LANES] = his[s].astype(BF16)

    @pl.when(jnp.logical_and(g == 1, i + 1 < n_used))
    def _():
        start_gather(src_nxt_ref, sub_blocks(i + 1) * sub)

    @pl.when(jnp.logical_and(used, g < n_up))
    def _():
        wgb_ref[...] = wg_ref[0].astype(BF16)
        wub_ref[...] = wu_ref[0].astype(BF16)

        def sub_body(j, carry):
            rows = pl.ds(pl.multiple_of(j * sub, sub), sub)
            x = x_ref[rows, :]
            hg = _dot(x, wgb_ref[...]) + bg_ref[0]
            hu = _dot(x, wub_ref[...]) + bu_ref[0]
            gate = jnp.minimum(hg, SWIGLU_LIMIT)
            up = jnp.clip(hu, -SWIGLU_LIMIT, SWIGLU_LIMIT)
            act_ref[g, rows, :] = ((up + 1.0) * (gate * _sigmoid(SWIGLU_ALPHA * gate))).astype(BF16)
            return carry

        lax.fori_loop(0, nsub, sub_body, 0)

    @pl.when(jnp.logical_and(used, g >= n_up))
    def _():
        @pl.when(jnp.logical_and(g == n_up, i > 0))
        def _():
            wait_scatter(i - 1)

        wdlb_ref[...] = wdl_ref[0].astype(BF16)
        wdhb_ref[...] = wdh_ref[0].astype(BF16)
        tile0 = (g - n_up) * (th // LANES)

        def sub_body(j, carry):
            row0 = pl.multiple_of(j * sub, sub)
            a = jnp.concatenate([act_ref[c, pl.ds(row0, sub), :] for c in range(n_up)], axis=-1)
            ylo = _dot(a, wdlb_ref[...]) + bdl_ref[0]
            yhi = _dot(a, wdhb_ref[...]) + bdh_ref[0]
            for s in range(th // LANES):
                lo = ylo[:, s * LANES:(s + 1) * LANES].astype(BF16).astype(F32)
                hi = yhi[:, s * LANES:(s + 1) * LANES].astype(BF16).astype(F32)
                word = pltpu.bitcast(hi, jnp.int32) | lax.shift_right_logical(pltpu.bitcast(lo, jnp.int32), 16)
                obuf_ref[pl.ds(row0 * PACK_ROWS + tile0 + s, sub, stride=PACK_ROWS), :] = word
            return carry

        lax.fori_loop(0, nsub, sub_body, 0)

        @pl.when(g == last_g)
        def _():
            start_scatter(nsub * sub)

    @pl.when(jnp.logical_and(i == last_i, g == last_g))
    def _():
        wait_scatter(jnp.minimum(i, n_used - 1))


def _expert_ffn(h_packed, slot_assign, blk_exp, blk_valid, n_used, w_up, b_up, w_down, b_down):
    rb, tf, th = FFN_ROWS, FFN_TF, FFN_TH
    nblk = slot_assign.shape[0] // rb
    n_up = D_FF // tf
    n_dn = D_MODEL // 2 // th
    n_assign = h_packed.shape[0] // PACK_ROWS * TOP_K
    slot_src = jnp.maximum(slot_assign, 0) // TOP_K * PACK_ROWS

    def clamp(i, nu):
        return jnp.minimum(i, nu[0] - 1)

    def up_sel(i, g, nu):
        return jnp.where(i < nu[0], jnp.minimum(g, n_up - 1), n_up - 1)

    def dn_sel(i, g, nu):
        return jnp.where(i < nu[0], jnp.clip(g - n_up, 0, n_dn - 1), n_dn - 1)

    grid_spec = pltpu.PrefetchScalarGridSpec(
        num_scalar_prefetch=3,
        grid=(nblk, n_up + n_dn),
        in_specs=[
            pl.BlockSpec((rb,), lambda i, g, be, nu, *_: (clamp(i, nu),), memory_space=pltpu.SMEM),
            pl.BlockSpec((rb,), lambda i, g, be, nu, *_: (clamp(i, nu),), memory_space=pltpu.SMEM),
            pl.BlockSpec((rb,), lambda i, g, be, nu, *_: (clamp(i + 1, nu),), memory_space=pltpu.SMEM),
            pl.BlockSpec((1, D_MODEL, tf), lambda i, g, be, nu, *_: (be[clamp(i, nu)], 0, up_sel(i, g, nu))),
            pl.BlockSpec((1, D_MODEL, tf), lambda i, g, be, nu, *_: (be[clamp(i, nu)], 0, n_up + up_sel(i, g, nu))),
            pl.BlockSpec((1, D_FF, th), lambda i, g, be, nu, *_: (be[clamp(i, nu)], 0, dn_sel(i, g, nu))),
            pl.BlockSpec((1, D_FF, th), lambda i, g, be, nu, *_: (be[clamp(i, nu)], 0, n_dn + dn_sel(i, g, nu))),
            pl.BlockSpec((1, 1, tf), lambda i, g, be, nu, *_: (be[clamp(i, nu)], 0, up_sel(i, g, nu))),
            pl.BlockSpec((1, 1, tf), lambda i, g, be, nu, *_: (be[clamp(i, nu)], 0, n_up + up_sel(i, g, nu))),
            pl.BlockSpec((1, 1, th), lambda i, g, be, nu, *_: (be[clamp(i, nu)], 0, dn_sel(i, g, nu))),
            pl.BlockSpec((1, 1, th), lambda i, g, be, nu, *_: (be[clamp(i, nu)], 0, n_dn + dn_sel(i, g, nu))),
            pl.BlockSpec(memory_space=pl.ANY),
        ],
        out_specs=pl.BlockSpec(memory_space=pl.ANY),
        scratch_shapes=[pltpu.VMEM((rb * PACK_ROWS, LANES), jnp.int32),
                        pltpu.VMEM((rb, D_MODEL), BF16),
                        pltpu.VMEM((n_up, rb, tf), BF16),
                        pltpu.VMEM((rb * PACK_ROWS, LANES), jnp.int32),
                        pltpu.VMEM((D_MODEL, tf), BF16), pltpu.VMEM((D_MODEL, tf), BF16),
                        pltpu.VMEM((D_FF, th), BF16), pltpu.VMEM((D_FF, th), BF16),
                        pltpu.SemaphoreType.DMA, pltpu.SemaphoreType.DMA],
    )
    assert n_up >= 2, "the next block's gather is issued at step 1, after this block's rows are unpacked at step 0"
    return pl.pallas_call(
        _ffn_kernel,
        grid_spec=grid_spec,
        out_shape=jax.ShapeDtypeStruct((n_assign * PACK_ROWS, LANES), jnp.int32),
        compiler_params=pltpu.CompilerParams(dimension_semantics=("arbitrary", "arbitrary"),
                                             vmem_limit_bytes=FFN_VMEM_LIMIT, disable_bounds_checks=True),
        name="expert_ffn",
    )(blk_exp, n_used, blk_valid, slot_assign, slot_src, slot_src, w_up, w_up, w_down, w_down,
      b_up.reshape(N_EXPERTS, 1, 2 * D_FF), b_up.reshape(N_EXPERTS, 1, 2 * D_FF),
      b_down.reshape(N_EXPERTS, 1, D_MODEL), b_down.reshape(N_EXPERTS, 1, D_MODEL), h_packed)


def _combine_kernel(gate_ref, h_ref, g_ref, b_ref, ys_ref, o_ref):
    tm = h_ref.shape[0]
    gates = gate_ref[...]
    cols = [None] * (2 * PACK_ROWS)
    for k in range(TOP_K):
        gk = gates[:, k:k + 1]
        for s in range(PACK_ROWS):
            word = ys_ref[pl.ds(k * PACK_ROWS + s, tm, stride=TOP_K * PACK_ROWS), :]
            lo = pltpu.bitcast(lax.shift_left(word, 16), F32)
            hi = pltpu.bitcast(word & jnp.int32(-65536), F32)
            for c, blk in ((s, lo), (PACK_ROWS + s, hi)):
                cols[c] = gk * blk if cols[c] is None else cols[c] + gk * blk
    z = DN_ALPHA * h_ref[...] + jnp.concatenate(cols, axis=-1)
    o_ref[...] = _layer_norm(z, g_ref[...], b_ref[...])


def _combine_ln(gates, h1, ys, g, b):
    n = h1.shape[0]
    tm = min(COMBINE_LN_TM, n)
    row = lambda w: pl.BlockSpec((tm, w), lambda i: (i, 0))
    full = lambda a, b_: pl.BlockSpec((a, b_), lambda i: (0, 0))
    return pl.pallas_call(
        _combine_kernel,
        grid=(n // tm,),
        in_specs=[row(LANES), row(D_MODEL), full(1, D_MODEL), full(1, D_MODEL),
                  pl.BlockSpec((tm * TOP_K * PACK_ROWS, LANES), lambda i: (i, 0))],
        out_specs=row(D_MODEL),
        out_shape=jax.ShapeDtypeStruct((n, D_MODEL), F32),
        compiler_params=_cparams(("parallel",)),
        name="combine_ln",
    )(gates, h1, g.reshape(1, -1), b.reshape(1, -1), ys)


def _layer(h, w_in, b_igate, b_fgate, conv_w, conv_b, rel_bias, beta_attn, beta_mlstm, w_out,
           ln1_g, ln1_b, w_router, b_router, w_up, b_up, w_down, b_down, ln2_g, ln2_b, batch, seq):
    n = batch * seq
    w_qkv = w_in[:, :3 * D_ATTN].astype(BF16)
    w_mix = w_in[:, 3 * D_ATTN:MAIN_COLS].astype(BF16)
    w_gate = jnp.zeros((D_MODEL, LANES), BF16).at[:, :2 * N_HEADS_M].set(w_in[:, MAIN_COLS:].astype(BF16))
    qkvs = _qkv_project(h, w_qkv, min(QKV_TM, n))
    proj_m = _project(h, w_mix, BF16, min(PROJ_TM, n), PROJ_TN)
    gates = _project(h, w_gate, F32, min(PROJ_TM, n), LANES)
    gate_bias = jnp.zeros((1, LANES), F32).at[0, :2 * N_HEADS_M].set(jnp.concatenate([b_igate, b_fgate]))

    outs, lses = [], []
    for (_, dil), qkv in zip(DILATED_CONFIGS, qkvs):
        o, l = _dilated_attention(qkv, _attn_bias_tables(rel_bias, dil), batch, seq, dil)
        outs.append(o)
        lses.append(l)
    y_attn = _attn_combine(outs, lses, beta_attn)
    y_mlstm = _mlstm(proj_m, gates, gate_bias, conv_w, conv_b, beta_mlstm, batch, seq)

    h1, h1_packed = _outproj_ln(y_attn, y_mlstm, h, w_out.astype(BF16), ln1_g, ln1_b)

    top_idx, top_gate, rank, counts = _router(h1, w_router, b_router)
    counts = counts[0, :N_EXPERTS]
    padded = (counts + FFN_ROWS - 1) // FFN_ROWS * FFN_ROWS
    pad_end = jnp.cumsum(padded)
    pad_start = pad_end - padded
    dest = (pad_start[top_idx[:, :TOP_K]] + rank[:, :TOP_K]).reshape(-1).astype(jnp.int32)
    nblk = n * TOP_K // FFN_ROWS + N_EXPERTS
    blk_row0 = jnp.arange(nblk, dtype=jnp.int32) * FFN_ROWS
    blk_exp = jnp.minimum(jnp.sum(pad_end[None, :] <= blk_row0[:, None], axis=1), N_EXPERTS - 1).astype(jnp.int32)
    n_used = (pad_end[-1:] // FFN_ROWS).astype(jnp.int32)
    blk_valid = jnp.clip((pad_start + counts)[blk_exp] - blk_row0, 0, FFN_ROWS).astype(jnp.int32)
    slot_assign = jnp.full((nblk * FFN_ROWS,), -1, jnp.int32).at[dest].set(
        jnp.arange(n * TOP_K, dtype=jnp.int32), unique_indices=True)

    ys = _expert_ffn(h1_packed, slot_assign, blk_exp, blk_valid, n_used, w_up, b_up, w_down, b_down)
    return _combine_ln(top_gate, h1, ys, ln2_g, ln2_b)


def kernel(x, w_in, b_igate, b_fgate, conv_w, conv_b, rel_bias, beta_attn, beta_mlstm, w_out, ln1_g, ln1_b,
           w_router, b_router, w_up, b_up, w_down, b_down, ln2_g, ln2_b):
    batch, seq, d = x.shape
    h = x.reshape(batch * seq, d)
    for l in range(DEPTH):
        h = _layer(h, w_in[l], b_igate[l], b_fgate[l], conv_w[l], conv_b[l], rel_bias, beta_attn[l], beta_mlstm[l],
                   w_out[l], ln1_g[l], ln1_b[l], w_router[l], b_router[l], w_up[l], b_up[l], w_down[l], b_down[l],
                   ln2_g[l], ln2_b[l], batch, seq)
    return h.reshape(batch, seq, d)
```

```python
import functools
import math

import numpy as np
import jax
import jax.numpy as jnp
from jax import lax
from jax.experimental import pallas as pl
from jax.experimental.pallas import tpu as pltpu

F32 = jnp.float32
BF16 = jnp.bfloat16

D_MODEL = 2048
D_ATTN = 1024
HEAD_DIM_A = 64
N_HEADS_A = 16
DILATED_CONFIGS = ((128, 1), (512, 4), (2048, 16))
ATTN_BLOCK = 128
NUM_BUCKETS = 32
MAX_DISTANCE = 2048
D_MLSTM = 1024
N_HEADS_M = 4
HEAD_DIM_M = 256
CONV_K = 4
MLSTM_CHUNK = 128
MAIN_COLS = 3 * D_ATTN + 4 * D_MLSTM
N_EXPERTS = 32
TOP_K = 4
D_FF = 2048
SWIGLU_LIMIT = 7.0
SWIGLU_ALPHA = 1.702
DEPTH = 1
DN_ALPHA = (2 * DEPTH) ** 0.25
LN_EPS = 1e-5
HEAD_NORM_EPS = 1e-6
NEG_INF = -1e30

LANES = 128
VMEM_LIMIT = 48 * 1024 * 1024

PROJ_TM = 1024
PROJ_TN = 1024
COMBINE_TM = 512
OUTPROJ_TM = 512
ROUTER_TM = 512
QKV_TM = 512
ATTN_GROUP = 2
FFN_ROWS = 1024
FFN_SUB = 256
FFN_TF = 512
FFN_TH = 256
FFN_VMEM_LIMIT = 56 * 1024 * 1024
COMBINE_LN_TM = 256


def _cparams(sem, vmem=VMEM_LIMIT):
    return pltpu.CompilerParams(dimension_semantics=sem, vmem_limit_bytes=vmem)


def _dot(a, b):
    return jnp.dot(a, b, preferred_element_type=F32)


def _dot_f32_rhs(a_bf16, b_f32):
    hi = b_f32.astype(BF16)
    lo = (b_f32 - hi.astype(F32)).astype(BF16)
    return _dot(a_bf16, hi) + _dot(a_bf16, lo)


def _dot_f32_lhs(a_f32, b_bf16):
    hi = a_f32.astype(BF16)
    lo = (a_f32 - hi.astype(F32)).astype(BF16)
    return _dot(hi, b_bf16) + _dot(lo, b_bf16)


def _sigmoid(x):
    return 1.0 / (1.0 + jnp.exp(-x))


def _log_sigmoid(x):
    return jnp.minimum(x, 0.0) - jnp.log(1.0 + jnp.exp(-jnp.abs(x)))


def _proj_kernel(x_ref, w_ref, o_ref):
    o_ref[...] = _dot(x_ref[...].astype(BF16), w_ref[...]).astype(o_ref.dtype)


def _project(x, w, out_dtype, tm, tn):
    m, k = x.shape
    n = w.shape[1]
    return pl.pallas_call(
        _proj_kernel,
        grid=(m // tm, n // tn),
        in_specs=[pl.BlockSpec((tm, k), lambda i, j: (i, 0)),
                  pl.BlockSpec((k, tn), lambda i, j: (0, j))],
        out_specs=pl.BlockSpec((tm, tn), lambda i, j: (i, j)),
        out_shape=jax.ShapeDtypeStruct((m, n), out_dtype),
        compiler_params=_cparams(("parallel", "parallel")),
        name="in_proj",
    )(x, w)


def _qkv_proj_kernel(x_ref, w_ref, *refs):
    o_refs, r_ref = refs[:-1], refs[-1]
    res = _dot(x_ref[...].astype(BF16), w_ref[...])
    ntile, tm, _ = r_ref.shape
    wid = ntile * LANES
    for c in range(ntile):
        r_ref[c] = res[:, c * LANES:(c + 1) * LANES]
    for (_, dil), o_ref in zip(DILATED_CONFIGS, o_refs):
        if dil == 1:
            o_ref[...] = res.astype(o_ref.dtype)
        else:
            for r in range(dil):
                for c in range(ntile):
                    col = r * wid + c * LANES
                    o_ref[:, col:col + LANES] = r_ref[c, pl.ds(r, tm // dil, stride=dil), :].astype(o_ref.dtype)


def _qkv_project(x, w, tm):
    m, k = x.shape
    wid = w.shape[1]
    dils = [d for _, d in DILATED_CONFIGS]
    return pl.pallas_call(
        _qkv_proj_kernel,
        grid=(m // tm,),
        in_specs=[pl.BlockSpec((tm, k), lambda i: (i, 0)),
                  pl.BlockSpec((k, wid), lambda i: (0, 0), pipeline_mode=pl.Buffered(1))],
        out_specs=[pl.BlockSpec((tm // d, d * wid), lambda i: (i, 0)) for d in dils],
        out_shape=[jax.ShapeDtypeStruct((m // d, d * wid), BF16) for d in dils],
        scratch_shapes=[pltpu.VMEM((wid // LANES, tm, LANES), F32)],
        compiler_params=_cparams(("parallel",), 56 * 1024 * 1024),
        name="qkv_proj",
    )(x, w)


def _attn_kernel(q_ref, kp_ref, kc_ref, vp_ref, vc_ref, bias_ref, o_ref, lse_ref):
    n = pl.program_id(2)
    tab = jnp.minimum(n, 1)
    lse_ref[...] = jnp.zeros(lse_ref.shape, F32)
    grp, dh, nk = ATTN_GROUP, HEAD_DIM_A, 2 * ATTN_BLOCK
    wid = grp * dh
    lane_head = lax.broadcasted_iota(jnp.int32, (nk, wid), 1) // dh
    zero = jnp.zeros((nk, wid), BF16)
    ones_bd = jnp.concatenate([jnp.where(lane_head == j, 1.0, 0.0).astype(BF16) for j in range(grp)], axis=0)
    for g in range(N_HEADS_A // grp):
        cols = slice(g * wid, (g + 1) * wid)
        q = q_ref[0, :, cols] * (dh ** -0.5)
        kslab = jnp.concatenate([kp_ref[0, :, cols], kc_ref[0, :, cols]], axis=0)
        vslab = jnp.concatenate([vp_ref[0, :, cols], vc_ref[0, :, cols]], axis=0)
        k_bd = jnp.concatenate([jnp.where(lane_head == j, kslab, zero) for j in range(grp)], axis=0)
        v_bd = jnp.concatenate([jnp.where(lane_head == j, vslab, zero) for j in range(grp)], axis=0)
        s_all = lax.dot_general(q, k_bd, (((1,), (1,)), ((), ())), preferred_element_type=F32)
        ps, ms = [], []
        for j in range(grp):
            s = s_all[:, j * nk:(j + 1) * nk] + bias_ref[tab, g * grp + j]
            m = jnp.max(s, axis=-1, keepdims=True)
            ps.append(jnp.exp(s - m).astype(BF16))
            ms.append(m)
        p_all = jnp.concatenate(ps, axis=-1)
        res = _dot(p_all, jnp.concatenate([v_bd, ones_bd], axis=-1))
        den = res[:, wid:]
        o_ref[0, :, cols] = (res[:, :wid] / den).astype(o_ref.dtype)
        for j in range(grp):
            h = g * grp + j
            lse_ref[0, :, h:h + 1] = ms[j] + jnp.log(den[:, j * dh:j * dh + 1])


def _attn_bias_tables(rel_bias, dil):
    blk = ATTN_BLOCK
    period = 3 * blk
    k = np.arange(period)
    valid = k <= blk
    dist = np.where(valid, blk - k, 0) * dil
    max_exact = NUM_BUCKETS // 2
    d_f = np.maximum(dist, 1).astype(np.float32)
    large = max_exact + (np.log(d_f / np.float32(max_exact)) / np.float32(math.log(MAX_DISTANCE / max_exact))
                         * np.float32(NUM_BUCKETS - max_exact)).astype(np.int32)
    large = np.minimum(large, NUM_BUCKETS - 1)
    bucket = np.where(dist < max_exact, dist, large).astype(np.int32)
    w = jnp.where(jnp.asarray(valid)[None, :], rel_bias[jnp.asarray(bucket)].T.astype(F32), NEG_INF)
    t1 = jnp.tile(w, (1, blk))[:, :blk * (period - 1)].reshape(N_HEADS_A, blk, period - 1)[:, :, :2 * blk]
    has_prev = np.arange(2 * blk)[None, None, :] >= blk
    t0 = jnp.where(jnp.asarray(has_prev), t1, NEG_INF)
    return jnp.stack([t0, t1])


def _dilated_attention(qkv, bias_tab, batch, seq, dil):
    blk = ATTN_BLOCK
    l = seq // dil
    nb = l // blk
    ncb = 3
    pv = qkv.reshape(batch, l, dil * ncb * D_ATTN)

    def cur(c):
        return pl.BlockSpec((1, blk, D_ATTN), lambda b, r, n: (b, n, r * ncb + c))

    def prev(c):
        return pl.BlockSpec((1, blk, D_ATTN), lambda b, r, n: (b, jnp.maximum(n - 1, 0), r * ncb + c))

    o, lse = pl.pallas_call(
        _attn_kernel,
        grid=(batch, dil, nb),
        in_specs=[cur(0), prev(1), cur(1), prev(2), cur(2),
                  pl.BlockSpec((2, N_HEADS_A, blk, 2 * blk), lambda b, r, n: (0, 0, 0, 0))],
        out_specs=[pl.BlockSpec((1, blk, D_ATTN), lambda b, r, n: (b, n, r)),
                   pl.BlockSpec((1, blk, LANES), lambda b, r, n: (b, n, r))],
        out_shape=[jax.ShapeDtypeStruct((batch, l, dil * D_ATTN), BF16),
                   jax.ShapeDtypeStruct((batch, l, dil * LANES), F32)],
        compiler_params=_cparams(("parallel", "parallel", "arbitrary")),
        name=f"dilated_attn_d{dil}",
    )(pv, pv, pv, pv, pv, bias_tab)
    return o.reshape(batch * seq, D_ATTN), lse.reshape(batch * seq, LANES)


def _attn_combine_kernel(o1_ref, o2_ref, o3_ref, l1_ref, l2_ref, l3_ref, e_ref, et_ref, beta_ref, y_ref):
    lses = [l1_ref[...], l2_ref[...], l3_ref[...]]
    outs = [o1_ref, o2_ref, o3_ref]
    mx = jnp.maximum(jnp.maximum(lses[0], lses[1]), lses[2])
    ws = [jnp.exp(l - mx) for l in lses]
    tot = ws[0] + ws[1] + ws[2]
    e = e_ref[...]
    acc = None
    for w, o_ref in zip(ws, outs):
        term = _dot_f32_lhs(w / tot, e) * o_ref[...].astype(F32)
        acc = term if acc is None else acc + term
    ss = _dot_f32_lhs(acc * acc, et_ref[...])
    inv = lax.rsqrt(ss * (1.0 / HEAD_DIM_A) + HEAD_NORM_EPS)
    y_ref[...] = (acc * _dot_f32_lhs(inv, e) * beta_ref[...]).astype(y_ref.dtype)


def _attn_combine(os_, lses, beta_attn):
    n = os_[0].shape[0]
    tm = min(COMBINE_TM, n)
    head_of_lane = np.arange(D_ATTN) // HEAD_DIM_A
    e = (np.arange(LANES)[:, None] == head_of_lane[None, :]).astype(np.float32)
    e_j = jnp.asarray(e, BF16)
    et_j = jnp.asarray(e.T, BF16)
    row = lambda w: pl.BlockSpec((tm, w), lambda i: (i, 0))
    full = lambda a, b: pl.BlockSpec((a, b), lambda i: (0, 0))
    return pl.pallas_call(
        _attn_combine_kernel,
        grid=(n // tm,),
        in_specs=[row(D_ATTN)] * 3 + [row(LANES)] * 3 + [full(LANES, D_ATTN), full(D_ATTN, LANES), full(1, D_ATTN)],
        out_specs=row(D_ATTN),
        out_shape=jax.ShapeDtypeStruct((n, D_ATTN), BF16),
        compiler_params=_cparams(("parallel",)),
        name="attn_combine",
    )(*os_, *lses, e_j, et_j, beta_attn.reshape(1, D_ATTN).astype(F32))


def _mlstm_kernel(qp_ref, kp_ref, qprev_ref, kprev_ref, v_ref, og_ref, g_ref, gb_ref, cw_ref, cb_ref,
                  beta_ref, y_ref, c_ref, n_ref, m_ref):
    step = pl.program_id(1)
    ch = MLSTM_CHUNK
    dh = HEAD_DIM_M

    @pl.when(step == 0)
    def _():
        c_ref[...] = jnp.zeros(c_ref.shape, F32)
        n_ref[...] = jnp.zeros(n_ref.shape, F32)
        m_ref[...] = jnp.zeros(m_ref.shape, F32)

    def conv_silu(x_ref, prev_ref, coff):
        x = x_ref[0].astype(F32)
        p = jnp.where(step > 0, prev_ref[0].astype(F32), 0.0)
        xe = jnp.concatenate([p, x], axis=0)
        npad = p.shape[0]
        cols = slice(coff, coff + D_MLSTM)
        acc = cb_ref[:, cols] + cw_ref[CONV_K - 1:CONV_K, cols] * x
        for s in range(1, CONV_K):
            shifted = pltpu.roll(xe, s, 0)[npad:]
            acc = acc + cw_ref[CONV_K - 1 - s:CONV_K - s, cols] * shifted
        return acc * _sigmoid(acc)

    qf = conv_silu(qp_ref, qprev_ref, 0)
    kf = conv_silu(kp_ref, kprev_ref, D_MLSTM) * (dh ** -0.5)
    qb = qf.astype(BF16)
    kb = kf.astype(BF16)

    g = g_ref[0] + gb_ref[...]
    gt = g.T
    row_i = lax.broadcasted_iota(jnp.int32, (ch, ch), 0)
    col_i = lax.broadcasted_iota(jnp.int32, (ch, ch), 1)
    causal = row_i >= col_i
    tri = jnp.where(causal, 1.0, 0.0).astype(BF16)
    upp = jnp.where(row_i <= col_i, 1.0, 0.0).astype(BF16)
    b_cols = _dot_f32_rhs(tri, _log_sigmoid(g))
    b_rows = _dot_f32_lhs(_log_sigmoid(gt), upp)

    for h in range(N_HEADS_M):
        hs = slice(h * dh, (h + 1) * dh)
        fi = N_HEADS_M + h
        i_row = gt[h:h + 1, :]
        i_col = g[:, h:h + 1]
        b_row = b_rows[fi:fi + 1, :]
        b_col = b_cols[:, fi:fi + 1]
        m_prev = m_ref[h][:, 0:1]
        q_h, k_h = qb[:, hs], kb[:, hs]
        v_h = v_ref[0, :, hs]

        dmat = jnp.where(causal, b_col - b_row + i_row, NEG_INF)
        m_inter = b_col + m_prev
        m_t = jnp.maximum(m_inter, jnp.max(dmat, axis=-1, keepdims=True))
        w = jnp.exp(dmat - m_t) * lax.dot_general(q_h, k_h, (((1,), (1,)), ((), ())),
                                                  preferred_element_type=F32)
        decay = jnp.exp(m_inter - m_t)
        c_old = c_ref[h]
        inter = lax.dot_general(q_h, c_old.astype(BF16), (((1,), (1,)), ((), ())), preferred_element_type=F32)
        num = _dot(w.astype(BF16), v_h) + decay * inter
        n_old = n_ref[h]
        den = jnp.sum(w, axis=-1, keepdims=True) + decay * jnp.sum(qf[:, hs] * n_old, axis=-1, keepdims=True)
        hh = num / jnp.maximum(jnp.abs(den), jnp.exp(-m_t))

        g_last = b_col[ch - 1:ch, :]
        a_row = g_last - b_row + i_row
        a_col = g_last - b_col + i_col
        m_new = jnp.maximum(g_last + m_prev, jnp.max(a_row, axis=-1, keepdims=True))
        carry = jnp.exp(g_last + m_prev - m_new)
        wa_col = jnp.exp(a_col - m_new)
        wv = (wa_col * v_h.astype(F32)).astype(BF16)
        c_ref[h] = carry * c_old + lax.dot_general(wv, k_h, (((0,), (0,)), ((), ())), preferred_element_type=F32)
        n_ref[h] = carry * n_old + jnp.sum(wa_col * kf[:, hs], axis=0, keepdims=True)
        m_ref[h] = jnp.broadcast_to(m_new, (1, LANES))

        gated = _sigmoid(og_ref[0, :, hs].astype(F32)) * hh
        ms = jnp.sum(gated * gated, axis=-1, keepdims=True) * (1.0 / dh)
        y_ref[0, :, hs] = (gated * lax.rsqrt(ms + HEAD_NORM_EPS) * beta_ref[:, hs]).astype(y_ref.dtype)


def _mlstm(proj, gates, gate_bias, conv_w, conv_b, beta_mlstm, batch, seq):
    ch = MLSTM_CHUNK
    nchunk = seq // ch
    pv = proj.reshape(batch, seq, 4 * D_MLSTM)
    gv = gates.reshape(batch, seq, LANES)
    prev_rows = 16
    cb0 = 0

    def cur(c):
        return pl.BlockSpec((1, ch, D_MLSTM), lambda b, n: (b, n, c))

    def prev(c):
        per = ch // prev_rows
        return pl.BlockSpec((1, prev_rows, D_MLSTM), lambda b, n: (b, jnp.maximum(n * per - 1, 0), c))

    const = lambda a, b_: pl.BlockSpec((a, b_), lambda b, n: (0, 0))
    y = pl.pallas_call(
        _mlstm_kernel,
        grid=(batch, nchunk),
        in_specs=[cur(cb0), cur(cb0 + 1), prev(cb0), prev(cb0 + 1), cur(cb0 + 2), cur(cb0 + 3),
                  pl.BlockSpec((1, ch, LANES), lambda b, n: (b, n, 0)),
                  const(1, LANES), const(CONV_K, 2 * D_MLSTM), const(1, 2 * D_MLSTM), const(1, D_MLSTM)],
        out_specs=pl.BlockSpec((1, ch, D_MLSTM), lambda b, n: (b, n, 0)),
        out_shape=jax.ShapeDtypeStruct((batch, seq, D_MLSTM), BF16),
        scratch_shapes=[pltpu.VMEM((N_HEADS_M, HEAD_DIM_M, HEAD_DIM_M), F32),
                        pltpu.VMEM((N_HEADS_M, 1, HEAD_DIM_M), F32),
                        pltpu.VMEM((N_HEADS_M, 1, LANES), F32)],
        compiler_params=_cparams(("parallel", "arbitrary")),
        name="mlstm",
    )(pv, pv, pv, pv, pv, pv, gv, gate_bias, conv_w.astype(F32), conv_b.reshape(1, -1).astype(F32),
      beta_mlstm.reshape(1, D_MLSTM).astype(F32))
    return y.reshape(batch * seq, D_MLSTM)


def _layer_norm(z, g, b):
    mu = jnp.mean(z, axis=-1, keepdims=True)
    zc = z - mu
    var = jnp.mean(zc * zc, axis=-1, keepdims=True)
    return zc * lax.rsqrt(var + LN_EPS) * g + b


PACK_ROWS = D_MODEL // (2 * LANES)


def _store_packed_rows(dst_ref, x, first=0):
    rows = x.shape[0]
    half = D_MODEL // 2
    for s in range(PACK_ROWS):
        lo = x[:, s * LANES:(s + 1) * LANES].astype(BF16).astype(F32)
        hi = x[:, half + s * LANES:half + (s + 1) * LANES].astype(BF16).astype(F32)
        word = pltpu.bitcast(hi, jnp.int32) | lax.shift_right_logical(pltpu.bitcast(lo, jnp.int32), 16)
        dst_ref[pl.ds(first * PACK_ROWS + s, rows, stride=PACK_ROWS), :] = word


def _load_packed_rows(src_ref, first, rows):
    los, his = [], []
    for s in range(PACK_ROWS):
        word = src_ref[pl.ds(first * PACK_ROWS + s, rows, stride=PACK_ROWS), :]
        los.append(pltpu.bitcast(lax.shift_left(word, 16), F32))
        his.append(pltpu.bitcast(word & jnp.int32(-65536), F32))
    return los, his


def _outproj_kernel(ya_ref, ym_ref, x_ref, w_ref, g_ref, b_ref, h_ref, hp_ref):
    y = _dot(ya_ref[...], w_ref[0:D_ATTN, :]) + _dot(ym_ref[...], w_ref[D_ATTN:D_MODEL, :])
    h = _layer_norm(DN_ALPHA * x_ref[...] + y, g_ref[...], b_ref[...])
    h_ref[...] = h
    _store_packed_rows(hp_ref, h)


def _outproj_ln(ya, ym, x, w_out, g, b):
    n = x.shape[0]
    tm = min(OUTPROJ_TM, n)
    row = lambda w: pl.BlockSpec((tm, w), lambda i: (i, 0))
    full = lambda a, b_: pl.BlockSpec((a, b_), lambda i: (0, 0))
    return pl.pallas_call(
        _outproj_kernel,
        grid=(n // tm,),
        in_specs=[row(D_ATTN), row(D_MLSTM), row(D_MODEL), full(D_MODEL, D_MODEL), full(1, D_MODEL), full(1, D_MODEL)],
        out_specs=[row(D_MODEL), pl.BlockSpec((tm * PACK_ROWS, LANES), lambda i: (i, 0))],
        out_shape=[jax.ShapeDtypeStruct((n, D_MODEL), F32),
                   jax.ShapeDtypeStruct((n * PACK_ROWS, LANES), jnp.int32)],
        compiler_params=_cparams(("parallel",)),
        name="out_proj_ln",
    )(ya, ym, x, w_out, g.reshape(1, -1), b.reshape(1, -1))


def _router_kernel(h_ref, whi_ref, wlo_ref, b_ref, tri_ref, idx_ref, gate_ref, rank_ref, cnt_ref, carry_ref):
    i = pl.program_id(0)

    @pl.when(i == 0)
    def _():
        carry_ref[...] = jnp.zeros(carry_ref.shape, F32)

    x = h_ref[...]
    xhi = x.astype(BF16)
    xlo = (x - xhi.astype(F32)).astype(BF16)
    logits = _dot(xhi, whi_ref[...]) + _dot(xhi, wlo_ref[...]) + _dot(xlo, whi_ref[...]) + b_ref[...]
    tm = logits.shape[0]
    lane = lax.broadcasted_iota(jnp.int32, (tm, LANES), 1)
    lane_f = lane.astype(F32)
    vals = jnp.where(lane < N_EXPERTS, logits, NEG_INF)

    sels, tops = [], []
    for _ in range(TOP_K):
        mx = jnp.max(vals, axis=-1, keepdims=True)
        first = jnp.min(jnp.where(vals == mx, lane_f, float(LANES)), axis=-1, keepdims=True)
        sel = lane_f == first
        sels.append(sel)
        tops.append((mx, first))
        vals = jnp.where(sel, 2.0 * NEG_INF, vals)

    exps = [jnp.exp(mx - tops[0][0]) for mx, _ in tops]
    tot = exps[0] + exps[1] + exps[2] + exps[3]

    onehot = jnp.zeros((tm, LANES), F32)
    for sel in sels:
        onehot = jnp.where(sel, 1.0, onehot)
    before = _dot(tri_ref[...], onehot.astype(BF16)) + carry_ref[...]

    idx_out = jnp.zeros((tm, LANES), F32)
    gate_out = jnp.zeros((tm, LANES), F32)
    rank_out = jnp.zeros((tm, LANES), F32)
    for k in range(TOP_K):
        rank_k = jnp.sum(jnp.where(sels[k], before, 0.0), axis=-1, keepdims=True)
        idx_out = jnp.where(lane == k, tops[k][1], idx_out)
        gate_out = jnp.where(lane == k, exps[k] / tot, gate_out)
        rank_out = jnp.where(lane == k, rank_k, rank_out)
    idx_ref[...] = idx_out.astype(jnp.int32)
    gate_ref[...] = gate_out
    rank_ref[...] = rank_out.astype(jnp.int32)

    carry = carry_ref[...] + jnp.sum(onehot, axis=0, keepdims=True)
    carry_ref[...] = carry
    cnt_ref[...] = carry.astype(jnp.int32)


def _router(h1, w_router, b_router):
    n = h1.shape[0]
    tm = min(ROUTER_TM, n)
    wpad = jnp.zeros((D_MODEL, LANES), F32).at[:, :N_EXPERTS].set(w_router)
    whi = wpad.astype(BF16)
    wlo = (wpad - whi.astype(F32)).astype(BF16)
    bpad = jnp.zeros((1, LANES), F32).at[0, :N_EXPERTS].set(b_router)
    tri = jnp.asarray(np.tril(np.ones((tm, tm), np.float32), -1), BF16)
    row = lambda w: pl.BlockSpec((tm, w), lambda i: (i, 0))
    full = lambda a, b_: pl.BlockSpec((a, b_), lambda i: (0, 0))
    return pl.pallas_call(
        _router_kernel,
        grid=(n // tm,),
        in_specs=[row(D_MODEL), full(D_MODEL, LANES), full(D_MODEL, LANES), full(1, LANES), full(tm, tm)],
        out_specs=[row(LANES), row(LANES), row(LANES), full(1, LANES)],
        out_shape=[jax.ShapeDtypeStruct((n, LANES), jnp.int32), jax.ShapeDtypeStruct((n, LANES), F32),
                   jax.ShapeDtypeStruct((n, LANES), jnp.int32), jax.ShapeDtypeStruct((1, LANES), jnp.int32)],
        scratch_shapes=[pltpu.VMEM((1, LANES), F32)],
        compiler_params=_cparams(("arbitrary",)),
        name="router",
    )(h1, whi, wlo, bpad, tri)


def _ffn_kernel(bexp_ref, nused_ref, nvalid_ref, slot_ref, src_cur_ref, src_nxt_ref, wg_ref, wu_ref, wdl_ref, wdh_ref,
                bg_ref, bu_ref, bdl_ref, bdh_ref, xp_ref, ys_ref, xbuf_ref, x_ref, act_ref, obuf_ref, wgb_ref, wub_ref,
                wdlb_ref, wdhb_ref, in_sem, out_sem):
    del bexp_ref
    i = pl.program_id(0)
    g = pl.program_id(1)
    n_up = act_ref.shape[0]
    last_g = pl.num_programs(1) - 1
    last_i = pl.num_programs(0) - 1
    n_used = nused_ref[0]
    used = i < n_used
    rb = x_ref.shape[0]
    sub = FFN_SUB
    th = wdlb_ref.shape[1]
    unroll = 4

    def sub_blocks(blk):
        return (nvalid_ref[blk] + (sub - 1)) // sub

    nsub = sub_blocks(i)

    def start_gather(src_ref, nrows):
        def body(j, carry):
            for u in range(unroll):
                jj = j * unroll + u
                src = pl.multiple_of(src_ref[jj], PACK_ROWS)
                dst = pl.multiple_of(jj * PACK_ROWS, PACK_ROWS)
                pltpu.make_async_copy(xp_ref.at[pl.ds(src, PACK_ROWS)], xbuf_ref.at[pl.ds(dst, PACK_ROWS)],
                                      in_sem).start(priority=u % 2)
            return carry

        lax.fori_loop(0, nrows // unroll, body, 0)

    def wait_gather(nrows):
        npk = pl.multiple_of(nrows * PACK_ROWS, PACK_ROWS)
        pltpu.make_async_copy(xp_ref.at[pl.ds(0, npk)], xbuf_ref.at[pl.ds(0, npk)], in_sem).wait()

    def start_scatter(nrows):
        def body(j, carry):
            for u in range(unroll):
                jj = j * unroll + u
                a = slot_ref[jj]

                @pl.when(a >= 0)
                def _():
                    src = pl.multiple_of(jj * PACK_ROWS, PACK_ROWS)
                    dst = pl.multiple_of(a * PACK_ROWS, PACK_ROWS)
                    pltpu.make_async_copy(obuf_ref.at[pl.ds(src, PACK_ROWS)], ys_ref.at[pl.ds(dst, PACK_ROWS)],
                                          out_sem).start(priority=u % 2)
            return carry

        lax.fori_loop(0, nrows // unroll, body, 0)

    def wait_scatter(blk):
        npk = pl.multiple_of(nvalid_ref[blk] * PACK_ROWS, PACK_ROWS)
        pltpu.make_async_copy(obuf_ref.at[pl.ds(0, npk)], ys_ref.at[pl.ds(0, npk)], out_sem).wait()

    @pl.when(jnp.logical_and(used, g == 0))
    def _():
        @pl.when(i == 0)
        def _():
            start_gather(src_cur_ref, nsub * sub)

        wait_gather(nsub * sub)
        for j in range(rb // sub):
            @pl.when(j < nsub)
            def _():
                los, his = _load_packed_rows(xbuf_ref, j * sub, sub)
                rows = slice(j * sub, (j + 1) * sub)
                for s in range(PACK_ROWS):
                    x_ref[rows, s * LANES:(s + 1) * LANES] = los[s].astype(BF16)
                    x_ref[rows, D_MODEL // 2 + s * LANES:D_MODEL // 2 + (s + 1) * LANES] = his[s].astype(BF16)

    @pl.when(jnp.logical_and(g == 1, i + 1 < n_used))
    def _():
        start_gather(src_nxt_ref, sub_blocks(i + 1) * sub)

    @pl.when(jnp.logical_and(used, g < n_up))
    def _():
        wgb_ref[...] = wg_ref[0].astype(BF16)
        wub_ref[...] = wu_ref[0].astype(BF16)

        def sub_body(j, carry):
            rows = pl.ds(pl.multiple_of(j * sub, sub), sub)
            x = x_ref[rows, :]
            hg = _dot(x, wgb_ref[...]) + bg_ref[0]
            hu = _dot(x, wub_ref[...]) + bu_ref[0]
            gate = jnp.minimum(hg, SWIGLU_LIMIT)
            up = jnp.clip(hu, -SWIGLU_LIMIT, SWIGLU_LIMIT)
            act_ref[g, rows, :] = ((up + 1.0) * (gate * _sigmoid(SWIGLU_ALPHA * gate))).astype(BF16)
            return carry

        lax.fori_loop(0, nsub, sub_body, 0)

    @pl.when(jnp.logical_and(used, g >= n_up))
    def _():
        @pl.when(jnp.logical_and(g == n_up, i > 0))
        def _():
            wait_scatter(i - 1)

        wdlb_ref[...] = wdl_ref[0].astype(BF16)
        wdhb_ref[...] = wdh_ref[0].astype(BF16)
        tile0 = (g - n_up) * (th // LANES)

        def sub_body(j, carry):
            row0 = pl.multiple_of(j * sub, sub)
            a = jnp.concatenate([act_ref[c, pl.ds(row0, sub), :] for c in range(n_up)], axis=-1)
            ylo = _dot(a, wdlb_ref[...]) + bdl_ref[0]
            yhi = _dot(a, wdhb_ref[...]) + bdh_ref[0]
            for s in range(th // LANES):
                lo = ylo[:, s * LANES:(s + 1) * LANES].astype(BF16).astype(F32)
                hi = yhi[:, s * LANES:(s + 1) * LANES].astype(BF16).astype(F32)
                word = pltpu.bitcast(hi, jnp.int32) | lax.shift_right_logical(pltpu.bitcast(lo, jnp.int32), 16)
                obuf_ref[pl.ds(row0 * PACK_ROWS + tile0 + s, sub, stride=PACK_ROWS), :] = word
            return carry

        lax.fori_loop(0, nsub, sub_body, 0)

        @pl.when(g == last_g)
        def _():
            start_scatter(nsub * sub)

    @pl.when(jnp.logical_and(i == last_i, g == last_g))
    def _():
        wait_scatter(jnp.minimum(i, n_used - 1))


def _expert_ffn(h_packed, slot_assign, blk_exp, blk_valid, n_used, w_up, b_up, w_down, b_down):
    rb, tf, th = FFN_ROWS, FFN_TF, FFN_TH
    nblk = slot_assign.shape[0] // rb
    n_up = D_FF // tf
    n_dn = D_MODEL // 2 // th
    n_assign = h_packed.shape[0] // PACK_ROWS * TOP_K
    slot_src = jnp.maximum(slot_assign, 0) // TOP_K * PACK_ROWS

    def clamp(i, nu):
        return jnp.minimum(i, nu[0] - 1)

    def up_sel(i, g, nu):
        return jnp.where(i < nu[0], jnp.minimum(g, n_up - 1), n_up - 1)

    def dn_sel(i, g, nu):
        return jnp.where(i < nu[0], jnp.clip(g - n_up, 0, n_dn - 1), n_dn - 1)

    grid_spec = pltpu.PrefetchScalarGridSpec(
        num_scalar_prefetch=3,
        grid=(nblk, n_up + n_dn),
        in_specs=[
            pl.BlockSpec((rb,), lambda i, g, be, nu, *_: (clamp(i, nu),), memory_space=pltpu.SMEM),
            pl.BlockSpec((rb,), lambda i, g, be, nu, *_: (clamp(i, nu),), memory_space=pltpu.SMEM),
            pl.BlockSpec((rb,), lambda i, g, be, nu, *_: (clamp(i + 1, nu),), memory_space=pltpu.SMEM),
            pl.BlockSpec((1, D_MODEL, tf), lambda i, g, be, nu, *_: (be[clamp(i, nu)], 0, up_sel(i, g, nu))),
            pl.BlockSpec((1, D_MODEL, tf), lambda i, g, be, nu, *_: (be[clamp(i, nu)], 0, n_up + up_sel(i, g, nu))),
            pl.BlockSpec((1, D_FF, th), lambda i, g, be, nu, *_: (be[clamp(i, nu)], 0, dn_sel(i, g, nu))),
            pl.BlockSpec((1, D_FF, th), lambda i, g, be, nu, *_: (be[clamp(i, nu)], 0, n_dn + dn_sel(i, g, nu))),
            pl.BlockSpec((1, 1, tf), lambda i, g, be, nu, *_: (be[clamp(i, nu)], 0, up_sel(i, g, nu))),
            pl.BlockSpec((1, 1, tf), lambda i, g, be, nu, *_: (be[clamp(i, nu)], 0, n_up + up_sel(i, g, nu))),
            pl.BlockSpec((1, 1, th), lambda i, g, be, nu, *_: (be[clamp(i, nu)], 0, dn_sel(i, g, nu))),
            pl.BlockSpec((1, 1, th), lambda i, g, be, nu, *_: (be[clamp(i, nu)], 0, n_dn + dn_sel(i, g, nu))),
            pl.BlockSpec(memory_space=pl.ANY),
        ],
        out_specs=pl.BlockSpec(memory_space=pl.ANY),
        scratch_shapes=[pltpu.VMEM((rb * PACK_ROWS, LANES), jnp.int32),
                        pltpu.VMEM((rb, D_MODEL), BF16),
                        pltpu.VMEM((n_up, rb, tf), BF16),
                        pltpu.VMEM((rb * PACK_ROWS, LANES), jnp.int32),
                        pltpu.VMEM((D_MODEL, tf), BF16), pltpu.VMEM((D_MODEL, tf), BF16),
                        pltpu.VMEM((D_FF, th), BF16), pltpu.VMEM((D_FF, th), BF16),
                        pltpu.SemaphoreType.DMA, pltpu.SemaphoreType.DMA],
    )
    assert n_up >= 2, "the next block's gather is issued at step 1, after this block's rows are unpacked at step 0"
    return pl.pallas_call(
        _ffn_kernel,
        grid_spec=grid_spec,
        out_shape=jax.ShapeDtypeStruct((n_assign * PACK_ROWS, LANES), jnp.int32),
        compiler_params=pltpu.CompilerParams(dimension_semantics=("arbitrary", "arbitrary"),
                                             vmem_limit_bytes=FFN_VMEM_LIMIT, disable_bounds_checks=True),
        name="expert_ffn",
    )(blk_exp, n_used, blk_valid, slot_assign, slot_src, slot_src, w_up, w_up, w_down, w_down,
      b_up.reshape(N_EXPERTS, 1, 2 * D_FF), b_up.reshape(N_EXPERTS, 1, 2 * D_FF),
      b_down.reshape(N_EXPERTS, 1, D_MODEL), b_down.reshape(N_EXPERTS, 1, D_MODEL), h_packed)


def _combine_kernel(gate_ref, h_ref, g_ref, b_ref, ys_ref, o_ref):
    tm = h_ref.shape[0]
    gates = gate_ref[...]
    cols = [None] * (2 * PACK_ROWS)
    for k in range(TOP_K):
        gk = gates[:, k:k + 1]
        for s in range(PACK_ROWS):
            word = ys_ref[pl.ds(k * PACK_ROWS + s, tm, stride=TOP_K * PACK_ROWS), :]
            lo = pltpu.bitcast(lax.shift_left(word, 16), F32)
            hi = pltpu.bitcast(word & jnp.int32(-65536), F32)
            for c, blk in ((s, lo), (PACK_ROWS + s, hi)):
                cols[c] = gk * blk if cols[c] is None else cols[c] + gk * blk
    z = DN_ALPHA * h_ref[...] + jnp.concatenate(cols, axis=-1)
    o_ref[...] = _layer_norm(z, g_ref[...], b_ref[...])


def _combine_ln(gates, h1, ys, g, b):
    n = h1.shape[0]
    tm = min(COMBINE_LN_TM, n)
    row = lambda w: pl.BlockSpec((tm, w), lambda i: (i, 0))
    full = lambda a, b_: pl.BlockSpec((a, b_), lambda i: (0, 0))
    return pl.pallas_call(
        _combine_kernel,
        grid=(n // tm,),
        in_specs=[row(LANES), row(D_MODEL), full(1, D_MODEL), full(1, D_MODEL),
                  pl.BlockSpec((tm * TOP_K * PACK_ROWS, LANES), lambda i: (i, 0))],
        out_specs=row(D_MODEL),
        out_shape=jax.ShapeDtypeStruct((n, D_MODEL), F32),
        compiler_params=_cparams(("parallel",)),
        name="combine_ln",
    )(gates, h1, g.reshape(1, -1), b.reshape(1, -1), ys)


def _layer(h, w_in, b_igate, b_fgate, conv_w, conv_b, rel_bias, beta_attn, beta_mlstm, w_out,
           ln1_g, ln1_b, w_router, b_router, w_up, b_up, w_down, b_down, ln2_g, ln2_b, batch, seq):
    n = batch * seq
    w_qkv = w_in[:, :3 * D_ATTN].astype(BF16)
    w_mix = w_in[:, 3 * D_ATTN:MAIN_COLS].astype(BF16)
    w_gate = jnp.zeros((D_MODEL, LANES), BF16).at[:, :2 * N_HEADS_M].set(w_in[:, MAIN_COLS:].astype(BF16))
    qkvs = _qkv_project(h, w_qkv, min(QKV_TM, n))
    proj_m = _project(h, w_mix, BF16, min(PROJ_TM, n), PROJ_TN)
    gates = _project(h, w_gate, F32, min(PROJ_TM, n), LANES)
    gate_bias = jnp.zeros((1, LANES), F32).at[0, :2 * N_HEADS_M].set(jnp.concatenate([b_igate, b_fgate]))

    outs, lses = [], []
    for (_, dil), qkv in zip(DILATED_CONFIGS, qkvs):
        o, l = _dilated_attention(qkv, _attn_bias_tables(rel_bias, dil), batch, seq, dil)
        outs.append(o)
        lses.append(l)
    y_attn = _attn_combine(outs, lses, beta_attn)
    y_mlstm = _mlstm(proj_m, gates, gate_bias, conv_w, conv_b, beta_mlstm, batch, seq)

    h1, h1_packed = _outproj_ln(y_attn, y_mlstm, h, w_out.astype(BF16), ln1_g, ln1_b)

    top_idx, top_gate, rank, counts = _router(h1, w_router, b_router)
    counts = counts[0, :N_EXPERTS]
    nb = (counts + FFN_ROWS - 1) // FFN_ROWS
    per = (counts + nb * FFN_SUB - 1) // jnp.maximum(nb * FFN_SUB, 1) * FFN_SUB
    blk_end = jnp.cumsum(nb)
    blk_start = blk_end - nb
    e_idx, r_idx = top_idx[:, :TOP_K], rank[:, :TOP_K]
    per_a = jnp.maximum(per[e_idx], 1)
    dest = ((blk_start[e_idx] + r_idx // per_a) * FFN_ROWS + r_idx % per_a).reshape(-1).astype(jnp.int32)
    nblk = n * TOP_K // FFN_ROWS + N_EXPERTS
    blk_id = jnp.arange(nblk, dtype=jnp.int32)
    blk_exp = jnp.minimum(jnp.sum(blk_end[None, :] <= blk_id[:, None], axis=1), N_EXPERTS - 1).astype(jnp.int32)
    n_used = blk_end[-1:].astype(jnp.int32)
    blk_valid = jnp.clip(counts[blk_exp] - (blk_id - blk_start[blk_exp]) * per[blk_exp], 0,
                         per[blk_exp]).astype(jnp.int32)
    slot_assign = jnp.full((nblk * FFN_ROWS,), -1, jnp.int32).at[dest].set(
        jnp.arange(n * TOP_K, dtype=jnp.int32), unique_indices=True)

    ys = _expert_ffn(h1_packed, slot_assign, blk_exp, blk_valid, n_used, w_up, b_up, w_down, b_down)
    return _combine_ln(top_gate, h1, ys, ln2_g, ln2_b)


def kernel(x, w_in, b_igate, b_fgate, conv_w, conv_b, rel_bias, beta_attn, beta_mlstm, w_out, ln1_g, ln1_b,
           w_router, b_router, w_up, b_up, w_down, b_down, ln2_g, ln2_b):
    batch, seq, d = x.shape
    h = x.reshape(batch * seq, d)
    for l in range(DEPTH):
        h = _layer(h, w_in[l], b_igate[l], b_fgate[l], conv_w[l], conv_b[l], rel_bias, beta_attn[l], beta_mlstm[l],
                   w_out[l], ln1_g[l], ln1_b[l], w_router[l], b_router[l], w_up[l], b_up[l], w_down[l], b_down[l],
                   ln2_g[l], ln2_b[l], batch, seq)
    return h.reshape(batch, seq, d)
```

```python
import functools
import math

import numpy as np
import jax
import jax.numpy as jnp
from jax import lax
from jax.experimental import pallas as pl
from jax.experimental.pallas import tpu as pltpu

F32 = jnp.float32
BF16 = jnp.bfloat16

D_MODEL = 2048
D_ATTN = 1024
HEAD_DIM_A = 64
N_HEADS_A = 16
DILATED_CONFIGS = ((128, 1), (512, 4), (2048, 16))
ATTN_BLOCK = 128
NUM_BUCKETS = 32
MAX_DISTANCE = 2048
D_MLSTM = 1024
N_HEADS_M = 4
HEAD_DIM_M = 256
CONV_K = 4
MLSTM_CHUNK = 128
MAIN_COLS = 3 * D_ATTN + 4 * D_MLSTM
N_EXPERTS = 32
TOP_K = 4
D_FF = 2048
SWIGLU_LIMIT = 7.0
SWIGLU_ALPHA = 1.702
DEPTH = 1
DN_ALPHA = (2 * DEPTH) ** 0.25
LN_EPS = 1e-5
HEAD_NORM_EPS = 1e-6
NEG_INF = -1e30

LANES = 128
VMEM_LIMIT = 48 * 1024 * 1024

PROJ_TM = 1024
PROJ_TN = 1024
COMBINE_TM = 512
OUTPROJ_TM = 512
ROUTER_TM = 512
QKV_TM = 512
ATTN_GROUP = 2
FFN_ROWS = 1024
FFN_SUB = 256
FFN_TF = 512
FFN_TH = 256
FFN_VMEM_LIMIT = 56 * 1024 * 1024
ROW_DMA_PRIORITY = 1
COMBINE_LN_TM = 256


def _cparams(sem, vmem=VMEM_LIMIT):
    return pltpu.CompilerParams(dimension_semantics=sem, vmem_limit_bytes=vmem)


def _dot(a, b):
    return jnp.dot(a, b, preferred_element_type=F32)


def _dot_f32_rhs(a_bf16, b_f32):
    hi = b_f32.astype(BF16)
    lo = (b_f32 - hi.astype(F32)).astype(BF16)
    return _dot(a_bf16, hi) + _dot(a_bf16, lo)


def _dot_f32_lhs(a_f32, b_bf16):
    hi = a_f32.astype(BF16)
    lo = (a_f32 - hi.astype(F32)).astype(BF16)
    return _dot(hi, b_bf16) + _dot(lo, b_bf16)


def _sigmoid(x):
    return 1.0 / (1.0 + jnp.exp(-x))


def _log_sigmoid(x):
    return jnp.minimum(x, 0.0) - jnp.log(1.0 + jnp.exp(-jnp.abs(x)))


def _proj_kernel(x_ref, w_ref, o_ref):
    o_ref[...] = _dot(x_ref[...].astype(BF16), w_ref[...]).astype(o_ref.dtype)


def _project(x, w, out_dtype, tm, tn):
    m, k = x.shape
    n = w.shape[1]
    return pl.pallas_call(
        _proj_kernel,
        grid=(m // tm, n // tn),
        in_specs=[pl.BlockSpec((tm, k), lambda i, j: (i, 0)),
                  pl.BlockSpec((k, tn), lambda i, j: (0, j))],
        out_specs=pl.BlockSpec((tm, tn), lambda i, j: (i, j)),
        out_shape=jax.ShapeDtypeStruct((m, n), out_dtype),
        compiler_params=_cparams(("parallel", "parallel")),
        name="in_proj",
    )(x, w)


def _qkv_proj_kernel(x_ref, w_ref, *refs):
    o_refs, r_ref = refs[:-1], refs[-1]
    res = _dot(x_ref[...].astype(BF16), w_ref[...])
    ntile, tm, _ = r_ref.shape
    wid = ntile * LANES
    for c in range(ntile):
        r_ref[c] = res[:, c * LANES:(c + 1) * LANES]
    for (_, dil), o_ref in zip(DILATED_CONFIGS, o_refs):
        if dil == 1:
            o_ref[...] = res.astype(o_ref.dtype)
        else:
            for r in range(dil):
                for c in range(ntile):
                    col = r * wid + c * LANES
                    o_ref[:, col:col + LANES] = r_ref[c, pl.ds(r, tm // dil, stride=dil), :].astype(o_ref.dtype)


def _qkv_project(x, w, tm):
    m, k = x.shape
    wid = w.shape[1]
    dils = [d for _, d in DILATED_CONFIGS]
    return pl.pallas_call(
        _qkv_proj_kernel,
        grid=(m // tm,),
        in_specs=[pl.BlockSpec((tm, k), lambda i: (i, 0)),
                  pl.BlockSpec((k, wid), lambda i: (0, 0), pipeline_mode=pl.Buffered(1))],
        out_specs=[pl.BlockSpec((tm // d, d * wid), lambda i: (i, 0)) for d in dils],
        out_shape=[jax.ShapeDtypeStruct((m // d, d * wid), BF16) for d in dils],
        scratch_shapes=[pltpu.VMEM((wid // LANES, tm, LANES), F32)],
        compiler_params=_cparams(("parallel",), 56 * 1024 * 1024),
        name="qkv_proj",
    )(x, w)


def _attn_kernel(q_ref, kp_ref, kc_ref, vp_ref, vc_ref, bias_ref, o_ref, lse_ref):
    n = pl.program_id(2)
    tab = jnp.minimum(n, 1)
    lse_ref[...] = jnp.zeros(lse_ref.shape, F32)
    grp, dh, nk = ATTN_GROUP, HEAD_DIM_A, 2 * ATTN_BLOCK
    wid = grp * dh
    lane_head = lax.broadcasted_iota(jnp.int32, (nk, wid), 1) // dh
    zero = jnp.zeros((nk, wid), BF16)
    ones_bd = jnp.concatenate([jnp.where(lane_head == j, 1.0, 0.0).astype(BF16) for j in range(grp)], axis=0)
    for g in range(N_HEADS_A // grp):
        cols = slice(g * wid, (g + 1) * wid)
        q = q_ref[0, :, cols] * (dh ** -0.5)
        kslab = jnp.concatenate([kp_ref[0, :, cols], kc_ref[0, :, cols]], axis=0)
        vslab = jnp.concatenate([vp_ref[0, :, cols], vc_ref[0, :, cols]], axis=0)
        k_bd = jnp.concatenate([jnp.where(lane_head == j, kslab, zero) for j in range(grp)], axis=0)
        v_bd = jnp.concatenate([jnp.where(lane_head == j, vslab, zero) for j in range(grp)], axis=0)
        s_all = lax.dot_general(q, k_bd, (((1,), (1,)), ((), ())), preferred_element_type=F32)
        ps, ms = [], []
        for j in range(grp):
            s = s_all[:, j * nk:(j + 1) * nk] + bias_ref[tab, g * grp + j]
            m = jnp.max(s, axis=-1, keepdims=True)
            ps.append(jnp.exp(s - m).astype(BF16))
            ms.append(m)
        p_all = jnp.concatenate(ps, axis=-1)
        res = _dot(p_all, jnp.concatenate([v_bd, ones_bd], axis=-1))
        den = res[:, wid:]
        o_ref[0, :, cols] = (res[:, :wid] / den).astype(o_ref.dtype)
        for j in range(grp):
            h = g * grp + j
            lse_ref[0, :, h:h + 1] = ms[j] + jnp.log(den[:, j * dh:j * dh + 1])


def _attn_bias_tables(rel_bias, dil):
    blk = ATTN_BLOCK
    period = 3 * blk
    k = np.arange(period)
    valid = k <= blk
    dist = np.where(valid, blk - k, 0) * dil
    max_exact = NUM_BUCKETS // 2
    d_f = np.maximum(dist, 1).astype(np.float32)
    large = max_exact + (np.log(d_f / np.float32(max_exact)) / np.float32(math.log(MAX_DISTANCE / max_exact))
                         * np.float32(NUM_BUCKETS - max_exact)).astype(np.int32)
    large = np.minimum(large, NUM_BUCKETS - 1)
    bucket = np.where(dist < max_exact, dist, large).astype(np.int32)
    w = jnp.where(jnp.asarray(valid)[None, :], rel_bias[jnp.asarray(bucket)].T.astype(F32), NEG_INF)
    t1 = jnp.tile(w, (1, blk))[:, :blk * (period - 1)].reshape(N_HEADS_A, blk, period - 1)[:, :, :2 * blk]
    has_prev = np.arange(2 * blk)[None, None, :] >= blk
    t0 = jnp.where(jnp.asarray(has_prev), t1, NEG_INF)
    return jnp.stack([t0, t1])


def _dilated_attention(qkv, bias_tab, batch, seq, dil):
    blk = ATTN_BLOCK
    l = seq // dil
    nb = l // blk
    ncb = 3
    pv = qkv.reshape(batch, l, dil * ncb * D_ATTN)

    def cur(c):
        return pl.BlockSpec((1, blk, D_ATTN), lambda b, r, n: (b, n, r * ncb + c))

    def prev(c):
        return pl.BlockSpec((1, blk, D_ATTN), lambda b, r, n: (b, jnp.maximum(n - 1, 0), r * ncb + c))

    o, lse = pl.pallas_call(
        _attn_kernel,
        grid=(batch, dil, nb),
        in_specs=[cur(0), prev(1), cur(1), prev(2), cur(2),
                  pl.BlockSpec((2, N_HEADS_A, blk, 2 * blk), lambda b, r, n: (0, 0, 0, 0))],
        out_specs=[pl.BlockSpec((1, blk, D_ATTN), lambda b, r, n: (b, n, r)),
                   pl.BlockSpec((1, blk, LANES), lambda b, r, n: (b, n, r))],
        out_shape=[jax.ShapeDtypeStruct((batch, l, dil * D_ATTN), BF16),
                   jax.ShapeDtypeStruct((batch, l, dil * LANES), F32)],
        compiler_params=_cparams(("parallel", "parallel", "arbitrary")),
        name=f"dilated_attn_d{dil}",
    )(pv, pv, pv, pv, pv, bias_tab)
    return o.reshape(batch * seq, D_ATTN), lse.reshape(batch * seq, LANES)


def _attn_combine_kernel(o1_ref, o2_ref, o3_ref, l1_ref, l2_ref, l3_ref, e_ref, et_ref, beta_ref, y_ref):
    lses = [l1_ref[...], l2_ref[...], l3_ref[...]]
    outs = [o1_ref, o2_ref, o3_ref]
    mx = jnp.maximum(jnp.maximum(lses[0], lses[1]), lses[2])
    ws = [jnp.exp(l - mx) for l in lses]
    tot = ws[0] + ws[1] + ws[2]
    e = e_ref[...]
    acc = None
    for w, o_ref in zip(ws, outs):
        term = _dot_f32_lhs(w / tot, e) * o_ref[...].astype(F32)
        acc = term if acc is None else acc + term
    ss = _dot_f32_lhs(acc * acc, et_ref[...])
    inv = lax.rsqrt(ss * (1.0 / HEAD_DIM_A) + HEAD_NORM_EPS)
    y_ref[...] = (acc * _dot_f32_lhs(inv, e) * beta_ref[...]).astype(y_ref.dtype)


def _attn_combine(os_, lses, beta_attn):
    n = os_[0].shape[0]
    tm = min(COMBINE_TM, n)
    head_of_lane = np.arange(D_ATTN) // HEAD_DIM_A
    e = (np.arange(LANES)[:, None] == head_of_lane[None, :]).astype(np.float32)
    e_j = jnp.asarray(e, BF16)
    et_j = jnp.asarray(e.T, BF16)
    row = lambda w: pl.BlockSpec((tm, w), lambda i: (i, 0))
    full = lambda a, b: pl.BlockSpec((a, b), lambda i: (0, 0))
    return pl.pallas_call(
        _attn_combine_kernel,
        grid=(n // tm,),
        in_specs=[row(D_ATTN)] * 3 + [row(LANES)] * 3 + [full(LANES, D_ATTN), full(D_ATTN, LANES), full(1, D_ATTN)],
        out_specs=row(D_ATTN),
        out_shape=jax.ShapeDtypeStruct((n, D_ATTN), BF16),
        compiler_params=_cparams(("parallel",)),
        name="attn_combine",
    )(*os_, *lses, e_j, et_j, beta_attn.reshape(1, D_ATTN).astype(F32))


def _mlstm_kernel(qp_ref, kp_ref, qprev_ref, kprev_ref, v_ref, og_ref, g_ref, gb_ref, cw_ref, cb_ref,
                  beta_ref, y_ref, c_ref, n_ref, m_ref):
    step = pl.program_id(1)
    ch = MLSTM_CHUNK
    dh = HEAD_DIM_M

    @pl.when(step == 0)
    def _():
        c_ref[...] = jnp.zeros(c_ref.shape, F32)
        n_ref[...] = jnp.zeros(n_ref.shape, F32)
        m_ref[...] = jnp.zeros(m_ref.shape, F32)

    def conv_silu(x_ref, prev_ref, coff):
        x = x_ref[0].astype(F32)
        p = jnp.where(step > 0, prev_ref[0].astype(F32), 0.0)
        xe = jnp.concatenate([p, x], axis=0)
        npad = p.shape[0]
        cols = slice(coff, coff + D_MLSTM)
        acc = cb_ref[:, cols] + cw_ref[CONV_K - 1:CONV_K, cols] * x
        for s in range(1, CONV_K):
            shifted = pltpu.roll(xe, s, 0)[npad:]
            acc = acc + cw_ref[CONV_K - 1 - s:CONV_K - s, cols] * shifted
        return acc * _sigmoid(acc)

    qf = conv_silu(qp_ref, qprev_ref, 0)
    kf = conv_silu(kp_ref, kprev_ref, D_MLSTM) * (dh ** -0.5)
    qb = qf.astype(BF16)
    kb = kf.astype(BF16)

    g = g_ref[0] + gb_ref[...]
    gt = g.T
    row_i = lax.broadcasted_iota(jnp.int32, (ch, ch), 0)
    col_i = lax.broadcasted_iota(jnp.int32, (ch, ch), 1)
    causal = row_i >= col_i
    tri = jnp.where(causal, 1.0, 0.0).astype(BF16)
    upp = jnp.where(row_i <= col_i, 1.0, 0.0).astype(BF16)
    b_cols = _dot_f32_rhs(tri, _log_sigmoid(g))
    b_rows = _dot_f32_lhs(_log_sigmoid(gt), upp)

    for h in range(N_HEADS_M):
        hs = slice(h * dh, (h + 1) * dh)
        fi = N_HEADS_M + h
        i_row = gt[h:h + 1, :]
        i_col = g[:, h:h + 1]
        b_row = b_rows[fi:fi + 1, :]
        b_col = b_cols[:, fi:fi + 1]
        m_prev = m_ref[h][:, 0:1]
        q_h, k_h = qb[:, hs], kb[:, hs]
        v_h = v_ref[0, :, hs]

        dmat = jnp.where(causal, b_col - b_row + i_row, NEG_INF)
        m_inter = b_col + m_prev
        m_t = jnp.maximum(m_inter, jnp.max(dmat, axis=-1, keepdims=True))
        w = jnp.exp(dmat - m_t) * lax.dot_general(q_h, k_h, (((1,), (1,)), ((), ())),
                                                  preferred_element_type=F32)
        decay = jnp.exp(m_inter - m_t)
        c_old = c_ref[h]
        inter = lax.dot_general(q_h, c_old.astype(BF16), (((1,), (1,)), ((), ())), preferred_element_type=F32)
        num = _dot(w.astype(BF16), v_h) + decay * inter
        n_old = n_ref[h]
        den = jnp.sum(w, axis=-1, keepdims=True) + decay * jnp.sum(qf[:, hs] * n_old, axis=-1, keepdims=True)
        hh = num / jnp.maximum(jnp.abs(den), jnp.exp(-m_t))

        g_last = b_col[ch - 1:ch, :]
        a_row = g_last - b_row + i_row
        a_col = g_last - b_col + i_col
        m_new = jnp.maximum(g_last + m_prev, jnp.max(a_row, axis=-1, keepdims=True))
        carry = jnp.exp(g_last + m_prev - m_new)
        wa_col = jnp.exp(a_col - m_new)
        wv = (wa_col * v_h.astype(F32)).astype(BF16)
        c_ref[h] = carry * c_old + lax.dot_general(wv, k_h, (((0,), (0,)), ((), ())), preferred_element_type=F32)
        n_ref[h] = carry * n_old + jnp.sum(wa_col * kf[:, hs], axis=0, keepdims=True)
        m_ref[h] = jnp.broadcast_to(m_new, (1, LANES))

        gated = _sigmoid(og_ref[0, :, hs].astype(F32)) * hh
        ms = jnp.sum(gated * gated, axis=-1, keepdims=True) * (1.0 / dh)
        y_ref[0, :, hs] = (gated * lax.rsqrt(ms + HEAD_NORM_EPS) * beta_ref[:, hs]).astype(y_ref.dtype)


def _mlstm(proj, gates, gate_bias, conv_w, conv_b, beta_mlstm, batch, seq):
    ch = MLSTM_CHUNK
    nchunk = seq // ch
    pv = proj.reshape(batch, seq, 4 * D_MLSTM)
    gv = gates.reshape(batch, seq, LANES)
    prev_rows = 16
    cb0 = 0

    def cur(c):
        return pl.BlockSpec((1, ch, D_MLSTM), lambda b, n: (b, n, c))

    def prev(c):
        per = ch // prev_rows
        return pl.BlockSpec((1, prev_rows, D_MLSTM), lambda b, n: (b, jnp.maximum(n * per - 1, 0), c))

    const = lambda a, b_: pl.BlockSpec((a, b_), lambda b, n: (0, 0))
    y = pl.pallas_call(
        _mlstm_kernel,
        grid=(batch, nchunk),
        in_specs=[cur(cb0), cur(cb0 + 1), prev(cb0), prev(cb0 + 1), cur(cb0 + 2), cur(cb0 + 3),
                  pl.BlockSpec((1, ch, LANES), lambda b, n: (b, n, 0)),
                  const(1, LANES), const(CONV_K, 2 * D_MLSTM), const(1, 2 * D_MLSTM), const(1, D_MLSTM)],
        out_specs=pl.BlockSpec((1, ch, D_MLSTM), lambda b, n: (b, n, 0)),
        out_shape=jax.ShapeDtypeStruct((batch, seq, D_MLSTM), BF16),
        scratch_shapes=[pltpu.VMEM((N_HEADS_M, HEAD_DIM_M, HEAD_DIM_M), F32),
                        pltpu.VMEM((N_HEADS_M, 1, HEAD_DIM_M), F32),
                        pltpu.VMEM((N_HEADS_M, 1, LANES), F32)],
        compiler_params=_cparams(("parallel", "arbitrary")),
        name="mlstm",
    )(pv, pv, pv, pv, pv, pv, gv, gate_bias, conv_w.astype(F32), conv_b.reshape(1, -1).astype(F32),
      beta_mlstm.reshape(1, D_MLSTM).astype(F32))
    return y.reshape(batch * seq, D_MLSTM)


def _layer_norm(z, g, b):
    mu = jnp.mean(z, axis=-1, keepdims=True)
    zc = z - mu
    var = jnp.mean(zc * zc, axis=-1, keepdims=True)
    return zc * lax.rsqrt(var + LN_EPS) * g + b


PACK_ROWS = D_MODEL // (2 * LANES)


def _store_packed_rows(dst_ref, x, first=0):
    rows = x.shape[0]
    half = D_MODEL // 2
    for s in range(PACK_ROWS):
        lo = x[:, s * LANES:(s + 1) * LANES].astype(BF16).astype(F32)
        hi = x[:, half + s * LANES:half + (s + 1) * LANES].astype(BF16).astype(F32)
        word = pltpu.bitcast(hi, jnp.int32) | lax.shift_right_logical(pltpu.bitcast(lo, jnp.int32), 16)
        dst_ref[pl.ds(first * PACK_ROWS + s, rows, stride=PACK_ROWS), :] = word


def _load_packed_rows(src_ref, first, rows):
    los, his = [], []
    for s in range(PACK_ROWS):
        word = src_ref[pl.ds(first * PACK_ROWS + s, rows, stride=PACK_ROWS), :]
        los.append(pltpu.bitcast(lax.shift_left(word, 16), F32))
        his.append(pltpu.bitcast(word & jnp.int32(-65536), F32))
    return los, his


def _outproj_kernel(ya_ref, ym_ref, x_ref, w_ref, g_ref, b_ref, h_ref, hp_ref):
    y = _dot(ya_ref[...], w_ref[0:D_ATTN, :]) + _dot(ym_ref[...], w_ref[D_ATTN:D_MODEL, :])
    h = _layer_norm(DN_ALPHA * x_ref[...] + y, g_ref[...], b_ref[...])
    h_ref[...] = h
    _store_packed_rows(hp_ref, h)


def _outproj_ln(ya, ym, x, w_out, g, b):
    n = x.shape[0]
    tm = min(OUTPROJ_TM, n)
    row = lambda w: pl.BlockSpec((tm, w), lambda i: (i, 0))
    full = lambda a, b_: pl.BlockSpec((a, b_), lambda i: (0, 0))
    return pl.pallas_call(
        _outproj_kernel,
        grid=(n // tm,),
        in_specs=[row(D_ATTN), row(D_MLSTM), row(D_MODEL), full(D_MODEL, D_MODEL), full(1, D_MODEL), full(1, D_MODEL)],
        out_specs=[row(D_MODEL), pl.BlockSpec((tm * PACK_ROWS, LANES), lambda i: (i, 0))],
        out_shape=[jax.ShapeDtypeStruct((n, D_MODEL), F32),
                   jax.ShapeDtypeStruct((n * PACK_ROWS, LANES), jnp.int32)],
        compiler_params=_cparams(("parallel",)),
        name="out_proj_ln",
    )(ya, ym, x, w_out, g.reshape(1, -1), b.reshape(1, -1))


def _router_kernel(h_ref, whi_ref, wlo_ref, b_ref, tri_ref, idx_ref, gate_ref, rank_ref, cnt_ref, carry_ref):
    i = pl.program_id(0)

    @pl.when(i == 0)
    def _():
        carry_ref[...] = jnp.zeros(carry_ref.shape, F32)

    x = h_ref[...]
    xhi = x.astype(BF16)
    xlo = (x - xhi.astype(F32)).astype(BF16)
    logits = _dot(xhi, whi_ref[...]) + _dot(xhi, wlo_ref[...]) + _dot(xlo, whi_ref[...]) + b_ref[...]
    tm = logits.shape[0]
    lane = lax.broadcasted_iota(jnp.int32, (tm, LANES), 1)
    lane_f = lane.astype(F32)
    vals = jnp.where(lane < N_EXPERTS, logits, NEG_INF)

    sels, tops = [], []
    for _ in range(TOP_K):
        mx = jnp.max(vals, axis=-1, keepdims=True)
        first = jnp.min(jnp.where(vals == mx, lane_f, float(LANES)), axis=-1, keepdims=True)
        sel = lane_f == first
        sels.append(sel)
        tops.append((mx, first))
        vals = jnp.where(sel, 2.0 * NEG_INF, vals)

    exps = [jnp.exp(mx - tops[0][0]) for mx, _ in tops]
    tot = exps[0] + exps[1] + exps[2] + exps[3]

    onehot = jnp.zeros((tm, LANES), F32)
    for sel in sels:
        onehot = jnp.where(sel, 1.0, onehot)
    before = _dot(tri_ref[...], onehot.astype(BF16)) + carry_ref[...]

    idx_out = jnp.zeros((tm, LANES), F32)
    gate_out = jnp.zeros((tm, LANES), F32)
    rank_out = jnp.zeros((tm, LANES), F32)
    for k in range(TOP_K):
        rank_k = jnp.sum(jnp.where(sels[k], before, 0.0), axis=-1, keepdims=True)
        idx_out = jnp.where(lane == k, tops[k][1], idx_out)
        gate_out = jnp.where(lane == k, exps[k] / tot, gate_out)
        rank_out = jnp.where(lane == k, rank_k, rank_out)
    idx_ref[...] = idx_out.astype(jnp.int32)
    gate_ref[...] = gate_out
    rank_ref[...] = rank_out.astype(jnp.int32)

    carry = carry_ref[...] + jnp.sum(onehot, axis=0, keepdims=True)
    carry_ref[...] = carry
    cnt_ref[...] = carry.astype(jnp.int32)


def _router(h1, w_router, b_router):
    n = h1.shape[0]
    tm = min(ROUTER_TM, n)
    wpad = jnp.zeros((D_MODEL, LANES), F32).at[:, :N_EXPERTS].set(w_router)
    whi = wpad.astype(BF16)
    wlo = (wpad - whi.astype(F32)).astype(BF16)
    bpad = jnp.zeros((1, LANES), F32).at[0, :N_EXPERTS].set(b_router)
    tri = jnp.asarray(np.tril(np.ones((tm, tm), np.float32), -1), BF16)
    row = lambda w: pl.BlockSpec((tm, w), lambda i: (i, 0))
    full = lambda a, b_: pl.BlockSpec((a, b_), lambda i: (0, 0))
    return pl.pallas_call(
        _router_kernel,
        grid=(n // tm,),
        in_specs=[row(D_MODEL), full(D_MODEL, LANES), full(D_MODEL, LANES), full(1, LANES), full(tm, tm)],
        out_specs=[row(LANES), row(LANES), row(LANES), full(1, LANES)],
        out_shape=[jax.ShapeDtypeStruct((n, LANES), jnp.int32), jax.ShapeDtypeStruct((n, LANES), F32),
                   jax.ShapeDtypeStruct((n, LANES), jnp.int32), jax.ShapeDtypeStruct((1, LANES), jnp.int32)],
        scratch_shapes=[pltpu.VMEM((1, LANES), F32)],
        compiler_params=_cparams(("arbitrary",)),
        name="router",
    )(h1, whi, wlo, bpad, tri)


def _ffn_kernel(bexp_ref, nused_ref, nvalid_ref, slot_ref, src_cur_ref, src_nxt_ref, wg_ref, wu_ref, wdl_ref, wdh_ref,
                bg_ref, bu_ref, bdl_ref, bdh_ref, xp_ref, ys_ref, xbuf_ref, x_ref, act_ref, obuf_ref, wgb_ref, wub_ref,
                wdlb_ref, wdhb_ref, in_sem, out_sem):
    del bexp_ref
    i = pl.program_id(0)
    g = pl.program_id(1)
    n_up = act_ref.shape[0]
    last_g = pl.num_programs(1) - 1
    last_i = pl.num_programs(0) - 1
    n_used = nused_ref[0]
    used = i < n_used
    rb = x_ref.shape[0]
    sub = FFN_SUB
    th = wdlb_ref.shape[1]
    unroll = 4

    def sub_blocks(blk):
        return (nvalid_ref[blk] + (sub - 1)) // sub

    nsub = sub_blocks(i)

    def start_gather(src_ref, nrows):
        def body(j, carry):
            for u in range(unroll):
                jj = j * unroll + u
                src = pl.multiple_of(src_ref[jj], PACK_ROWS)
                dst = pl.multiple_of(jj * PACK_ROWS, PACK_ROWS)
                pltpu.make_async_copy(xp_ref.at[pl.ds(src, PACK_ROWS)], xbuf_ref.at[pl.ds(dst, PACK_ROWS)],
                                      in_sem).start(priority=ROW_DMA_PRIORITY)
            return carry

        lax.fori_loop(0, nrows // unroll, body, 0)

    def wait_gather(nrows):
        npk = pl.multiple_of(nrows * PACK_ROWS, PACK_ROWS)
        pltpu.make_async_copy(xp_ref.at[pl.ds(0, npk)], xbuf_ref.at[pl.ds(0, npk)], in_sem).wait()

    def start_scatter(nrows):
        def body(j, carry):
            for u in range(unroll):
                jj = j * unroll + u
                a = slot_ref[jj]

                @pl.when(a >= 0)
                def _():
                    src = pl.multiple_of(jj * PACK_ROWS, PACK_ROWS)
                    dst = pl.multiple_of(a * PACK_ROWS, PACK_ROWS)
                    pltpu.make_async_copy(obuf_ref.at[pl.ds(src, PACK_ROWS)], ys_ref.at[pl.ds(dst, PACK_ROWS)],
                                          out_sem).start(priority=ROW_DMA_PRIORITY)
            return carry

        lax.fori_loop(0, nrows // unroll, body, 0)

    def wait_scatter(blk):
        npk = pl.multiple_of(nvalid_ref[blk] * PACK_ROWS, PACK_ROWS)
        pltpu.make_async_copy(obuf_ref.at[pl.ds(0, npk)], ys_ref.at[pl.ds(0, npk)], out_sem).wait()

    @pl.when(jnp.logical_and(used, g == 0))
    def _():
        @pl.when(i == 0)
        def _():
            start_gather(src_cur_ref, nsub * sub)

        wait_gather(nsub * sub)
        for j in range(rb // sub):
            @pl.when(j < nsub)
            def _():
                los, his = _load_packed_rows(xbuf_ref, j * sub, sub)
                rows = slice(j * sub, (j + 1) * sub)
                for s in range(PACK_ROWS):
                    x_ref[rows, s * LANES:(s + 1) * LANES] = los[s].astype(BF16)
                    x_ref[rows, D_MODEL // 2 + s * LANES:D_MODEL // 2 + (s + 1) * LANES] = his[s].astype(BF16)

    @pl.when(jnp.logical_and(g == 1, i + 1 < n_used))
    def _():
        start_gather(src_nxt_ref, sub_blocks(i + 1) * sub)

    @pl.when(jnp.logical_and(used, g < n_up))
    def _():
        wgb_ref[...] = wg_ref[0].astype(BF16)
        wub_ref[...] = wu_ref[0].astype(BF16)

        def sub_body(j, carry):
            rows = pl.ds(pl.multiple_of(j * sub, sub), sub)
            x = x_ref[rows, :]
            hg = _dot(x, wgb_ref[...]) + bg_ref[0]
            hu = _dot(x, wub_ref[...]) + bu_ref[0]
            gate = jnp.minimum(hg, SWIGLU_LIMIT)
            up = jnp.clip(hu, -SWIGLU_LIMIT, SWIGLU_LIMIT)
            act_ref[g, rows, :] = ((up + 1.0) * (gate * _sigmoid(SWIGLU_ALPHA * gate))).astype(BF16)
            return carry

        lax.fori_loop(0, nsub, sub_body, 0)

    @pl.when(jnp.logical_and(used, g >= n_up))
    def _():
        @pl.when(jnp.logical_and(g == n_up, i > 0))
        def _():
            wait_scatter(i - 1)

        wdlb_ref[...] = wdl_ref[0].astype(BF16)
        wdhb_ref[...] = wdh_ref[0].astype(BF16)
        tile0 = (g - n_up) * (th // LANES)

        def sub_body(j, carry):
            row0 = pl.multiple_of(j * sub, sub)
            a = jnp.concatenate([act_ref[c, pl.ds(row0, sub), :] for c in range(n_up)], axis=-1)
            ylo = _dot(a, wdlb_ref[...]) + bdl_ref[0]
            yhi = _dot(a, wdhb_ref[...]) + bdh_ref[0]
            for s in range(th // LANES):
                lo = ylo[:, s * LANES:(s + 1) * LANES].astype(BF16).astype(F32)
                hi = yhi[:, s * LANES:(s + 1) * LANES].astype(BF16).astype(F32)
                word = pltpu.bitcast(hi, jnp.int32) | lax.shift_right_logical(pltpu.bitcast(lo, jnp.int32), 16)
                obuf_ref[pl.ds(row0 * PACK_ROWS + tile0 + s, sub, stride=PACK_ROWS), :] = word
            return carry

        lax.fori_loop(0, nsub, sub_body, 0)

        @pl.when(g == last_g)
        def _():
            start_scatter(nsub * sub)

    @pl.when(jnp.logical_and(i == last_i, g == last_g))
    def _():
        wait_scatter(jnp.minimum(i, n_used - 1))


def _expert_ffn(h_packed, slot_assign, blk_exp, blk_valid, n_used, w_up, b_up, w_down, b_down):
    rb, tf, th = FFN_ROWS, FFN_TF, FFN_TH
    nblk = slot_assign.shape[0] // rb
    n_up = D_FF // tf
    n_dn = D_MODEL // 2 // th
    n_assign = h_packed.shape[0] // PACK_ROWS * TOP_K
    slot_src = jnp.maximum(slot_assign, 0) // TOP_K * PACK_ROWS

    def clamp(i, nu):
        return jnp.minimum(i, nu[0] - 1)

    def up_sel(i, g, nu):
        return jnp.where(i < nu[0], jnp.minimum(g, n_up - 1), n_up - 1)

    def dn_sel(i, g, nu):
        return jnp.where(i < nu[0], jnp.clip(g - n_up, 0, n_dn - 1), n_dn - 1)

    grid_spec = pltpu.PrefetchScalarGridSpec(
        num_scalar_prefetch=3,
        grid=(nblk, n_up + n_dn),
        in_specs=[
            pl.BlockSpec((rb,), lambda i, g, be, nu, *_: (clamp(i, nu),), memory_space=pltpu.SMEM),
            pl.BlockSpec((rb,), lambda i, g, be, nu, *_: (clamp(i, nu),), memory_space=pltpu.SMEM),
            pl.BlockSpec((rb,), lambda i, g, be, nu, *_: (clamp(i + 1, nu),), memory_space=pltpu.SMEM),
            pl.BlockSpec((1, D_MODEL, tf), lambda i, g, be, nu, *_: (be[clamp(i, nu)], 0, up_sel(i, g, nu))),
            pl.BlockSpec((1, D_MODEL, tf), lambda i, g, be, nu, *_: (be[clamp(i, nu)], 0, n_up + up_sel(i, g, nu))),
            pl.BlockSpec((1, D_FF, th), lambda i, g, be, nu, *_: (be[clamp(i, nu)], 0, dn_sel(i, g, nu))),
            pl.BlockSpec((1, D_FF, th), lambda i, g, be, nu, *_: (be[clamp(i, nu)], 0, n_dn + dn_sel(i, g, nu))),
            pl.BlockSpec((1, 1, tf), lambda i, g, be, nu, *_: (be[clamp(i, nu)], 0, up_sel(i, g, nu))),
            pl.BlockSpec((1, 1, tf), lambda i, g, be, nu, *_: (be[clamp(i, nu)], 0, n_up + up_sel(i, g, nu))),
            pl.BlockSpec((1, 1, th), lambda i, g, be, nu, *_: (be[clamp(i, nu)], 0, dn_sel(i, g, nu))),
            pl.BlockSpec((1, 1, th), lambda i, g, be, nu, *_: (be[clamp(i, nu)], 0, n_dn + dn_sel(i, g, nu))),
            pl.BlockSpec(memory_space=pl.ANY),
        ],
        out_specs=pl.BlockSpec(memory_space=pl.ANY),
        scratch_shapes=[pltpu.VMEM((rb * PACK_ROWS, LANES), jnp.int32),
                        pltpu.VMEM((rb, D_MODEL), BF16),
                        pltpu.VMEM((n_up, rb, tf), BF16),
                        pltpu.VMEM((rb * PACK_ROWS, LANES), jnp.int32),
                        pltpu.VMEM((D_MODEL, tf), BF16), pltpu.VMEM((D_MODEL, tf), BF16),
                        pltpu.VMEM((D_FF, th), BF16), pltpu.VMEM((D_FF, th), BF16),
                        pltpu.SemaphoreType.DMA, pltpu.SemaphoreType.DMA],
    )
    assert n_up >= 2, "the next block's gather is issued at step 1, after this block's rows are unpacked at step 0"
    return pl.pallas_call(
        _ffn_kernel,
        grid_spec=grid_spec,
        out_shape=jax.ShapeDtypeStruct((n_assign * PACK_ROWS, LANES), jnp.int32),
        compiler_params=pltpu.CompilerParams(dimension_semantics=("arbitrary", "arbitrary"),
                                             vmem_limit_bytes=FFN_VMEM_LIMIT, disable_bounds_checks=True),
        name="expert_ffn",
    )(blk_exp, n_used, blk_valid, slot_assign, slot_src, slot_src, w_up, w_up, w_down, w_down,
      b_up.reshape(N_EXPERTS, 1, 2 * D_FF), b_up.reshape(N_EXPERTS, 1, 2 * D_FF),
      b_down.reshape(N_EXPERTS, 1, D_MODEL), b_down.reshape(N_EXPERTS, 1, D_MODEL), h_packed)


def _combine_kernel(gate_ref, h_ref, g_ref, b_ref, ys_ref, o_ref):
    tm = h_ref.shape[0]
    gates = gate_ref[...]
    cols = [None] * (2 * PACK_ROWS)
    for k in range(TOP_K):
        gk = gates[:, k:k + 1]
        for s in range(PACK_ROWS):
            word = ys_ref[pl.ds(k * PACK_ROWS + s, tm, stride=TOP_K * PACK_ROWS), :]
            lo = pltpu.bitcast(lax.shift_left(word, 16), F32)
            hi = pltpu.bitcast(word & jnp.int32(-65536), F32)
            for c, blk in ((s, lo), (PACK_ROWS + s, hi)):
                cols[c] = gk * blk if cols[c] is None else cols[c] + gk * blk
    z = DN_ALPHA * h_ref[...] + jnp.concatenate(cols, axis=-1)
    o_ref[...] = _layer_norm(z, g_ref[...], b_ref[...])


def _combine_ln(gates, h1, ys, g, b):
    n = h1.shape[0]
    tm = min(COMBINE_LN_TM, n)
    row = lambda w: pl.BlockSpec((tm, w), lambda i: (i, 0))
    full = lambda a, b_: pl.BlockSpec((a, b_), lambda i: (0, 0))
    return pl.pallas_call(
        _combine_kernel,
        grid=(n // tm,),
        in_specs=[row(LANES), row(D_MODEL), full(1, D_MODEL), full(1, D_MODEL),
                  pl.BlockSpec((tm * TOP_K * PACK_ROWS, LANES), lambda i: (i, 0))],
        out_specs=row(D_MODEL),
        out_shape=jax.ShapeDtypeStruct((n, D_MODEL), F32),
        compiler_params=_cparams(("parallel",)),
        name="combine_ln",
    )(gates, h1, g.reshape(1, -1), b.reshape(1, -1), ys)


def _layer(h, w_in, b_igate, b_fgate, conv_w, conv_b, rel_bias, beta_attn, beta_mlstm, w_out,
           ln1_g, ln1_b, w_router, b_router, w_up, b_up, w_down, b_down, ln2_g, ln2_b, batch, seq):
    n = batch * seq
    w_qkv = w_in[:, :3 * D_ATTN].astype(BF16)
    w_mix = w_in[:, 3 * D_ATTN:MAIN_COLS].astype(BF16)
    w_gate = jnp.zeros((D_MODEL, LANES), BF16).at[:, :2 * N_HEADS_M].set(w_in[:, MAIN_COLS:].astype(BF16))
    qkvs = _qkv_project(h, w_qkv, min(QKV_TM, n))
    proj_m = _project(h, w_mix, BF16, min(PROJ_TM, n), PROJ_TN)
    gates = _project(h, w_gate, F32, min(PROJ_TM, n), LANES)
    gate_bias = jnp.zeros((1, LANES), F32).at[0, :2 * N_HEADS_M].set(jnp.concatenate([b_igate, b_fgate]))

    outs, lses = [], []
    for (_, dil), qkv in zip(DILATED_CONFIGS, qkvs):
        o, l = _dilated_attention(qkv, _attn_bias_tables(rel_bias, dil), batch, seq, dil)
        outs.append(o)
        lses.append(l)
    y_attn = _attn_combine(outs, lses, beta_attn)
    y_mlstm = _mlstm(proj_m, gates, gate_bias, conv_w, conv_b, beta_mlstm, batch, seq)

    h1, h1_packed = _outproj_ln(y_attn, y_mlstm, h, w_out.astype(BF16), ln1_g, ln1_b)

    top_idx, top_gate, rank, counts = _router(h1, w_router, b_router)
    counts = counts[0, :N_EXPERTS]
    nb = (counts + FFN_ROWS - 1) // FFN_ROWS
    per = (counts + nb * FFN_SUB - 1) // jnp.maximum(nb * FFN_SUB, 1) * FFN_SUB
    blk_end = jnp.cumsum(nb)
    blk_start = blk_end - nb
    e_idx, r_idx = top_idx[:, :TOP_K], rank[:, :TOP_K]
    per_a = jnp.maximum(per[e_idx], 1)
    blk_in_e = jnp.floor((r_idx.astype(F32) + 0.5) / per_a.astype(F32)).astype(jnp.int32)
    dest = ((blk_start[e_idx] + blk_in_e) * FFN_ROWS + (r_idx - blk_in_e * per_a)).reshape(-1).astype(jnp.int32)
    nblk = n * TOP_K // FFN_ROWS + N_EXPERTS
    blk_id = jnp.arange(nblk, dtype=jnp.int32)
    blk_exp = jnp.minimum(jnp.sum(blk_end[None, :] <= blk_id[:, None], axis=1), N_EXPERTS - 1).astype(jnp.int32)
    n_used = blk_end[-1:].astype(jnp.int32)
    blk_valid = jnp.clip(counts[blk_exp] - (blk_id - blk_start[blk_exp]) * per[blk_exp], 0,
                         per[blk_exp]).astype(jnp.int32)
    slot_assign = jnp.full((nblk * FFN_ROWS,), -1, jnp.int32).at[dest].set(
        jnp.arange(n * TOP_K, dtype=jnp.int32), unique_indices=True)

    ys = _expert_ffn(h1_packed, slot_assign, blk_exp, blk_valid, n_used, w_up, b_up, w_down, b_down)
    return _combine_ln(top_gate, h1, ys, ln2_g, ln2_b)


def kernel(x, w_in, b_igate, b_fgate, conv_w, conv_b, rel_bias, beta_attn, beta_mlstm, w_out, ln1_g, ln1_b,
           w_router, b_router, w_up, b_up, w_down, b_down, ln2_g, ln2_b):
    batch, seq, d = x.shape
    h = x.reshape(batch * seq, d)
    for l in range(DEPTH):
        h = _layer(h, w_in[l], b_igate[l], b_fgate[l], conv_w[l], conv_b[l], rel_bias, beta_attn[l], beta_mlstm[l],
                   w_out[l], ln1_g[l], ln1_b[l], w_router[l], b_router[l], w_up[l], b_up[l], w_down[l], b_down[l],
                   ln2_g[l], ln2_b[l], batch, seq)
    return h.reshape(batch, seq, d)
```

```python
import functools
import math

import numpy as np
import jax
import jax.numpy as jnp
from jax import lax
from jax.experimental import pallas as pl
from jax.experimental.pallas import tpu as pltpu

F32 = jnp.float32
BF16 = jnp.bfloat16

D_MODEL = 2048
D_ATTN = 1024
HEAD_DIM_A = 64
N_HEADS_A = 16
DILATED_CONFIGS = ((128, 1), (512, 4), (2048, 16))
ATTN_BLOCK = 128
NUM_BUCKETS = 32
MAX_DISTANCE = 2048
D_MLSTM = 1024
N_HEADS_M = 4
HEAD_DIM_M = 256
CONV_K = 4
MLSTM_CHUNK = 128
MAIN_COLS = 3 * D_ATTN + 4 * D_MLSTM
N_EXPERTS = 32
TOP_K = 4
D_FF = 2048
SWIGLU_LIMIT = 7.0
SWIGLU_ALPHA = 1.702
DEPTH = 1
DN_ALPHA = (2 * DEPTH) ** 0.25
LN_EPS = 1e-5
HEAD_NORM_EPS = 1e-6
NEG_INF = -1e30

LANES = 128
VMEM_LIMIT = 48 * 1024 * 1024

PROJ_TM = 1024
PROJ_TN = 1024
COMBINE_TM = 512
OUTPROJ_TM = 512
ROUTER_TM = 512
QKV_TM = 512
ATTN_GROUP = 2
FFN_ROWS = 1024
FFN_SUB = 256
FFN_TF = 512
FFN_TH = 256
FFN_VMEM_LIMIT = 56 * 1024 * 1024
ROW_DMA_PRIORITY = 1
COMBINE_LN_TM = 256


def _cparams(sem, vmem=VMEM_LIMIT):
    return pltpu.CompilerParams(dimension_semantics=sem, vmem_limit_bytes=vmem)


def _dot(a, b):
    return jnp.dot(a, b, preferred_element_type=F32)


def _dot_f32_rhs(a_bf16, b_f32):
    hi = b_f32.astype(BF16)
    lo = (b_f32 - hi.astype(F32)).astype(BF16)
    return _dot(a_bf16, hi) + _dot(a_bf16, lo)


def _dot_f32_lhs(a_f32, b_bf16):
    hi = a_f32.astype(BF16)
    lo = (a_f32 - hi.astype(F32)).astype(BF16)
    return _dot(hi, b_bf16) + _dot(lo, b_bf16)


def _sigmoid(x):
    return 1.0 / (1.0 + jnp.exp(-x))


def _log_sigmoid(x):
    return jnp.minimum(x, 0.0) - jnp.log(1.0 + jnp.exp(-jnp.abs(x)))


def _proj_kernel(x_ref, w_ref, o_ref):
    o_ref[...] = _dot(x_ref[...].astype(BF16), w_ref[...]).astype(o_ref.dtype)


def _project(x, w, out_dtype, tm, tn):
    m, k = x.shape
    n = w.shape[1]
    return pl.pallas_call(
        _proj_kernel,
        grid=(m // tm, n // tn),
        in_specs=[pl.BlockSpec((tm, k), lambda i, j: (i, 0)),
                  pl.BlockSpec((k, tn), lambda i, j: (0, j))],
        out_specs=pl.BlockSpec((tm, tn), lambda i, j: (i, j)),
        out_shape=jax.ShapeDtypeStruct((m, n), out_dtype),
        compiler_params=_cparams(("parallel", "parallel")),
        name="in_proj",
    )(x, w)


def _qkv_proj_kernel(x_ref, w_ref, *refs):
    o_refs, r_ref = refs[:-1], refs[-1]
    res = _dot(x_ref[...].astype(BF16), w_ref[...])
    ntile, tm, _ = r_ref.shape
    wid = ntile * LANES
    for c in range(ntile):
        r_ref[c] = res[:, c * LANES:(c + 1) * LANES]
    for (_, dil), o_ref in zip(DILATED_CONFIGS, o_refs):
        if dil == 1:
            o_ref[...] = res.astype(o_ref.dtype)
        else:
            for r in range(dil):
                for c in range(ntile):
                    col = r * wid + c * LANES
                    o_ref[:, col:col + LANES] = r_ref[c, pl.ds(r, tm // dil, stride=dil), :].astype(o_ref.dtype)


def _qkv_project(x, w, tm):
    m, k = x.shape
    wid = w.shape[1]
    dils = [d for _, d in DILATED_CONFIGS]
    return pl.pallas_call(
        _qkv_proj_kernel,
        grid=(m // tm,),
        in_specs=[pl.BlockSpec((tm, k), lambda i: (i, 0)),
                  pl.BlockSpec((k, wid), lambda i: (0, 0), pipeline_mode=pl.Buffered(1))],
        out_specs=[pl.BlockSpec((tm // d, d * wid), lambda i: (i, 0)) for d in dils],
        out_shape=[jax.ShapeDtypeStruct((m // d, d * wid), BF16) for d in dils],
        scratch_shapes=[pltpu.VMEM((wid // LANES, tm, LANES), F32)],
        compiler_params=_cparams(("parallel",), 56 * 1024 * 1024),
        name="qkv_proj",
    )(x, w)


def _attn_kernel(q_ref, kp_ref, kc_ref, vp_ref, vc_ref, bias_ref, o_ref, lse_ref):
    n = pl.program_id(2)
    tab = jnp.minimum(n, 1)
    lse_ref[...] = jnp.zeros(lse_ref.shape, F32)
    grp, dh, nk = ATTN_GROUP, HEAD_DIM_A, 2 * ATTN_BLOCK
    wid = grp * dh
    lane_head = lax.broadcasted_iota(jnp.int32, (nk, wid), 1) // dh
    zero = jnp.zeros((nk, wid), BF16)
    ones_bd = jnp.concatenate([jnp.where(lane_head == j, 1.0, 0.0).astype(BF16) for j in range(grp)], axis=0)
    for g in range(N_HEADS_A // grp):
        cols = slice(g * wid, (g + 1) * wid)
        q = q_ref[0, :, cols] * (dh ** -0.5)
        kslab = jnp.concatenate([kp_ref[0, :, cols], kc_ref[0, :, cols]], axis=0)
        vslab = jnp.concatenate([vp_ref[0, :, cols], vc_ref[0, :, cols]], axis=0)
        k_bd = jnp.concatenate([jnp.where(lane_head == j, kslab, zero) for j in range(grp)], axis=0)
        v_bd = jnp.concatenate([jnp.where(lane_head == j, vslab, zero) for j in range(grp)], axis=0)
        s_all = lax.dot_general(q, k_bd, (((1,), (1,)), ((), ())), preferred_element_type=F32)
        ps, ms = [], []
        for j in range(grp):
            s = s_all[:, j * nk:(j + 1) * nk] + bias_ref[tab, g * grp + j]
            m = jnp.max(s, axis=-1, keepdims=True)
            ps.append(jnp.exp(s - m).astype(BF16))
            ms.append(m)
        p_all = jnp.concatenate(ps, axis=-1)
        res = _dot(p_all, jnp.concatenate([v_bd, ones_bd], axis=-1))
        den = res[:, wid:]
        o_ref[0, :, cols] = (res[:, :wid] / den).astype(o_ref.dtype)
        for j in range(grp):
            h = g * grp + j
            lse_ref[0, :, h:h + 1] = ms[j] + jnp.log(den[:, j * dh:j * dh + 1])


def _attn_bias_tables(rel_bias, dil):
    blk = ATTN_BLOCK
    period = 3 * blk
    k = np.arange(period)
    valid = k <= blk
    dist = np.where(valid, blk - k, 0) * dil
    max_exact = NUM_BUCKETS // 2
    d_f = np.maximum(dist, 1).astype(np.float32)
    large = max_exact + (np.log(d_f / np.float32(max_exact)) / np.float32(math.log(MAX_DISTANCE / max_exact))
                         * np.float32(NUM_BUCKETS - max_exact)).astype(np.int32)
    large = np.minimum(large, NUM_BUCKETS - 1)
    bucket = np.where(dist < max_exact, dist, large).astype(np.int32)
    w = jnp.where(jnp.asarray(valid)[None, :], rel_bias[jnp.asarray(bucket)].T.astype(F32), NEG_INF)
    t1 = jnp.tile(w, (1, blk))[:, :blk * (period - 1)].reshape(N_HEADS_A, blk, period - 1)[:, :, :2 * blk]
    has_prev = np.arange(2 * blk)[None, None, :] >= blk
    t0 = jnp.where(jnp.asarray(has_prev), t1, NEG_INF)
    return jnp.stack([t0, t1])


def _dilated_attention(qkv, bias_tab, batch, seq, dil):
    blk = ATTN_BLOCK
    l = seq // dil
    nb = l // blk
    ncb = 3
    pv = qkv.reshape(batch, l, dil * ncb * D_ATTN)

    def cur(c):
        return pl.BlockSpec((1, blk, D_ATTN), lambda b, r, n: (b, n, r * ncb + c))

    def prev(c):
        return pl.BlockSpec((1, blk, D_ATTN), lambda b, r, n: (b, jnp.maximum(n - 1, 0), r * ncb + c))

    o, lse = pl.pallas_call(
        _attn_kernel,
        grid=(batch, dil, nb),
        in_specs=[cur(0), prev(1), cur(1), prev(2), cur(2),
                  pl.BlockSpec((2, N_HEADS_A, blk, 2 * blk), lambda b, r, n: (0, 0, 0, 0))],
        out_specs=[pl.BlockSpec((1, blk, D_ATTN), lambda b, r, n: (b, n, r)),
                   pl.BlockSpec((1, blk, LANES), lambda b, r, n: (b, n, r))],
        out_shape=[jax.ShapeDtypeStruct((batch, l, dil * D_ATTN), BF16),
                   jax.ShapeDtypeStruct((batch, l, dil * LANES), F32)],
        compiler_params=_cparams(("parallel", "parallel", "arbitrary")),
        name=f"dilated_attn_d{dil}",
    )(pv, pv, pv, pv, pv, bias_tab)
    return o.reshape(batch * seq, D_ATTN), lse.reshape(batch * seq, LANES)


def _attn_combine_kernel(o1_ref, o2_ref, o3_ref, l1_ref, l2_ref, l3_ref, e_ref, et_ref, beta_ref, y_ref):
    lses = [l1_ref[...], l2_ref[...], l3_ref[...]]
    outs = [o1_ref, o2_ref, o3_ref]
    mx = jnp.maximum(jnp.maximum(lses[0], lses[1]), lses[2])
    ws = [jnp.exp(l - mx) for l in lses]
    tot = ws[0] + ws[1] + ws[2]
    e = e_ref[...]
    acc = None
    for w, o_ref in zip(ws, outs):
        term = _dot_f32_lhs(w / tot, e) * o_ref[...].astype(F32)
        acc = term if acc is None else acc + term
    ss = _dot_f32_lhs(acc * acc, et_ref[...])
    inv = lax.rsqrt(ss * (1.0 / HEAD_DIM_A) + HEAD_NORM_EPS)
    y_ref[...] = (acc * _dot_f32_lhs(inv, e) * beta_ref[...]).astype(y_ref.dtype)


def _attn_combine(os_, lses, beta_attn):
    n = os_[0].shape[0]
    tm = min(COMBINE_TM, n)
    head_of_lane = np.arange(D_ATTN) // HEAD_DIM_A
    e = (np.arange(LANES)[:, None] == head_of_lane[None, :]).astype(np.float32)
    e_j = jnp.asarray(e, BF16)
    et_j = jnp.asarray(e.T, BF16)
    row = lambda w: pl.BlockSpec((tm, w), lambda i: (i, 0))
    full = lambda a, b: pl.BlockSpec((a, b), lambda i: (0, 0))
    return pl.pallas_call(
        _attn_combine_kernel,
        grid=(n // tm,),
        in_specs=[row(D_ATTN)] * 3 + [row(LANES)] * 3 + [full(LANES, D_ATTN), full(D_ATTN, LANES), full(1, D_ATTN)],
        out_specs=row(D_ATTN),
        out_shape=jax.ShapeDtypeStruct((n, D_ATTN), BF16),
        compiler_params=_cparams(("parallel",)),
        name="attn_combine",
    )(*os_, *lses, e_j, et_j, beta_attn.reshape(1, D_ATTN).astype(F32))


def _mlstm_kernel(qp_ref, kp_ref, qprev_ref, kprev_ref, v_ref, og_ref, g_ref, gb_ref, cw_ref, cb_ref,
                  beta_ref, y_ref, c_ref, n_ref, m_ref):
    step = pl.program_id(0)
    ch = MLSTM_CHUNK
    dh = HEAD_DIM_M

    @pl.when(step == 0)
    def _():
        c_ref[...] = jnp.zeros(c_ref.shape, F32)
        n_ref[...] = jnp.zeros(n_ref.shape, F32)
        m_ref[...] = jnp.zeros(m_ref.shape, F32)

    def conv_silu(x_ref, prev_ref, coff, b):
        x = x_ref[b].astype(F32)
        p = jnp.where(step > 0, prev_ref[b].astype(F32), 0.0)
        xe = jnp.concatenate([p, x], axis=0)
        npad = p.shape[0]
        cols = slice(coff, coff + D_MLSTM)
        acc = cb_ref[:, cols] + cw_ref[CONV_K - 1:CONV_K, cols] * x
        for s in range(1, CONV_K):
            shifted = pltpu.roll(xe, s, 0)[npad:]
            acc = acc + cw_ref[CONV_K - 1 - s:CONV_K - s, cols] * shifted
        return acc * _sigmoid(acc)

    row_i = lax.broadcasted_iota(jnp.int32, (ch, ch), 0)
    col_i = lax.broadcasted_iota(jnp.int32, (ch, ch), 1)
    causal = row_i >= col_i
    tri = jnp.where(causal, 1.0, 0.0).astype(BF16)
    upp = jnp.where(row_i <= col_i, 1.0, 0.0).astype(BF16)

    for b, h in [(b, h) for b in range(qp_ref.shape[0]) for h in range(N_HEADS_M)]:
        if h == 0:
            qf = conv_silu(qp_ref, qprev_ref, 0, b)
            kf = conv_silu(kp_ref, kprev_ref, D_MLSTM, b) * (dh ** -0.5)
            qb = qf.astype(BF16)
            kb = kf.astype(BF16)
            g = g_ref[b] + gb_ref[...]
            gt = g.T
            b_cols = _dot_f32_rhs(tri, _log_sigmoid(g))
            b_rows = _dot_f32_lhs(_log_sigmoid(gt), upp)
        st = b * N_HEADS_M + h
        hs = slice(h * dh, (h + 1) * dh)
        fi = N_HEADS_M + h
        i_row = gt[h:h + 1, :]
        i_col = g[:, h:h + 1]
        b_row = b_rows[fi:fi + 1, :]
        b_col = b_cols[:, fi:fi + 1]
        m_prev = m_ref[st][:, 0:1]
        q_h, k_h = qb[:, hs], kb[:, hs]
        v_h = v_ref[b, :, hs]

        dmat = jnp.where(causal, b_col - b_row + i_row, NEG_INF)
        m_inter = b_col + m_prev
        m_t = jnp.maximum(m_inter, jnp.max(dmat, axis=-1, keepdims=True))
        w = jnp.exp(dmat - m_t) * lax.dot_general(q_h, k_h, (((1,), (1,)), ((), ())),
                                                  preferred_element_type=F32)
        decay = jnp.exp(m_inter - m_t)
        c_old = c_ref[st]
        inter = lax.dot_general(q_h, c_old.astype(BF16), (((1,), (1,)), ((), ())), preferred_element_type=F32)
        num = _dot(w.astype(BF16), v_h) + decay * inter
        n_old = n_ref[st]
        den = jnp.sum(w, axis=-1, keepdims=True) + decay * jnp.sum(qf[:, hs] * n_old, axis=-1, keepdims=True)
        hh = num / jnp.maximum(jnp.abs(den), jnp.exp(-m_t))

        g_last = b_col[ch - 1:ch, :]
        a_row = g_last - b_row + i_row
        a_col = g_last - b_col + i_col
        m_new = jnp.maximum(g_last + m_prev, jnp.max(a_row, axis=-1, keepdims=True))
        carry = jnp.exp(g_last + m_prev - m_new)
        wa_col = jnp.exp(a_col - m_new)
        wv = (wa_col * v_h.astype(F32)).astype(BF16)
        c_ref[st] = carry * c_old + lax.dot_general(wv, k_h, (((0,), (0,)), ((), ())), preferred_element_type=F32)
        n_ref[st] = carry * n_old + jnp.sum(wa_col * kf[:, hs], axis=0, keepdims=True)
        m_ref[st] = jnp.broadcast_to(m_new, (1, LANES))

        gated = _sigmoid(og_ref[b, :, hs].astype(F32)) * hh
        ms = jnp.sum(gated * gated, axis=-1, keepdims=True) * (1.0 / dh)
        y_ref[b, :, hs] = (gated * lax.rsqrt(ms + HEAD_NORM_EPS) * beta_ref[:, hs]).astype(y_ref.dtype)


def _mlstm(proj, gates, gate_bias, conv_w, conv_b, beta_mlstm, batch, seq):
    ch = MLSTM_CHUNK
    nchunk = seq // ch
    pv = proj.reshape(batch, seq, 4 * D_MLSTM)
    gv = gates.reshape(batch, seq, LANES)
    prev_rows = 16
    cb0 = 0

    def cur(c):
        return pl.BlockSpec((batch, ch, D_MLSTM), lambda n: (0, n, c))

    def prev(c):
        per = ch // prev_rows
        return pl.BlockSpec((batch, prev_rows, D_MLSTM), lambda n: (0, jnp.maximum(n * per - 1, 0), c))

    const = lambda a, b_: pl.BlockSpec((a, b_), lambda n: (0, 0))
    y = pl.pallas_call(
        _mlstm_kernel,
        grid=(nchunk,),
        in_specs=[cur(cb0), cur(cb0 + 1), prev(cb0), prev(cb0 + 1), cur(cb0 + 2), cur(cb0 + 3),
                  pl.BlockSpec((batch, ch, LANES), lambda n: (0, n, 0)),
                  const(1, LANES), const(CONV_K, 2 * D_MLSTM), const(1, 2 * D_MLSTM), const(1, D_MLSTM)],
        out_specs=pl.BlockSpec((batch, ch, D_MLSTM), lambda n: (0, n, 0)),
        out_shape=jax.ShapeDtypeStruct((batch, seq, D_MLSTM), BF16),
        scratch_shapes=[pltpu.VMEM((batch * N_HEADS_M, HEAD_DIM_M, HEAD_DIM_M), F32),
                        pltpu.VMEM((batch * N_HEADS_M, 1, HEAD_DIM_M), F32),
                        pltpu.VMEM((batch * N_HEADS_M, 1, LANES), F32)],
        compiler_params=_cparams(("arbitrary",)),
        name="mlstm",
    )(pv, pv, pv, pv, pv, pv, gv, gate_bias, conv_w.astype(F32), conv_b.reshape(1, -1).astype(F32),
      beta_mlstm.reshape(1, D_MLSTM).astype(F32))
    return y.reshape(batch * seq, D_MLSTM)


def _layer_norm(z, g, b):
    mu = jnp.mean(z, axis=-1, keepdims=True)
    zc = z - mu
    var = jnp.mean(zc * zc, axis=-1, keepdims=True)
    return zc * lax.rsqrt(var + LN_EPS) * g + b


PACK_ROWS = D_MODEL // (2 * LANES)


def _store_packed_rows(dst_ref, x, first=0):
    rows = x.shape[0]
    half = D_MODEL // 2
    for s in range(PACK_ROWS):
        lo = x[:, s * LANES:(s + 1) * LANES].astype(BF16).astype(F32)
        hi = x[:, half + s * LANES:half + (s + 1) * LANES].astype(BF16).astype(F32)
        word = pltpu.bitcast(hi, jnp.int32) | lax.shift_right_logical(pltpu.bitcast(lo, jnp.int32), 16)
        dst_ref[pl.ds(first * PACK_ROWS + s, rows, stride=PACK_ROWS), :] = word


def _load_packed_rows(src_ref, first, rows):
    los, his = [], []
    for s in range(PACK_ROWS):
        word = src_ref[pl.ds(first * PACK_ROWS + s, rows, stride=PACK_ROWS), :]
        los.append(pltpu.bitcast(lax.shift_left(word, 16), F32))
        his.append(pltpu.bitcast(word & jnp.int32(-65536), F32))
    return los, his


def _outproj_kernel(ya_ref, ym_ref, x_ref, w_ref, g_ref, b_ref, h_ref, hp_ref):
    y = _dot(ya_ref[...], w_ref[0:D_ATTN, :]) + _dot(ym_ref[...], w_ref[D_ATTN:D_MODEL, :])
    h = _layer_norm(DN_ALPHA * x_ref[...] + y, g_ref[...], b_ref[...])
    h_ref[...] = h
    _store_packed_rows(hp_ref, h)


def _outproj_ln(ya, ym, x, w_out, g, b):
    n = x.shape[0]
    tm = min(OUTPROJ_TM, n)
    row = lambda w: pl.BlockSpec((tm, w), lambda i: (i, 0))
    full = lambda a, b_: pl.BlockSpec((a, b_), lambda i: (0, 0))
    return pl.pallas_call(
        _outproj_kernel,
        grid=(n // tm,),
        in_specs=[row(D_ATTN), row(D_MLSTM), row(D_MODEL), full(D_MODEL, D_MODEL), full(1, D_MODEL), full(1, D_MODEL)],
        out_specs=[row(D_MODEL), pl.BlockSpec((tm * PACK_ROWS, LANES), lambda i: (i, 0))],
        out_shape=[jax.ShapeDtypeStruct((n, D_MODEL), F32),
                   jax.ShapeDtypeStruct((n * PACK_ROWS, LANES), jnp.int32)],
        compiler_params=_cparams(("parallel",)),
        name="out_proj_ln",
    )(ya, ym, x, w_out, g.reshape(1, -1), b.reshape(1, -1))


def _router_kernel(h_ref, whi_ref, wlo_ref, b_ref, tri_ref, idx_ref, gate_ref, rank_ref, cnt_ref, carry_ref):
    i = pl.program_id(0)

    @pl.when(i == 0)
    def _():
        carry_ref[...] = jnp.zeros(carry_ref.shape, F32)

    x = h_ref[...]
    xhi = x.astype(BF16)
    xlo = (x - xhi.astype(F32)).astype(BF16)
    logits = _dot(xhi, whi_ref[...]) + _dot(xhi, wlo_ref[...]) + _dot(xlo, whi_ref[...]) + b_ref[...]
    tm = logits.shape[0]
    lane = lax.broadcasted_iota(jnp.int32, (tm, LANES), 1)
    lane_f = lane.astype(F32)
    vals = jnp.where(lane < N_EXPERTS, logits, NEG_INF)

    sels, tops = [], []
    for _ in range(TOP_K):
        mx = jnp.max(vals, axis=-1, keepdims=True)
        first = jnp.min(jnp.where(vals == mx, lane_f, float(LANES)), axis=-1, keepdims=True)
        sel = lane_f == first
        sels.append(sel)
        tops.append((mx, first))
        vals = jnp.where(sel, 2.0 * NEG_INF, vals)

    exps = [jnp.exp(mx - tops[0][0]) for mx, _ in tops]
    tot = exps[0] + exps[1] + exps[2] + exps[3]

    onehot = jnp.zeros((tm, LANES), F32)
    for sel in sels:
        onehot = jnp.where(sel, 1.0, onehot)
    before = _dot(tri_ref[...], onehot.astype(BF16)) + carry_ref[...]

    idx_out = jnp.zeros((tm, LANES), F32)
    gate_out = jnp.zeros((tm, LANES), F32)
    rank_out = jnp.zeros((tm, LANES), F32)
    for k in range(TOP_K):
        rank_k = jnp.sum(jnp.where(sels[k], before, 0.0), axis=-1, keepdims=True)
        idx_out = jnp.where(lane == k, tops[k][1], idx_out)
        gate_out = jnp.where(lane == k, exps[k] / tot, gate_out)
        rank_out = jnp.where(lane == k, rank_k, rank_out)
    idx_ref[...] = idx_out.astype(jnp.int32)
    gate_ref[...] = gate_out
    rank_ref[...] = rank_out.astype(jnp.int32)

    carry = carry_ref[...] + jnp.sum(onehot, axis=0, keepdims=True)
    carry_ref[...] = carry
    cnt_ref[...] = carry.astype(jnp.int32)


def _router(h1, w_router, b_router):
    n = h1.shape[0]
    tm = min(ROUTER_TM, n)
    wpad = jnp.zeros((D_MODEL, LANES), F32).at[:, :N_EXPERTS].set(w_router)
    whi = wpad.astype(BF16)
    wlo = (wpad - whi.astype(F32)).astype(BF16)
    bpad = jnp.zeros((1, LANES), F32).at[0, :N_EXPERTS].set(b_router)
    tri = jnp.asarray(np.tril(np.ones((tm, tm), np.float32), -1), BF16)
    row = lambda w: pl.BlockSpec((tm, w), lambda i: (i, 0))
    full = lambda a, b_: pl.BlockSpec((a, b_), lambda i: (0, 0))
    return pl.pallas_call(
        _router_kernel,
        grid=(n // tm,),
        in_specs=[row(D_MODEL), full(D_MODEL, LANES), full(D_MODEL, LANES), full(1, LANES), full(tm, tm)],
        out_specs=[row(LANES), row(LANES), row(LANES), full(1, LANES)],
        out_shape=[jax.ShapeDtypeStruct((n, LANES), jnp.int32), jax.ShapeDtypeStruct((n, LANES), F32),
                   jax.ShapeDtypeStruct((n, LANES), jnp.int32), jax.ShapeDtypeStruct((1, LANES), jnp.int32)],
        scratch_shapes=[pltpu.VMEM((1, LANES), F32)],
        compiler_params=_cparams(("arbitrary",)),
        name="router",
    )(h1, whi, wlo, bpad, tri)


def _slot_map_kernel(dest_ref, slot_ref):
    unroll = 8

    def fill(j, carry):
        for u in range(unroll):
            slot_ref[j * unroll + u] = -1
        return carry

    lax.fori_loop(0, slot_ref.shape[0] // unroll, fill, 0)

    def put(j, carry):
        for u in range(unroll):
            a = j * unroll + u
            slot_ref[dest_ref[a]] = a
        return carry

    lax.fori_loop(0, dest_ref.shape[0] // unroll, put, 0)


def _slot_map(dest, n_slots):
    return pl.pallas_call(
        _slot_map_kernel,
        in_specs=[pl.BlockSpec(memory_space=pltpu.SMEM)],
        out_specs=pl.BlockSpec(memory_space=pltpu.SMEM),
        out_shape=jax.ShapeDtypeStruct((n_slots,), jnp.int32),
        name="slot_map",
    )(dest)


def _ffn_kernel(bexp_ref, nused_ref, nvalid_ref, slot_ref, src_cur_ref, src_nxt_ref, wg_ref, wu_ref, wdl_ref, wdh_ref,
                bg_ref, bu_ref, bdl_ref, bdh_ref, xp_ref, ys_ref, xbuf_ref, x_ref, act_ref, obuf_ref, wgb_ref, wub_ref,
                wdlb_ref, wdhb_ref, in_sem, out_sem):
    del bexp_ref
    i = pl.program_id(0)
    g = pl.program_id(1)
    n_up = act_ref.shape[0]
    last_g = pl.num_programs(1) - 1
    last_i = pl.num_programs(0) - 1
    n_used = nused_ref[0]
    used = i < n_used
    rb = x_ref.shape[0]
    sub = FFN_SUB
    th = wdlb_ref.shape[1]
    unroll = 4

    def sub_blocks(blk):
        return (nvalid_ref[blk] + (sub - 1)) // sub

    nsub = sub_blocks(i)

    def start_gather(src_ref, nrows):
        def body(j, carry):
            for u in range(unroll):
                jj = j * unroll + u
                src = pl.multiple_of(src_ref[jj], PACK_ROWS)
                dst = pl.multiple_of(jj * PACK_ROWS, PACK_ROWS)
                pltpu.make_async_copy(xp_ref.at[pl.ds(src, PACK_ROWS)], xbuf_ref.at[pl.ds(dst, PACK_ROWS)],
                                      in_sem).start(priority=ROW_DMA_PRIORITY)
            return carry

        lax.fori_loop(0, nrows // unroll, body, 0)

    def wait_gather(nrows):
        npk = pl.multiple_of(nrows * PACK_ROWS, PACK_ROWS)
        pltpu.make_async_copy(xp_ref.at[pl.ds(0, npk)], xbuf_ref.at[pl.ds(0, npk)], in_sem).wait()

    def start_scatter(nrows):
        def body(j, carry):
            for u in range(unroll):
                jj = j * unroll + u
                a = slot_ref[jj]

                @pl.when(a >= 0)
                def _():
                    src = pl.multiple_of(jj * PACK_ROWS, PACK_ROWS)
                    dst = pl.multiple_of(a * PACK_ROWS, PACK_ROWS)
                    pltpu.make_async_copy(obuf_ref.at[pl.ds(src, PACK_ROWS)], ys_ref.at[pl.ds(dst, PACK_ROWS)],
                                          out_sem).start(priority=ROW_DMA_PRIORITY)
            return carry

        lax.fori_loop(0, nrows // unroll, body, 0)

    def wait_scatter(blk):
        npk = pl.multiple_of(nvalid_ref[blk] * PACK_ROWS, PACK_ROWS)
        pltpu.make_async_copy(obuf_ref.at[pl.ds(0, npk)], ys_ref.at[pl.ds(0, npk)], out_sem).wait()

    @pl.when(jnp.logical_and(used, g == 0))
    def _():
        @pl.when(i == 0)
        def _():
            start_gather(src_cur_ref, nsub * sub)

        wait_gather(nsub * sub)
        for j in range(rb // sub):
            @pl.when(j < nsub)
            def _():
                los, his = _load_packed_rows(xbuf_ref, j * sub, sub)
                rows = slice(j * sub, (j + 1) * sub)
                for s in range(PACK_ROWS):
                    x_ref[rows, s * LANES:(s + 1) * LANES] = los[s].astype(BF16)
                    x_ref[rows, D_MODEL // 2 + s * LANES:D_MODEL // 2 + (s + 1) * LANES] = his[s].astype(BF16)

    @pl.when(jnp.logical_and(g == 1, i + 1 < n_used))
    def _():
        start_gather(src_nxt_ref, sub_blocks(i + 1) * sub)

    @pl.when(jnp.logical_and(used, g < n_up))
    def _():
        wgb_ref[...] = wg_ref[0].astype(BF16)
        wub_ref[...] = wu_ref[0].astype(BF16)

        def sub_body(j, carry):
            rows = pl.ds(pl.multiple_of(j * sub, sub), sub)
            x = x_ref[rows, :]
            hg = _dot(x, wgb_ref[...]) + bg_ref[0]
            hu = _dot(x, wub_ref[...]) + bu_ref[0]
            gate = jnp.minimum(hg, SWIGLU_LIMIT)
            up = jnp.clip(hu, -SWIGLU_LIMIT, SWIGLU_LIMIT)
            act_ref[g, rows, :] = ((up + 1.0) * (gate * _sigmoid(SWIGLU_ALPHA * gate))).astype(BF16)
            return carry

        lax.fori_loop(0, nsub, sub_body, 0)

    @pl.when(jnp.logical_and(used, g >= n_up))
    def _():
        @pl.when(jnp.logical_and(g == n_up, i > 0))
        def _():
            wait_scatter(i - 1)

        wdlb_ref[...] = wdl_ref[0].astype(BF16)
        wdhb_ref[...] = wdh_ref[0].astype(BF16)
        tile0 = (g - n_up) * (th // LANES)

        def sub_body(j, carry):
            row0 = pl.multiple_of(j * sub, sub)
            a = jnp.concatenate([act_ref[c, pl.ds(row0, sub), :] for c in range(n_up)], axis=-1)
            ylo = _dot(a, wdlb_ref[...]) + bdl_ref[0]
            yhi = _dot(a, wdhb_ref[...]) + bdh_ref[0]
            for s in range(th // LANES):
                lo = ylo[:, s * LANES:(s + 1) * LANES].astype(BF16).astype(F32)
                hi = yhi[:, s * LANES:(s + 1) * LANES].astype(BF16).astype(F32)
                word = pltpu.bitcast(hi, jnp.int32) | lax.shift_right_logical(pltpu.bitcast(lo, jnp.int32), 16)
                obuf_ref[pl.ds(row0 * PACK_ROWS + tile0 + s, sub, stride=PACK_ROWS), :] = word
            return carry

        lax.fori_loop(0, nsub, sub_body, 0)

        @pl.when(g == last_g)
        def _():
            start_scatter(nsub * sub)

    @pl.when(jnp.logical_and(i == last_i, g == last_g))
    def _():
        wait_scatter(jnp.minimum(i, n_used - 1))


def _expert_ffn(h_packed, slot_assign, blk_exp, blk_valid, n_used, w_up, b_up, w_down, b_down):
    rb, tf, th = FFN_ROWS, FFN_TF, FFN_TH
    nblk = slot_assign.shape[0] // rb
    n_up = D_FF // tf
    n_dn = D_MODEL // 2 // th
    n_assign = h_packed.shape[0] // PACK_ROWS * TOP_K
    slot_src = jnp.maximum(slot_assign, 0) // TOP_K * PACK_ROWS

    def clamp(i, nu):
        return jnp.minimum(i, nu[0] - 1)

    def up_sel(i, g, nu):
        return jnp.where(i < nu[0], jnp.minimum(g, n_up - 1), n_up - 1)

    def dn_sel(i, g, nu):
        return jnp.where(i < nu[0], jnp.clip(g - n_up, 0, n_dn - 1), n_dn - 1)

    grid_spec = pltpu.PrefetchScalarGridSpec(
        num_scalar_prefetch=3,
        grid=(nblk, n_up + n_dn),
        in_specs=[
            pl.BlockSpec((rb,), lambda i, g, be, nu, *_: (clamp(i, nu),), memory_space=pltpu.SMEM),
            pl.BlockSpec((rb,), lambda i, g, be, nu, *_: (clamp(i, nu),), memory_space=pltpu.SMEM),
            pl.BlockSpec((rb,), lambda i, g, be, nu, *_: (clamp(i + 1, nu),), memory_space=pltpu.SMEM),
            pl.BlockSpec((1, D_MODEL, tf), lambda i, g, be, nu, *_: (be[clamp(i, nu)], 0, up_sel(i, g, nu))),
            pl.BlockSpec((1, D_MODEL, tf), lambda i, g, be, nu, *_: (be[clamp(i, nu)], 0, n_up + up_sel(i, g, nu))),
            pl.BlockSpec((1, D_FF, th), lambda i, g, be, nu, *_: (be[clamp(i, nu)], 0, dn_sel(i, g, nu))),
            pl.BlockSpec((1, D_FF, th), lambda i, g, be, nu, *_: (be[clamp(i, nu)], 0, n_dn + dn_sel(i, g, nu))),
            pl.BlockSpec((1, 1, tf), lambda i, g, be, nu, *_: (be[clamp(i, nu)], 0, up_sel(i, g, nu))),
            pl.BlockSpec((1, 1, tf), lambda i, g, be, nu, *_: (be[clamp(i, nu)], 0, n_up + up_sel(i, g, nu))),
            pl.BlockSpec((1, 1, th), lambda i, g, be, nu, *_: (be[clamp(i, nu)], 0, dn_sel(i, g, nu))),
            pl.BlockSpec((1, 1, th), lambda i, g, be, nu, *_: (be[clamp(i, nu)], 0, n_dn + dn_sel(i, g, nu))),
            pl.BlockSpec(memory_space=pl.ANY),
        ],
        out_specs=pl.BlockSpec(memory_space=pl.ANY),
        scratch_shapes=[pltpu.VMEM((rb * PACK_ROWS, LANES), jnp.int32),
                        pltpu.VMEM((rb, D_MODEL), BF16),
                        pltpu.VMEM((n_up, rb, tf), BF16),
                        pltpu.VMEM((rb * PACK_ROWS, LANES), jnp.int32),
                        pltpu.VMEM((D_MODEL, tf), BF16), pltpu.VMEM((D_MODEL, tf), BF16),
                        pltpu.VMEM((D_FF, th), BF16), pltpu.VMEM((D_FF, th), BF16),
                        pltpu.SemaphoreType.DMA, pltpu.SemaphoreType.DMA],
    )
    assert n_up >= 2, "the next block's gather is issued at step 1, after this block's rows are unpacked at step 0"
    return pl.pallas_call(
        _ffn_kernel,
        grid_spec=grid_spec,
        out_shape=jax.ShapeDtypeStruct((n_assign * PACK_ROWS, LANES), jnp.int32),
        compiler_params=pltpu.CompilerParams(dimension_semantics=("arbitrary", "arbitrary"),
                                             vmem_limit_bytes=FFN_VMEM_LIMIT, disable_bounds_checks=True),
        name="expert_ffn",
    )(blk_exp, n_used, blk_valid, slot_assign, slot_src, slot_src, w_up, w_up, w_down, w_down,
      b_up.reshape(N_EXPERTS, 1, 2 * D_FF), b_up.reshape(N_EXPERTS, 1, 2 * D_FF),
      b_down.reshape(N_EXPERTS, 1, D_MODEL), b_down.reshape(N_EXPERTS, 1, D_MODEL), h_packed)


def _combine_kernel(gate_ref, h_ref, g_ref, b_ref, ys_ref, o_ref):
    tm = h_ref.shape[0]
    gates = gate_ref[...]
    cols = [None] * (2 * PACK_ROWS)
    for k in range(TOP_K):
        gk = gates[:, k:k + 1]
        for s in range(PACK_ROWS):
            word = ys_ref[pl.ds(k * PACK_ROWS + s, tm, stride=TOP_K * PACK_ROWS), :]
            lo = pltpu.bitcast(lax.shift_left(word, 16), F32)
            hi = pltpu.bitcast(word & jnp.int32(-65536), F32)
            for c, blk in ((s, lo), (PACK_ROWS + s, hi)):
                cols[c] = gk * blk if cols[c] is None else cols[c] + gk * blk
    z = DN_ALPHA * h_ref[...] + jnp.concatenate(cols, axis=-1)
    o_ref[...] = _layer_norm(z, g_ref[...], b_ref[...])


def _combine_ln(gates, h1, ys, g, b):
    n = h1.shape[0]
    tm = min(COMBINE_LN_TM, n)
    row = lambda w: pl.BlockSpec((tm, w), lambda i: (i, 0))
    full = lambda a, b_: pl.BlockSpec((a, b_), lambda i: (0, 0))
    return pl.pallas_call(
        _combine_kernel,
        grid=(n // tm,),
        in_specs=[row(LANES), row(D_MODEL), full(1, D_MODEL), full(1, D_MODEL),
                  pl.BlockSpec((tm * TOP_K * PACK_ROWS, LANES), lambda i: (i, 0))],
        out_specs=row(D_MODEL),
        out_shape=jax.ShapeDtypeStruct((n, D_MODEL), F32),
        compiler_params=_cparams(("parallel",)),
        name="combine_ln",
    )(gates, h1, g.reshape(1, -1), b.reshape(1, -1), ys)


def _layer(h, w_in, b_igate, b_fgate, conv_w, conv_b, rel_bias, beta_attn, beta_mlstm, w_out,
           ln1_g, ln1_b, w_router, b_router, w_up, b_up, w_down, b_down, ln2_g, ln2_b, batch, seq):
    n = batch * seq
    w_qkv = w_in[:, :3 * D_ATTN].astype(BF16)
    w_mix = w_in[:, 3 * D_ATTN:MAIN_COLS].astype(BF16)
    w_gate = jnp.zeros((D_MODEL, LANES), BF16).at[:, :2 * N_HEADS_M].set(w_in[:, MAIN_COLS:].astype(BF16))
    qkvs = _qkv_project(h, w_qkv, min(QKV_TM, n))
    proj_m = _project(h, w_mix, BF16, min(PROJ_TM, n), PROJ_TN)
    gates = _project(h, w_gate, F32, min(PROJ_TM, n), LANES)
    gate_bias = jnp.zeros((1, LANES), F32).at[0, :2 * N_HEADS_M].set(jnp.concatenate([b_igate, b_fgate]))

    outs, lses = [], []
    for (_, dil), qkv in zip(DILATED_CONFIGS, qkvs):
        o, l = _dilated_attention(qkv, _attn_bias_tables(rel_bias, dil), batch, seq, dil)
        outs.append(o)
        lses.append(l)
    y_attn = _attn_combine(outs, lses, beta_attn)
    y_mlstm = _mlstm(proj_m, gates, gate_bias, conv_w, conv_b, beta_mlstm, batch, seq)

    h1, h1_packed = _outproj_ln(y_attn, y_mlstm, h, w_out.astype(BF16), ln1_g, ln1_b)

    top_idx, top_gate, rank, counts = _router(h1, w_router, b_router)
    counts = counts[0, :N_EXPERTS]
    nb = (counts + FFN_ROWS - 1) // FFN_ROWS
    per = (counts + nb * FFN_SUB - 1) // jnp.maximum(nb * FFN_SUB, 1) * FFN_SUB
    blk_end = jnp.cumsum(nb)
    blk_start = blk_end - nb
    e_idx, r_idx = top_idx[:, :TOP_K], rank[:, :TOP_K]
    per_a = jnp.maximum(per[e_idx], 1)
    blk_in_e = jnp.floor((r_idx.astype(F32) + 0.5) / per_a.astype(F32)).astype(jnp.int32)
    dest = ((blk_start[e_idx] + blk_in_e) * FFN_ROWS + (r_idx - blk_in_e * per_a)).reshape(-1).astype(jnp.int32)
    nblk = n * TOP_K // FFN_ROWS + N_EXPERTS
    blk_id = jnp.arange(nblk, dtype=jnp.int32)
    blk_exp = jnp.minimum(jnp.sum(blk_end[None, :] <= blk_id[:, None], axis=1), N_EXPERTS - 1).astype(jnp.int32)
    n_used = blk_end[-1:].astype(jnp.int32)
    blk_valid = jnp.clip(counts[blk_exp] - (blk_id - blk_start[blk_exp]) * per[blk_exp], 0,
                         per[blk_exp]).astype(jnp.int32)
    slot_assign = _slot_map(dest, nblk * FFN_ROWS)

    ys = _expert_ffn(h1_packed, slot_assign, blk_exp, blk_valid, n_used, w_up, b_up, w_down, b_down)
    return _combine_ln(top_gate, h1, ys, ln2_g, ln2_b)


def kernel(x, w_in, b_igate, b_fgate, conv_w, conv_b, rel_bias, beta_attn, beta_mlstm, w_out, ln1_g, ln1_b,
           w_router, b_router, w_up, b_up, w_down, b_down, ln2_g, ln2_b):
    batch, seq, d = x.shape
    h = x.reshape(batch * seq, d)
    for l in range(DEPTH):
        h = _layer(h, w_in[l], b_igate[l], b_fgate[l], conv_w[l], conv_b[l], rel_bias, beta_attn[l], beta_mlstm[l],
                   w_out[l], ln1_g[l], ln1_b[l], w_router[l], b_router[l], w_up[l], b_up[l], w_down[l], b_down[l],
                   ln2_g[l], ln2_b[l], batch, seq)
    return h.reshape(batch, seq, d)
```

```python
import functools
import math

import numpy as np
import jax
import jax.numpy as jnp
from jax import lax
from jax.experimental import pallas as pl
from jax.experimental.pallas import tpu as pltpu

F32 = jnp.float32
BF16 = jnp.bfloat16

D_MODEL = 2048
D_ATTN = 1024
HEAD_DIM_A = 64
N_HEADS_A = 16
DILATED_CONFIGS = ((128, 1), (512, 4), (2048, 16))
ATTN_BLOCK = 128
NUM_BUCKETS = 32
MAX_DISTANCE = 2048
D_MLSTM = 1024
N_HEADS_M = 4
HEAD_DIM_M = 256
CONV_K = 4
MLSTM_CHUNK = 128
MAIN_COLS = 3 * D_ATTN + 4 * D_MLSTM
N_EXPERTS = 32
TOP_K = 4
D_FF = 2048
SWIGLU_LIMIT = 7.0
SWIGLU_ALPHA = 1.702
DEPTH = 1
DN_ALPHA = (2 * DEPTH) ** 0.25
LN_EPS = 1e-5
HEAD_NORM_EPS = 1e-6
NEG_INF = -1e30

LANES = 128
VMEM_LIMIT = 48 * 1024 * 1024

PROJ_TM = 1024
PROJ_TN = 1024
COMBINE_TM = 512
OUTPROJ_TM = 512
ROUTER_TM = 512
QKV_TM = 512
ATTN_GROUP = 2
FFN_ROWS = 1024
FFN_SUB = 256
FFN_TF = 512
FFN_TH = 256
FFN_VMEM_LIMIT = 56 * 1024 * 1024
ROW_DMA_PRIORITY = 1
COMBINE_LN_TM = 256


def _cparams(sem, vmem=VMEM_LIMIT):
    return pltpu.CompilerParams(dimension_semantics=sem, vmem_limit_bytes=vmem)


def _dot(a, b):
    return jnp.dot(a, b, preferred_element_type=F32)


def _dot_f32_rhs(a_bf16, b_f32):
    hi = b_f32.astype(BF16)
    lo = (b_f32 - hi.astype(F32)).astype(BF16)
    return _dot(a_bf16, hi) + _dot(a_bf16, lo)


def _dot_f32_lhs(a_f32, b_bf16):
    hi = a_f32.astype(BF16)
    lo = (a_f32 - hi.astype(F32)).astype(BF16)
    return _dot(hi, b_bf16) + _dot(lo, b_bf16)


def _sigmoid(x):
    return 1.0 / (1.0 + jnp.exp(-x))


def _log_sigmoid(x):
    return jnp.minimum(x, 0.0) - jnp.log(1.0 + jnp.exp(-jnp.abs(x)))


def _proj_kernel(x_ref, w_ref, wg_ref, o_ref, g_ref):
    x = x_ref[...].astype(BF16)
    o_ref[...] = _dot(x, w_ref[...]).astype(o_ref.dtype)

    @pl.when(pl.program_id(1) == 0)
    def _():
        g_ref[...] = _dot(x, wg_ref[...])


def _project(x, w, w_gate, tm, tn):
    m, k = x.shape
    n = w.shape[1]
    return pl.pallas_call(
        _proj_kernel,
        grid=(m // tm, n // tn),
        in_specs=[pl.BlockSpec((tm, k), lambda i, j: (i, 0)),
                  pl.BlockSpec((k, tn), lambda i, j: (0, j)),
                  pl.BlockSpec((k, LANES), lambda i, j: (0, 0))],
        out_specs=[pl.BlockSpec((tm, tn), lambda i, j: (i, j)),
                   pl.BlockSpec((tm, LANES), lambda i, j: (i, 0))],
        out_shape=[jax.ShapeDtypeStruct((m, n), BF16), jax.ShapeDtypeStruct((m, LANES), F32)],
        compiler_params=_cparams(("parallel", "arbitrary")),
        name="in_proj",
    )(x, w, w_gate)


def _qkv_proj_kernel(x_ref, w_ref, *refs):
    o_refs, r_ref = refs[:-1], refs[-1]
    res = _dot(x_ref[...].astype(BF16), w_ref[...])
    ntile, tm, _ = r_ref.shape
    wid = ntile * LANES
    for c in range(ntile):
        r_ref[c] = res[:, c * LANES:(c + 1) * LANES]
    for (_, dil), o_ref in zip(DILATED_CONFIGS, o_refs):
        if dil == 1:
            o_ref[...] = res.astype(o_ref.dtype)
        else:
            for r in range(dil):
                for c in range(ntile):
                    col = r * wid + c * LANES
                    o_ref[:, col:col + LANES] = r_ref[c, pl.ds(r, tm // dil, stride=dil), :].astype(o_ref.dtype)


def _qkv_project(x, w, tm):
    m, k = x.shape
    wid = w.shape[1]
    dils = [d for _, d in DILATED_CONFIGS]
    return pl.pallas_call(
        _qkv_proj_kernel,
        grid=(m // tm,),
        in_specs=[pl.BlockSpec((tm, k), lambda i: (i, 0)),
                  pl.BlockSpec((k, wid), lambda i: (0, 0), pipeline_mode=pl.Buffered(1))],
        out_specs=[pl.BlockSpec((tm // d, d * wid), lambda i: (i, 0)) for d in dils],
        out_shape=[jax.ShapeDtypeStruct((m // d, d * wid), BF16) for d in dils],
        scratch_shapes=[pltpu.VMEM((wid // LANES, tm, LANES), F32)],
        compiler_params=_cparams(("parallel",), 56 * 1024 * 1024),
        name="qkv_proj",
    )(x, w)


def _attn_kernel(q_ref, kp_ref, kc_ref, vp_ref, vc_ref, bias_ref, o_ref, lse_ref):
    n = pl.program_id(2)
    tab = jnp.minimum(n, 1)
    lse_ref[...] = jnp.zeros(lse_ref.shape, F32)
    grp, dh, nk = ATTN_GROUP, HEAD_DIM_A, 2 * ATTN_BLOCK
    wid = grp * dh
    lane_head = lax.broadcasted_iota(jnp.int32, (nk, wid), 1) // dh
    zero = jnp.zeros((nk, wid), BF16)
    ones_bd = jnp.concatenate([jnp.where(lane_head == j, 1.0, 0.0).astype(BF16) for j in range(grp)], axis=0)
    for g in range(N_HEADS_A // grp):
        cols = slice(g * wid, (g + 1) * wid)
        q = q_ref[0, :, cols] * (dh ** -0.5)
        kslab = jnp.concatenate([kp_ref[0, :, cols], kc_ref[0, :, cols]], axis=0)
        vslab = jnp.concatenate([vp_ref[0, :, cols], vc_ref[0, :, cols]], axis=0)
        k_bd = jnp.concatenate([jnp.where(lane_head == j, kslab, zero) for j in range(grp)], axis=0)
        v_bd = jnp.concatenate([jnp.where(lane_head == j, vslab, zero) for j in range(grp)], axis=0)
        s_all = lax.dot_general(q, k_bd, (((1,), (1,)), ((), ())), preferred_element_type=F32)
        ps, ms = [], []
        for j in range(grp):
            s = s_all[:, j * nk:(j + 1) * nk] + bias_ref[tab, g * grp + j]
            m = jnp.max(s, axis=-1, keepdims=True)
            ps.append(jnp.exp(s - m).astype(BF16))
            ms.append(m)
        p_all = jnp.concatenate(ps, axis=-1)
        res = _dot(p_all, jnp.concatenate([v_bd, ones_bd], axis=-1))
        den = res[:, wid:]
        o_ref[0, :, cols] = (res[:, :wid] / den).astype(o_ref.dtype)
        for j in range(grp):
            h = g * grp + j
            lse_ref[0, :, h:h + 1] = ms[j] + jnp.log(den[:, j * dh:j * dh + 1])


def _attn_bias_tables(rel_bias, dil):
    blk = ATTN_BLOCK
    period = 3 * blk
    k = np.arange(period)
    valid = k <= blk
    dist = np.where(valid, blk - k, 0) * dil
    max_exact = NUM_BUCKETS // 2
    d_f = np.maximum(dist, 1).astype(np.float32)
    large = max_exact + (np.log(d_f / np.float32(max_exact)) / np.float32(math.log(MAX_DISTANCE / max_exact))
                         * np.float32(NUM_BUCKETS - max_exact)).astype(np.int32)
    large = np.minimum(large, NUM_BUCKETS - 1)
    bucket = np.where(dist < max_exact, dist, large).astype(np.int32)
    w = jnp.where(jnp.asarray(valid)[None, :], rel_bias[jnp.asarray(bucket)].T.astype(F32), NEG_INF)
    t1 = jnp.tile(w, (1, blk))[:, :blk * (period - 1)].reshape(N_HEADS_A, blk, period - 1)[:, :, :2 * blk]
    has_prev = np.arange(2 * blk)[None, None, :] >= blk
    t0 = jnp.where(jnp.asarray(has_prev), t1, NEG_INF)
    return jnp.stack([t0, t1])


def _dilated_attention(qkv, bias_tab, batch, seq, dil):
    blk = ATTN_BLOCK
    l = seq // dil
    nb = l // blk
    ncb = 3
    pv = qkv.reshape(batch, l, dil * ncb * D_ATTN)

    def cur(c):
        return pl.BlockSpec((1, blk, D_ATTN), lambda b, r, n: (b, n, r * ncb + c))

    def prev(c):
        return pl.BlockSpec((1, blk, D_ATTN), lambda b, r, n: (b, jnp.maximum(n - 1, 0), r * ncb + c))

    o, lse = pl.pallas_call(
        _attn_kernel,
        grid=(batch, dil, nb),
        in_specs=[cur(0), prev(1), cur(1), prev(2), cur(2),
                  pl.BlockSpec((2, N_HEADS_A, blk, 2 * blk), lambda b, r, n: (0, 0, 0, 0))],
        out_specs=[pl.BlockSpec((1, blk, D_ATTN), lambda b, r, n: (b, n, r)),
                   pl.BlockSpec((1, blk, LANES), lambda b, r, n: (b, n, r))],
        out_shape=[jax.ShapeDtypeStruct((batch, l, dil * D_ATTN), BF16),
                   jax.ShapeDtypeStruct((batch, l, dil * LANES), F32)],
        compiler_params=_cparams(("parallel", "parallel", "arbitrary")),
        name=f"dilated_attn_d{dil}",
    )(pv, pv, pv, pv, pv, bias_tab)
    return o.reshape(batch * l, dil * D_ATTN), lse.reshape(batch * l, dil * LANES)


def _attn_combine_kernel(o1_ref, o2_ref, o3_ref, l1_ref, l2_ref, l3_ref, e_ref, et_ref, beta_ref, y_ref,
                         osc_ref, lsc_ref):
    tm = y_ref.shape[0]
    ntile = D_ATTN // LANES
    lses, outs = [], []
    for slot, ((_, dil), o_ref, l_ref) in enumerate(zip(DILATED_CONFIGS, (o1_ref, o2_ref, o3_ref),
                                                        (l1_ref, l2_ref, l3_ref))):
        if dil == 1:
            lses.append(l_ref[...])
            outs.append(lambda o_ref=o_ref: o_ref[...].astype(F32))
            continue
        rows = tm // dil
        for r in range(dil):
            lsc_ref[slot, pl.ds(r, rows, stride=dil), :] = l_ref[:, r * LANES:(r + 1) * LANES]
            for c in range(ntile):
                col = r * D_ATTN + c * LANES
                osc_ref[slot, c, pl.ds(r, rows, stride=dil), :] = o_ref[:, col:col + LANES].astype(F32)
        lses.append(lsc_ref[slot])
        outs.append(lambda slot=slot: jnp.concatenate([osc_ref[slot, c] for c in range(ntile)], axis=-1))
    mx = jnp.maximum(jnp.maximum(lses[0], lses[1]), lses[2])
    ws = [jnp.exp(l - mx) for l in lses]
    tot = ws[0] + ws[1] + ws[2]
    e = e_ref[...]
    acc = None
    for w, load_o in zip(ws, outs):
        term = _dot_f32_lhs(w / tot, e) * load_o()
        acc = term if acc is None else acc + term
    ss = _dot_f32_lhs(acc * acc, et_ref[...])
    inv = lax.rsqrt(ss * (1.0 / HEAD_DIM_A) + HEAD_NORM_EPS)
    y_ref[...] = (acc * _dot_f32_lhs(inv, e) * beta_ref[...]).astype(y_ref.dtype)


def _attn_combine(os_, lses, beta_attn):
    n = os_[0].shape[0]
    tm = min(COMBINE_TM, n)
    head_of_lane = np.arange(D_ATTN) // HEAD_DIM_A
    e = (np.arange(LANES)[:, None] == head_of_lane[None, :]).astype(np.float32)
    e_j = jnp.asarray(e, BF16)
    et_j = jnp.asarray(e.T, BF16)
    dils = [d for _, d in DILATED_CONFIGS]
    dilated = lambda w: [pl.BlockSpec((tm // d, d * w), lambda i: (i, 0)) for d in dils]
    full = lambda a, b: pl.BlockSpec((a, b), lambda i: (0, 0))
    return pl.pallas_call(
        _attn_combine_kernel,
        grid=(n // tm,),
        in_specs=dilated(D_ATTN) + dilated(LANES) + [full(LANES, D_ATTN), full(D_ATTN, LANES), full(1, D_ATTN)],
        out_specs=pl.BlockSpec((tm, D_ATTN), lambda i: (i, 0)),
        out_shape=jax.ShapeDtypeStruct((n, D_ATTN), BF16),
        scratch_shapes=[pltpu.VMEM((len(dils), D_ATTN // LANES, tm, LANES), F32),
                        pltpu.VMEM((len(dils), tm, LANES), F32)],
        compiler_params=_cparams(("parallel",)),
        name="attn_combine",
    )(*os_, *lses, e_j, et_j, beta_attn.reshape(1, D_ATTN).astype(F32))


def _mlstm_kernel(qp_ref, kp_ref, qprev_ref, kprev_ref, v_ref, og_ref, g_ref, gb_ref, cw_ref, cb_ref,
                  beta_ref, y_ref, c_ref, n_ref, m_ref):
    step = pl.program_id(1)
    ch = MLSTM_CHUNK
    dh = HEAD_DIM_M

    @pl.when(step == 0)
    def _():
        c_ref[...] = jnp.zeros(c_ref.shape, F32)
        n_ref[...] = jnp.zeros(n_ref.shape, F32)
        m_ref[...] = jnp.zeros(m_ref.shape, F32)

    def conv_silu(x_ref, prev_ref, coff, b):
        x = x_ref[b].astype(F32)
        p = jnp.where(step > 0, prev_ref[b].astype(F32), 0.0)
        xe = jnp.concatenate([p, x], axis=0)
        npad = p.shape[0]
        cols = slice(coff, coff + D_MLSTM)
        acc = cb_ref[:, cols] + cw_ref[CONV_K - 1:CONV_K, cols] * x
        for s in range(1, CONV_K):
            shifted = pltpu.roll(xe, s, 0)[npad:]
            acc = acc + cw_ref[CONV_K - 1 - s:CONV_K - s, cols] * shifted
        return acc * _sigmoid(acc)

    row_i = lax.broadcasted_iota(jnp.int32, (ch, ch), 0)
    col_i = lax.broadcasted_iota(jnp.int32, (ch, ch), 1)
    causal = row_i >= col_i
    tri = jnp.where(causal, 1.0, 0.0).astype(BF16)
    upp = jnp.where(row_i <= col_i, 1.0, 0.0).astype(BF16)
    b = 0
    qf = conv_silu(qp_ref, qprev_ref, 0, b)
    kf = conv_silu(kp_ref, kprev_ref, D_MLSTM, b) * (dh ** -0.5)
    qb = qf.astype(BF16)
    kb = kf.astype(BF16)
    g = g_ref[b] + gb_ref[...]
    gt = g.T
    b_cols = _dot_f32_rhs(tri, _log_sigmoid(g))
    b_rows = _dot_f32_lhs(_log_sigmoid(gt), upp)

    for h in range(N_HEADS_M):
        st = h
        hs = slice(h * dh, (h + 1) * dh)
        fi = N_HEADS_M + h
        i_row = gt[h:h + 1, :]
        i_col = g[:, h:h + 1]
        b_row = b_rows[fi:fi + 1, :]
        b_col = b_cols[:, fi:fi + 1]
        m_prev = m_ref[st][:, 0:1]
        q_h, k_h = qb[:, hs], kb[:, hs]
        v_h = v_ref[b, :, hs]

        dmat = jnp.where(causal, b_col - b_row + i_row, NEG_INF)
        m_inter = b_col + m_prev
        m_t = jnp.maximum(m_inter, jnp.max(dmat, axis=-1, keepdims=True))
        w = jnp.exp(dmat - m_t) * lax.dot_general(q_h, k_h, (((1,), (1,)), ((), ())),
                                                  preferred_element_type=F32)
        decay = jnp.exp(m_inter - m_t)
        c_old = c_ref[st]
        inter = lax.dot_general(q_h, c_old.astype(BF16), (((1,), (1,)), ((), ())), preferred_element_type=F32)
        num = _dot(w.astype(BF16), v_h) + decay * inter
        n_old = n_ref[st]
        den = jnp.sum(w, axis=-1, keepdims=True) + decay * jnp.sum(qf[:, hs] * n_old, axis=-1, keepdims=True)
        hh = num / jnp.maximum(jnp.abs(den), jnp.exp(-m_t))

        g_last = b_col[ch - 1:ch, :]
        a_row = g_last - b_row + i_row
        a_col = g_last - b_col + i_col
        m_new = jnp.maximum(g_last + m_prev, jnp.max(a_row, axis=-1, keepdims=True))
        carry = jnp.exp(g_last + m_prev - m_new)
        wa_col = jnp.exp(a_col - m_new)
        wv = (wa_col * v_h.astype(F32)).astype(BF16)
        c_ref[st] = carry * c_old + lax.dot_general(wv, k_h, (((0,), (0,)), ((), ())), preferred_element_type=F32)
        n_ref[st] = carry * n_old + jnp.sum(wa_col * kf[:, hs], axis=0, keepdims=True)
        m_ref[st] = jnp.broadcast_to(m_new, (1, LANES))

        gated = _sigmoid(og_ref[b, :, hs].astype(F32)) * hh
        ms = jnp.sum(gated * gated, axis=-1, keepdims=True) * (1.0 / dh)
        y_ref[b, :, hs] = (gated * lax.rsqrt(ms + HEAD_NORM_EPS) * beta_ref[:, hs]).astype(y_ref.dtype)


def _mlstm(proj, gates, gate_bias, conv_w, conv_b, beta_mlstm, batch, seq):
    ch = MLSTM_CHUNK
    nchunk = seq // ch
    pv = proj.reshape(batch, seq, 4 * D_MLSTM)
    gv = gates.reshape(batch, seq, LANES)
    prev_rows = 16
    cb0 = 0

    def cur(c):
        return pl.BlockSpec((1, ch, D_MLSTM), lambda b, n: (b, n, c))

    def prev(c):
        per = ch // prev_rows
        return pl.BlockSpec((1, prev_rows, D_MLSTM), lambda b, n: (b, jnp.maximum(n * per - 1, 0), c))

    const = lambda a, b_: pl.BlockSpec((a, b_), lambda b, n: (0, 0))
    y = pl.pallas_call(
        _mlstm_kernel,
        grid=(batch, nchunk),
        in_specs=[cur(cb0), cur(cb0 + 1), prev(cb0), prev(cb0 + 1), cur(cb0 + 2), cur(cb0 + 3),
                  pl.BlockSpec((1, ch, LANES), lambda b, n: (b, n, 0)),
                  const(1, LANES), const(CONV_K, 2 * D_MLSTM), const(1, 2 * D_MLSTM), const(1, D_MLSTM)],
        out_specs=pl.BlockSpec((1, ch, D_MLSTM), lambda b, n: (b, n, 0)),
        out_shape=jax.ShapeDtypeStruct((batch, seq, D_MLSTM), BF16),
        scratch_shapes=[pltpu.VMEM((N_HEADS_M, HEAD_DIM_M, HEAD_DIM_M), F32),
                        pltpu.VMEM((N_HEADS_M, 1, HEAD_DIM_M), F32),
                        pltpu.VMEM((N_HEADS_M, 1, LANES), F32)],
        compiler_params=_cparams(("parallel", "arbitrary")),
        name="mlstm",
    )(pv, pv, pv, pv, pv, pv, gv, gate_bias, conv_w.astype(F32), conv_b.reshape(1, -1).astype(F32),
      beta_mlstm.reshape(1, D_MLSTM).astype(F32))
    return y.reshape(batch * seq, D_MLSTM)


def _layer_norm(z, g, b):
    mu = jnp.mean(z, axis=-1, keepdims=True)
    zc = z - mu
    var = jnp.mean(zc * zc, axis=-1, keepdims=True)
    return zc * lax.rsqrt(var + LN_EPS) * g + b


PACK_ROWS = D_MODEL // (2 * LANES)


def _store_packed_rows(dst_ref, x, first=0):
    rows = x.shape[0]
    half = D_MODEL // 2
    for s in range(PACK_ROWS):
        lo = x[:, s * LANES:(s + 1) * LANES].astype(BF16).astype(F32)
        hi = x[:, half + s * LANES:half + (s + 1) * LANES].astype(BF16).astype(F32)
        word = pltpu.bitcast(hi, jnp.int32) | lax.shift_right_logical(pltpu.bitcast(lo, jnp.int32), 16)
        dst_ref[pl.ds(first * PACK_ROWS + s, rows, stride=PACK_ROWS), :] = word


def _load_packed_rows(src_ref, first, rows):
    los, his = [], []
    for s in range(PACK_ROWS):
        word = src_ref[pl.ds(first * PACK_ROWS + s, rows, stride=PACK_ROWS), :]
        los.append(pltpu.bitcast(lax.shift_left(word, 16), F32))
        his.append(pltpu.bitcast(word & jnp.int32(-65536), F32))
    return los, his


def _outproj_kernel(ya_ref, ym_ref, x_ref, w_ref, g_ref, b_ref, h_ref, hp_ref):
    y = _dot(ya_ref[...], w_ref[0:D_ATTN, :]) + _dot(ym_ref[...], w_ref[D_ATTN:D_MODEL, :])
    h = _layer_norm(DN_ALPHA * x_ref[...] + y, g_ref[...], b_ref[...])
    h_ref[...] = h
    _store_packed_rows(hp_ref, h)


def _outproj_ln(ya, ym, x, w_out, g, b):
    n = x.shape[0]
    tm = min(OUTPROJ_TM, n)
    row = lambda w: pl.BlockSpec((tm, w), lambda i: (i, 0))
    full = lambda a, b_: pl.BlockSpec((a, b_), lambda i: (0, 0))
    return pl.pallas_call(
        _outproj_kernel,
        grid=(n // tm,),
        in_specs=[row(D_ATTN), row(D_MLSTM), row(D_MODEL), full(D_MODEL, D_MODEL), full(1, D_MODEL), full(1, D_MODEL)],
        out_specs=[row(D_MODEL), pl.BlockSpec((tm * PACK_ROWS, LANES), lambda i: (i, 0))],
        out_shape=[jax.ShapeDtypeStruct((n, D_MODEL), F32),
                   jax.ShapeDtypeStruct((n * PACK_ROWS, LANES), jnp.int32)],
        compiler_params=_cparams(("parallel",)),
        name="out_proj_ln",
    )(ya, ym, x, w_out, g.reshape(1, -1), b.reshape(1, -1))


def _router_kernel(h_ref, whi_ref, wlo_ref, b_ref, tri_ref, idx_ref, gate_ref, rank_ref, cnt_ref, carry_ref):
    i = pl.program_id(0)

    @pl.when(i == 0)
    def _():
        carry_ref[...] = jnp.zeros(carry_ref.shape, F32)

    x = h_ref[...]
    xhi = x.astype(BF16)
    xlo = (x - xhi.astype(F32)).astype(BF16)
    logits = _dot(xhi, whi_ref[...]) + _dot(xhi, wlo_ref[...]) + _dot(xlo, whi_ref[...]) + b_ref[...]
    tm = logits.shape[0]
    lane = lax.broadcasted_iota(jnp.int32, (tm, LANES), 1)
    lane_f = lane.astype(F32)
    vals = jnp.where(lane < N_EXPERTS, logits, NEG_INF)

    sels, tops = [], []
    for _ in range(TOP_K):
        mx = jnp.max(vals, axis=-1, keepdims=True)
        first = jnp.min(jnp.where(vals == mx, lane_f, float(LANES)), axis=-1, keepdims=True)
        sel = lane_f == first
        sels.append(sel)
        tops.append((mx, first))
        vals = jnp.where(sel, 2.0 * NEG_INF, vals)

    exps = [jnp.exp(mx - tops[0][0]) for mx, _ in tops]
    tot = exps[0] + exps[1] + exps[2] + exps[3]

    onehot = jnp.zeros((tm, LANES), F32)
    for sel in sels:
        onehot = jnp.where(sel, 1.0, onehot)
    before = _dot(tri_ref[...], onehot.astype(BF16)) + carry_ref[...]

    idx_out = jnp.zeros((tm, LANES), F32)
    gate_out = jnp.zeros((tm, LANES), F32)
    rank_out = jnp.zeros((tm, LANES), F32)
    for k in range(TOP_K):
        rank_k = jnp.sum(jnp.where(sels[k], before, 0.0), axis=-1, keepdims=True)
        idx_out = jnp.where(lane == k, tops[k][1], idx_out)
        gate_out = jnp.where(lane == k, exps[k] / tot, gate_out)
        rank_out = jnp.where(lane == k, rank_k, rank_out)
    idx_ref[...] = idx_out.astype(jnp.int32)
    gate_ref[...] = gate_out
    rank_ref[...] = rank_out.astype(jnp.int32)

    carry = carry_ref[...] + jnp.sum(onehot, axis=0, keepdims=True)
    carry_ref[...] = carry
    cnt_ref[...] = carry.astype(jnp.int32)


def _router(h1, w_router, b_router):
    n = h1.shape[0]
    tm = min(ROUTER_TM, n)
    wpad = jnp.zeros((D_MODEL, LANES), F32).at[:, :N_EXPERTS].set(w_router)
    whi = wpad.astype(BF16)
    wlo = (wpad - whi.astype(F32)).astype(BF16)
    bpad = jnp.zeros((1, LANES), F32).at[0, :N_EXPERTS].set(b_router)
    tri = jnp.asarray(np.tril(np.ones((tm, tm), np.float32), -1), BF16)
    row = lambda w: pl.BlockSpec((tm, w), lambda i: (i, 0))
    full = lambda a, b_: pl.BlockSpec((a, b_), lambda i: (0, 0))
    return pl.pallas_call(
        _router_kernel,
        grid=(n // tm,),
        in_specs=[row(D_MODEL), full(D_MODEL, LANES), full(D_MODEL, LANES), full(1, LANES), full(tm, tm)],
        out_specs=[row(LANES), row(LANES), row(LANES), full(1, LANES)],
        out_shape=[jax.ShapeDtypeStruct((n, LANES), jnp.int32), jax.ShapeDtypeStruct((n, LANES), F32),
                   jax.ShapeDtypeStruct((n, LANES), jnp.int32), jax.ShapeDtypeStruct((1, LANES), jnp.int32)],
        scratch_shapes=[pltpu.VMEM((1, LANES), F32)],
        compiler_params=_cparams(("arbitrary",)),
        name="router",
    )(h1, whi, wlo, bpad, tri)


def _ffn_kernel(bexp_ref, nused_ref, nvalid_ref, slot_ref, src_cur_ref, src_nxt_ref, wg_ref, wu_ref, wdl_ref, wdh_ref,
                bg_ref, bu_ref, bdl_ref, bdh_ref, xp_ref, ys_ref, xbuf_ref, x_ref, act_ref, obuf_ref, wgb_ref, wub_ref,
                wdlb_ref, wdhb_ref, in_sem, out_sem):
    del bexp_ref
    i = pl.program_id(0)
    g = pl.program_id(1)
    n_up = act_ref.shape[0]
    last_g = pl.num_programs(1) - 1
    last_i = pl.num_programs(0) - 1
    n_used = nused_ref[0]
    used = i < n_used
    rb = x_ref.shape[0]
    sub = FFN_SUB
    th = wdlb_ref.shape[1]
    unroll = 8

    def sub_blocks(blk):
        return (nvalid_ref[blk] + (sub - 1)) // sub

    nsub = sub_blocks(i)

    def start_gather(src_ref, nrows):
        def body(j, carry):
            for u in range(unroll):
                jj = j * unroll + u
                src = pl.multiple_of(src_ref[jj], PACK_ROWS)
                dst = pl.multiple_of(jj * PACK_ROWS, PACK_ROWS)
                pltpu.make_async_copy(xp_ref.at[pl.ds(src, PACK_ROWS)], xbuf_ref.at[pl.ds(dst, PACK_ROWS)],
                                      in_sem).start(priority=ROW_DMA_PRIORITY)
            return carry

        lax.fori_loop(0, nrows // unroll, body, 0)

    def wait_gather(nrows):
        npk = pl.multiple_of(nrows * PACK_ROWS, PACK_ROWS)
        pltpu.make_async_copy(xp_ref.at[pl.ds(0, npk)], xbuf_ref.at[pl.ds(0, npk)], in_sem).wait()

    def start_scatter(nvalid):
        def one(jj):
            src = pl.multiple_of(jj * PACK_ROWS, PACK_ROWS)
            dst = pl.multiple_of(slot_ref[jj] * PACK_ROWS, PACK_ROWS)
            pltpu.make_async_copy(obuf_ref.at[pl.ds(src, PACK_ROWS)], ys_ref.at[pl.ds(dst, PACK_ROWS)],
                                  out_sem).start(priority=ROW_DMA_PRIORITY)

        def body(j, carry):
            for u in range(unroll):
                one(j * unroll + u)
            return carry

        nfull = nvalid // unroll
        lax.fori_loop(0, nfull, body, 0)
        for u in range(unroll - 1):
            @pl.when(nfull * unroll + u < nvalid)
            def _():
                one(nfull * unroll + u)

    def wait_scatter(blk):
        npk = pl.multiple_of(nvalid_ref[blk] * PACK_ROWS, PACK_ROWS)
        pltpu.make_async_copy(obuf_ref.at[pl.ds(0, npk)], ys_ref.at[pl.ds(0, npk)], out_sem).wait()

    @pl.when(jnp.logical_and(used, g == 0))
    def _():
        @pl.when(i == 0)
        def _():
            start_gather(src_cur_ref, nsub * sub)

        wait_gather(nsub * sub)
        for j in range(rb // sub):
            @pl.when(j < nsub)
            def _():
                los, his = _load_packed_rows(xbuf_ref, j * sub, sub)
                rows = slice(j * sub, (j + 1) * sub)
                for s in range(PACK_ROWS):
                    x_ref[rows, s * LANES:(s + 1) * LANES] = los[s].astype(BF16)
                    x_ref[rows, D_MODEL // 2 + s * LANES:D_MODEL // 2 + (s + 1) * LANES] = his[s].astype(BF16)

    @pl.when(jnp.logical_and(g == 1, i + 1 < n_used))
    def _():
        start_gather(src_nxt_ref, sub_blocks(i + 1) * sub)

    @pl.when(jnp.logical_and(used, g < n_up))
    def _():
        wgb_ref[...] = wg_ref[0].astype(BF16)
        wub_ref[...] = wu_ref[0].astype(BF16)

        def sub_body(j, carry):
            rows = pl.ds(pl.multiple_of(j * sub, sub), sub)
            x = x_ref[rows, :]
            hg = _dot(x, wgb_ref[...]) + bg_ref[0]
            hu = _dot(x, wub_ref[...]) + bu_ref[0]
            gate = jnp.minimum(hg, SWIGLU_LIMIT)
            up = jnp.clip(hu, -SWIGLU_LIMIT, SWIGLU_LIMIT)
            act_ref[g, rows, :] = ((up + 1.0) * (gate * _sigmoid(SWIGLU_ALPHA * gate))).astype(BF16)
            return carry

        lax.fori_loop(0, nsub, sub_body, 0)

    @pl.when(jnp.logical_and(used, g >= n_up))
    def _():
        @pl.when(jnp.logical_and(g == n_up, i > 0))
        def _():
            wait_scatter(i - 1)

        wdlb_ref[...] = wdl_ref[0].astype(BF16)
        wdhb_ref[...] = wdh_ref[0].astype(BF16)
        tile0 = (g - n_up) * (th // LANES)

        def sub_body(j, carry):
            row0 = pl.multiple_of(j * sub, sub)
            a = jnp.concatenate([act_ref[c, pl.ds(row0, sub), :] for c in range(n_up)], axis=-1)
            ylo = _dot(a, wdlb_ref[...]) + bdl_ref[0]
            yhi = _dot(a, wdhb_ref[...]) + bdh_ref[0]
            for s in range(th // LANES):
                lo = ylo[:, s * LANES:(s + 1) * LANES].astype(BF16).astype(F32)
                hi = yhi[:, s * LANES:(s + 1) * LANES].astype(BF16).astype(F32)
                word = pltpu.bitcast(hi, jnp.int32) | lax.shift_right_logical(pltpu.bitcast(lo, jnp.int32), 16)
                obuf_ref[pl.ds(row0 * PACK_ROWS + tile0 + s, sub, stride=PACK_ROWS), :] = word
            return carry

        lax.fori_loop(0, nsub, sub_body, 0)

        @pl.when(g == last_g)
        def _():
            start_scatter(nvalid_ref[i])

    @pl.when(jnp.logical_and(i == last_i, g == last_g))
    def _():
        wait_scatter(jnp.minimum(i, n_used - 1))


def _expert_ffn(h_packed, slot_assign, blk_exp, blk_valid, n_used, w_up, b_up, w_down, b_down):
    rb, tf, th = FFN_ROWS, FFN_TF, FFN_TH
    nblk = slot_assign.shape[0] // rb
    n_up = D_FF // tf
    n_dn = D_MODEL // 2 // th
    n_assign = h_packed.shape[0] // PACK_ROWS * TOP_K
    slot_src = jnp.maximum(slot_assign, 0) // TOP_K * PACK_ROWS

    def clamp(i, nu):
        return jnp.minimum(i, nu[0] - 1)

    def up_sel(i, g, nu):
        return jnp.where(i < nu[0], jnp.minimum(g, n_up - 1), n_up - 1)

    def dn_sel(i, g, nu):
        return jnp.where(i < nu[0], jnp.clip(g - n_up, 0, n_dn - 1), n_dn - 1)

    def dn_exp(i, g, be, nu):
        early = jnp.logical_and(g < n_up - 1, i < nu[0])
        return be[jnp.where(early, jnp.maximum(clamp(i, nu) - 1, 0), clamp(i, nu))]

    def dn_col(i, g, nu):
        early = jnp.logical_and(g < n_up - 1, i < nu[0])
        return jnp.where(early, n_dn - 1, dn_sel(i, g, nu))

    grid_spec = pltpu.PrefetchScalarGridSpec(
        num_scalar_prefetch=3,
        grid=(nblk, n_up + n_dn),
        in_specs=[
            pl.BlockSpec((rb,), lambda i, g, be, nu, *_: (clamp(i, nu),), memory_space=pltpu.SMEM),
            pl.BlockSpec((rb,), lambda i, g, be, nu, *_: (clamp(i, nu),), memory_space=pltpu.SMEM),
            pl.BlockSpec((rb,), lambda i, g, be, nu, *_: (clamp(i + 1, nu),), memory_space=pltpu.SMEM),
            pl.BlockSpec((1, D_MODEL, tf), lambda i, g, be, nu, *_: (be[clamp(i, nu)], 0, up_sel(i, g, nu))),
            pl.BlockSpec((1, D_MODEL, tf), lambda i, g, be, nu, *_: (be[clamp(i, nu)], 0, n_up + up_sel(i, g, nu))),
            pl.BlockSpec((1, D_FF, th), lambda i, g, be, nu, *_: (dn_exp(i, g, be, nu), 0, dn_col(i, g, nu))),
            pl.BlockSpec((1, D_FF, th), lambda i, g, be, nu, *_: (dn_exp(i, g, be, nu), 0, n_dn + dn_col(i, g, nu))),
            pl.BlockSpec((1, 1, tf), lambda i, g, be, nu, *_: (be[clamp(i, nu)], 0, up_sel(i, g, nu))),
            pl.BlockSpec((1, 1, tf), lambda i, g, be, nu, *_: (be[clamp(i, nu)], 0, n_up + up_sel(i, g, nu))),
            pl.BlockSpec((1, 1, th), lambda i, g, be, nu, *_: (be[clamp(i, nu)], 0, dn_sel(i, g, nu))),
            pl.BlockSpec((1, 1, th), lambda i, g, be, nu, *_: (be[clamp(i, nu)], 0, n_dn + dn_sel(i, g, nu))),
            pl.BlockSpec(memory_space=pl.ANY),
        ],
        out_specs=pl.BlockSpec(memory_space=pl.ANY),
        scratch_shapes=[pltpu.VMEM((rb * PACK_ROWS, LANES), jnp.int32),
                        pltpu.VMEM((rb, D_MODEL), BF16),
                        pltpu.VMEM((n_up, rb, tf), BF16),
                        pltpu.VMEM((rb * PACK_ROWS, LANES), jnp.int32),
                        pltpu.VMEM((D_MODEL, tf), BF16), pltpu.VMEM((D_MODEL, tf), BF16),
                        pltpu.VMEM((D_FF, th), BF16), pltpu.VMEM((D_FF, th), BF16),
                        pltpu.SemaphoreType.DMA, pltpu.SemaphoreType.DMA],
    )
    assert n_up >= 2, "the next block's gather is issued at step 1, after this block's rows are unpacked at step 0"
    return pl.pallas_call(
        _ffn_kernel,
        grid_spec=grid_spec,
        out_shape=jax.ShapeDtypeStruct((n_assign * PACK_ROWS, LANES), jnp.int32),
        compiler_params=pltpu.CompilerParams(dimension_semantics=("arbitrary", "arbitrary"),
                                             vmem_limit_bytes=FFN_VMEM_LIMIT, disable_bounds_checks=True),
        name="expert_ffn",
    )(blk_exp, n_used, blk_valid, slot_assign, slot_src, slot_src, w_up, w_up, w_down, w_down,
      b_up.reshape(N_EXPERTS, 1, 2 * D_FF), b_up.reshape(N_EXPERTS, 1, 2 * D_FF),
      b_down.reshape(N_EXPERTS, 1, D_MODEL), b_down.reshape(N_EXPERTS, 1, D_MODEL), h_packed)


def _combine_kernel(gate_ref, h_ref, g_ref, b_ref, ys_ref, o_ref):
    tm = h_ref.shape[0]
    gates = gate_ref[...]
    cols = [None] * (2 * PACK_ROWS)
    for k in range(TOP_K):
        gk = gates[:, k:k + 1]
        for s in range(PACK_ROWS):
            word = ys_ref[pl.ds(k * PACK_ROWS + s, tm, stride=TOP_K * PACK_ROWS), :]
            lo = pltpu.bitcast(lax.shift_left(word, 16), F32)
            hi = pltpu.bitcast(word & jnp.int32(-65536), F32)
            for c, blk in ((s, lo), (PACK_ROWS + s, hi)):
                cols[c] = gk * blk if cols[c] is None else cols[c] + gk * blk
    z = DN_ALPHA * h_ref[...] + jnp.concatenate(cols, axis=-1)
    o_ref[...] = _layer_norm(z, g_ref[...], b_ref[...])


def _combine_ln(gates, h1, ys, g, b):
    n = h1.shape[0]
    tm = min(COMBINE_LN_TM, n)
    row = lambda w: pl.BlockSpec((tm, w), lambda i: (i, 0))
    full = lambda a, b_: pl.BlockSpec((a, b_), lambda i: (0, 0))
    return pl.pallas_call(
        _combine_kernel,
        grid=(n // tm,),
        in_specs=[row(LANES), row(D_MODEL), full(1, D_MODEL), full(1, D_MODEL),
                  pl.BlockSpec((tm * TOP_K * PACK_ROWS, LANES), lambda i: (i, 0))],
        out_specs=row(D_MODEL),
        out_shape=jax.ShapeDtypeStruct((n, D_MODEL), F32),
        compiler_params=_cparams(("parallel",)),
        name="combine_ln",
    )(gates, h1, g.reshape(1, -1), b.reshape(1, -1), ys)


def _layer(h, w_in, b_igate, b_fgate, conv_w, conv_b, rel_bias, beta_attn, beta_mlstm, w_out,
           ln1_g, ln1_b, w_router, b_router, w_up, b_up, w_down, b_down, ln2_g, ln2_b, batch, seq):
    n = batch * seq
    w_qkv = w_in[:, :3 * D_ATTN].astype(BF16)
    w_mix = w_in[:, 3 * D_ATTN:MAIN_COLS].astype(BF16)
    w_gate = jnp.zeros((D_MODEL, LANES), BF16).at[:, :2 * N_HEADS_M].set(w_in[:, MAIN_COLS:].astype(BF16))
    qkvs = _qkv_project(h, w_qkv, min(QKV_TM, n))
    proj_m, gates = _project(h, w_mix, w_gate, min(PROJ_TM, n), PROJ_TN)
    gate_bias = jnp.zeros((1, LANES), F32).at[0, :2 * N_HEADS_M].set(jnp.concatenate([b_igate, b_fgate]))

    outs, lses = [], []
    for (_, dil), qkv in zip(DILATED_CONFIGS, qkvs):
        o, l = _dilated_attention(qkv, _attn_bias_tables(rel_bias, dil), batch, seq, dil)
        outs.append(o)
        lses.append(l)
    y_attn = _attn_combine(outs, lses, beta_attn)
    y_mlstm = _mlstm(proj_m, gates, gate_bias, conv_w, conv_b, beta_mlstm, batch, seq)

    h1, h1_packed = _outproj_ln(y_attn, y_mlstm, h, w_out.astype(BF16), ln1_g, ln1_b)

    top_idx, top_gate, rank, counts = _router(h1, w_router, b_router)
    counts = counts[0, :N_EXPERTS]
    nb = (counts + FFN_ROWS - 1) // FFN_ROWS
    per = (counts + nb * FFN_SUB - 1) // jnp.maximum(nb * FFN_SUB, 1) * FFN_SUB
    blk_end = jnp.cumsum(nb)
    blk_start = blk_end - nb
    e_idx, r_idx = top_idx[:, :TOP_K], rank[:, :TOP_K]
    per_a = jnp.maximum(per[e_idx], 1)
    blk_in_e = jnp.floor((r_idx.astype(F32) + 0.5) / per_a.astype(F32)).astype(jnp.int32)
    dest = ((blk_start[e_idx] + blk_in_e) * FFN_ROWS + (r_idx - blk_in_e * per_a)).reshape(-1).astype(jnp.int32)
    nblk = n * TOP_K // FFN_ROWS + N_EXPERTS
    blk_id = jnp.arange(nblk, dtype=jnp.int32)
    blk_exp = jnp.minimum(jnp.sum(blk_end[None, :] <= blk_id[:, None], axis=1), N_EXPERTS - 1).astype(jnp.int32)
    n_used = blk_end[-1:].astype(jnp.int32)
    blk_valid = jnp.clip(counts[blk_exp] - (blk_id - blk_start[blk_exp]) * per[blk_exp], 0,
                         per[blk_exp]).astype(jnp.int32)
    slot_assign = jnp.full((nblk * FFN_ROWS,), -1, jnp.int32).at[dest].set(
        jnp.arange(n * TOP_K, dtype=jnp.int32), unique_indices=True)

    ys = _expert_ffn(h1_packed, slot_assign, blk_exp, blk_valid, n_used, w_up, b_up, w_down, b_down)
    return _combine_ln(top_gate, h1, ys, ln2_g, ln2_b)


def kernel(x, w_in, b_igate, b_fgate, conv_w, conv_b, rel_bias, beta_attn, beta_mlstm, w_out, ln1_g, ln1_b,
           w_router, b_router, w_up, b_up, w_down, b_down, ln2_g, ln2_b):
    batch, seq, d = x.shape
    h = x.reshape(batch * seq, d)
    for l in range(DEPTH):
        h = _layer(h, w_in[l], b_igate[l], b_fgate[l], conv_w[l], conv_b[l], rel_bias, beta_attn[l], beta_mlstm[l],
                   w_out[l], ln1_g[l], ln1_b[l], w_router[l], b_router[l], w_up[l], b_up[l], w_down[l], b_down[l],
                   ln2_g[l], ln2_b[l], batch, seq)
    return h.reshape(batch, seq, d)
```

```python
import functools
import math

import numpy as np
import jax
import jax.numpy as jnp
from jax import lax
from jax.experimental import pallas as pl
from jax.experimental.pallas import tpu as pltpu

F32 = jnp.float32
BF16 = jnp.bfloat16

D_MODEL = 2048
D_ATTN = 1024
HEAD_DIM_A = 64
N_HEADS_A = 16
DILATED_CONFIGS = ((128, 1), (512, 4), (2048, 16))
ATTN_BLOCK = 128
NUM_BUCKETS = 32
MAX_DISTANCE = 2048
D_MLSTM = 1024
N_HEADS_M = 4
HEAD_DIM_M = 256
CONV_K = 4
MLSTM_CHUNK = 128
MAIN_COLS = 3 * D_ATTN + 4 * D_MLSTM
N_EXPERTS = 32
TOP_K = 4
D_FF = 2048
SWIGLU_LIMIT = 7.0
SWIGLU_ALPHA = 1.702
DEPTH = 1
DN_ALPHA = (2 * DEPTH) ** 0.25
LN_EPS = 1e-5
HEAD_NORM_EPS = 1e-6
NEG_INF = -1e30

LANES = 128
VMEM_LIMIT = 48 * 1024 * 1024

PROJ_TM = 1024
PROJ_TN = 1024
COMBINE_TM = 512
OUTPROJ_TM = 512
ROUTER_TM = 512
QKV_TM = 512
ATTN_GROUP = 2
FFN_ROWS = 1024
FFN_SUB = 256
FFN_TF = 512
FFN_TH = 256
FFN_VMEM_LIMIT = 56 * 1024 * 1024
ROW_DMA_PRIORITY = 1
COMBINE_LN_TM = 256


def _cparams(sem, vmem=VMEM_LIMIT):
    return pltpu.CompilerParams(dimension_semantics=sem, vmem_limit_bytes=vmem)


def _dot(a, b):
    return jnp.dot(a, b, preferred_element_type=F32)


def _dot_f32_rhs(a_bf16, b_f32):
    hi = b_f32.astype(BF16)
    lo = (b_f32 - hi.astype(F32)).astype(BF16)
    return _dot(a_bf16, hi) + _dot(a_bf16, lo)


def _dot_f32_lhs(a_f32, b_bf16):
    hi = a_f32.astype(BF16)
    lo = (a_f32 - hi.astype(F32)).astype(BF16)
    return _dot(hi, b_bf16) + _dot(lo, b_bf16)


def _sigmoid(x):
    return 1.0 / (1.0 + jnp.exp(-x))


def _log_sigmoid(x):
    return jnp.minimum(x, 0.0) - jnp.log(1.0 + jnp.exp(-jnp.abs(x)))


def _proj_kernel(x_ref, w_ref, wg_ref, o_ref, g_ref):
    x = x_ref[...].astype(BF16)
    o_ref[...] = _dot(x, w_ref[...]).astype(o_ref.dtype)

    @pl.when(pl.program_id(1) == 0)
    def _():
        g_ref[...] = _dot(x, wg_ref[...])


def _project(x, w, w_gate, tm, tn):
    m, k = x.shape
    n = w.shape[1]
    return pl.pallas_call(
        _proj_kernel,
        grid=(m // tm, n // tn),
        in_specs=[pl.BlockSpec((tm, k), lambda i, j: (i, 0)),
                  pl.BlockSpec((k, tn), lambda i, j: (0, j)),
                  pl.BlockSpec((k, LANES), lambda i, j: (0, 0))],
        out_specs=[pl.BlockSpec((tm, tn), lambda i, j: (i, j)),
                   pl.BlockSpec((tm, LANES), lambda i, j: (i, 0))],
        out_shape=[jax.ShapeDtypeStruct((m, n), BF16), jax.ShapeDtypeStruct((m, LANES), F32)],
        compiler_params=_cparams(("parallel", "arbitrary")),
        name="in_proj",
    )(x, w, w_gate)


def _qkv_proj_kernel(x_ref, w_ref, *refs):
    o_refs, r_ref = refs[:-1], refs[-1]
    res = _dot(x_ref[...].astype(BF16), w_ref[...])
    ntile, tm, _ = r_ref.shape
    wid = ntile * LANES
    for c in range(ntile):
        r_ref[c] = res[:, c * LANES:(c + 1) * LANES]
    for (_, dil), o_ref in zip(DILATED_CONFIGS, o_refs):
        if dil == 1:
            o_ref[...] = res.astype(o_ref.dtype)
        else:
            for r in range(dil):
                for c in range(ntile):
                    col = r * wid + c * LANES
                    o_ref[:, col:col + LANES] = r_ref[c, pl.ds(r, tm // dil, stride=dil), :].astype(o_ref.dtype)


def _qkv_project(x, w, tm):
    m, k = x.shape
    wid = w.shape[1]
    dils = [d for _, d in DILATED_CONFIGS]
    return pl.pallas_call(
        _qkv_proj_kernel,
        grid=(m // tm,),
        in_specs=[pl.BlockSpec((tm, k), lambda i: (i, 0)),
                  pl.BlockSpec((k, wid), lambda i: (0, 0), pipeline_mode=pl.Buffered(1))],
        out_specs=[pl.BlockSpec((tm // d, d * wid), lambda i: (i, 0)) for d in dils],
        out_shape=[jax.ShapeDtypeStruct((m // d, d * wid), BF16) for d in dils],
        scratch_shapes=[pltpu.VMEM((wid // LANES, tm, LANES), F32)],
        compiler_params=_cparams(("parallel",), 56 * 1024 * 1024),
        name="qkv_proj",
    )(x, w)


def _attn_kernel(q_ref, kp_ref, kc_ref, vp_ref, vc_ref, bias_ref, o_ref, lse_ref):
    n = pl.program_id(2)
    tab = jnp.minimum(n, 1)
    lse_ref[...] = jnp.zeros(lse_ref.shape, F32)
    grp, dh, nk = ATTN_GROUP, HEAD_DIM_A, 2 * ATTN_BLOCK
    wid = grp * dh
    lane_head = lax.broadcasted_iota(jnp.int32, (nk, wid), 1) // dh
    zero = jnp.zeros((nk, wid), BF16)
    ones_bd = jnp.concatenate([jnp.where(lane_head == j, 1.0, 0.0).astype(BF16) for j in range(grp)], axis=0)
    for g in range(N_HEADS_A // grp):
        cols = slice(g * wid, (g + 1) * wid)
        q = q_ref[0, :, cols] * (dh ** -0.5)
        kslab = jnp.concatenate([kp_ref[0, :, cols], kc_ref[0, :, cols]], axis=0)
        vslab = jnp.concatenate([vp_ref[0, :, cols], vc_ref[0, :, cols]], axis=0)
        k_bd = jnp.concatenate([jnp.where(lane_head == j, kslab, zero) for j in range(grp)], axis=0)
        v_bd = jnp.concatenate([jnp.where(lane_head == j, vslab, zero) for j in range(grp)], axis=0)
        s_all = lax.dot_general(q, k_bd, (((1,), (1,)), ((), ())), preferred_element_type=F32)
        ps, ms = [], []
        for j in range(grp):
            s = s_all[:, j * nk:(j + 1) * nk] + bias_ref[tab, g * grp + j]
            m = jnp.max(s, axis=-1, keepdims=True)
            ps.append(jnp.exp(s - m).astype(BF16))
            ms.append(m)
        p_all = jnp.concatenate(ps, axis=-1)
        res = _dot(p_all, jnp.concatenate([v_bd, ones_bd], axis=-1))
        den = res[:, wid:]
        o_ref[0, :, cols] = (res[:, :wid] / den).astype(o_ref.dtype)
        for j in range(grp):
            h = g * grp + j
            lse_ref[0, :, h:h + 1] = ms[j] + jnp.log(den[:, j * dh:j * dh + 1])


def _attn_bias_tables(rel_bias, dil):
    blk = ATTN_BLOCK
    period = 3 * blk
    k = np.arange(period)
    valid = k <= blk
    dist = np.where(valid, blk - k, 0) * dil
    max_exact = NUM_BUCKETS // 2
    d_f = np.maximum(dist, 1).astype(np.float32)
    large = max_exact + (np.log(d_f / np.float32(max_exact)) / np.float32(math.log(MAX_DISTANCE / max_exact))
                         * np.float32(NUM_BUCKETS - max_exact)).astype(np.int32)
    large = np.minimum(large, NUM_BUCKETS - 1)
    bucket = np.where(dist < max_exact, dist, large).astype(np.int32)
    w = jnp.where(jnp.asarray(valid)[None, :], rel_bias[jnp.asarray(bucket)].T.astype(F32), NEG_INF)
    t1 = jnp.tile(w, (1, blk))[:, :blk * (period - 1)].reshape(N_HEADS_A, blk, period - 1)[:, :, :2 * blk]
    has_prev = np.arange(2 * blk)[None, None, :] >= blk
    t0 = jnp.where(jnp.asarray(has_prev), t1, NEG_INF)
    return jnp.stack([t0, t1])


def _dilated_attention(qkv, bias_tab, batch, seq, dil):
    blk = ATTN_BLOCK
    l = seq // dil
    nb = l // blk
    ncb = 3
    pv = qkv.reshape(batch, l, dil * ncb * D_ATTN)

    def cur(c):
        return pl.BlockSpec((1, blk, D_ATTN), lambda b, r, n: (b, n, r * ncb + c))

    def prev(c):
        return pl.BlockSpec((1, blk, D_ATTN), lambda b, r, n: (b, jnp.maximum(n - 1, 0), r * ncb + c))

    o, lse = pl.pallas_call(
        _attn_kernel,
        grid=(batch, dil, nb),
        in_specs=[cur(0), prev(1), cur(1), prev(2), cur(2),
                  pl.BlockSpec((2, N_HEADS_A, blk, 2 * blk), lambda b, r, n: (0, 0, 0, 0))],
        out_specs=[pl.BlockSpec((1, blk, D_ATTN), lambda b, r, n: (b, n, r)),
                   pl.BlockSpec((1, blk, LANES), lambda b, r, n: (b, n, r))],
        out_shape=[jax.ShapeDtypeStruct((batch, l, dil * D_ATTN), BF16),
                   jax.ShapeDtypeStruct((batch, l, dil * LANES), F32)],
        compiler_params=_cparams(("parallel", "parallel", "arbitrary")),
        name=f"dilated_attn_d{dil}",
    )(pv, pv, pv, pv, pv, bias_tab)
    return o.reshape(batch * l, dil * D_ATTN), lse.reshape(batch * l, dil * LANES)


def _attn_combine_kernel(o1_ref, o2_ref, o3_ref, l1_ref, l2_ref, l3_ref, e_ref, et_ref, beta_ref, y_ref,
                         osc_ref, lsc_ref):
    tm = y_ref.shape[0]
    ntile = D_ATTN // LANES
    lses, outs = [], []
    for slot, ((_, dil), o_ref, l_ref) in enumerate(zip(DILATED_CONFIGS, (o1_ref, o2_ref, o3_ref),
                                                        (l1_ref, l2_ref, l3_ref))):
        if dil == 1:
            lses.append(l_ref[...])
            outs.append(lambda o_ref=o_ref: o_ref[...].astype(F32))
            continue
        rows = tm // dil
        for r in range(dil):
            lsc_ref[slot, pl.ds(r, rows, stride=dil), :] = l_ref[:, r * LANES:(r + 1) * LANES]
            for c in range(ntile):
                col = r * D_ATTN + c * LANES
                osc_ref[slot, c, pl.ds(r, rows, stride=dil), :] = o_ref[:, col:col + LANES].astype(F32)
        lses.append(lsc_ref[slot])
        outs.append(lambda slot=slot: jnp.concatenate([osc_ref[slot, c] for c in range(ntile)], axis=-1))
    mx = jnp.maximum(jnp.maximum(lses[0], lses[1]), lses[2])
    ws = [jnp.exp(l - mx) for l in lses]
    tot = ws[0] + ws[1] + ws[2]
    e = e_ref[...]
    acc = None
    for w, load_o in zip(ws, outs):
        term = _dot_f32_lhs(w / tot, e) * load_o()
        acc = term if acc is None else acc + term
    ss = _dot_f32_lhs(acc * acc, et_ref[...])
    inv = lax.rsqrt(ss * (1.0 / HEAD_DIM_A) + HEAD_NORM_EPS)
    y_ref[...] = (acc * _dot_f32_lhs(inv, e) * beta_ref[...]).astype(y_ref.dtype)


def _attn_combine(os_, lses, beta_attn):
    n = os_[0].shape[0]
    tm = min(COMBINE_TM, n)
    head_of_lane = np.arange(D_ATTN) // HEAD_DIM_A
    e = (np.arange(LANES)[:, None] == head_of_lane[None, :]).astype(np.float32)
    e_j = jnp.asarray(e, BF16)
    et_j = jnp.asarray(e.T, BF16)
    dils = [d for _, d in DILATED_CONFIGS]
    dilated = lambda w: [pl.BlockSpec((tm // d, d * w), lambda i: (i, 0)) for d in dils]
    full = lambda a, b: pl.BlockSpec((a, b), lambda i: (0, 0))
    return pl.pallas_call(
        _attn_combine_kernel,
        grid=(n // tm,),
        in_specs=dilated(D_ATTN) + dilated(LANES) + [full(LANES, D_ATTN), full(D_ATTN, LANES), full(1, D_ATTN)],
        out_specs=pl.BlockSpec((tm, D_ATTN), lambda i: (i, 0)),
        out_shape=jax.ShapeDtypeStruct((n, D_ATTN), BF16),
        scratch_shapes=[pltpu.VMEM((len(dils), D_ATTN // LANES, tm, LANES), F32),
                        pltpu.VMEM((len(dils), tm, LANES), F32)],
        compiler_params=_cparams(("parallel",)),
        name="attn_combine",
    )(*os_, *lses, e_j, et_j, beta_attn.reshape(1, D_ATTN).astype(F32))


def _mlstm_kernel(qp_ref, kp_ref, qprev_ref, kprev_ref, v_ref, og_ref, g_ref, gb_ref, cw_ref, cb_ref,
                  beta_ref, y_ref, c_ref, n_ref, m_ref):
    step = pl.program_id(1)
    ch = MLSTM_CHUNK
    dh = HEAD_DIM_M

    @pl.when(step == 0)
    def _():
        c_ref[...] = jnp.zeros(c_ref.shape, F32)
        n_ref[...] = jnp.zeros(n_ref.shape, F32)
        m_ref[...] = jnp.zeros(m_ref.shape, F32)

    def conv_silu(x_ref, prev_ref, coff, b):
        x = x_ref[b].astype(F32)
        p = jnp.where(step > 0, prev_ref[b].astype(F32), 0.0)
        xe = jnp.concatenate([p, x], axis=0)
        npad = p.shape[0]
        cols = slice(coff, coff + D_MLSTM)
        acc = cb_ref[:, cols] + cw_ref[CONV_K - 1:CONV_K, cols] * x
        for s in range(1, CONV_K):
            shifted = pltpu.roll(xe, s, 0)[npad:]
            acc = acc + cw_ref[CONV_K - 1 - s:CONV_K - s, cols] * shifted
        return acc * _sigmoid(acc)

    row_i = lax.broadcasted_iota(jnp.int32, (ch, ch), 0)
    col_i = lax.broadcasted_iota(jnp.int32, (ch, ch), 1)
    causal = row_i >= col_i
    tri = jnp.where(causal, 1.0, 0.0).astype(BF16)
    upp = jnp.where(row_i <= col_i, 1.0, 0.0).astype(BF16)
    b = 0
    qf = conv_silu(qp_ref, qprev_ref, 0, b)
    kf = conv_silu(kp_ref, kprev_ref, D_MLSTM, b) * (dh ** -0.5)
    qb = qf.astype(BF16)
    kb = kf.astype(BF16)
    g = g_ref[b] + gb_ref[...]
    gt = g.T
    b_cols = _dot_f32_rhs(tri, _log_sigmoid(g))
    b_rows = _dot_f32_lhs(_log_sigmoid(gt), upp)

    for h in range(N_HEADS_M):
        st = h
        hs = slice(h * dh, (h + 1) * dh)
        fi = N_HEADS_M + h
        i_row = gt[h:h + 1, :]
        i_col = g[:, h:h + 1]
        b_row = b_rows[fi:fi + 1, :]
        b_col = b_cols[:, fi:fi + 1]
        m_prev = m_ref[st][:, 0:1]
        q_h, k_h = qb[:, hs], kb[:, hs]
        v_h = v_ref[b, :, hs]

        dmat = jnp.where(causal, b_col - b_row + i_row, NEG_INF)
        m_inter = b_col + m_prev
        m_t = jnp.maximum(m_inter, jnp.max(dmat, axis=-1, keepdims=True))
        w = jnp.exp(dmat - m_t) * lax.dot_general(q_h, k_h, (((1,), (1,)), ((), ())),
                                                  preferred_element_type=F32)
        decay = jnp.exp(m_inter - m_t)
        c_old = c_ref[st]
        inter = lax.dot_general(q_h, c_old.astype(BF16), (((1,), (1,)), ((), ())), preferred_element_type=F32)
        num = _dot(w.astype(BF16), v_h) + decay * inter
        n_old = n_ref[st]
        den = jnp.sum(w, axis=-1, keepdims=True) + decay * jnp.sum(qf[:, hs] * n_old, axis=-1, keepdims=True)
        hh = num / jnp.maximum(jnp.abs(den), jnp.exp(-m_t))

        g_last = b_col[ch - 1:ch, :]
        a_row = g_last - b_row + i_row
        a_col = g_last - b_col + i_col
        m_new = jnp.maximum(g_last + m_prev, jnp.max(a_row, axis=-1, keepdims=True))
        carry = jnp.exp(g_last + m_prev - m_new)
        wa_col = jnp.exp(a_col - m_new)
        wv = (wa_col * v_h.astype(F32)).astype(BF16)
        c_ref[st] = carry * c_old + lax.dot_general(wv, k_h, (((0,), (0,)), ((), ())), preferred_element_type=F32)
        n_ref[st] = carry * n_old + jnp.sum(wa_col * kf[:, hs], axis=0, keepdims=True)
        m_ref[st] = jnp.broadcast_to(m_new, (1, LANES))

        gated = _sigmoid(og_ref[b, :, hs].astype(F32)) * hh
        ms = jnp.sum(gated * gated, axis=-1, keepdims=True) * (1.0 / dh)
        y_ref[b, :, hs] = (gated * lax.rsqrt(ms + HEAD_NORM_EPS) * beta_ref[:, hs]).astype(y_ref.dtype)


def _mlstm(proj, gates, gate_bias, conv_w, conv_b, beta_mlstm, batch, seq):
    ch = MLSTM_CHUNK
    nchunk = seq // ch
    pv = proj.reshape(batch, seq, 4 * D_MLSTM)
    gv = gates.reshape(batch, seq, LANES)
    prev_rows = 16
    cb0 = 0

    def cur(c):
        return pl.BlockSpec((1, ch, D_MLSTM), lambda b, n: (b, n, c))

    def prev(c):
        per = ch // prev_rows
        return pl.BlockSpec((1, prev_rows, D_MLSTM), lambda b, n: (b, jnp.maximum(n * per - 1, 0), c))

    const = lambda a, b_: pl.BlockSpec((a, b_), lambda b, n: (0, 0))
    y = pl.pallas_call(
        _mlstm_kernel,
        grid=(batch, nchunk),
        in_specs=[cur(cb0), cur(cb0 + 1), prev(cb0), prev(cb0 + 1), cur(cb0 + 2), cur(cb0 + 3),
                  pl.BlockSpec((1, ch, LANES), lambda b, n: (b, n, 0)),
                  const(1, LANES), const(CONV_K, 2 * D_MLSTM), const(1, 2 * D_MLSTM), const(1, D_MLSTM)],
        out_specs=pl.BlockSpec((1, ch, D_MLSTM), lambda b, n: (b, n, 0)),
        out_shape=jax.ShapeDtypeStruct((batch, seq, D_MLSTM), BF16),
        scratch_shapes=[pltpu.VMEM((N_HEADS_M, HEAD_DIM_M, HEAD_DIM_M), F32),
                        pltpu.VMEM((N_HEADS_M, 1, HEAD_DIM_M), F32),
                        pltpu.VMEM((N_HEADS_M, 1, LANES), F32)],
        compiler_params=_cparams(("parallel", "arbitrary")),
        name="mlstm",
    )(pv, pv, pv, pv, pv, pv, gv, gate_bias, conv_w.astype(F32), conv_b.reshape(1, -1).astype(F32),
      beta_mlstm.reshape(1, D_MLSTM).astype(F32))
    return y.reshape(batch * seq, D_MLSTM)


def _layer_norm(z, g, b):
    mu = jnp.mean(z, axis=-1, keepdims=True)
    zc = z - mu
    var = jnp.mean(zc * zc, axis=-1, keepdims=True)
    return zc * lax.rsqrt(var + LN_EPS) * g + b


PACK_ROWS = D_MODEL // (2 * LANES)


def _store_packed_rows(dst_ref, x, first=0):
    rows = x.shape[0]
    half = D_MODEL // 2
    for s in range(PACK_ROWS):
        lo = x[:, s * LANES:(s + 1) * LANES].astype(BF16).astype(F32)
        hi = x[:, half + s * LANES:half + (s + 1) * LANES].astype(BF16).astype(F32)
        word = pltpu.bitcast(hi, jnp.int32) | lax.shift_right_logical(pltpu.bitcast(lo, jnp.int32), 16)
        dst_ref[pl.ds(first * PACK_ROWS + s, rows, stride=PACK_ROWS), :] = word


def _load_packed_rows(src_ref, first, rows):
    los, his = [], []
    for s in range(PACK_ROWS):
        word = src_ref[pl.ds(first * PACK_ROWS + s, rows, stride=PACK_ROWS), :]
        los.append(pltpu.bitcast(lax.shift_left(word, 16), F32))
        his.append(pltpu.bitcast(word & jnp.int32(-65536), F32))
    return los, his


def _outproj_kernel(ya_ref, ym_ref, x_ref, w_ref, g_ref, b_ref, h_ref, hp_ref):
    y = _dot(ya_ref[...], w_ref[0:D_ATTN, :]) + _dot(ym_ref[...], w_ref[D_ATTN:D_MODEL, :])
    h = _layer_norm(DN_ALPHA * x_ref[...] + y, g_ref[...], b_ref[...])
    h_ref[...] = h
    _store_packed_rows(hp_ref, h)


def _outproj_ln(ya, ym, x, w_out, g, b):
    n = x.shape[0]
    tm = min(OUTPROJ_TM, n)
    row = lambda w: pl.BlockSpec((tm, w), lambda i: (i, 0))
    full = lambda a, b_: pl.BlockSpec((a, b_), lambda i: (0, 0))
    return pl.pallas_call(
        _outproj_kernel,
        grid=(n // tm,),
        in_specs=[row(D_ATTN), row(D_MLSTM), row(D_MODEL), full(D_MODEL, D_MODEL), full(1, D_MODEL), full(1, D_MODEL)],
        out_specs=[row(D_MODEL), pl.BlockSpec((tm * PACK_ROWS, LANES), lambda i: (i, 0))],
        out_shape=[jax.ShapeDtypeStruct((n, D_MODEL), F32),
                   jax.ShapeDtypeStruct((n * PACK_ROWS, LANES), jnp.int32)],
        compiler_params=_cparams(("parallel",)),
        name="out_proj_ln",
    )(ya, ym, x, w_out, g.reshape(1, -1), b.reshape(1, -1))


def _router_kernel(h_ref, whi_ref, wlo_ref, b_ref, tri_ref, idx_ref, gate_ref, rank_ref, cnt_ref, carry_ref):
    i = pl.program_id(0)

    @pl.when(i == 0)
    def _():
        carry_ref[...] = jnp.zeros(carry_ref.shape, F32)

    x = h_ref[...]
    xhi = x.astype(BF16)
    xlo = (x - xhi.astype(F32)).astype(BF16)
    logits = _dot(xhi, whi_ref[...]) + _dot(xhi, wlo_ref[...]) + _dot(xlo, whi_ref[...]) + b_ref[...]
    tm = logits.shape[0]
    lane = lax.broadcasted_iota(jnp.int32, (tm, LANES), 1)
    lane_f = lane.astype(F32)
    vals = jnp.where(lane < N_EXPERTS, logits, NEG_INF)

    sels, tops = [], []
    for _ in range(TOP_K):
        mx = jnp.max(vals, axis=-1, keepdims=True)
        first = jnp.min(jnp.where(vals == mx, lane_f, float(LANES)), axis=-1, keepdims=True)
        sel = lane_f == first
        sels.append(sel)
        tops.append((mx, first))
        vals = jnp.where(sel, 2.0 * NEG_INF, vals)

    exps = [jnp.exp(mx - tops[0][0]) for mx, _ in tops]
    tot = exps[0] + exps[1] + exps[2] + exps[3]

    onehot = jnp.zeros((tm, LANES), F32)
    for sel in sels:
        onehot = jnp.where(sel, 1.0, onehot)
    before = _dot(tri_ref[...], onehot.astype(BF16)) + carry_ref[...]

    idx_out = jnp.zeros((tm, LANES), F32)
    gate_out = jnp.zeros((tm, LANES), F32)
    rank_out = jnp.zeros((tm, LANES), F32)
    for k in range(TOP_K):
        rank_k = jnp.sum(jnp.where(sels[k], before, 0.0), axis=-1, keepdims=True)
        idx_out = jnp.where(lane == k, tops[k][1], idx_out)
        gate_out = jnp.where(lane == k, exps[k] / tot, gate_out)
        rank_out = jnp.where(lane == k, rank_k, rank_out)
    idx_ref[...] = idx_out.astype(jnp.int32)
    gate_ref[...] = gate_out
    rank_ref[...] = rank_out.astype(jnp.int32)

    carry = carry_ref[...] + jnp.sum(onehot, axis=0, keepdims=True)
    carry_ref[...] = carry
    cnt_ref[...] = carry.astype(jnp.int32)


def _router(h1, w_router, b_router):
    n = h1.shape[0]
    tm = min(ROUTER_TM, n)
    wpad = jnp.zeros((D_MODEL, LANES), F32).at[:, :N_EXPERTS].set(w_router)
    whi = wpad.astype(BF16)
    wlo = (wpad - whi.astype(F32)).astype(BF16)
    bpad = jnp.zeros((1, LANES), F32).at[0, :N_EXPERTS].set(b_router)
    tri = jnp.asarray(np.tril(np.ones((tm, tm), np.float32), -1), BF16)
    row = lambda w: pl.BlockSpec((tm, w), lambda i: (i, 0))
    full = lambda a, b_: pl.BlockSpec((a, b_), lambda i: (0, 0))
    return pl.pallas_call(
        _router_kernel,
        grid=(n // tm,),
        in_specs=[row(D_MODEL), full(D_MODEL, LANES), full(D_MODEL, LANES), full(1, LANES), full(tm, tm)],
        out_specs=[row(LANES), row(LANES), row(LANES), full(1, LANES)],
        out_shape=[jax.ShapeDtypeStruct((n, LANES), jnp.int32), jax.ShapeDtypeStruct((n, LANES), F32),
                   jax.ShapeDtypeStruct((n, LANES), jnp.int32), jax.ShapeDtypeStruct((1, LANES), jnp.int32)],
        scratch_shapes=[pltpu.VMEM((1, LANES), F32)],
        compiler_params=_cparams(("arbitrary",)),
        name="router",
    )(h1, whi, wlo, bpad, tri)


def _ffn_kernel(bexp_ref, nused_ref, nvalid_ref, slot_ref, slot_prv_ref, src_cur_ref, src_nxt_ref, wg_ref, wu_ref,
                wdl_ref, wdh_ref,
                bg_ref, bu_ref, bdl_ref, bdh_ref, xp_ref, ys_ref, xbuf_ref, x_ref, act_ref, obuf_ref, wgb_ref, wub_ref,
                wdlb_ref, wdhb_ref, in_sem, out_sem):
    del bexp_ref
    i = pl.program_id(0)
    g = pl.program_id(1)
    n_up = act_ref.shape[0]
    last_g = pl.num_programs(1) - 1
    last_i = pl.num_programs(0) - 1
    n_used = nused_ref[0]
    used = i < n_used
    rb = x_ref.shape[0]
    sub = FFN_SUB
    th = wdlb_ref.shape[1]
    unroll = 8

    def sub_blocks(blk):
        return (nvalid_ref[blk] + (sub - 1)) // sub

    nsub = sub_blocks(i)

    def start_gather(src_ref, lo, hi):
        def body(j, carry):
            for u in range(unroll):
                jj = j * unroll + u
                src = pl.multiple_of(src_ref[jj], PACK_ROWS)
                dst = pl.multiple_of(jj * PACK_ROWS, PACK_ROWS)
                pltpu.make_async_copy(xp_ref.at[pl.ds(src, PACK_ROWS)], xbuf_ref.at[pl.ds(dst, PACK_ROWS)],
                                      in_sem).start(priority=ROW_DMA_PRIORITY)
            return carry

        lax.fori_loop(lo // unroll, hi // unroll, body, 0)

    def wait_gather(nrows):
        npk = pl.multiple_of(nrows * PACK_ROWS, PACK_ROWS)
        pltpu.make_async_copy(xp_ref.at[pl.ds(0, npk)], xbuf_ref.at[pl.ds(0, npk)], in_sem).wait()

    def start_scatter(slots_ref, lo, hi):
        def one(jj):
            src = pl.multiple_of(jj * PACK_ROWS, PACK_ROWS)
            dst = pl.multiple_of(slots_ref[jj] * PACK_ROWS, PACK_ROWS)
            pltpu.make_async_copy(obuf_ref.at[pl.ds(src, PACK_ROWS)], ys_ref.at[pl.ds(dst, PACK_ROWS)],
                                  out_sem).start(priority=ROW_DMA_PRIORITY)

        def body(j, carry):
            for u in range(unroll):
                one(j * unroll + u)
            return carry

        nfull = jnp.maximum(hi - lo, 0) // unroll
        lax.fori_loop(lo // unroll, lo // unroll + nfull, body, 0)
        for u in range(unroll - 1):
            @pl.when(lo + nfull * unroll + u < hi)
            def _():
                one(lo + nfull * unroll + u)

    def wait_scatter(blk):
        npk = pl.multiple_of(nvalid_ref[blk] * PACK_ROWS, PACK_ROWS)
        pltpu.make_async_copy(obuf_ref.at[pl.ds(0, npk)], ys_ref.at[pl.ds(0, npk)], out_sem).wait()

    @pl.when(jnp.logical_and(used, g == 0))
    def _():
        @pl.when(i == 0)
        def _():
            start_gather(src_cur_ref, 0, nsub * sub)

        wait_gather(nsub * sub)
        for j in range(rb // sub):
            @pl.when(j < nsub)
            def _():
                los, his = _load_packed_rows(xbuf_ref, j * sub, sub)
                rows = slice(j * sub, (j + 1) * sub)
                for s in range(PACK_ROWS):
                    x_ref[rows, s * LANES:(s + 1) * LANES] = los[s].astype(BF16)
                    x_ref[rows, D_MODEL // 2 + s * LANES:D_MODEL // 2 + (s + 1) * LANES] = his[s].astype(BF16)

    @pl.when(jnp.logical_and(jnp.logical_and(g >= 1, g - 1 < sub_blocks(jnp.minimum(i + 1, last_i))), i + 1 < n_used))
    def _():
        start_gather(src_nxt_ref, (g - 1) * sub, g * sub)

    @pl.when(jnp.logical_and(jnp.logical_and(used, i > 0), g + 1 < rb // sub))
    def _():
        start_scatter(slot_prv_ref, (g + 1) * sub, jnp.minimum((g + 2) * sub, nvalid_ref[i - 1]))

    @pl.when(jnp.logical_and(used, g < n_up))
    def _():
        wgb_ref[...] = wg_ref[0].astype(BF16)
        wub_ref[...] = wu_ref[0].astype(BF16)

        def sub_body(j, carry):
            rows = pl.ds(pl.multiple_of(j * sub, sub), sub)
            x = x_ref[rows, :]
            hg = _dot(x, wgb_ref[...]) + bg_ref[0]
            hu = _dot(x, wub_ref[...]) + bu_ref[0]
            gate = jnp.minimum(hg, SWIGLU_LIMIT)
            up = jnp.clip(hu, -SWIGLU_LIMIT, SWIGLU_LIMIT)
            act_ref[g, rows, :] = ((up + 1.0) * (gate * _sigmoid(SWIGLU_ALPHA * gate))).astype(BF16)
            return carry

        lax.fori_loop(0, nsub, sub_body, 0)

    @pl.when(jnp.logical_and(used, g >= n_up))
    def _():
        @pl.when(jnp.logical_and(g == n_up, i > 0))
        def _():
            wait_scatter(i - 1)

        wdlb_ref[...] = wdl_ref[0].astype(BF16)
        wdhb_ref[...] = wdh_ref[0].astype(BF16)
        tile0 = (g - n_up) * (th // LANES)

        def sub_body(j, carry):
            row0 = pl.multiple_of(j * sub, sub)
            a = jnp.concatenate([act_ref[c, pl.ds(row0, sub), :] for c in range(n_up)], axis=-1)
            ylo = _dot(a, wdlb_ref[...]) + bdl_ref[0]
            yhi = _dot(a, wdhb_ref[...]) + bdh_ref[0]
            for s in range(th // LANES):
                lo = ylo[:, s * LANES:(s + 1) * LANES].astype(BF16).astype(F32)
                hi = yhi[:, s * LANES:(s + 1) * LANES].astype(BF16).astype(F32)
                word = pltpu.bitcast(hi, jnp.int32) | lax.shift_right_logical(pltpu.bitcast(lo, jnp.int32), 16)
                obuf_ref[pl.ds(row0 * PACK_ROWS + tile0 + s, sub, stride=PACK_ROWS), :] = word
            return carry

        lax.fori_loop(0, nsub, sub_body, 0)

        @pl.when(g == last_g)
        def _():
            nv = nvalid_ref[i]
            start_scatter(slot_ref, 0, jnp.where(i + 1 < n_used, jnp.minimum(sub, nv), nv))

    @pl.when(jnp.logical_and(i == last_i, g == last_g))
    def _():
        wait_scatter(jnp.minimum(i, n_used - 1))


def _expert_ffn(h_packed, slot_assign, blk_exp, blk_valid, n_used, w_up, b_up, w_down, b_down):
    rb, tf, th = FFN_ROWS, FFN_TF, FFN_TH
    nblk = slot_assign.shape[0] // rb
    n_up = D_FF // tf
    n_dn = D_MODEL // 2 // th
    n_assign = h_packed.shape[0] // PACK_ROWS * TOP_K
    slot_src = jnp.maximum(slot_assign, 0) // TOP_K * PACK_ROWS

    def clamp(i, nu):
        return jnp.minimum(i, nu[0] - 1)

    def up_sel(i, g, nu):
        return jnp.where(i < nu[0], jnp.minimum(g, n_up - 1), n_up - 1)

    def dn_sel(i, g, nu):
        return jnp.where(i < nu[0], jnp.clip(g - n_up, 0, n_dn - 1), n_dn - 1)

    def dn_exp(i, g, be, nu):
        early = jnp.logical_and(g < n_up - 1, i < nu[0])
        return be[jnp.where(early, jnp.maximum(clamp(i, nu) - 1, 0), clamp(i, nu))]

    def dn_col(i, g, nu):
        early = jnp.logical_and(g < n_up - 1, i < nu[0])
        return jnp.where(early, n_dn - 1, dn_sel(i, g, nu))

    grid_spec = pltpu.PrefetchScalarGridSpec(
        num_scalar_prefetch=3,
        grid=(nblk, n_up + n_dn),
        in_specs=[
            pl.BlockSpec((rb,), lambda i, g, be, nu, *_: (clamp(i, nu),), memory_space=pltpu.SMEM),
            pl.BlockSpec((rb,), lambda i, g, be, nu, *_: (jnp.maximum(clamp(i, nu) - 1, 0),), memory_space=pltpu.SMEM),
            pl.BlockSpec((rb,), lambda i, g, be, nu, *_: (clamp(i, nu),), memory_space=pltpu.SMEM),
            pl.BlockSpec((rb,), lambda i, g, be, nu, *_: (clamp(i + 1, nu),), memory_space=pltpu.SMEM),
            pl.BlockSpec((1, D_MODEL, tf), lambda i, g, be, nu, *_: (be[clamp(i, nu)], 0, up_sel(i, g, nu))),
            pl.BlockSpec((1, D_MODEL, tf), lambda i, g, be, nu, *_: (be[clamp(i, nu)], 0, n_up + up_sel(i, g, nu))),
            pl.BlockSpec((1, D_FF, th), lambda i, g, be, nu, *_: (dn_exp(i, g, be, nu), 0, dn_col(i, g, nu))),
            pl.BlockSpec((1, D_FF, th), lambda i, g, be, nu, *_: (dn_exp(i, g, be, nu), 0, n_dn + dn_col(i, g, nu))),
            pl.BlockSpec((1, 1, tf), lambda i, g, be, nu, *_: (be[clamp(i, nu)], 0, up_sel(i, g, nu))),
            pl.BlockSpec((1, 1, tf), lambda i, g, be, nu, *_: (be[clamp(i, nu)], 0, n_up + up_sel(i, g, nu))),
            pl.BlockSpec((1, 1, th), lambda i, g, be, nu, *_: (be[clamp(i, nu)], 0, dn_sel(i, g, nu))),
            pl.BlockSpec((1, 1, th), lambda i, g, be, nu, *_: (be[clamp(i, nu)], 0, n_dn + dn_sel(i, g, nu))),
            pl.BlockSpec(memory_space=pl.ANY),
        ],
        out_specs=pl.BlockSpec(memory_space=pl.ANY),
        scratch_shapes=[pltpu.VMEM((rb * PACK_ROWS, LANES), jnp.int32),
                        pltpu.VMEM((rb, D_MODEL), BF16),
                        pltpu.VMEM((n_up, rb, tf), BF16),
                        pltpu.VMEM((rb * PACK_ROWS, LANES), jnp.int32),
                        pltpu.VMEM((D_MODEL, tf), BF16), pltpu.VMEM((D_MODEL, tf), BF16),
                        pltpu.VMEM((D_FF, th), BF16), pltpu.VMEM((D_FF, th), BF16),
                        pltpu.SemaphoreType.DMA, pltpu.SemaphoreType.DMA],
    )
    assert n_up >= rb // FFN_SUB - 1 and n_up + n_dn > rb // FFN_SUB, "row DMA bursts are spread over the grid steps"
    return pl.pallas_call(
        _ffn_kernel,
        grid_spec=grid_spec,
        out_shape=jax.ShapeDtypeStruct((n_assign * PACK_ROWS, LANES), jnp.int32),
        compiler_params=pltpu.CompilerParams(dimension_semantics=("arbitrary", "arbitrary"),
                                             vmem_limit_bytes=FFN_VMEM_LIMIT, disable_bounds_checks=True),
        name="expert_ffn",
    )(blk_exp, n_used, blk_valid, slot_assign, slot_assign, slot_src, slot_src, w_up, w_up, w_down, w_down,
      b_up.reshape(N_EXPERTS, 1, 2 * D_FF), b_up.reshape(N_EXPERTS, 1, 2 * D_FF),
      b_down.reshape(N_EXPERTS, 1, D_MODEL), b_down.reshape(N_EXPERTS, 1, D_MODEL), h_packed)


def _combine_kernel(gate_ref, h_ref, g_ref, b_ref, ys_ref, o_ref):
    tm = h_ref.shape[0]
    gates = gate_ref[...]
    cols = [None] * (2 * PACK_ROWS)
    for k in range(TOP_K):
        gk = gates[:, k:k + 1]
        for s in range(PACK_ROWS):
            word = ys_ref[pl.ds(k * PACK_ROWS + s, tm, stride=TOP_K * PACK_ROWS), :]
            lo = pltpu.bitcast(lax.shift_left(word, 16), F32)
            hi = pltpu.bitcast(word & jnp.int32(-65536), F32)
            for c, blk in ((s, lo), (PACK_ROWS + s, hi)):
                cols[c] = gk * blk if cols[c] is None else cols[c] + gk * blk
    z = DN_ALPHA * h_ref[...] + jnp.concatenate(cols, axis=-1)
    o_ref[...] = _layer_norm(z, g_ref[...], b_ref[...])


def _combine_ln(gates, h1, ys, g, b):
    n = h1.shape[0]
    tm = min(COMBINE_LN_TM, n)
    row = lambda w: pl.BlockSpec((tm, w), lambda i: (i, 0))
    full = lambda a, b_: pl.BlockSpec((a, b_), lambda i: (0, 0))
    return pl.pallas_call(
        _combine_kernel,
        grid=(n // tm,),
        in_specs=[row(LANES), row(D_MODEL), full(1, D_MODEL), full(1, D_MODEL),
                  pl.BlockSpec((tm * TOP_K * PACK_ROWS, LANES), lambda i: (i, 0))],
        out_specs=row(D_MODEL),
        out_shape=jax.ShapeDtypeStruct((n, D_MODEL), F32),
        compiler_params=_cparams(("parallel",)),
        name="combine_ln",
    )(gates, h1, g.reshape(1, -1), b.reshape(1, -1), ys)


def _layer(h, w_in, b_igate, b_fgate, conv_w, conv_b, rel_bias, beta_attn, beta_mlstm, w_out,
           ln1_g, ln1_b, w_router, b_router, w_up, b_up, w_down, b_down, ln2_g, ln2_b, batch, seq):
    n = batch * seq
    w_qkv = w_in[:, :3 * D_ATTN].astype(BF16)
    w_mix = w_in[:, 3 * D_ATTN:MAIN_COLS].astype(BF16)
    w_gate = jnp.zeros((D_MODEL, LANES), BF16).at[:, :2 * N_HEADS_M].set(w_in[:, MAIN_COLS:].astype(BF16))
    qkvs = _qkv_project(h, w_qkv, min(QKV_TM, n))
    proj_m, gates = _project(h, w_mix, w_gate, min(PROJ_TM, n), PROJ_TN)
    gate_bias = jnp.zeros((1, LANES), F32).at[0, :2 * N_HEADS_M].set(jnp.concatenate([b_igate, b_fgate]))

    outs, lses = [], []
    for (_, dil), qkv in zip(DILATED_CONFIGS, qkvs):
        o, l = _dilated_attention(qkv, _attn_bias_tables(rel_bias, dil), batch, seq, dil)
        outs.append(o)
        lses.append(l)
    y_attn = _attn_combine(outs, lses, beta_attn)
    y_mlstm = _mlstm(proj_m, gates, gate_bias, conv_w, conv_b, beta_mlstm, batch, seq)

    h1, h1_packed = _outproj_ln(y_attn, y_mlstm, h, w_out.astype(BF16), ln1_g, ln1_b)

    top_idx, top_gate, rank, counts = _router(h1, w_router, b_router)
    counts = counts[0, :N_EXPERTS]
    nb = (counts + FFN_ROWS - 1) // FFN_ROWS
    per = (counts + nb * FFN_SUB - 1) // jnp.maximum(nb * FFN_SUB, 1) * FFN_SUB
    blk_end = jnp.cumsum(nb)
    blk_start = blk_end - nb
    e_idx, r_idx = top_idx[:, :TOP_K], rank[:, :TOP_K]
    onehot = (e_idx.reshape(-1, 1) == jnp.arange(N_EXPERTS, dtype=jnp.int32)[None, :]).astype(F32)
    table = jnp.stack([jnp.maximum(per, 1), blk_start], axis=1).astype(F32)
    looked = jnp.dot(onehot, table, precision=lax.Precision.HIGHEST)
    per_a, start_a = looked[:, 0], looked[:, 1]
    r_f = r_idx.reshape(-1).astype(F32)
    blk_in_e = jnp.floor((r_f + 0.5) / per_a)
    dest = ((start_a + blk_in_e) * FFN_ROWS + (r_f - blk_in_e * per_a)).astype(jnp.int32)
    nblk = n * TOP_K // FFN_ROWS + N_EXPERTS
    blk_id = jnp.arange(nblk, dtype=jnp.int32)
    blk_exp = jnp.minimum(jnp.sum(blk_end[None, :] <= blk_id[:, None], axis=1), N_EXPERTS - 1).astype(jnp.int32)
    n_used = blk_end[-1:].astype(jnp.int32)
    blk_valid = jnp.clip(counts[blk_exp] - (blk_id - blk_start[blk_exp]) * per[blk_exp], 0,
                         per[blk_exp]).astype(jnp.int32)
    slot_assign = jnp.full((nblk * FFN_ROWS,), -1, jnp.int32).at[dest].set(
        jnp.arange(n * TOP_K, dtype=jnp.int32), unique_indices=True)

    ys = _expert_ffn(h1_packed, slot_assign, blk_exp, blk_valid, n_used, w_up, b_up, w_down, b_down)
    return _combine_ln(top_gate, h1, ys, ln2_g, ln2_b)


def kernel(x, w_in, b_igate, b_fgate, conv_w, conv_b, rel_bias, beta_attn, beta_mlstm, w_out, ln1_g, ln1_b,
           w_router, b_router, w_up, b_up, w_down, b_down, ln2_g, ln2_b):
    batch, seq, d = x.shape
    h = x.reshape(batch * seq, d)
    for l in range(DEPTH):
        h = _layer(h, w_in[l], b_igate[l], b_fgate[l], conv_w[l], conv_b[l], rel_bias, beta_attn[l], beta_mlstm[l],
                   w_out[l], ln1_g[l], ln1_b[l], w_router[l], b_router[l], w_up[l], b_up[l], w_down[l], b_down[l],
                   ln2_g[l], ln2_b[l], batch, seq)
    return h.reshape(batch, seq, d)
```

```python
import functools
import math

import numpy as np
import jax
import jax.numpy as jnp
from jax import lax
from jax.experimental import pallas as pl
from jax.experimental.pallas import tpu as pltpu

F32 = jnp.float32
BF16 = jnp.bfloat16

D_MODEL = 2048
D_ATTN = 1024
HEAD_DIM_A = 64
N_HEADS_A = 16
DILATED_CONFIGS = ((128, 1), (512, 4), (2048, 16))
ATTN_BLOCK = 128
NUM_BUCKETS = 32
MAX_DISTANCE = 2048
D_MLSTM = 1024
N_HEADS_M = 4
HEAD_DIM_M = 256
CONV_K = 4
MLSTM_CHUNK = 128
MAIN_COLS = 3 * D_ATTN + 4 * D_MLSTM
N_EXPERTS = 32
TOP_K = 4
D_FF = 2048
SWIGLU_LIMIT = 7.0
SWIGLU_ALPHA = 1.702
DEPTH = 1
DN_ALPHA = (2 * DEPTH) ** 0.25
LN_EPS = 1e-5
HEAD_NORM_EPS = 1e-6
NEG_INF = -1e30

LANES = 128
VMEM_LIMIT = 48 * 1024 * 1024

PROJ_TM = 1024
PROJ_TN = 1024
COMBINE_TM = 512
OUTPROJ_TM = 512
ROUTER_TM = 512
QKV_TM = 512
ATTN_GROUP = 2
FFN_ROWS = 1024
FFN_SUB = 256
FFN_TF = 512
FFN_TH = 512
FFN_VMEM_LIMIT = 60 * 1024 * 1024
ROW_DMA_PRIORITY = 1
COMBINE_LN_TM = 256


def _cparams(sem, vmem=VMEM_LIMIT):
    return pltpu.CompilerParams(dimension_semantics=sem, vmem_limit_bytes=vmem)


def _dot(a, b):
    return jnp.dot(a, b, preferred_element_type=F32)


def _dot_f32_rhs(a_bf16, b_f32):
    hi = b_f32.astype(BF16)
    lo = (b_f32 - hi.astype(F32)).astype(BF16)
    return _dot(a_bf16, hi) + _dot(a_bf16, lo)


def _dot_f32_lhs(a_f32, b_bf16):
    hi = a_f32.astype(BF16)
    lo = (a_f32 - hi.astype(F32)).astype(BF16)
    return _dot(hi, b_bf16) + _dot(lo, b_bf16)


def _sigmoid(x):
    return 1.0 / (1.0 + jnp.exp(-x))


def _log_sigmoid(x):
    return jnp.minimum(x, 0.0) - jnp.log(1.0 + jnp.exp(-jnp.abs(x)))


def _proj_kernel(x_ref, w_ref, wg_ref, o_ref, g_ref):
    x = x_ref[...].astype(BF16)
    o_ref[...] = _dot(x, w_ref[...]).astype(o_ref.dtype)

    @pl.when(pl.program_id(1) == 0)
    def _():
        g_ref[...] = _dot(x, wg_ref[...])


def _project(x, w, w_gate, tm, tn):
    m, k = x.shape
    n = w.shape[1]
    return pl.pallas_call(
        _proj_kernel,
        grid=(m // tm, n // tn),
        in_specs=[pl.BlockSpec((tm, k), lambda i, j: (i, 0)),
                  pl.BlockSpec((k, tn), lambda i, j: (0, j)),
                  pl.BlockSpec((k, LANES), lambda i, j: (0, 0))],
        out_specs=[pl.BlockSpec((tm, tn), lambda i, j: (i, j)),
                   pl.BlockSpec((tm, LANES), lambda i, j: (i, 0))],
        out_shape=[jax.ShapeDtypeStruct((m, n), BF16), jax.ShapeDtypeStruct((m, LANES), F32)],
        compiler_params=_cparams(("parallel", "arbitrary")),
        name="in_proj",
    )(x, w, w_gate)


def _qkv_proj_kernel(x_ref, w_ref, *refs):
    o_refs, r_ref = refs[:-1], refs[-1]
    res = _dot(x_ref[...].astype(BF16), w_ref[...])
    ntile, tm, _ = r_ref.shape
    wid = ntile * LANES
    for c in range(ntile):
        r_ref[c] = res[:, c * LANES:(c + 1) * LANES]
    for (_, dil), o_ref in zip(DILATED_CONFIGS, o_refs):
        if dil == 1:
            o_ref[...] = res.astype(o_ref.dtype)
        else:
            for r in range(dil):
                for c in range(ntile):
                    col = r * wid + c * LANES
                    o_ref[:, col:col + LANES] = r_ref[c, pl.ds(r, tm // dil, stride=dil), :].astype(o_ref.dtype)


def _qkv_project(x, w, tm):
    m, k = x.shape
    wid = w.shape[1]
    dils = [d for _, d in DILATED_CONFIGS]
    return pl.pallas_call(
        _qkv_proj_kernel,
        grid=(m // tm,),
        in_specs=[pl.BlockSpec((tm, k), lambda i: (i, 0)),
                  pl.BlockSpec((k, wid), lambda i: (0, 0), pipeline_mode=pl.Buffered(1))],
        out_specs=[pl.BlockSpec((tm // d, d * wid), lambda i: (i, 0)) for d in dils],
        out_shape=[jax.ShapeDtypeStruct((m // d, d * wid), BF16) for d in dils],
        scratch_shapes=[pltpu.VMEM((wid // LANES, tm, LANES), F32)],
        compiler_params=_cparams(("parallel",), 56 * 1024 * 1024),
        name="qkv_proj",
    )(x, w)


def _attn_kernel(q_ref, kp_ref, kc_ref, vp_ref, vc_ref, bias_ref, o_ref, lse_ref):
    n = pl.program_id(2)
    tab = jnp.minimum(n, 1)
    lse_ref[...] = jnp.zeros(lse_ref.shape, F32)
    grp, dh, nk = ATTN_GROUP, HEAD_DIM_A, 2 * ATTN_BLOCK
    wid = grp * dh
    lane_head = lax.broadcasted_iota(jnp.int32, (nk, wid), 1) // dh
    zero = jnp.zeros((nk, wid), BF16)
    ones_bd = jnp.concatenate([jnp.where(lane_head == j, 1.0, 0.0).astype(BF16) for j in range(grp)], axis=0)
    for g in range(N_HEADS_A // grp):
        cols = slice(g * wid, (g + 1) * wid)
        q = q_ref[0, :, cols] * (dh ** -0.5)
        kslab = jnp.concatenate([kp_ref[0, :, cols], kc_ref[0, :, cols]], axis=0)
        vslab = jnp.concatenate([vp_ref[0, :, cols], vc_ref[0, :, cols]], axis=0)
        k_bd = jnp.concatenate([jnp.where(lane_head == j, kslab, zero) for j in range(grp)], axis=0)
        v_bd = jnp.concatenate([jnp.where(lane_head == j, vslab, zero) for j in range(grp)], axis=0)
        s_all = lax.dot_general(q, k_bd, (((1,), (1,)), ((), ())), preferred_element_type=F32)
        ps, ms = [], []
        for j in range(grp):
            s = s_all[:, j * nk:(j + 1) * nk] + bias_ref[tab, g * grp + j]
            m = jnp.max(s, axis=-1, keepdims=True)
            ps.append(jnp.exp(s - m).astype(BF16))
            ms.append(m)
        p_all = jnp.concatenate(ps, axis=-1)
        res = _dot(p_all, jnp.concatenate([v_bd, ones_bd], axis=-1))
        den = res[:, wid:]
        o_ref[0, :, cols] = (res[:, :wid] / den).astype(o_ref.dtype)
        for j in range(grp):
            h = g * grp + j
            lse_ref[0, :, h:h + 1] = ms[j] + jnp.log(den[:, j * dh:j * dh + 1])


def _attn_bias_tables(rel_bias, dil):
    blk = ATTN_BLOCK
    period = 3 * blk
    k = np.arange(period)
    valid = k <= blk
    dist = np.where(valid, blk - k, 0) * dil
    max_exact = NUM_BUCKETS // 2
    d_f = np.maximum(dist, 1).astype(np.float32)
    large = max_exact + (np.log(d_f / np.float32(max_exact)) / np.float32(math.log(MAX_DISTANCE / max_exact))
                         * np.float32(NUM_BUCKETS - max_exact)).astype(np.int32)
    large = np.minimum(large, NUM_BUCKETS - 1)
    bucket = np.where(dist < max_exact, dist, large).astype(np.int32)
    w = jnp.where(jnp.asarray(valid)[None, :], rel_bias[jnp.asarray(bucket)].T.astype(F32), NEG_INF)
    t1 = jnp.tile(w, (1, blk))[:, :blk * (period - 1)].reshape(N_HEADS_A, blk, period - 1)[:, :, :2 * blk]
    has_prev = np.arange(2 * blk)[None, None, :] >= blk
    t0 = jnp.where(jnp.asarray(has_prev), t1, NEG_INF)
    return jnp.stack([t0, t1])


def _dilated_attention(qkv, bias_tab, batch, seq, dil):
    blk = ATTN_BLOCK
    l = seq // dil
    nb = l // blk
    ncb = 3
    pv = qkv.reshape(batch, l, dil * ncb * D_ATTN)

    def cur(c):
        return pl.BlockSpec((1, blk, D_ATTN), lambda b, r, n: (b, n, r * ncb + c))

    def prev(c):
        return pl.BlockSpec((1, blk, D_ATTN), lambda b, r, n: (b, jnp.maximum(n - 1, 0), r * ncb + c))

    o, lse = pl.pallas_call(
        _attn_kernel,
        grid=(batch, dil, nb),
        in_specs=[cur(0), prev(1), cur(1), prev(2), cur(2),
                  pl.BlockSpec((2, N_HEADS_A, blk, 2 * blk), lambda b, r, n: (0, 0, 0, 0))],
        out_specs=[pl.BlockSpec((1, blk, D_ATTN), lambda b, r, n: (b, n, r)),
                   pl.BlockSpec((1, blk, LANES), lambda b, r, n: (b, n, r))],
        out_shape=[jax.ShapeDtypeStruct((batch, l, dil * D_ATTN), BF16),
                   jax.ShapeDtypeStruct((batch, l, dil * LANES), F32)],
        compiler_params=_cparams(("parallel", "parallel", "arbitrary")),
        name=f"dilated_attn_d{dil}",
    )(pv, pv, pv, pv, pv, bias_tab)
    return o.reshape(batch * l, dil * D_ATTN), lse.reshape(batch * l, dil * LANES)


def _attn_combine_kernel(o1_ref, o2_ref, o3_ref, l1_ref, l2_ref, l3_ref, e_ref, et_ref, beta_ref, y_ref,
                         osc_ref, lsc_ref):
    tm = y_ref.shape[0]
    ntile = D_ATTN // LANES
    lses, outs = [], []
    for slot, ((_, dil), o_ref, l_ref) in enumerate(zip(DILATED_CONFIGS, (o1_ref, o2_ref, o3_ref),
                                                        (l1_ref, l2_ref, l3_ref))):
        if dil == 1:
            lses.append(l_ref[...])
            outs.append(lambda o_ref=o_ref: o_ref[...].astype(F32))
            continue
        rows = tm // dil
        for r in range(dil):
            lsc_ref[slot, pl.ds(r, rows, stride=dil), :] = l_ref[:, r * LANES:(r + 1) * LANES]
            for c in range(ntile):
                col = r * D_ATTN + c * LANES
                osc_ref[slot, c, pl.ds(r, rows, stride=dil), :] = o_ref[:, col:col + LANES].astype(F32)
        lses.append(lsc_ref[slot])
        outs.append(lambda slot=slot: jnp.concatenate([osc_ref[slot, c] for c in range(ntile)], axis=-1))
    mx = jnp.maximum(jnp.maximum(lses[0], lses[1]), lses[2])
    ws = [jnp.exp(l - mx) for l in lses]
    tot = ws[0] + ws[1] + ws[2]
    e = e_ref[...]
    acc = None
    for w, load_o in zip(ws, outs):
        term = _dot_f32_lhs(w / tot, e) * load_o()
        acc = term if acc is None else acc + term
    ss = _dot_f32_lhs(acc * acc, et_ref[...])
    inv = lax.rsqrt(ss * (1.0 / HEAD_DIM_A) + HEAD_NORM_EPS)
    y_ref[...] = (acc * _dot_f32_lhs(inv, e) * beta_ref[...]).astype(y_ref.dtype)


def _attn_combine(os_, lses, beta_attn):
    n = os_[0].shape[0]
    tm = min(COMBINE_TM, n)
    head_of_lane = np.arange(D_ATTN) // HEAD_DIM_A
    e = (np.arange(LANES)[:, None] == head_of_lane[None, :]).astype(np.float32)
    e_j = jnp.asarray(e, BF16)
    et_j = jnp.asarray(e.T, BF16)
    dils = [d for _, d in DILATED_CONFIGS]
    dilated = lambda w: [pl.BlockSpec((tm // d, d * w), lambda i: (i, 0)) for d in dils]
    full = lambda a, b: pl.BlockSpec((a, b), lambda i: (0, 0))
    return pl.pallas_call(
        _attn_combine_kernel,
        grid=(n // tm,),
        in_specs=dilated(D_ATTN) + dilated(LANES) + [full(LANES, D_ATTN), full(D_ATTN, LANES), full(1, D_ATTN)],
        out_specs=pl.BlockSpec((tm, D_ATTN), lambda i: (i, 0)),
        out_shape=jax.ShapeDtypeStruct((n, D_ATTN), BF16),
        scratch_shapes=[pltpu.VMEM((len(dils), D_ATTN // LANES, tm, LANES), F32),
                        pltpu.VMEM((len(dils), tm, LANES), F32)],
        compiler_params=_cparams(("parallel",)),
        name="attn_combine",
    )(*os_, *lses, e_j, et_j, beta_attn.reshape(1, D_ATTN).astype(F32))


def _mlstm_kernel(qp_ref, kp_ref, qprev_ref, kprev_ref, v_ref, og_ref, g_ref, gb_ref, cw_ref, cb_ref,
                  beta_ref, y_ref, c_ref, n_ref, m_ref):
    step = pl.program_id(1)
    ch = MLSTM_CHUNK
    dh = HEAD_DIM_M

    @pl.when(step == 0)
    def _():
        c_ref[...] = jnp.zeros(c_ref.shape, F32)
        n_ref[...] = jnp.zeros(n_ref.shape, F32)
        m_ref[...] = jnp.zeros(m_ref.shape, F32)

    def conv_silu(x_ref, prev_ref, coff, b):
        x = x_ref[b].astype(F32)
        p = jnp.where(step > 0, prev_ref[b].astype(F32), 0.0)
        xe = jnp.concatenate([p, x], axis=0)
        npad = p.shape[0]
        cols = slice(coff, coff + D_MLSTM)
        acc = cb_ref[:, cols] + cw_ref[CONV_K - 1:CONV_K, cols] * x
        for s in range(1, CONV_K):
            shifted = pltpu.roll(xe, s, 0)[npad:]
            acc = acc + cw_ref[CONV_K - 1 - s:CONV_K - s, cols] * shifted
        return acc * _sigmoid(acc)

    row_i = lax.broadcasted_iota(jnp.int32, (ch, ch), 0)
    col_i = lax.broadcasted_iota(jnp.int32, (ch, ch), 1)
    causal = row_i >= col_i
    tri = jnp.where(causal, 1.0, 0.0).astype(BF16)
    upp = jnp.where(row_i <= col_i, 1.0, 0.0).astype(BF16)
    b = 0
    qf = conv_silu(qp_ref, qprev_ref, 0, b)
    kf = conv_silu(kp_ref, kprev_ref, D_MLSTM, b) * (dh ** -0.5)
    qb = qf.astype(BF16)
    kb = kf.astype(BF16)
    g = g_ref[b] + gb_ref[...]
    gt = g.T
    b_cols = _dot_f32_rhs(tri, _log_sigmoid(g))
    b_rows = _dot_f32_lhs(_log_sigmoid(gt), upp)

    for h in range(N_HEADS_M):
        st = h
        hs = slice(h * dh, (h + 1) * dh)
        fi = N_HEADS_M + h
        i_row = gt[h:h + 1, :]
        i_col = g[:, h:h + 1]
        b_row = b_rows[fi:fi + 1, :]
        b_col = b_cols[:, fi:fi + 1]
        m_prev = m_ref[st][:, 0:1]
        q_h, k_h = qb[:, hs], kb[:, hs]
        v_h = v_ref[b, :, hs]

        dmat = jnp.where(causal, b_col - b_row + i_row, NEG_INF)
        m_inter = b_col + m_prev
        m_t = jnp.maximum(m_inter, jnp.max(dmat, axis=-1, keepdims=True))
        w = jnp.exp(dmat - m_t) * lax.dot_general(q_h, k_h, (((1,), (1,)), ((), ())),
                                                  preferred_element_type=F32)
        decay = jnp.exp(m_inter - m_t)
        c_old = c_ref[st]
        inter = lax.dot_general(q_h, c_old.astype(BF16), (((1,), (1,)), ((), ())), preferred_element_type=F32)
        num = _dot(w.astype(BF16), v_h) + decay * inter
        n_old = n_ref[st]
        den = jnp.sum(w, axis=-1, keepdims=True) + decay * jnp.sum(qf[:, hs] * n_old, axis=-1, keepdims=True)
        hh = num / jnp.maximum(jnp.abs(den), jnp.exp(-m_t))

        g_last = b_col[ch - 1:ch, :]
        a_row = g_last - b_row + i_row
        a_col = g_last - b_col + i_col
        m_new = jnp.maximum(g_last + m_prev, jnp.max(a_row, axis=-1, keepdims=True))
        carry = jnp.exp(g_last + m_prev - m_new)
        wa_col = jnp.exp(a_col - m_new)
        wv = (wa_col * v_h.astype(F32)).astype(BF16)
        c_ref[st] = carry * c_old + lax.dot_general(wv, k_h, (((0,), (0,)), ((), ())), preferred_element_type=F32)
        n_ref[st] = carry * n_old + jnp.sum(wa_col * kf[:, hs], axis=0, keepdims=True)
        m_ref[st] = jnp.broadcast_to(m_new, (1, LANES))

        gated = _sigmoid(og_ref[b, :, hs].astype(F32)) * hh
        ms = jnp.sum(gated * gated, axis=-1, keepdims=True) * (1.0 / dh)
        y_ref[b, :, hs] = (gated * lax.rsqrt(ms + HEAD_NORM_EPS) * beta_ref[:, hs]).astype(y_ref.dtype)


def _mlstm(proj, gates, gate_bias, conv_w, conv_b, beta_mlstm, batch, seq):
    ch = MLSTM_CHUNK
    nchunk = seq // ch
    pv = proj.reshape(batch, seq, 4 * D_MLSTM)
    gv = gates.reshape(batch, seq, LANES)
    prev_rows = 16
    cb0 = 0

    def cur(c):
        return pl.BlockSpec((1, ch, D_MLSTM), lambda b, n: (b, n, c))

    def prev(c):
        per = ch // prev_rows
        return pl.BlockSpec((1, prev_rows, D_MLSTM), lambda b, n: (b, jnp.maximum(n * per - 1, 0), c))

    const = lambda a, b_: pl.BlockSpec((a, b_), lambda b, n: (0, 0))
    y = pl.pallas_call(
        _mlstm_kernel,
        grid=(batch, nchunk),
        in_specs=[cur(cb0), cur(cb0 + 1), prev(cb0), prev(cb0 + 1), cur(cb0 + 2), cur(cb0 + 3),
                  pl.BlockSpec((1, ch, LANES), lambda b, n: (b, n, 0)),
                  const(1, LANES), const(CONV_K, 2 * D_MLSTM), const(1, 2 * D_MLSTM), const(1, D_MLSTM)],
        out_specs=pl.BlockSpec((1, ch, D_MLSTM), lambda b, n: (b, n, 0)),
        out_shape=jax.ShapeDtypeStruct((batch, seq, D_MLSTM), BF16),
        scratch_shapes=[pltpu.VMEM((N_HEADS_M, HEAD_DIM_M, HEAD_DIM_M), F32),
                        pltpu.VMEM((N_HEADS_M, 1, HEAD_DIM_M), F32),
                        pltpu.VMEM((N_HEADS_M, 1, LANES), F32)],
        compiler_params=_cparams(("parallel", "arbitrary")),
        name="mlstm",
    )(pv, pv, pv, pv, pv, pv, gv, gate_bias, conv_w.astype(F32), conv_b.reshape(1, -1).astype(F32),
      beta_mlstm.reshape(1, D_MLSTM).astype(F32))
    return y.reshape(batch * seq, D_MLSTM)


def _layer_norm(z, g, b):
    mu = jnp.mean(z, axis=-1, keepdims=True)
    zc = z - mu
    var = jnp.mean(zc * zc, axis=-1, keepdims=True)
    return zc * lax.rsqrt(var + LN_EPS) * g + b


PACK_ROWS = D_MODEL // (2 * LANES)


def _store_packed_rows(dst_ref, x, first=0):
    rows = x.shape[0]
    half = D_MODEL // 2
    for s in range(PACK_ROWS):
        lo = x[:, s * LANES:(s + 1) * LANES].astype(BF16).astype(F32)
        hi = x[:, half + s * LANES:half + (s + 1) * LANES].astype(BF16).astype(F32)
        word = pltpu.bitcast(hi, jnp.int32) | lax.shift_right_logical(pltpu.bitcast(lo, jnp.int32), 16)
        dst_ref[pl.ds(first * PACK_ROWS + s, rows, stride=PACK_ROWS), :] = word


def _load_packed_rows(src_ref, first, rows):
    los, his = [], []
    for s in range(PACK_ROWS):
        word = src_ref[pl.ds(first * PACK_ROWS + s, rows, stride=PACK_ROWS), :]
        los.append(pltpu.bitcast(lax.shift_left(word, 16), F32))
        his.append(pltpu.bitcast(word & jnp.int32(-65536), F32))
    return los, his


def _outproj_kernel(ya_ref, ym_ref, x_ref, w_ref, g_ref, b_ref, h_ref, hp_ref):
    y = _dot(ya_ref[...], w_ref[0:D_ATTN, :]) + _dot(ym_ref[...], w_ref[D_ATTN:D_MODEL, :])
    h = _layer_norm(DN_ALPHA * x_ref[...] + y, g_ref[...], b_ref[...])
    h_ref[...] = h
    _store_packed_rows(hp_ref, h)


def _outproj_ln(ya, ym, x, w_out, g, b):
    n = x.shape[0]
    tm = min(OUTPROJ_TM, n)
    row = lambda w: pl.BlockSpec((tm, w), lambda i: (i, 0))
    full = lambda a, b_: pl.BlockSpec((a, b_), lambda i: (0, 0))
    return pl.pallas_call(
        _outproj_kernel,
        grid=(n // tm,),
        in_specs=[row(D_ATTN), row(D_MLSTM), row(D_MODEL), full(D_MODEL, D_MODEL), full(1, D_MODEL), full(1, D_MODEL)],
        out_specs=[row(D_MODEL), pl.BlockSpec((tm * PACK_ROWS, LANES), lambda i: (i, 0))],
        out_shape=[jax.ShapeDtypeStruct((n, D_MODEL), F32),
                   jax.ShapeDtypeStruct((n * PACK_ROWS, LANES), jnp.int32)],
        compiler_params=_cparams(("parallel",)),
        name="out_proj_ln",
    )(ya, ym, x, w_out, g.reshape(1, -1), b.reshape(1, -1))


def _router_kernel(h_ref, whi_ref, wlo_ref, b_ref, tri_ref, idx_ref, gate_ref, rank_ref, cnt_ref, carry_ref):
    i = pl.program_id(0)

    @pl.when(i == 0)
    def _():
        carry_ref[...] = jnp.zeros(carry_ref.shape, F32)

    x = h_ref[...]
    xhi = x.astype(BF16)
    xlo = (x - xhi.astype(F32)).astype(BF16)
    logits = _dot(xhi, whi_ref[...]) + _dot(xhi, wlo_ref[...]) + _dot(xlo, whi_ref[...]) + b_ref[...]
    tm = logits.shape[0]
    lane = lax.broadcasted_iota(jnp.int32, (tm, LANES), 1)
    lane_f = lane.astype(F32)
    vals = jnp.where(lane < N_EXPERTS, logits, NEG_INF)

    sels, tops = [], []
    for _ in range(TOP_K):
        mx = jnp.max(vals, axis=-1, keepdims=True)
        first = jnp.min(jnp.where(vals == mx, lane_f, float(LANES)), axis=-1, keepdims=True)
        sel = lane_f == first
        sels.append(sel)
        tops.append((mx, first))
        vals = jnp.where(sel, 2.0 * NEG_INF, vals)

    exps = [jnp.exp(mx - tops[0][0]) for mx, _ in tops]
    tot = exps[0] + exps[1] + exps[2] + exps[3]

    onehot = jnp.zeros((tm, LANES), F32)
    for sel in sels:
        onehot = jnp.where(sel, 1.0, onehot)
    before = _dot(tri_ref[...], onehot.astype(BF16)) + carry_ref[...]

    idx_out = jnp.zeros((tm, LANES), F32)
    gate_out = jnp.zeros((tm, LANES), F32)
    rank_out = jnp.zeros((tm, LANES), F32)
    for k in range(TOP_K):
        rank_k = jnp.sum(jnp.where(sels[k], before, 0.0), axis=-1, keepdims=True)
        idx_out = jnp.where(lane == k, tops[k][1], idx_out)
        gate_out = jnp.where(lane == k, exps[k] / tot, gate_out)
        rank_out = jnp.where(lane == k, rank_k, rank_out)
    idx_ref[...] = idx_out.astype(jnp.int32)
    gate_ref[...] = gate_out
    rank_ref[...] = rank_out.astype(jnp.int32)

    carry = carry_ref[...] + jnp.sum(onehot, axis=0, keepdims=True)
    carry_ref[...] = carry
    cnt_ref[...] = carry.astype(jnp.int32)


def _router(h1, w_router, b_router):
    n = h1.shape[0]
    tm = min(ROUTER_TM, n)
    wpad = jnp.zeros((D_MODEL, LANES), F32).at[:, :N_EXPERTS].set(w_router)
    whi = wpad.astype(BF16)
    wlo = (wpad - whi.astype(F32)).astype(BF16)
    bpad = jnp.zeros((1, LANES), F32).at[0, :N_EXPERTS].set(b_router)
    tri = jnp.asarray(np.tril(np.ones((tm, tm), np.float32), -1), BF16)
    row = lambda w: pl.BlockSpec((tm, w), lambda i: (i, 0))
    full = lambda a, b_: pl.BlockSpec((a, b_), lambda i: (0, 0))
    return pl.pallas_call(
        _router_kernel,
        grid=(n // tm,),
        in_specs=[row(D_MODEL), full(D_MODEL, LANES), full(D_MODEL, LANES), full(1, LANES), full(tm, tm)],
        out_specs=[row(LANES), row(LANES), row(LANES), full(1, LANES)],
        out_shape=[jax.ShapeDtypeStruct((n, LANES), jnp.int32), jax.ShapeDtypeStruct((n, LANES), F32),
                   jax.ShapeDtypeStruct((n, LANES), jnp.int32), jax.ShapeDtypeStruct((1, LANES), jnp.int32)],
        scratch_shapes=[pltpu.VMEM((1, LANES), F32)],
        compiler_params=_cparams(("arbitrary",)),
        name="router",
    )(h1, whi, wlo, bpad, tri)


def _ffn_kernel(bexp_ref, nused_ref, nvalid_ref, slot_ref, slot_prv_ref, src_cur_ref, src_nxt_ref, wg_ref, wu_ref,
                wdl_ref, wdh_ref,
                bg_ref, bu_ref, bdl_ref, bdh_ref, xp_ref, ys_ref, xbuf_ref, x_ref, act_ref, obuf_ref, wgb_ref, wub_ref,
                wdlb_ref, wdhb_ref, in_sem, out_sem):
    del bexp_ref
    i = pl.program_id(0)
    g = pl.program_id(1)
    n_up = act_ref.shape[0]
    last_g = pl.num_programs(1) - 1
    last_i = pl.num_programs(0) - 1
    n_used = nused_ref[0]
    used = i < n_used
    rb = x_ref.shape[0]
    sub = FFN_SUB
    th = wdlb_ref.shape[1]
    unroll = 8

    def sub_blocks(blk):
        return (nvalid_ref[blk] + (sub - 1)) // sub

    nsub = sub_blocks(i)

    def start_gather(src_ref, lo, hi):
        def body(j, carry):
            for u in range(unroll):
                jj = j * unroll + u
                src = pl.multiple_of(src_ref[jj], PACK_ROWS)
                dst = pl.multiple_of(jj * PACK_ROWS, PACK_ROWS)
                pltpu.make_async_copy(xp_ref.at[pl.ds(src, PACK_ROWS)], xbuf_ref.at[pl.ds(dst, PACK_ROWS)],
                                      in_sem).start(priority=ROW_DMA_PRIORITY)
            return carry

        lax.fori_loop(lo // unroll, hi // unroll, body, 0)

    def wait_gather(nrows):
        npk = pl.multiple_of(nrows * PACK_ROWS, PACK_ROWS)
        pltpu.make_async_copy(xp_ref.at[pl.ds(0, npk)], xbuf_ref.at[pl.ds(0, npk)], in_sem).wait()

    def start_scatter(slots_ref, lo, hi):
        def one(jj):
            src = pl.multiple_of(jj * PACK_ROWS, PACK_ROWS)
            dst = pl.multiple_of(slots_ref[jj] * PACK_ROWS, PACK_ROWS)
            pltpu.make_async_copy(obuf_ref.at[pl.ds(src, PACK_ROWS)], ys_ref.at[pl.ds(dst, PACK_ROWS)],
                                  out_sem).start(priority=ROW_DMA_PRIORITY)

        def body(j, carry):
            for u in range(unroll):
                one(j * unroll + u)
            return carry

        nfull = jnp.maximum(hi - lo, 0) // unroll
        lax.fori_loop(lo // unroll, lo // unroll + nfull, body, 0)
        for u in range(unroll - 1):
            @pl.when(lo + nfull * unroll + u < hi)
            def _():
                one(lo + nfull * unroll + u)

    def wait_scatter(blk):
        npk = pl.multiple_of(nvalid_ref[blk] * PACK_ROWS, PACK_ROWS)
        pltpu.make_async_copy(obuf_ref.at[pl.ds(0, npk)], ys_ref.at[pl.ds(0, npk)], out_sem).wait()

    @pl.when(jnp.logical_and(used, g == 0))
    def _():
        @pl.when(i == 0)
        def _():
            start_gather(src_cur_ref, 0, nsub * sub)

        wait_gather(nsub * sub)
        for j in range(rb // sub):
            @pl.when(j < nsub)
            def _():
                los, his = _load_packed_rows(xbuf_ref, j * sub, sub)
                rows = slice(j * sub, (j + 1) * sub)
                for s in range(PACK_ROWS):
                    x_ref[rows, s * LANES:(s + 1) * LANES] = los[s].astype(BF16)
                    x_ref[rows, D_MODEL // 2 + s * LANES:D_MODEL // 2 + (s + 1) * LANES] = his[s].astype(BF16)

    @pl.when(jnp.logical_and(jnp.logical_and(g >= 1, g - 1 < sub_blocks(jnp.minimum(i + 1, last_i))), i + 1 < n_used))
    def _():
        start_gather(src_nxt_ref, (g - 1) * sub, g * sub)

    @pl.when(jnp.logical_and(jnp.logical_and(used, i > 0), g + 1 < rb // sub))
    def _():
        start_scatter(slot_prv_ref, (g + 1) * sub, jnp.minimum((g + 2) * sub, nvalid_ref[i - 1]))

    @pl.when(jnp.logical_and(used, g < n_up))
    def _():
        wgb_ref[...] = wg_ref[0].astype(BF16)
        wub_ref[...] = wu_ref[0].astype(BF16)

        def sub_body(j, carry):
            rows = pl.ds(pl.multiple_of(j * sub, sub), sub)
            x = x_ref[rows, :]
            hg = _dot(x, wgb_ref[...]) + bg_ref[0]
            hu = _dot(x, wub_ref[...]) + bu_ref[0]
            gate = jnp.minimum(hg, SWIGLU_LIMIT)
            up = jnp.clip(hu, -SWIGLU_LIMIT, SWIGLU_LIMIT)
            act_ref[g, rows, :] = ((up + 1.0) * (gate * _sigmoid(SWIGLU_ALPHA * gate))).astype(BF16)
            return carry

        lax.fori_loop(0, nsub, sub_body, 0)

    @pl.when(jnp.logical_and(used, g >= n_up))
    def _():
        @pl.when(jnp.logical_and(g == n_up, i > 0))
        def _():
            wait_scatter(i - 1)

        wdlb_ref[...] = wdl_ref[0].astype(BF16)
        wdhb_ref[...] = wdh_ref[0].astype(BF16)
        tile0 = (g - n_up) * (th // LANES)

        def sub_body(j, carry):
            row0 = pl.multiple_of(j * sub, sub)
            a = jnp.concatenate([act_ref[c, pl.ds(row0, sub), :] for c in range(n_up)], axis=-1)
            ylo = _dot(a, wdlb_ref[...]) + bdl_ref[0]
            yhi = _dot(a, wdhb_ref[...]) + bdh_ref[0]
            for s in range(th // LANES):
                lo = ylo[:, s * LANES:(s + 1) * LANES].astype(BF16).astype(F32)
                hi = yhi[:, s * LANES:(s + 1) * LANES].astype(BF16).astype(F32)
                word = pltpu.bitcast(hi, jnp.int32) | lax.shift_right_logical(pltpu.bitcast(lo, jnp.int32), 16)
                obuf_ref[pl.ds(row0 * PACK_ROWS + tile0 + s, sub, stride=PACK_ROWS), :] = word
            return carry

        lax.fori_loop(0, nsub, sub_body, 0)

        @pl.when(g == last_g)
        def _():
            nv = nvalid_ref[i]
            start_scatter(slot_ref, 0, jnp.where(i + 1 < n_used, jnp.minimum(sub, nv), nv))

    @pl.when(jnp.logical_and(i == last_i, g == last_g))
    def _():
        wait_scatter(jnp.minimum(i, n_used - 1))


def _expert_ffn(h_packed, slot_assign, blk_exp, blk_valid, n_used, w_up, b_up, w_down, b_down):
    rb, tf, th = FFN_ROWS, FFN_TF, FFN_TH
    nblk = slot_assign.shape[0] // rb
    n_up = D_FF // tf
    n_dn = D_MODEL // 2 // th
    n_assign = h_packed.shape[0] // PACK_ROWS * TOP_K
    slot_src = jnp.maximum(slot_assign, 0) // TOP_K * PACK_ROWS

    def clamp(i, nu):
        return jnp.minimum(i, nu[0] - 1)

    def up_sel(i, g, nu):
        return jnp.where(i < nu[0], jnp.minimum(g, n_up - 1), n_up - 1)

    def dn_sel(i, g, nu):
        return jnp.where(i < nu[0], jnp.clip(g - n_up, 0, n_dn - 1), n_dn - 1)

    def dn_exp(i, g, be, nu):
        early = jnp.logical_and(g < n_up - 1, i < nu[0])
        return be[jnp.where(early, jnp.maximum(clamp(i, nu) - 1, 0), clamp(i, nu))]

    def dn_col(i, g, nu):
        early = jnp.logical_and(g < n_up - 1, i < nu[0])
        return jnp.where(early, n_dn - 1, dn_sel(i, g, nu))

    grid_spec = pltpu.PrefetchScalarGridSpec(
        num_scalar_prefetch=3,
        grid=(nblk, n_up + n_dn),
        in_specs=[
            pl.BlockSpec((rb,), lambda i, g, be, nu, *_: (clamp(i, nu),), memory_space=pltpu.SMEM),
            pl.BlockSpec((rb,), lambda i, g, be, nu, *_: (jnp.maximum(clamp(i, nu) - 1, 0),), memory_space=pltpu.SMEM),
            pl.BlockSpec((rb,), lambda i, g, be, nu, *_: (clamp(i, nu),), memory_space=pltpu.SMEM),
            pl.BlockSpec((rb,), lambda i, g, be, nu, *_: (clamp(i + 1, nu),), memory_space=pltpu.SMEM),
            pl.BlockSpec((1, D_MODEL, tf), lambda i, g, be, nu, *_: (be[clamp(i, nu)], 0, up_sel(i, g, nu))),
            pl.BlockSpec((1, D_MODEL, tf), lambda i, g, be, nu, *_: (be[clamp(i, nu)], 0, n_up + up_sel(i, g, nu))),
            pl.BlockSpec((1, D_FF, th), lambda i, g, be, nu, *_: (dn_exp(i, g, be, nu), 0, dn_col(i, g, nu))),
            pl.BlockSpec((1, D_FF, th), lambda i, g, be, nu, *_: (dn_exp(i, g, be, nu), 0, n_dn + dn_col(i, g, nu))),
            pl.BlockSpec((1, 1, tf), lambda i, g, be, nu, *_: (be[clamp(i, nu)], 0, up_sel(i, g, nu))),
            pl.BlockSpec((1, 1, tf), lambda i, g, be, nu, *_: (be[clamp(i, nu)], 0, n_up + up_sel(i, g, nu))),
            pl.BlockSpec((1, 1, th), lambda i, g, be, nu, *_: (be[clamp(i, nu)], 0, dn_sel(i, g, nu))),
            pl.BlockSpec((1, 1, th), lambda i, g, be, nu, *_: (be[clamp(i, nu)], 0, n_dn + dn_sel(i, g, nu))),
            pl.BlockSpec(memory_space=pl.ANY),
        ],
        out_specs=pl.BlockSpec(memory_space=pl.ANY),
        scratch_shapes=[pltpu.VMEM((rb * PACK_ROWS, LANES), jnp.int32),
                        pltpu.VMEM((rb, D_MODEL), BF16),
                        pltpu.VMEM((n_up, rb, tf), BF16),
                        pltpu.VMEM((rb * PACK_ROWS, LANES), jnp.int32),
                        pltpu.VMEM((D_MODEL, tf), BF16), pltpu.VMEM((D_MODEL, tf), BF16),
                        pltpu.VMEM((D_FF, th), BF16), pltpu.VMEM((D_FF, th), BF16),
                        pltpu.SemaphoreType.DMA, pltpu.SemaphoreType.DMA],
    )
    assert n_up >= rb // FFN_SUB - 1 and n_up + n_dn > rb // FFN_SUB, "row DMA bursts are spread over the grid steps"
    return pl.pallas_call(
        _ffn_kernel,
        grid_spec=grid_spec,
        out_shape=jax.ShapeDtypeStruct((n_assign * PACK_ROWS, LANES), jnp.int32),
        compiler_params=pltpu.CompilerParams(dimension_semantics=("arbitrary", "arbitrary"),
                                             vmem_limit_bytes=FFN_VMEM_LIMIT, disable_bounds_checks=True),
        name="expert_ffn",
    )(blk_exp, n_used, blk_valid, slot_assign, slot_assign, slot_src, slot_src, w_up, w_up, w_down, w_down,
      b_up.reshape(N_EXPERTS, 1, 2 * D_FF), b_up.reshape(N_EXPERTS, 1, 2 * D_FF),
      b_down.reshape(N_EXPERTS, 1, D_MODEL), b_down.reshape(N_EXPERTS, 1, D_MODEL), h_packed)


def _combine_kernel(gate_ref, h_ref, g_ref, b_ref, ys_ref, o_ref):
    tm = h_ref.shape[0]
    gates = gate_ref[...]
    cols = [None] * (2 * PACK_ROWS)
    for k in range(TOP_K):
        gk = gates[:, k:k + 1]
        for s in range(PACK_ROWS):
            word = ys_ref[pl.ds(k * PACK_ROWS + s, tm, stride=TOP_K * PACK_ROWS), :]
            lo = pltpu.bitcast(lax.shift_left(word, 16), F32)
            hi = pltpu.bitcast(word & jnp.int32(-65536), F32)
            for c, blk in ((s, lo), (PACK_ROWS + s, hi)):
                cols[c] = gk * blk if cols[c] is None else cols[c] + gk * blk
    z = DN_ALPHA * h_ref[...] + jnp.concatenate(cols, axis=-1)
    o_ref[...] = _layer_norm(z, g_ref[...], b_ref[...])


def _combine_ln(gates, h1, ys, g, b):
    n = h1.shape[0]
    tm = min(COMBINE_LN_TM, n)
    row = lambda w: pl.BlockSpec((tm, w), lambda i: (i, 0))
    full = lambda a, b_: pl.BlockSpec((a, b_), lambda i: (0, 0))
    return pl.pallas_call(
        _combine_kernel,
        grid=(n // tm,),
        in_specs=[row(LANES), row(D_MODEL), full(1, D_MODEL), full(1, D_MODEL),
                  pl.BlockSpec((tm * TOP_K * PACK_ROWS, LANES), lambda i: (i, 0))],
        out_specs=row(D_MODEL),
        out_shape=jax.ShapeDtypeStruct((n, D_MODEL), F32),
        compiler_params=_cparams(("parallel",)),
        name="combine_ln",
    )(gates, h1, g.reshape(1, -1), b.reshape(1, -1), ys)


def _layer(h, w_in, b_igate, b_fgate, conv_w, conv_b, rel_bias, beta_attn, beta_mlstm, w_out,
           ln1_g, ln1_b, w_router, b_router, w_up, b_up, w_down, b_down, ln2_g, ln2_b, batch, seq):
    n = batch * seq
    w_qkv = w_in[:, :3 * D_ATTN].astype(BF16)
    w_mix = w_in[:, 3 * D_ATTN:MAIN_COLS].astype(BF16)
    w_gate = jnp.zeros((D_MODEL, LANES), BF16).at[:, :2 * N_HEADS_M].set(w_in[:, MAIN_COLS:].astype(BF16))
    qkvs = _qkv_project(h, w_qkv, min(QKV_TM, n))
    proj_m, gates = _project(h, w_mix, w_gate, min(PROJ_TM, n), PROJ_TN)
    gate_bias = jnp.zeros((1, LANES), F32).at[0, :2 * N_HEADS_M].set(jnp.concatenate([b_igate, b_fgate]))

    outs, lses = [], []
    for (_, dil), qkv in zip(DILATED_CONFIGS, qkvs):
        o, l = _dilated_attention(qkv, _attn_bias_tables(rel_bias, dil), batch, seq, dil)
        outs.append(o)
        lses.append(l)
    y_attn = _attn_combine(outs, lses, beta_attn)
    y_mlstm = _mlstm(proj_m, gates, gate_bias, conv_w, conv_b, beta_mlstm, batch, seq)

    h1, h1_packed = _outproj_ln(y_attn, y_mlstm, h, w_out.astype(BF16), ln1_g, ln1_b)

    top_idx, top_gate, rank, counts = _router(h1, w_router, b_router)
    counts = counts[0, :N_EXPERTS]
    nb = (counts + FFN_ROWS - 1) // FFN_ROWS
    per = (counts + nb * FFN_SUB - 1) // jnp.maximum(nb * FFN_SUB, 1) * FFN_SUB
    blk_end = jnp.cumsum(nb)
    blk_start = blk_end - nb
    e_idx, r_idx = top_idx[:, :TOP_K], rank[:, :TOP_K]
    onehot = (e_idx.reshape(-1, 1) == jnp.arange(N_EXPERTS, dtype=jnp.int32)[None, :]).astype(F32)
    table = jnp.stack([jnp.maximum(per, 1), blk_start], axis=1).astype(F32)
    looked = jnp.dot(onehot, table, precision=lax.Precision.HIGHEST)
    per_a, start_a = looked[:, 0], looked[:, 1]
    r_f = r_idx.reshape(-1).astype(F32)
    blk_in_e = jnp.floor((r_f + 0.5) / per_a)
    dest = ((start_a + blk_in_e) * FFN_ROWS + (r_f - blk_in_e * per_a)).astype(jnp.int32)
    nblk = n * TOP_K // FFN_ROWS + N_EXPERTS
    blk_id = jnp.arange(nblk, dtype=jnp.int32)
    blk_exp = jnp.minimum(jnp.sum(blk_end[None, :] <= blk_id[:, None], axis=1), N_EXPERTS - 1).astype(jnp.int32)
    n_used = blk_end[-1:].astype(jnp.int32)
    blk_valid = jnp.clip(counts[blk_exp] - (blk_id - blk_start[blk_exp]) * per[blk_exp], 0,
                         per[blk_exp]).astype(jnp.int32)
    slot_assign = jnp.full((nblk * FFN_ROWS,), -1, jnp.int32).at[dest].set(
        jnp.arange(n * TOP_K, dtype=jnp.int32), unique_indices=True)

    ys = _expert_ffn(h1_packed, slot_assign, blk_exp, blk_valid, n_used, w_up, b_up, w_down, b_down)
    return _combine_ln(top_gate, h1, ys, ln2_g, ln2_b)


def kernel(x, w_in, b_igate, b_fgate, conv_w, conv_b, rel_bias, beta_attn, beta_mlstm, w_out, ln1_g, ln1_b,
           w_router, b_router, w_up, b_up, w_down, b_down, ln2_g, ln2_b):
    batch, seq, d = x.shape
    h = x.reshape(batch * seq, d)
    for l in range(DEPTH):
        h = _layer(h, w_in[l], b_igate[l], b_fgate[l], conv_w[l], conv_b[l], rel_bias, beta_attn[l], beta_mlstm[l],
                   w_out[l], ln1_g[l], ln1_b[l], w_router[l], b_router[l], w_up[l], b_up[l], w_down[l], b_down[l],
                   ln2_g[l], ln2_b[l], batch, seq)
    return h.reshape(batch, seq, d)
```

```python
import functools
import math

import numpy as np
import jax
import jax.numpy as jnp
from jax import lax
from jax.experimental import pallas as pl
from jax.experimental.pallas import tpu as pltpu

F32 = jnp.float32
BF16 = jnp.bfloat16

D_MODEL = 2048
D_ATTN = 1024
HEAD_DIM_A = 64
N_HEADS_A = 16
DILATED_CONFIGS = ((128, 1), (512, 4), (2048, 16))
ATTN_BLOCK = 128
NUM_BUCKETS = 32
MAX_DISTANCE = 2048
D_MLSTM = 1024
N_HEADS_M = 4
HEAD_DIM_M = 256
CONV_K = 4
MLSTM_CHUNK = 128
MAIN_COLS = 3 * D_ATTN + 4 * D_MLSTM
N_EXPERTS = 32
TOP_K = 4
D_FF = 2048
SWIGLU_LIMIT = 7.0
SWIGLU_ALPHA = 1.702
DEPTH = 1
DN_ALPHA = (2 * DEPTH) ** 0.25
LN_EPS = 1e-5
HEAD_NORM_EPS = 1e-6
NEG_INF = -1e30

LANES = 128
VMEM_LIMIT = 48 * 1024 * 1024

PROJ_TM = 1024
PROJ_TN = 1024
COMBINE_TM = 512
OUTPROJ_TM = 512
ROUTER_TM = 512
QKV_TM = 512
ATTN_GROUP = 2
FFN_ROWS = 1024
FFN_SUB = 256
FFN_TF = 512
FFN_TH = 512
FFN_VMEM_LIMIT = 60 * 1024 * 1024
ROW_DMA_PRIORITY = 1
COMBINE_LN_TM = 256


def _cparams(sem, vmem=VMEM_LIMIT):
    return pltpu.CompilerParams(dimension_semantics=sem, vmem_limit_bytes=vmem)


def _dot(a, b):
    return jnp.dot(a, b, preferred_element_type=F32)


def _dot_f32_rhs(a_bf16, b_f32):
    hi = b_f32.astype(BF16)
    lo = (b_f32 - hi.astype(F32)).astype(BF16)
    return _dot(a_bf16, hi) + _dot(a_bf16, lo)


def _dot_f32_lhs(a_f32, b_bf16):
    hi = a_f32.astype(BF16)
    lo = (a_f32 - hi.astype(F32)).astype(BF16)
    return _dot(hi, b_bf16) + _dot(lo, b_bf16)


def _sigmoid(x):
    return 1.0 / (1.0 + jnp.exp(-x))


def _log_sigmoid(x):
    return jnp.minimum(x, 0.0) - jnp.log(1.0 + jnp.exp(-jnp.abs(x)))


def _proj_kernel(x_ref, w_ref, wg_ref, o_ref, g_ref):
    x = x_ref[...].astype(BF16)
    o_ref[...] = _dot(x, w_ref[...]).astype(o_ref.dtype)

    @pl.when(pl.program_id(1) == 0)
    def _():
        g_ref[...] = _dot(x, wg_ref[...])


def _project(x, w, w_gate, tm, tn):
    m, k = x.shape
    n = w.shape[1]
    return pl.pallas_call(
        _proj_kernel,
        grid=(m // tm, n // tn),
        in_specs=[pl.BlockSpec((tm, k), lambda i, j: (i, 0)),
                  pl.BlockSpec((k, tn), lambda i, j: (0, j)),
                  pl.BlockSpec((k, LANES), lambda i, j: (0, 0))],
        out_specs=[pl.BlockSpec((tm, tn), lambda i, j: (i, j)),
                   pl.BlockSpec((tm, LANES), lambda i, j: (i, 0))],
        out_shape=[jax.ShapeDtypeStruct((m, n), BF16), jax.ShapeDtypeStruct((m, LANES), F32)],
        compiler_params=_cparams(("parallel", "arbitrary")),
        name="in_proj",
    )(x, w, w_gate)


def _qkv_proj_kernel(x_ref, w_ref, *refs):
    o_refs, r_ref = refs[:-1], refs[-1]
    res = _dot(x_ref[...].astype(BF16), w_ref[...])
    ntile, tm, _ = r_ref.shape
    wid = ntile * LANES
    for c in range(ntile):
        r_ref[c] = res[:, c * LANES:(c + 1) * LANES]
    for (_, dil), o_ref in zip(DILATED_CONFIGS, o_refs):
        if dil == 1:
            o_ref[...] = res.astype(o_ref.dtype)
        else:
            for r in range(dil):
                for c in range(ntile):
                    col = r * wid + c * LANES
                    o_ref[:, col:col + LANES] = r_ref[c, pl.ds(r, tm // dil, stride=dil), :].astype(o_ref.dtype)


def _qkv_project(x, w, tm):
    m, k = x.shape
    wid = w.shape[1]
    dils = [d for _, d in DILATED_CONFIGS]
    return pl.pallas_call(
        _qkv_proj_kernel,
        grid=(m // tm,),
        in_specs=[pl.BlockSpec((tm, k), lambda i: (i, 0)),
                  pl.BlockSpec((k, wid), lambda i: (0, 0), pipeline_mode=pl.Buffered(1))],
        out_specs=[pl.BlockSpec((tm // d, d * wid), lambda i: (i, 0)) for d in dils],
        out_shape=[jax.ShapeDtypeStruct((m // d, d * wid), BF16) for d in dils],
        scratch_shapes=[pltpu.VMEM((wid // LANES, tm, LANES), F32)],
        compiler_params=_cparams(("parallel",), 56 * 1024 * 1024),
        name="qkv_proj",
    )(x, w)


def _attn_kernel(q_ref, kp_ref, kc_ref, vp_ref, vc_ref, bias_ref, sel_ref, o_ref, lse_ref):
    n = pl.program_id(2)
    tab = jnp.minimum(n, 1)
    grp, dh, nk = ATTN_GROUP, HEAD_DIM_A, 2 * ATTN_BLOCK
    wid = grp * dh
    lane_head = lax.broadcasted_iota(jnp.int32, (nk, wid), 1) // dh
    zero = jnp.zeros((nk, wid), BF16)
    ones_bd = jnp.concatenate([jnp.where(lane_head == j, 1.0, 0.0).astype(BF16) for j in range(grp)], axis=0)
    q_lane_head = lax.broadcasted_iota(jnp.int32, (ATTN_BLOCK, wid), 1) // dh
    lse_wide = []
    for g in range(N_HEADS_A // grp):
        cols = slice(g * wid, (g + 1) * wid)
        q = q_ref[0, :, cols] * (dh ** -0.5)
        kslab = jnp.concatenate([kp_ref[0, :, cols], kc_ref[0, :, cols]], axis=0)
        vslab = jnp.concatenate([vp_ref[0, :, cols], vc_ref[0, :, cols]], axis=0)
        k_bd = jnp.concatenate([jnp.where(lane_head == j, kslab, zero) for j in range(grp)], axis=0)
        v_bd = jnp.concatenate([jnp.where(lane_head == j, vslab, zero) for j in range(grp)], axis=0)
        s_all = lax.dot_general(q, k_bd, (((1,), (1,)), ((), ())), preferred_element_type=F32)
        ps, ms = [], []
        for j in range(grp):
            s = s_all[:, j * nk:(j + 1) * nk] + bias_ref[tab, g * grp + j]
            m = jnp.max(s, axis=-1, keepdims=True)
            ps.append(jnp.exp(s - m).astype(BF16))
            ms.append(m)
        p_all = jnp.concatenate(ps, axis=-1)
        res = _dot(p_all, jnp.concatenate([v_bd, ones_bd], axis=-1))
        den = res[:, wid:]
        o_ref[0, :, cols] = (res[:, :wid] / den).astype(o_ref.dtype)
        m_wide = ms[grp - 1]
        for j in range(grp - 1):
            m_wide = jnp.where(q_lane_head == j, ms[j], m_wide)
        lse_wide.append(m_wide + jnp.log(den))
    lse_ref[0] = _dot_f32_lhs(jnp.concatenate(lse_wide, axis=-1), sel_ref[...])


def _attn_bias_tables(rel_bias, dil):
    blk = ATTN_BLOCK
    period = 3 * blk
    k = np.arange(period)
    valid = k <= blk
    dist = np.where(valid, blk - k, 0) * dil
    max_exact = NUM_BUCKETS // 2
    d_f = np.maximum(dist, 1).astype(np.float32)
    large = max_exact + (np.log(d_f / np.float32(max_exact)) / np.float32(math.log(MAX_DISTANCE / max_exact))
                         * np.float32(NUM_BUCKETS - max_exact)).astype(np.int32)
    large = np.minimum(large, NUM_BUCKETS - 1)
    bucket = np.where(dist < max_exact, dist, large).astype(np.int32)
    w = jnp.where(jnp.asarray(valid)[None, :], rel_bias[jnp.asarray(bucket)].T.astype(F32), NEG_INF)
    t1 = jnp.tile(w, (1, blk))[:, :blk * (period - 1)].reshape(N_HEADS_A, blk, period - 1)[:, :, :2 * blk]
    has_prev = np.arange(2 * blk)[None, None, :] >= blk
    t0 = jnp.where(jnp.asarray(has_prev), t1, NEG_INF)
    return jnp.stack([t0, t1])


def _dilated_attention(qkv, bias_tab, batch, seq, dil):
    blk = ATTN_BLOCK
    l = seq // dil
    nb = l // blk
    ncb = 3
    pv = qkv.reshape(batch, l, dil * ncb * D_ATTN)
    head_sel = jnp.asarray((np.arange(D_ATTN)[:, None] == np.arange(LANES)[None, :] * HEAD_DIM_A), BF16)

    def cur(c):
        return pl.BlockSpec((1, blk, D_ATTN), lambda b, r, n: (b, n, r * ncb + c))

    def prev(c):
        return pl.BlockSpec((1, blk, D_ATTN), lambda b, r, n: (b, jnp.maximum(n - 1, 0), r * ncb + c))

    o, lse = pl.pallas_call(
        _attn_kernel,
        grid=(batch, dil, nb),
        in_specs=[cur(0), prev(1), cur(1), prev(2), cur(2),
                  pl.BlockSpec((2, N_HEADS_A, blk, 2 * blk), lambda b, r, n: (0, 0, 0, 0)),
                  pl.BlockSpec((D_ATTN, LANES), lambda b, r, n: (0, 0))],
        out_specs=[pl.BlockSpec((1, blk, D_ATTN), lambda b, r, n: (b, n, r)),
                   pl.BlockSpec((1, blk, LANES), lambda b, r, n: (b, n, r))],
        out_shape=[jax.ShapeDtypeStruct((batch, l, dil * D_ATTN), BF16),
                   jax.ShapeDtypeStruct((batch, l, dil * LANES), F32)],
        compiler_params=_cparams(("parallel", "parallel", "arbitrary")),
        name=f"dilated_attn_d{dil}",
    )(pv, pv, pv, pv, pv, bias_tab, head_sel)
    return o.reshape(batch * l, dil * D_ATTN), lse.reshape(batch * l, dil * LANES)


def _attn_combine_kernel(o1_ref, o2_ref, o3_ref, l1_ref, l2_ref, l3_ref, e_ref, et_ref, beta_ref, y_ref,
                         osc_ref, lsc_ref):
    tm = y_ref.shape[0]
    ntile = D_ATTN // LANES
    lses, outs = [], []
    for slot, ((_, dil), o_ref, l_ref) in enumerate(zip(DILATED_CONFIGS, (o1_ref, o2_ref, o3_ref),
                                                        (l1_ref, l2_ref, l3_ref))):
        if dil == 1:
            lses.append(l_ref[...])
            outs.append(lambda o_ref=o_ref: o_ref[...].astype(F32))
            continue
        rows = tm // dil
        for r in range(dil):
            lsc_ref[slot, pl.ds(r, rows, stride=dil), :] = l_ref[:, r * LANES:(r + 1) * LANES]
            for c in range(ntile):
                col = r * D_ATTN + c * LANES
                osc_ref[slot, c, pl.ds(r, rows, stride=dil), :] = o_ref[:, col:col + LANES].astype(F32)
        lses.append(lsc_ref[slot])
        outs.append(lambda slot=slot: jnp.concatenate([osc_ref[slot, c] for c in range(ntile)], axis=-1))
    mx = jnp.maximum(jnp.maximum(lses[0], lses[1]), lses[2])
    ws = [jnp.exp(l - mx) for l in lses]
    tot = ws[0] + ws[1] + ws[2]
    e = e_ref[...]
    acc = None
    for w, load_o in zip(ws, outs):
        term = _dot_f32_lhs(w / tot, e) * load_o()
        acc = term if acc is None else acc + term
    ss = _dot_f32_lhs(acc * acc, et_ref[...])
    inv = lax.rsqrt(ss * (1.0 / HEAD_DIM_A) + HEAD_NORM_EPS)
    y_ref[...] = (acc * _dot_f32_lhs(inv, e) * beta_ref[...]).astype(y_ref.dtype)


def _attn_combine(os_, lses, beta_attn):
    n = os_[0].shape[0]
    tm = min(COMBINE_TM, n)
    head_of_lane = np.arange(D_ATTN) // HEAD_DIM_A
    e = (np.arange(LANES)[:, None] == head_of_lane[None, :]).astype(np.float32)
    e_j = jnp.asarray(e, BF16)
    et_j = jnp.asarray(e.T, BF16)
    dils = [d for _, d in DILATED_CONFIGS]
    dilated = lambda w: [pl.BlockSpec((tm // d, d * w), lambda i: (i, 0)) for d in dils]
    full = lambda a, b: pl.BlockSpec((a, b), lambda i: (0, 0))
    return pl.pallas_call(
        _attn_combine_kernel,
        grid=(n // tm,),
        in_specs=dilated(D_ATTN) + dilated(LANES) + [full(LANES, D_ATTN), full(D_ATTN, LANES), full(1, D_ATTN)],
        out_specs=pl.BlockSpec((tm, D_ATTN), lambda i: (i, 0)),
        out_shape=jax.ShapeDtypeStruct((n, D_ATTN), BF16),
        scratch_shapes=[pltpu.VMEM((len(dils), D_ATTN // LANES, tm, LANES), F32),
                        pltpu.VMEM((len(dils), tm, LANES), F32)],
        compiler_params=_cparams(("parallel",)),
        name="attn_combine",
    )(*os_, *lses, e_j, et_j, beta_attn.reshape(1, D_ATTN).astype(F32))


def _mlstm_kernel(qp_ref, kp_ref, qprev_ref, kprev_ref, v_ref, og_ref, g_ref, gb_ref, cw_ref, cb_ref,
                  beta_ref, y_ref, c_ref, n_ref, m_ref):
    step = pl.program_id(1)
    ch = MLSTM_CHUNK
    dh = HEAD_DIM_M

    @pl.when(step == 0)
    def _():
        c_ref[...] = jnp.zeros(c_ref.shape, F32)
        n_ref[...] = jnp.zeros(n_ref.shape, F32)
        m_ref[...] = jnp.zeros(m_ref.shape, F32)

    def conv_silu(x_ref, prev_ref, coff, b):
        x = x_ref[b].astype(F32)
        p = jnp.where(step > 0, prev_ref[b].astype(F32), 0.0)
        xe = jnp.concatenate([p, x], axis=0)
        npad = p.shape[0]
        cols = slice(coff, coff + D_MLSTM)
        acc = cb_ref[:, cols] + cw_ref[CONV_K - 1:CONV_K, cols] * x
        for s in range(1, CONV_K):
            shifted = pltpu.roll(xe, s, 0)[npad:]
            acc = acc + cw_ref[CONV_K - 1 - s:CONV_K - s, cols] * shifted
        return acc * _sigmoid(acc)

    row_i = lax.broadcasted_iota(jnp.int32, (ch, ch), 0)
    col_i = lax.broadcasted_iota(jnp.int32, (ch, ch), 1)
    causal = row_i >= col_i
    tri = jnp.where(causal, 1.0, 0.0).astype(BF16)
    upp = jnp.where(row_i <= col_i, 1.0, 0.0).astype(BF16)
    b = 0
    qf = conv_silu(qp_ref, qprev_ref, 0, b)
    kf = conv_silu(kp_ref, kprev_ref, D_MLSTM, b) * (dh ** -0.5)
    qb = qf.astype(BF16)
    kb = kf.astype(BF16)
    g = g_ref[b] + gb_ref[...]
    gt = g.T
    b_cols = _dot_f32_rhs(tri, _log_sigmoid(g))
    b_rows = _dot_f32_lhs(_log_sigmoid(gt), upp)

    for h in range(N_HEADS_M):
        st = h
        hs = slice(h * dh, (h + 1) * dh)
        fi = N_HEADS_M + h
        i_row = gt[h:h + 1, :]
        i_col = g[:, h:h + 1]
        b_row = b_rows[fi:fi + 1, :]
        b_col = b_cols[:, fi:fi + 1]
        m_prev = m_ref[st][:, 0:1]
        q_h, k_h = qb[:, hs], kb[:, hs]
        v_h = v_ref[b, :, hs]

        dmat = jnp.where(causal, b_col - b_row + i_row, NEG_INF)
        m_inter = b_col + m_prev
        m_t = jnp.maximum(m_inter, jnp.max(dmat, axis=-1, keepdims=True))
        w = jnp.exp(dmat - m_t) * lax.dot_general(q_h, k_h, (((1,), (1,)), ((), ())),
                                                  preferred_element_type=F32)
        decay = jnp.exp(m_inter - m_t)
        c_old = c_ref[st]
        inter = lax.dot_general(q_h, c_old.astype(BF16), (((1,), (1,)), ((), ())), preferred_element_type=F32)
        num = _dot(w.astype(BF16), v_h) + decay * inter
        n_old = n_ref[st]
        den = jnp.sum(w, axis=-1, keepdims=True) + decay * jnp.sum(qf[:, hs] * n_old, axis=-1, keepdims=True)
        hh = num / jnp.maximum(jnp.abs(den), jnp.exp(-m_t))

        g_last = b_col[ch - 1:ch, :]
        a_row = g_last - b_row + i_row
        a_col = g_last - b_col + i_col
        m_new = jnp.maximum(g_last + m_prev, jnp.max(a_row, axis=-1, keepdims=True))
        carry = jnp.exp(g_last + m_prev - m_new)
        wa_col = jnp.exp(a_col - m_new)
        wv = (wa_col * v_h.astype(F32)).astype(BF16)
        c_ref[st] = carry * c_old + lax.dot_general(wv, k_h, (((0,), (0,)), ((), ())), preferred_element_type=F32)
        n_ref[st] = carry * n_old + jnp.sum(wa_col * kf[:, hs], axis=0, keepdims=True)
        m_ref[st] = jnp.broadcast_to(m_new, (1, LANES))

        gated = _sigmoid(og_ref[b, :, hs].astype(F32)) * hh
        ms = jnp.sum(gated * gated, axis=-1, keepdims=True) * (1.0 / dh)
        y_ref[b, :, hs] = (gated * lax.rsqrt(ms + HEAD_NORM_EPS) * beta_ref[:, hs]).astype(y_ref.dtype)


def _mlstm(proj, gates, gate_bias, conv_w, conv_b, beta_mlstm, batch, seq):
    ch = MLSTM_CHUNK
    nchunk = seq // ch
    pv = proj.reshape(batch, seq, 4 * D_MLSTM)
    gv = gates.reshape(batch, seq, LANES)
    prev_rows = 16
    cb0 = 0

    def cur(c):
        return pl.BlockSpec((1, ch, D_MLSTM), lambda b, n: (b, n, c))

    def prev(c):
        per = ch // prev_rows
        return pl.BlockSpec((1, prev_rows, D_MLSTM), lambda b, n: (b, jnp.maximum(n * per - 1, 0), c))

    const = lambda a, b_: pl.BlockSpec((a, b_), lambda b, n: (0, 0))
    y = pl.pallas_call(
        _mlstm_kernel,
        grid=(batch, nchunk),
        in_specs=[cur(cb0), cur(cb0 + 1), prev(cb0), prev(cb0 + 1), cur(cb0 + 2), cur(cb0 + 3),
                  pl.BlockSpec((1, ch, LANES), lambda b, n: (b, n, 0)),
                  const(1, LANES), const(CONV_K, 2 * D_MLSTM), const(1, 2 * D_MLSTM), const(1, D_MLSTM)],
        out_specs=pl.BlockSpec((1, ch, D_MLSTM), lambda b, n: (b, n, 0)),
        out_shape=jax.ShapeDtypeStruct((batch, seq, D_MLSTM), BF16),
        scratch_shapes=[pltpu.VMEM((N_HEADS_M, HEAD_DIM_M, HEAD_DIM_M), F32),
                        pltpu.VMEM((N_HEADS_M, 1, HEAD_DIM_M), F32),
                        pltpu.VMEM((N_HEADS_M, 1, LANES), F32)],
        compiler_params=_cparams(("parallel", "arbitrary")),
        name="mlstm",
    )(pv, pv, pv, pv, pv, pv, gv, gate_bias, conv_w.astype(F32), conv_b.reshape(1, -1).astype(F32),
      beta_mlstm.reshape(1, D_MLSTM).astype(F32))
    return y.reshape(batch * seq, D_MLSTM)


def _layer_norm(z, g, b):
    mu = jnp.mean(z, axis=-1, keepdims=True)
    zc = z - mu
    var = jnp.mean(zc * zc, axis=-1, keepdims=True)
    return zc * lax.rsqrt(var + LN_EPS) * g + b


PACK_ROWS = D_MODEL // (2 * LANES)


def _store_packed_rows(dst_ref, x, first=0):
    rows = x.shape[0]
    half = D_MODEL // 2
    for s in range(PACK_ROWS):
        lo = x[:, s * LANES:(s + 1) * LANES].astype(BF16).astype(F32)
        hi = x[:, half + s * LANES:half + (s + 1) * LANES].astype(BF16).astype(F32)
        word = pltpu.bitcast(hi, jnp.int32) | lax.shift_right_logical(pltpu.bitcast(lo, jnp.int32), 16)
        dst_ref[pl.ds(first * PACK_ROWS + s, rows, stride=PACK_ROWS), :] = word


def _load_packed_rows(src_ref, first, rows):
    los, his = [], []
    for s in range(PACK_ROWS):
        word = src_ref[pl.ds(first * PACK_ROWS + s, rows, stride=PACK_ROWS), :]
        los.append(pltpu.bitcast(lax.shift_left(word, 16), F32))
        his.append(pltpu.bitcast(word & jnp.int32(-65536), F32))
    return los, his


def _outproj_kernel(ya_ref, ym_ref, x_ref, w_ref, g_ref, b_ref, h_ref, hp_ref):
    y = _dot(ya_ref[...], w_ref[0:D_ATTN, :]) + _dot(ym_ref[...], w_ref[D_ATTN:D_MODEL, :])
    h = _layer_norm(DN_ALPHA * x_ref[...] + y, g_ref[...], b_ref[...])
    h_ref[...] = h
    _store_packed_rows(hp_ref, h)


def _outproj_ln(ya, ym, x, w_out, g, b):
    n = x.shape[0]
    tm = min(OUTPROJ_TM, n)
    row = lambda w: pl.BlockSpec((tm, w), lambda i: (i, 0))
    full = lambda a, b_: pl.BlockSpec((a, b_), lambda i: (0, 0))
    return pl.pallas_call(
        _outproj_kernel,
        grid=(n // tm,),
        in_specs=[row(D_ATTN), row(D_MLSTM), row(D_MODEL), full(D_MODEL, D_MODEL), full(1, D_MODEL), full(1, D_MODEL)],
        out_specs=[row(D_MODEL), pl.BlockSpec((tm * PACK_ROWS, LANES), lambda i: (i, 0))],
        out_shape=[jax.ShapeDtypeStruct((n, D_MODEL), F32),
                   jax.ShapeDtypeStruct((n * PACK_ROWS, LANES), jnp.int32)],
        compiler_params=_cparams(("parallel",)),
        name="out_proj_ln",
    )(ya, ym, x, w_out, g.reshape(1, -1), b.reshape(1, -1))


def _router_kernel(h_ref, whi_ref, wlo_ref, b_ref, tri_ref, idx_ref, gate_ref, rank_ref, cnt_ref, carry_ref):
    i = pl.program_id(0)

    @pl.when(i == 0)
    def _():
        carry_ref[...] = jnp.zeros(carry_ref.shape, F32)

    x = h_ref[...]
    xhi = x.astype(BF16)
    xlo = (x - xhi.astype(F32)).astype(BF16)
    logits = _dot(xhi, whi_ref[...]) + _dot(xhi, wlo_ref[...]) + _dot(xlo, whi_ref[...]) + b_ref[...]
    tm = logits.shape[0]
    lane = lax.broadcasted_iota(jnp.int32, (tm, LANES), 1)
    lane_f = lane.astype(F32)
    vals = jnp.where(lane < N_EXPERTS, logits, NEG_INF)

    sels, tops = [], []
    for _ in range(TOP_K):
        mx = jnp.max(vals, axis=-1, keepdims=True)
        first = jnp.min(jnp.where(vals == mx, lane_f, float(LANES)), axis=-1, keepdims=True)
        sel = lane_f == first
        sels.append(sel)
        tops.append((mx, first))
        vals = jnp.where(sel, 2.0 * NEG_INF, vals)

    exps = [jnp.exp(mx - tops[0][0]) for mx, _ in tops]
    tot = exps[0] + exps[1] + exps[2] + exps[3]

    onehot = jnp.zeros((tm, LANES), F32)
    for sel in sels:
        onehot = jnp.where(sel, 1.0, onehot)
    before = _dot(tri_ref[...], onehot.astype(BF16)) + carry_ref[...]

    idx_out = jnp.zeros((tm, LANES), F32)
    gate_out = jnp.zeros((tm, LANES), F32)
    rank_out = jnp.zeros((tm, LANES), F32)
    for k in range(TOP_K):
        rank_k = jnp.sum(jnp.where(sels[k], before, 0.0), axis=-1, keepdims=True)
        idx_out = jnp.where(lane == k, tops[k][1], idx_out)
        gate_out = jnp.where(lane == k, exps[k] / tot, gate_out)
        rank_out = jnp.where(lane == k, rank_k, rank_out)
    idx_ref[...] = idx_out.astype(jnp.int32)
    gate_ref[...] = gate_out
    rank_ref[...] = rank_out.astype(jnp.int32)

    carry = carry_ref[...] + jnp.sum(onehot, axis=0, keepdims=True)
    carry_ref[...] = carry
    cnt_ref[...] = carry.astype(jnp.int32)


def _router(h1, w_router, b_router):
    n = h1.shape[0]
    tm = min(ROUTER_TM, n)
    wpad = jnp.zeros((D_MODEL, LANES), F32).at[:, :N_EXPERTS].set(w_router)
    whi = wpad.astype(BF16)
    wlo = (wpad - whi.astype(F32)).astype(BF16)
    bpad = jnp.zeros((1, LANES), F32).at[0, :N_EXPERTS].set(b_router)
    tri = jnp.asarray(np.tril(np.ones((tm, tm), np.float32), -1), BF16)
    row = lambda w: pl.BlockSpec((tm, w), lambda i: (i, 0))
    full = lambda a, b_: pl.BlockSpec((a, b_), lambda i: (0, 0))
    return pl.pallas_call(
        _router_kernel,
        grid=(n // tm,),
        in_specs=[row(D_MODEL), full(D_MODEL, LANES), full(D_MODEL, LANES), full(1, LANES), full(tm, tm)],
        out_specs=[row(LANES), row(LANES), row(LANES), full(1, LANES)],
        out_shape=[jax.ShapeDtypeStruct((n, LANES), jnp.int32), jax.ShapeDtypeStruct((n, LANES), F32),
                   jax.ShapeDtypeStruct((n, LANES), jnp.int32), jax.ShapeDtypeStruct((1, LANES), jnp.int32)],
        scratch_shapes=[pltpu.VMEM((1, LANES), F32)],
        compiler_params=_cparams(("arbitrary",)),
        name="router",
    )(h1, whi, wlo, bpad, tri)


def _ffn_kernel(bexp_ref, nused_ref, nvalid_ref, slot_ref, slot_prv_ref, src_cur_ref, src_nxt_ref, wg_ref, wu_ref,
                wdl_ref, wdh_ref,
                bg_ref, bu_ref, bdl_ref, bdh_ref, xp_ref, ys_ref, xbuf_ref, x_ref, act_ref, obuf_ref, wgb_ref, wub_ref,
                wdlb_ref, wdhb_ref, in_sem, out_sem):
    del bexp_ref
    i = pl.program_id(0)
    g = pl.program_id(1)
    n_up = act_ref.shape[0]
    last_g = pl.num_programs(1) - 1
    last_i = pl.num_programs(0) - 1
    n_used = nused_ref[0]
    used = i < n_used
    rb = x_ref.shape[0]
    sub = FFN_SUB
    th = wdlb_ref.shape[1]
    unroll = 8

    def sub_blocks(blk):
        return (nvalid_ref[blk] + (sub - 1)) // sub

    nsub = sub_blocks(i)

    def start_gather(src_ref, lo, hi):
        def body(j, carry):
            for u in range(unroll):
                jj = j * unroll + u
                src = pl.multiple_of(src_ref[jj], PACK_ROWS)
                dst = pl.multiple_of(jj * PACK_ROWS, PACK_ROWS)
                pltpu.make_async_copy(xp_ref.at[pl.ds(src, PACK_ROWS)], xbuf_ref.at[pl.ds(dst, PACK_ROWS)],
                                      in_sem).start(priority=ROW_DMA_PRIORITY)
            return carry

        lax.fori_loop(lo // unroll, hi // unroll, body, 0)

    def wait_gather(nrows):
        npk = pl.multiple_of(nrows * PACK_ROWS, PACK_ROWS)
        pltpu.make_async_copy(xp_ref.at[pl.ds(0, npk)], xbuf_ref.at[pl.ds(0, npk)], in_sem).wait()

    def start_scatter(slots_ref, lo, hi):
        def one(jj):
            src = pl.multiple_of(jj * PACK_ROWS, PACK_ROWS)
            dst = pl.multiple_of(slots_ref[jj] * PACK_ROWS, PACK_ROWS)
            pltpu.make_async_copy(obuf_ref.at[pl.ds(src, PACK_ROWS)], ys_ref.at[pl.ds(dst, PACK_ROWS)],
                                  out_sem).start(priority=ROW_DMA_PRIORITY)

        def body(j, carry):
            for u in range(unroll):
                one(j * unroll + u)
            return carry

        nfull = jnp.maximum(hi - lo, 0) // unroll
        lax.fori_loop(lo // unroll, lo // unroll + nfull, body, 0)
        for u in range(unroll - 1):
            @pl.when(lo + nfull * unroll + u < hi)
            def _():
                one(lo + nfull * unroll + u)

    def wait_scatter(blk):
        npk = pl.multiple_of(nvalid_ref[blk] * PACK_ROWS, PACK_ROWS)
        pltpu.make_async_copy(obuf_ref.at[pl.ds(0, npk)], ys_ref.at[pl.ds(0, npk)], out_sem).wait()

    @pl.when(jnp.logical_and(used, g == 0))
    def _():
        @pl.when(i == 0)
        def _():
            start_gather(src_cur_ref, 0, nsub * sub)

        wait_gather(nsub * sub)
        for j in range(rb // sub):
            @pl.when(j < nsub)
            def _():
                los, his = _load_packed_rows(xbuf_ref, j * sub, sub)
                rows = slice(j * sub, (j + 1) * sub)
                for s in range(PACK_ROWS):
                    x_ref[rows, s * LANES:(s + 1) * LANES] = los[s].astype(BF16)
                    x_ref[rows, D_MODEL // 2 + s * LANES:D_MODEL // 2 + (s + 1) * LANES] = his[s].astype(BF16)

    @pl.when(jnp.logical_and(jnp.logical_and(g >= 1, g - 1 < sub_blocks(jnp.minimum(i + 1, last_i))), i + 1 < n_used))
    def _():
        start_gather(src_nxt_ref, (g - 1) * sub, g * sub)

    @pl.when(jnp.logical_and(jnp.logical_and(used, i > 0), g + 1 < rb // sub))
    def _():
        start_scatter(slot_prv_ref, (g + 1) * sub, jnp.minimum((g + 2) * sub, nvalid_ref[i - 1]))

    @pl.when(jnp.logical_and(used, g < n_up))
    def _():
        wgb_ref[...] = wg_ref[0].astype(BF16)
        wub_ref[...] = wu_ref[0].astype(BF16)

        def sub_body(j, carry):
            rows = pl.ds(pl.multiple_of(j * sub, sub), sub)
            x = x_ref[rows, :]
            hg = _dot(x, wgb_ref[...]) + bg_ref[0]
            hu = _dot(x, wub_ref[...]) + bu_ref[0]
            gate = jnp.minimum(hg, SWIGLU_LIMIT)
            up = jnp.clip(hu, -SWIGLU_LIMIT, SWIGLU_LIMIT)
            act_ref[g, rows, :] = ((up + 1.0) * (gate * _sigmoid(SWIGLU_ALPHA * gate))).astype(BF16)
            return carry

        lax.fori_loop(0, nsub, sub_body, 0)

    @pl.when(jnp.logical_and(used, g >= n_up))
    def _():
        @pl.when(jnp.logical_and(g == n_up, i > 0))
        def _():
            wait_scatter(i - 1)

        wdlb_ref[...] = wdl_ref[0].astype(BF16)
        wdhb_ref[...] = wdh_ref[0].astype(BF16)
        tile0 = (g - n_up) * (th // LANES)

        def sub_body(j, carry):
            row0 = pl.multiple_of(j * sub, sub)
            a = jnp.concatenate([act_ref[c, pl.ds(row0, sub), :] for c in range(n_up)], axis=-1)
            ylo = _dot(a, wdlb_ref[...]) + bdl_ref[0]
            yhi = _dot(a, wdhb_ref[...]) + bdh_ref[0]
            for s in range(th // LANES):
                lo = ylo[:, s * LANES:(s + 1) * LANES].astype(BF16).astype(F32)
                hi = yhi[:, s * LANES:(s + 1) * LANES].astype(BF16).astype(F32)
                word = pltpu.bitcast(hi, jnp.int32) | lax.shift_right_logical(pltpu.bitcast(lo, jnp.int32), 16)
                obuf_ref[pl.ds(row0 * PACK_ROWS + tile0 + s, sub, stride=PACK_ROWS), :] = word
            return carry

        lax.fori_loop(0, nsub, sub_body, 0)

        @pl.when(g == last_g)
        def _():
            nv = nvalid_ref[i]
            start_scatter(slot_ref, 0, jnp.where(i + 1 < n_used, jnp.minimum(sub, nv), nv))

    @pl.when(jnp.logical_and(i == last_i, g == last_g))
    def _():
        wait_scatter(jnp.minimum(i, n_used - 1))


def _expert_ffn(h_packed, slot_assign, blk_exp, blk_valid, n_used, w_up, b_up, w_down, b_down):
    rb, tf, th = FFN_ROWS, FFN_TF, FFN_TH
    nblk = slot_assign.shape[0] // rb
    n_up = D_FF // tf
    n_dn = D_MODEL // 2 // th
    n_assign = h_packed.shape[0] // PACK_ROWS * TOP_K
    slot_src = jnp.maximum(slot_assign, 0) // TOP_K * PACK_ROWS

    def clamp(i, nu):
        return jnp.minimum(i, nu[0] - 1)

    def up_sel(i, g, nu):
        return jnp.where(i < nu[0], jnp.minimum(g, n_up - 1), n_up - 1)

    def dn_sel(i, g, nu):
        return jnp.where(i < nu[0], jnp.clip(g - n_up, 0, n_dn - 1), n_dn - 1)

    def dn_exp(i, g, be, nu):
        early = jnp.logical_and(g < n_up - 1, i < nu[0])
        return be[jnp.where(early, jnp.maximum(clamp(i, nu) - 1, 0), clamp(i, nu))]

    def dn_col(i, g, nu):
        early = jnp.logical_and(g < n_up - 1, i < nu[0])
        return jnp.where(early, n_dn - 1, dn_sel(i, g, nu))

    grid_spec = pltpu.PrefetchScalarGridSpec(
        num_scalar_prefetch=3,
        grid=(nblk, n_up + n_dn),
        in_specs=[
            pl.BlockSpec((rb,), lambda i, g, be, nu, *_: (clamp(i, nu),), memory_space=pltpu.SMEM),
            pl.BlockSpec((rb,), lambda i, g, be, nu, *_: (jnp.maximum(clamp(i, nu) - 1, 0),), memory_space=pltpu.SMEM),
            pl.BlockSpec((rb,), lambda i, g, be, nu, *_: (clamp(i, nu),), memory_space=pltpu.SMEM),
            pl.BlockSpec((rb,), lambda i, g, be, nu, *_: (clamp(i + 1, nu),), memory_space=pltpu.SMEM),
            pl.BlockSpec((1, D_MODEL, tf), lambda i, g, be, nu, *_: (be[clamp(i, nu)], 0, up_sel(i, g, nu))),
            pl.BlockSpec((1, D_MODEL, tf), lambda i, g, be, nu, *_: (be[clamp(i, nu)], 0, n_up + up_sel(i, g, nu))),
            pl.BlockSpec((1, D_FF, th), lambda i, g, be, nu, *_: (dn_exp(i, g, be, nu), 0, dn_col(i, g, nu))),
            pl.BlockSpec((1, D_FF, th), lambda i, g, be, nu, *_: (dn_exp(i, g, be, nu), 0, n_dn + dn_col(i, g, nu))),
            pl.BlockSpec((1, 1, tf), lambda i, g, be, nu, *_: (be[clamp(i, nu)], 0, up_sel(i, g, nu))),
            pl.BlockSpec((1, 1, tf), lambda i, g, be, nu, *_: (be[clamp(i, nu)], 0, n_up + up_sel(i, g, nu))),
            pl.BlockSpec((1, 1, th), lambda i, g, be, nu, *_: (be[clamp(i, nu)], 0, dn_sel(i, g, nu))),
            pl.BlockSpec((1, 1, th), lambda i, g, be, nu, *_: (be[clamp(i, nu)], 0, n_dn + dn_sel(i, g, nu))),
            pl.BlockSpec(memory_space=pl.ANY),
        ],
        out_specs=pl.BlockSpec(memory_space=pl.ANY),
        scratch_shapes=[pltpu.VMEM((rb * PACK_ROWS, LANES), jnp.int32),
                        pltpu.VMEM((rb, D_MODEL), BF16),
                        pltpu.VMEM((n_up, rb, tf), BF16),
                        pltpu.VMEM((rb * PACK_ROWS, LANES), jnp.int32),
                        pltpu.VMEM((D_MODEL, tf), BF16), pltpu.VMEM((D_MODEL, tf), BF16),
                        pltpu.VMEM((D_FF, th), BF16), pltpu.VMEM((D_FF, th), BF16),
                        pltpu.SemaphoreType.DMA, pltpu.SemaphoreType.DMA],
    )
    assert n_up >= rb // FFN_SUB - 1 and n_up + n_dn > rb // FFN_SUB, "row DMA bursts are spread over the grid steps"
    return pl.pallas_call(
        _ffn_kernel,
        grid_spec=grid_spec,
        out_shape=jax.ShapeDtypeStruct((n_assign * PACK_ROWS, LANES), jnp.int32),
        compiler_params=pltpu.CompilerParams(dimension_semantics=("arbitrary", "arbitrary"),
                                             vmem_limit_bytes=FFN_VMEM_LIMIT, disable_bounds_checks=True),
        name="expert_ffn",
    )(blk_exp, n_used, blk_valid, slot_assign, slot_assign, slot_src, slot_src, w_up, w_up, w_down, w_down,
      b_up.reshape(N_EXPERTS, 1, 2 * D_FF), b_up.reshape(N_EXPERTS, 1, 2 * D_FF),
      b_down.reshape(N_EXPERTS, 1, D_MODEL), b_down.reshape(N_EXPERTS, 1, D_MODEL), h_packed)


def _combine_kernel(gate_ref, h_ref, g_ref, b_ref, ys_ref, o_ref):
    tm = h_ref.shape[0]
    gates = gate_ref[...]
    cols = [None] * (2 * PACK_ROWS)
    for k in range(TOP_K):
        gk = gates[:, k:k + 1]
        for s in range(PACK_ROWS):
            word = ys_ref[pl.ds(k * PACK_ROWS + s, tm, stride=TOP_K * PACK_ROWS), :]
            lo = pltpu.bitcast(lax.shift_left(word, 16), F32)
            hi = pltpu.bitcast(word & jnp.int32(-65536), F32)
            for c, blk in ((s, lo), (PACK_ROWS + s, hi)):
                cols[c] = gk * blk if cols[c] is None else cols[c] + gk * blk
    z = DN_ALPHA * h_ref[...] + jnp.concatenate(cols, axis=-1)
    o_ref[...] = _layer_norm(z, g_ref[...], b_ref[...])


def _combine_ln(gates, h1, ys, g, b):
    n = h1.shape[0]
    tm = min(COMBINE_LN_TM, n)
    row = lambda w: pl.BlockSpec((tm, w), lambda i: (i, 0))
    full = lambda a, b_: pl.BlockSpec((a, b_), lambda i: (0, 0))
    return pl.pallas_call(
        _combine_kernel,
        grid=(n // tm,),
        in_specs=[row(LANES), row(D_MODEL), full(1, D_MODEL), full(1, D_MODEL),
                  pl.BlockSpec((tm * TOP_K * PACK_ROWS, LANES), lambda i: (i, 0))],
        out_specs=row(D_MODEL),
        out_shape=jax.ShapeDtypeStruct((n, D_MODEL), F32),
        compiler_params=_cparams(("parallel",)),
        name="combine_ln",
    )(gates, h1, g.reshape(1, -1), b.reshape(1, -1), ys)


def _layer(h, w_in, b_igate, b_fgate, conv_w, conv_b, rel_bias, beta_attn, beta_mlstm, w_out,
           ln1_g, ln1_b, w_router, b_router, w_up, b_up, w_down, b_down, ln2_g, ln2_b, batch, seq):
    n = batch * seq
    w_qkv = w_in[:, :3 * D_ATTN].astype(BF16)
    w_mix = w_in[:, 3 * D_ATTN:MAIN_COLS].astype(BF16)
    w_gate = jnp.zeros((D_MODEL, LANES), BF16).at[:, :2 * N_HEADS_M].set(w_in[:, MAIN_COLS:].astype(BF16))
    qkvs = _qkv_project(h, w_qkv, min(QKV_TM, n))
    proj_m, gates = _project(h, w_mix, w_gate, min(PROJ_TM, n), PROJ_TN)
    gate_bias = jnp.zeros((1, LANES), F32).at[0, :2 * N_HEADS_M].set(jnp.concatenate([b_igate, b_fgate]))

    outs, lses = [], []
    for (_, dil), qkv in zip(DILATED_CONFIGS, qkvs):
        o, l = _dilated_attention(qkv, _attn_bias_tables(rel_bias, dil), batch, seq, dil)
        outs.append(o)
        lses.append(l)
    y_attn = _attn_combine(outs, lses, beta_attn)
    y_mlstm = _mlstm(proj_m, gates, gate_bias, conv_w, conv_b, beta_mlstm, batch, seq)

    h1, h1_packed = _outproj_ln(y_attn, y_mlstm, h, w_out.astype(BF16), ln1_g, ln1_b)

    top_idx, top_gate, rank, counts = _router(h1, w_router, b_router)
    counts = counts[0, :N_EXPERTS]
    nb = (counts + FFN_ROWS - 1) // FFN_ROWS
    per = (counts + nb * FFN_SUB - 1) // jnp.maximum(nb * FFN_SUB, 1) * FFN_SUB
    blk_end = jnp.cumsum(nb)
    blk_start = blk_end - nb
    e_idx, r_idx = top_idx[:, :TOP_K], rank[:, :TOP_K]
    onehot = (e_idx.reshape(-1, 1) == jnp.arange(N_EXPERTS, dtype=jnp.int32)[None, :]).astype(F32)
    table = jnp.stack([jnp.maximum(per, 1), blk_start], axis=1).astype(F32)
    looked = jnp.dot(onehot, table, precision=lax.Precision.HIGHEST)
    per_a, start_a = looked[:, 0], looked[:, 1]
    r_f = r_idx.reshape(-1).astype(F32)
    blk_in_e = jnp.floor((r_f + 0.5) / per_a)
    dest = ((start_a + blk_in_e) * FFN_ROWS + (r_f - blk_in_e * per_a)).astype(jnp.int32)
    nblk = n * TOP_K // FFN_ROWS + N_EXPERTS
    blk_id = jnp.arange(nblk, dtype=jnp.int32)
    blk_exp = jnp.minimum(jnp.sum(blk_end[None, :] <= blk_id[:, None], axis=1), N_EXPERTS - 1).astype(jnp.int32)
    n_used = blk_end[-1:].astype(jnp.int32)
    blk_valid = jnp.clip(counts[blk_exp] - (blk_id - blk_start[blk_exp]) * per[blk_exp], 0,
                         per[blk_exp]).astype(jnp.int32)
    slot_assign = jnp.full((nblk * FFN_ROWS,), -1, jnp.int32).at[dest].set(
        jnp.arange(n * TOP_K, dtype=jnp.int32), unique_indices=True)

    ys = _expert_ffn(h1_packed, slot_assign, blk_exp, blk_valid, n_used, w_up, b_up, w_down, b_down)
    return _combine_ln(top_gate, h1, ys, ln2_g, ln2_b)


def kernel(x, w_in, b_igate, b_fgate, conv_w, conv_b, rel_bias, beta_attn, beta_mlstm, w_out, ln1_g, ln1_b,
           w_router, b_router, w_up, b_up, w_down, b_down, ln2_g, ln2_b):
    batch, seq, d = x.shape
    h = x.reshape(batch * seq, d)
    for l in range(DEPTH):
        h = _layer(h, w_in[l], b_igate[l], b_fgate[l], conv_w[l], conv_b[l], rel_bias, beta_attn[l], beta_mlstm[l],
                   w_out[l], ln1_g[l], ln1_b[l], w_router[l], b_router[l], w_up[l], b_up[l], w_down[l], b_down[l],
                   ln2_g[l], ln2_b[l], batch, seq)
    return h.reshape(batch, seq, d)
```

```python
import functools
import math

import numpy as np
import jax
import jax.numpy as jnp
from jax import lax
from jax.experimental import pallas as pl
from jax.experimental.pallas import tpu as pltpu

F32 = jnp.float32
BF16 = jnp.bfloat16

D_MODEL = 2048
D_ATTN = 1024
HEAD_DIM_A = 64
N_HEADS_A = 16
DILATED_CONFIGS = ((128, 1), (512, 4), (2048, 16))
ATTN_BLOCK = 128
NUM_BUCKETS = 32
MAX_DISTANCE = 2048
D_MLSTM = 1024
N_HEADS_M = 4
HEAD_DIM_M = 256
CONV_K = 4
MLSTM_CHUNK = 128
MAIN_COLS = 3 * D_ATTN + 4 * D_MLSTM
N_EXPERTS = 32
TOP_K = 4
D_FF = 2048
SWIGLU_LIMIT = 7.0
SWIGLU_ALPHA = 1.702
DEPTH = 1
DN_ALPHA = (2 * DEPTH) ** 0.25
LN_EPS = 1e-5
HEAD_NORM_EPS = 1e-6
NEG_INF = -1e30

LANES = 128
VMEM_LIMIT = 48 * 1024 * 1024

PROJ_TM = 1024
PROJ_TN = 1024
COMBINE_TM = 512
OUTPROJ_TM = 512
ROUTER_TM = 512
QKV_TM = 512
ATTN_GROUP = 2
FFN_ROWS = 1024
FFN_SUB = 256
FFN_TF = 512
FFN_TH = 512
FFN_VMEM_LIMIT = 60 * 1024 * 1024
ROW_DMA_PRIORITY = 1
COMBINE_LN_TM = 256


def _cparams(sem, vmem=VMEM_LIMIT):
    return pltpu.CompilerParams(dimension_semantics=sem, vmem_limit_bytes=vmem)


def _dot(a, b):
    return jnp.dot(a, b, preferred_element_type=F32)


def _dot_f32_rhs(a_bf16, b_f32):
    hi = b_f32.astype(BF16)
    lo = (b_f32 - hi.astype(F32)).astype(BF16)
    return _dot(a_bf16, hi) + _dot(a_bf16, lo)


def _dot_f32_lhs(a_f32, b_bf16):
    hi = a_f32.astype(BF16)
    lo = (a_f32 - hi.astype(F32)).astype(BF16)
    return _dot(hi, b_bf16) + _dot(lo, b_bf16)


def _sigmoid(x):
    return 1.0 / (1.0 + jnp.exp(-x))


def _log_sigmoid(x):
    return jnp.minimum(x, 0.0) - jnp.log(1.0 + jnp.exp(-jnp.abs(x)))


def _proj_kernel(x_ref, w_ref, wg_ref, o_ref, g_ref):
    x = x_ref[...].astype(BF16)
    o_ref[...] = _dot(x, w_ref[...]).astype(o_ref.dtype)

    @pl.when(pl.program_id(1) == 0)
    def _():
        g_ref[...] = _dot(x, wg_ref[...])


def _project(x, w, w_gate, tm, tn):
    m, k = x.shape
    n = w.shape[1]
    return pl.pallas_call(
        _proj_kernel,
        grid=(m // tm, n // tn),
        in_specs=[pl.BlockSpec((tm, k), lambda i, j: (i, 0)),
                  pl.BlockSpec((k, tn), lambda i, j: (0, j)),
                  pl.BlockSpec((k, LANES), lambda i, j: (0, 0))],
        out_specs=[pl.BlockSpec((tm, tn), lambda i, j: (i, j)),
                   pl.BlockSpec((tm, LANES), lambda i, j: (i, 0))],
        out_shape=[jax.ShapeDtypeStruct((m, n), BF16), jax.ShapeDtypeStruct((m, LANES), F32)],
        compiler_params=_cparams(("parallel", "arbitrary")),
        name="in_proj",
    )(x, w, w_gate)


def _qkv_proj_kernel(x_ref, w_ref, *refs):
    n_out = len(DILATED_CONFIGS)
    o_refs, s_refs = refs[:n_out], refs[n_out:]
    res = _dot(x_ref[...].astype(BF16), w_ref[...])
    ntile, _, tm, _ = s_refs[0].shape
    wid = ntile * LANES
    for c in range(ntile):
        s_refs[0][c, 0] = res[:, c * LANES:(c + 1) * LANES]
    prev_dil = 1
    for level, ((_, dil), o_ref) in enumerate(zip(DILATED_CONFIGS, o_refs)):
        if dil == 1:
            o_ref[...] = res.astype(o_ref.dtype)
            continue
        step = dil // prev_dil
        rows = tm // dil
        keep = level + 1 < n_out
        for rp in range(prev_dil):
            for q in range(step):
                r = rp + prev_dil * q
                for c in range(ntile):
                    piece = s_refs[level - 1][c, rp, pl.ds(q, rows, stride=step), :]
                    col = r * wid + c * LANES
                    o_ref[:, col:col + LANES] = piece.astype(o_ref.dtype)
                    if keep:
                        s_refs[level][c, r] = piece
        prev_dil = dil


def _qkv_project(x, w, tm):
    m, k = x.shape
    wid = w.shape[1]
    dils = [d for _, d in DILATED_CONFIGS]
    return pl.pallas_call(
        _qkv_proj_kernel,
        grid=(m // tm,),
        in_specs=[pl.BlockSpec((tm, k), lambda i: (i, 0)),
                  pl.BlockSpec((k, wid), lambda i: (0, 0), pipeline_mode=pl.Buffered(1))],
        out_specs=[pl.BlockSpec((tm // d, d * wid), lambda i: (i, 0)) for d in dils],
        out_shape=[jax.ShapeDtypeStruct((m // d, d * wid), BF16) for d in dils],
        scratch_shapes=[pltpu.VMEM((wid // LANES, d, tm // d, LANES), F32) for d in dils[:-1]],
        compiler_params=_cparams(("parallel",), 56 * 1024 * 1024),
        name="qkv_proj",
    )(x, w)


def _attn_kernel(q_ref, kp_ref, kc_ref, vp_ref, vc_ref, bias_ref, sel_ref, o_ref, lse_ref):
    n = pl.program_id(2)
    tab = jnp.minimum(n, 1)
    grp, dh, nk = ATTN_GROUP, HEAD_DIM_A, 2 * ATTN_BLOCK
    wid = grp * dh
    lane_head = lax.broadcasted_iota(jnp.int32, (nk, wid), 1) // dh
    zero = jnp.zeros((nk, wid), BF16)
    ones_bd = jnp.concatenate([jnp.where(lane_head == j, 1.0, 0.0).astype(BF16) for j in range(grp)], axis=0)
    q_lane_head = lax.broadcasted_iota(jnp.int32, (ATTN_BLOCK, wid), 1) // dh
    lse_wide = []
    for g in range(N_HEADS_A // grp):
        cols = slice(g * wid, (g + 1) * wid)
        q = q_ref[0, :, cols] * (dh ** -0.5)
        kslab = jnp.concatenate([kp_ref[0, :, cols], kc_ref[0, :, cols]], axis=0)
        vslab = jnp.concatenate([vp_ref[0, :, cols], vc_ref[0, :, cols]], axis=0)
        k_bd = jnp.concatenate([jnp.where(lane_head == j, kslab, zero) for j in range(grp)], axis=0)
        v_bd = jnp.concatenate([jnp.where(lane_head == j, vslab, zero) for j in range(grp)], axis=0)
        s_all = lax.dot_general(q, k_bd, (((1,), (1,)), ((), ())), preferred_element_type=F32)
        ps, ms = [], []
        for j in range(grp):
            s = s_all[:, j * nk:(j + 1) * nk] + bias_ref[tab, g * grp + j]
            m = jnp.max(s, axis=-1, keepdims=True)
            ps.append(jnp.exp(s - m).astype(BF16))
            ms.append(m)
        p_all = jnp.concatenate(ps, axis=-1)
        res = _dot(p_all, jnp.concatenate([v_bd, ones_bd], axis=-1))
        den = res[:, wid:]
        o_ref[0, :, cols] = (res[:, :wid] / den).astype(o_ref.dtype)
        m_wide = ms[grp - 1]
        for j in range(grp - 1):
            m_wide = jnp.where(q_lane_head == j, ms[j], m_wide)
        lse_wide.append(m_wide + jnp.log(den))
    lse_ref[0] = _dot_f32_lhs(jnp.concatenate(lse_wide, axis=-1), sel_ref[...])


def _attn_bias_tables(rel_bias, dil):
    blk = ATTN_BLOCK
    period = 3 * blk
    k = np.arange(period)
    valid = k <= blk
    dist = np.where(valid, blk - k, 0) * dil
    max_exact = NUM_BUCKETS // 2
    d_f = np.maximum(dist, 1).astype(np.float32)
    large = max_exact + (np.log(d_f / np.float32(max_exact)) / np.float32(math.log(MAX_DISTANCE / max_exact))
                         * np.float32(NUM_BUCKETS - max_exact)).astype(np.int32)
    large = np.minimum(large, NUM_BUCKETS - 1)
    bucket = np.where(dist < max_exact, dist, large).astype(np.int32)
    w = jnp.where(jnp.asarray(valid)[None, :], rel_bias[jnp.asarray(bucket)].T.astype(F32), NEG_INF)
    t1 = jnp.tile(w, (1, blk))[:, :blk * (period - 1)].reshape(N_HEADS_A, blk, period - 1)[:, :, :2 * blk]
    has_prev = np.arange(2 * blk)[None, None, :] >= blk
    t0 = jnp.where(jnp.asarray(has_prev), t1, NEG_INF)
    return jnp.stack([t0, t1])


def _dilated_attention(qkv, bias_tab, batch, seq, dil):
    blk = ATTN_BLOCK
    l = seq // dil
    nb = l // blk
    ncb = 3
    pv = qkv.reshape(batch, l, dil * ncb * D_ATTN)
    head_sel = jnp.asarray((np.arange(D_ATTN)[:, None] == np.arange(LANES)[None, :] * HEAD_DIM_A), BF16)

    def cur(c):
        return pl.BlockSpec((1, blk, D_ATTN), lambda b, r, n: (b, n, r * ncb + c))

    def prev(c):
        return pl.BlockSpec((1, blk, D_ATTN), lambda b, r, n: (b, jnp.maximum(n - 1, 0), r * ncb + c))

    o, lse = pl.pallas_call(
        _attn_kernel,
        grid=(batch, dil, nb),
        in_specs=[cur(0), prev(1), cur(1), prev(2), cur(2),
                  pl.BlockSpec((2, N_HEADS_A, blk, 2 * blk), lambda b, r, n: (0, 0, 0, 0)),
                  pl.BlockSpec((D_ATTN, LANES), lambda b, r, n: (0, 0))],
        out_specs=[pl.BlockSpec((1, blk, D_ATTN), lambda b, r, n: (b, n, r)),
                   pl.BlockSpec((1, blk, LANES), lambda b, r, n: (b, n, r))],
        out_shape=[jax.ShapeDtypeStruct((batch, l, dil * D_ATTN), BF16),
                   jax.ShapeDtypeStruct((batch, l, dil * LANES), F32)],
        compiler_params=_cparams(("parallel", "parallel", "arbitrary")),
        name=f"dilated_attn_d{dil}",
    )(pv, pv, pv, pv, pv, bias_tab, head_sel)
    return o.reshape(batch * l, dil * D_ATTN), lse.reshape(batch * l, dil * LANES)


def _attn_combine_kernel(o1_ref, o2_ref, o3_ref, l1_ref, l2_ref, l3_ref, e_ref, et_ref, beta_ref, y_ref,
                         osc_ref, lsc_ref):
    tm = y_ref.shape[0]
    ntile = D_ATTN // LANES
    lses, outs = [], []
    for slot, ((_, dil), o_ref, l_ref) in enumerate(zip(DILATED_CONFIGS, (o1_ref, o2_ref, o3_ref),
                                                        (l1_ref, l2_ref, l3_ref))):
        if dil == 1:
            lses.append(l_ref[...])
            outs.append(lambda o_ref=o_ref: o_ref[...].astype(F32))
            continue
        rows = tm // dil
        for r in range(dil):
            lsc_ref[slot, pl.ds(r, rows, stride=dil), :] = l_ref[:, r * LANES:(r + 1) * LANES]
            for c in range(ntile):
                col = r * D_ATTN + c * LANES
                osc_ref[slot, c, pl.ds(r, rows, stride=dil), :] = o_ref[:, col:col + LANES].astype(F32)
        lses.append(lsc_ref[slot])
        outs.append(lambda slot=slot: jnp.concatenate([osc_ref[slot, c] for c in range(ntile)], axis=-1))
    mx = jnp.maximum(jnp.maximum(lses[0], lses[1]), lses[2])
    ws = [jnp.exp(l - mx) for l in lses]
    tot = ws[0] + ws[1] + ws[2]
    e = e_ref[...]
    acc = None
    for w, load_o in zip(ws, outs):
        term = _dot_f32_lhs(w / tot, e) * load_o()
        acc = term if acc is None else acc + term
    ss = _dot_f32_lhs(acc * acc, et_ref[...])
    inv = lax.rsqrt(ss * (1.0 / HEAD_DIM_A) + HEAD_NORM_EPS)
    y_ref[...] = (acc * _dot_f32_lhs(inv, e) * beta_ref[...]).astype(y_ref.dtype)


def _attn_combine(os_, lses, beta_attn):
    n = os_[0].shape[0]
    tm = min(COMBINE_TM, n)
    head_of_lane = np.arange(D_ATTN) // HEAD_DIM_A
    e = (np.arange(LANES)[:, None] == head_of_lane[None, :]).astype(np.float32)
    e_j = jnp.asarray(e, BF16)
    et_j = jnp.asarray(e.T, BF16)
    dils = [d for _, d in DILATED_CONFIGS]
    dilated = lambda w: [pl.BlockSpec((tm // d, d * w), lambda i: (i, 0)) for d in dils]
    full = lambda a, b: pl.BlockSpec((a, b), lambda i: (0, 0))
    return pl.pallas_call(
        _attn_combine_kernel,
        grid=(n // tm,),
        in_specs=dilated(D_ATTN) + dilated(LANES) + [full(LANES, D_ATTN), full(D_ATTN, LANES), full(1, D_ATTN)],
        out_specs=pl.BlockSpec((tm, D_ATTN), lambda i: (i, 0)),
        out_shape=jax.ShapeDtypeStruct((n, D_ATTN), BF16),
        scratch_shapes=[pltpu.VMEM((len(dils), D_ATTN // LANES, tm, LANES), F32),
                        pltpu.VMEM((len(dils), tm, LANES), F32)],
        compiler_params=_cparams(("parallel",)),
        name="attn_combine",
    )(*os_, *lses, e_j, et_j, beta_attn.reshape(1, D_ATTN).astype(F32))


def _mlstm_kernel(qp_ref, kp_ref, qprev_ref, kprev_ref, v_ref, og_ref, g_ref, gb_ref, cw_ref, cb_ref,
                  beta_ref, y_ref, c_ref, n_ref, m_ref):
    step = pl.program_id(1)
    ch = MLSTM_CHUNK
    dh = HEAD_DIM_M

    @pl.when(step == 0)
    def _():
        c_ref[...] = jnp.zeros(c_ref.shape, F32)
        n_ref[...] = jnp.zeros(n_ref.shape, F32)
        m_ref[...] = jnp.zeros(m_ref.shape, F32)

    def conv_silu(x_ref, prev_ref, coff, b):
        x = x_ref[b].astype(F32)
        p = jnp.where(step > 0, prev_ref[b].astype(F32), 0.0)
        xe = jnp.concatenate([p, x], axis=0)
        npad = p.shape[0]
        cols = slice(coff, coff + D_MLSTM)
        acc = cb_ref[:, cols] + cw_ref[CONV_K - 1:CONV_K, cols] * x
        for s in range(1, CONV_K):
            shifted = pltpu.roll(xe, s, 0)[npad:]
            acc = acc + cw_ref[CONV_K - 1 - s:CONV_K - s, cols] * shifted
        return acc * _sigmoid(acc)

    row_i = lax.broadcasted_iota(jnp.int32, (ch, ch), 0)
    col_i = lax.broadcasted_iota(jnp.int32, (ch, ch), 1)
    causal = row_i >= col_i
    tri = jnp.where(causal, 1.0, 0.0).astype(BF16)
    upp = jnp.where(row_i <= col_i, 1.0, 0.0).astype(BF16)
    b = 0
    qf = conv_silu(qp_ref, qprev_ref, 0, b)
    kf = conv_silu(kp_ref, kprev_ref, D_MLSTM, b) * (dh ** -0.5)
    qb = qf.astype(BF16)
    kb = kf.astype(BF16)
    g = g_ref[b] + gb_ref[...]
    gt = g.T
    b_cols = _dot_f32_rhs(tri, _log_sigmoid(g))
    b_rows = _dot_f32_lhs(_log_sigmoid(gt), upp)

    for h in range(N_HEADS_M):
        st = h
        hs = slice(h * dh, (h + 1) * dh)
        fi = N_HEADS_M + h
        i_row = gt[h:h + 1, :]
        i_col = g[:, h:h + 1]
        b_row = b_rows[fi:fi + 1, :]
        b_col = b_cols[:, fi:fi + 1]
        m_prev = m_ref[st][:, 0:1]
        q_h, k_h = qb[:, hs], kb[:, hs]
        v_h = v_ref[b, :, hs]

        dmat = jnp.where(causal, b_col - b_row + i_row, NEG_INF)
        m_inter = b_col + m_prev
        m_t = jnp.maximum(m_inter, jnp.max(dmat, axis=-1, keepdims=True))
        w = jnp.exp(dmat - m_t) * lax.dot_general(q_h, k_h, (((1,), (1,)), ((), ())),
                                                  preferred_element_type=F32)
        decay = jnp.exp(m_inter - m_t)
        c_old = c_ref[st]
        inter = lax.dot_general(q_h, c_old.astype(BF16), (((1,), (1,)), ((), ())), preferred_element_type=F32)
        num = _dot(w.astype(BF16), v_h) + decay * inter
        n_old = n_ref[st]
        den = jnp.sum(w, axis=-1, keepdims=True) + decay * jnp.sum(qf[:, hs] * n_old, axis=-1, keepdims=True)
        hh = num / jnp.maximum(jnp.abs(den), jnp.exp(-m_t))

        g_last = b_col[ch - 1:ch, :]
        a_row = g_last - b_row + i_row
        a_col = g_last - b_col + i_col
        m_new = jnp.maximum(g_last + m_prev, jnp.max(a_row, axis=-1, keepdims=True))
        carry = jnp.exp(g_last + m_prev - m_new)
        wa_col = jnp.exp(a_col - m_new)
        wv = (wa_col * v_h.astype(F32)).astype(BF16)
        c_ref[st] = carry * c_old + lax.dot_general(wv, k_h, (((0,), (0,)), ((), ())), preferred_element_type=F32)
        n_ref[st] = carry * n_old + jnp.sum(wa_col * kf[:, hs], axis=0, keepdims=True)
        m_ref[st] = jnp.broadcast_to(m_new, (1, LANES))

        gated = _sigmoid(og_ref[b, :, hs].astype(F32)) * hh
        ms = jnp.sum(gated * gated, axis=-1, keepdims=True) * (1.0 / dh)
        y_ref[b, :, hs] = (gated * lax.rsqrt(ms + HEAD_NORM_EPS) * beta_ref[:, hs]).astype(y_ref.dtype)


def _mlstm(proj, gates, gate_bias, conv_w, conv_b, beta_mlstm, batch, seq):
    ch = MLSTM_CHUNK
    nchunk = seq // ch
    pv = proj.reshape(batch, seq, 4 * D_MLSTM)
    gv = gates.reshape(batch, seq, LANES)
    prev_rows = 16
    cb0 = 0

    def cur(c):
        return pl.BlockSpec((1, ch, D_MLSTM), lambda b, n: (b, n, c))

    def prev(c):
        per = ch // prev_rows
        return pl.BlockSpec((1, prev_rows, D_MLSTM), lambda b, n: (b, jnp.maximum(n * per - 1, 0), c))

    const = lambda a, b_: pl.BlockSpec((a, b_), lambda b, n: (0, 0))
    y = pl.pallas_call(
        _mlstm_kernel,
        grid=(batch, nchunk),
        in_specs=[cur(cb0), cur(cb0 + 1), prev(cb0), prev(cb0 + 1), cur(cb0 + 2), cur(cb0 + 3),
                  pl.BlockSpec((1, ch, LANES), lambda b, n: (b, n, 0)),
                  const(1, LANES), const(CONV_K, 2 * D_MLSTM), const(1, 2 * D_MLSTM), const(1, D_MLSTM)],
        out_specs=pl.BlockSpec((1, ch, D_MLSTM), lambda b, n: (b, n, 0)),
        out_shape=jax.ShapeDtypeStruct((batch, seq, D_MLSTM), BF16),
        scratch_shapes=[pltpu.VMEM((N_HEADS_M, HEAD_DIM_M, HEAD_DIM_M), F32),
                        pltpu.VMEM((N_HEADS_M, 1, HEAD_DIM_M), F32),
                        pltpu.VMEM((N_HEADS_M, 1, LANES), F32)],
        compiler_params=_cparams(("parallel", "arbitrary")),
        name="mlstm",
    )(pv, pv, pv, pv, pv, pv, gv, gate_bias, conv_w.astype(F32), conv_b.reshape(1, -1).astype(F32),
      beta_mlstm.reshape(1, D_MLSTM).astype(F32))
    return y.reshape(batch * seq, D_MLSTM)


def _layer_norm(z, g, b):
    mu = jnp.mean(z, axis=-1, keepdims=True)
    zc = z - mu
    var = jnp.mean(zc * zc, axis=-1, keepdims=True)
    return zc * lax.rsqrt(var + LN_EPS) * g + b


PACK_ROWS = D_MODEL // (2 * LANES)


def _store_packed_rows(dst_ref, x, first=0):
    rows = x.shape[0]
    half = D_MODEL // 2
    for s in range(PACK_ROWS):
        lo = x[:, s * LANES:(s + 1) * LANES].astype(BF16).astype(F32)
        hi = x[:, half + s * LANES:half + (s + 1) * LANES].astype(BF16).astype(F32)
        word = pltpu.bitcast(hi, jnp.int32) | lax.shift_right_logical(pltpu.bitcast(lo, jnp.int32), 16)
        dst_ref[pl.ds(first * PACK_ROWS + s, rows, stride=PACK_ROWS), :] = word


def _load_packed_rows(src_ref, first, rows):
    los, his = [], []
    for s in range(PACK_ROWS):
        word = src_ref[pl.ds(first * PACK_ROWS + s, rows, stride=PACK_ROWS), :]
        los.append(pltpu.bitcast(lax.shift_left(word, 16), F32))
        his.append(pltpu.bitcast(word & jnp.int32(-65536), F32))
    return los, his


def _outproj_kernel(ya_ref, ym_ref, x_ref, w_ref, g_ref, b_ref, h_ref, hp_ref):
    y = _dot(ya_ref[...], w_ref[0:D_ATTN, :]) + _dot(ym_ref[...], w_ref[D_ATTN:D_MODEL, :])
    h = _layer_norm(DN_ALPHA * x_ref[...] + y, g_ref[...], b_ref[...])
    h_ref[...] = h
    _store_packed_rows(hp_ref, h)


def _outproj_ln(ya, ym, x, w_out, g, b):
    n = x.shape[0]
    tm = min(OUTPROJ_TM, n)
    row = lambda w: pl.BlockSpec((tm, w), lambda i: (i, 0))
    full = lambda a, b_: pl.BlockSpec((a, b_), lambda i: (0, 0))
    return pl.pallas_call(
        _outproj_kernel,
        grid=(n // tm,),
        in_specs=[row(D_ATTN), row(D_MLSTM), row(D_MODEL), full(D_MODEL, D_MODEL), full(1, D_MODEL), full(1, D_MODEL)],
        out_specs=[row(D_MODEL), pl.BlockSpec((tm * PACK_ROWS, LANES), lambda i: (i, 0))],
        out_shape=[jax.ShapeDtypeStruct((n, D_MODEL), F32),
                   jax.ShapeDtypeStruct((n * PACK_ROWS, LANES), jnp.int32)],
        compiler_params=_cparams(("parallel",)),
        name="out_proj_ln",
    )(ya, ym, x, w_out, g.reshape(1, -1), b.reshape(1, -1))


def _router_kernel(h_ref, whi_ref, wlo_ref, b_ref, tri_ref, idx_ref, gate_ref, rank_ref, cnt_ref, carry_ref):
    i = pl.program_id(0)

    @pl.when(i == 0)
    def _():
        carry_ref[...] = jnp.zeros(carry_ref.shape, F32)

    x = h_ref[...]
    xhi = x.astype(BF16)
    xlo = (x - xhi.astype(F32)).astype(BF16)
    logits = _dot(xhi, whi_ref[...]) + _dot(xhi, wlo_ref[...]) + _dot(xlo, whi_ref[...]) + b_ref[...]
    tm = logits.shape[0]
    lane = lax.broadcasted_iota(jnp.int32, (tm, LANES), 1)
    lane_f = lane.astype(F32)
    vals = jnp.where(lane < N_EXPERTS, logits, NEG_INF)

    sels, tops = [], []
    for _ in range(TOP_K):
        mx = jnp.max(vals, axis=-1, keepdims=True)
        first = jnp.min(jnp.where(vals == mx, lane_f, float(LANES)), axis=-1, keepdims=True)
        sel = lane_f == first
        sels.append(sel)
        tops.append((mx, first))
        vals = jnp.where(sel, 2.0 * NEG_INF, vals)

    exps = [jnp.exp(mx - tops[0][0]) for mx, _ in tops]
    tot = exps[0] + exps[1] + exps[2] + exps[3]

    onehot = jnp.zeros((tm, LANES), F32)
    for sel in sels:
        onehot = jnp.where(sel, 1.0, onehot)
    before = _dot(tri_ref[...], onehot.astype(BF16)) + carry_ref[...]

    idx_out = jnp.zeros((tm, LANES), F32)
    gate_out = jnp.zeros((tm, LANES), F32)
    rank_out = jnp.zeros((tm, LANES), F32)
    for k in range(TOP_K):
        rank_k = jnp.sum(jnp.where(sels[k], before, 0.0), axis=-1, keepdims=True)
        idx_out = jnp.where(lane == k, tops[k][1], idx_out)
        gate_out = jnp.where(lane == k, exps[k] / tot, gate_out)
        rank_out = jnp.where(lane == k, rank_k, rank_out)
    idx_ref[...] = idx_out.astype(jnp.int32)
    gate_ref[...] = gate_out
    rank_ref[...] = rank_out.astype(jnp.int32)

    carry = carry_ref[...] + jnp.sum(onehot, axis=0, keepdims=True)
    carry_ref[...] = carry
    cnt_ref[...] = carry.astype(jnp.int32)


def _router(h1, w_router, b_router):
    n = h1.shape[0]
    tm = min(ROUTER_TM, n)
    wpad = jnp.zeros((D_MODEL, LANES), F32).at[:, :N_EXPERTS].set(w_router)
    whi = wpad.astype(BF16)
    wlo = (wpad - whi.astype(F32)).astype(BF16)
    bpad = jnp.zeros((1, LANES), F32).at[0, :N_EXPERTS].set(b_router)
    tri = jnp.asarray(np.tril(np.ones((tm, tm), np.float32), -1), BF16)
    row = lambda w: pl.BlockSpec((tm, w), lambda i: (i, 0))
    full = lambda a, b_: pl.BlockSpec((a, b_), lambda i: (0, 0))
    return pl.pallas_call(
        _router_kernel,
        grid=(n // tm,),
        in_specs=[row(D_MODEL), full(D_MODEL, LANES), full(D_MODEL, LANES), full(1, LANES), full(tm, tm)],
        out_specs=[row(LANES), row(LANES), row(LANES), full(1, LANES)],
        out_shape=[jax.ShapeDtypeStruct((n, LANES), jnp.int32), jax.ShapeDtypeStruct((n, LANES), F32),
                   jax.ShapeDtypeStruct((n, LANES), jnp.int32), jax.ShapeDtypeStruct((1, LANES), jnp.int32)],
        scratch_shapes=[pltpu.VMEM((1, LANES), F32)],
        compiler_params=_cparams(("arbitrary",)),
        name="router",
    )(h1, whi, wlo, bpad, tri)


def _ffn_kernel(bexp_ref, nused_ref, nvalid_ref, slot_ref, slot_prv_ref, src_cur_ref, src_nxt_ref, wg_ref, wu_ref,
                wdl_ref, wdh_ref,
                bg_ref, bu_ref, bdl_ref, bdh_ref, xp_ref, ys_ref, xbuf_ref, x_ref, act_ref, obuf_ref, wgb_ref, wub_ref,
                wdlb_ref, wdhb_ref, in_sem, out_sem):
    del bexp_ref
    i = pl.program_id(0)
    g = pl.program_id(1)
    n_up = act_ref.shape[0]
    last_g = pl.num_programs(1) - 1
    last_i = pl.num_programs(0) - 1
    n_used = nused_ref[0]
    used = i < n_used
    rb = x_ref.shape[0]
    sub = FFN_SUB
    th = wdlb_ref.shape[1]
    unroll = 8

    def sub_blocks(blk):
        return (nvalid_ref[blk] + (sub - 1)) // sub

    nsub = sub_blocks(i)

    def start_gather(src_ref, lo, hi):
        def body(j, carry):
            for u in range(unroll):
                jj = j * unroll + u
                src = pl.multiple_of(src_ref[jj], PACK_ROWS)
                dst = pl.multiple_of(jj * PACK_ROWS, PACK_ROWS)
                pltpu.make_async_copy(xp_ref.at[pl.ds(src, PACK_ROWS)], xbuf_ref.at[pl.ds(dst, PACK_ROWS)],
                                      in_sem).start(priority=ROW_DMA_PRIORITY)
            return carry

        lax.fori_loop(lo // unroll, hi // unroll, body, 0)

    def wait_gather(nrows):
        npk = pl.multiple_of(nrows * PACK_ROWS, PACK_ROWS)
        pltpu.make_async_copy(xp_ref.at[pl.ds(0, npk)], xbuf_ref.at[pl.ds(0, npk)], in_sem).wait()

    def start_scatter(slots_ref, lo, hi):
        def one(jj):
            src = pl.multiple_of(jj * PACK_ROWS, PACK_ROWS)
            dst = pl.multiple_of(slots_ref[jj] * PACK_ROWS, PACK_ROWS)
            pltpu.make_async_copy(obuf_ref.at[pl.ds(src, PACK_ROWS)], ys_ref.at[pl.ds(dst, PACK_ROWS)],
                                  out_sem).start(priority=ROW_DMA_PRIORITY)

        def body(j, carry):
            for u in range(unroll):
                one(j * unroll + u)
            return carry

        nfull = jnp.maximum(hi - lo, 0) // unroll
        lax.fori_loop(lo // unroll, lo // unroll + nfull, body, 0)
        for u in range(unroll - 1):
            @pl.when(lo + nfull * unroll + u < hi)
            def _():
                one(lo + nfull * unroll + u)

    def wait_scatter(blk):
        npk = pl.multiple_of(nvalid_ref[blk] * PACK_ROWS, PACK_ROWS)
        pltpu.make_async_copy(obuf_ref.at[pl.ds(0, npk)], ys_ref.at[pl.ds(0, npk)], out_sem).wait()

    @pl.when(jnp.logical_and(used, g == 0))
    def _():
        @pl.when(i == 0)
        def _():
            start_gather(src_cur_ref, 0, nsub * sub)

        wait_gather(nsub * sub)
        for j in range(rb // sub):
            @pl.when(j < nsub)
            def _():
                los, his = _load_packed_rows(xbuf_ref, j * sub, sub)
                rows = slice(j * sub, (j + 1) * sub)
                for s in range(PACK_ROWS):
                    x_ref[rows, s * LANES:(s + 1) * LANES] = los[s].astype(BF16)
                    x_ref[rows, D_MODEL // 2 + s * LANES:D_MODEL // 2 + (s + 1) * LANES] = his[s].astype(BF16)

    @pl.when(jnp.logical_and(jnp.logical_and(g >= 1, g - 1 < sub_blocks(jnp.minimum(i + 1, last_i))), i + 1 < n_used))
    def _():
        start_gather(src_nxt_ref, (g - 1) * sub, g * sub)

    @pl.when(jnp.logical_and(jnp.logical_and(used, i > 0), g + 1 < rb // sub))
    def _():
        start_scatter(slot_prv_ref, (g + 1) * sub, jnp.minimum((g + 2) * sub, nvalid_ref[i - 1]))

    @pl.when(jnp.logical_and(used, g < n_up))
    def _():
        wgb_ref[...] = wg_ref[0].astype(BF16)
        wub_ref[...] = wu_ref[0].astype(BF16)

        def sub_body(j, carry):
            rows = pl.ds(pl.multiple_of(j * sub, sub), sub)
            x = x_ref[rows, :]
            hg = _dot(x, wgb_ref[...]) + bg_ref[0]
            hu = _dot(x, wub_ref[...]) + bu_ref[0]
            gate = jnp.minimum(hg, SWIGLU_LIMIT)
            up = jnp.clip(hu, -SWIGLU_LIMIT, SWIGLU_LIMIT)
            act_ref[g, rows, :] = ((up + 1.0) * (gate * _sigmoid(SWIGLU_ALPHA * gate))).astype(BF16)
            return carry

        lax.fori_loop(0, nsub, sub_body, 0)

    @pl.when(jnp.logical_and(used, g >= n_up))
    def _():
        @pl.when(jnp.logical_and(g == n_up, i > 0))
        def _():
            wait_scatter(i - 1)

        wdlb_ref[...] = wdl_ref[0].astype(BF16)
        wdhb_ref[...] = wdh_ref[0].astype(BF16)
        tile0 = (g - n_up) * (th // LANES)

        def sub_body(j, carry):
            row0 = pl.multiple_of(j * sub, sub)
            a = jnp.concatenate([act_ref[c, pl.ds(row0, sub), :] for c in range(n_up)], axis=-1)
            ylo = _dot(a, wdlb_ref[...]) + bdl_ref[0]
            yhi = _dot(a, wdhb_ref[...]) + bdh_ref[0]
            for s in range(th // LANES):
                lo = ylo[:, s * LANES:(s + 1) * LANES].astype(BF16).astype(F32)
                hi = yhi[:, s * LANES:(s + 1) * LANES].astype(BF16).astype(F32)
                word = pltpu.bitcast(hi, jnp.int32) | lax.shift_right_logical(pltpu.bitcast(lo, jnp.int32), 16)
                obuf_ref[pl.ds(row0 * PACK_ROWS + tile0 + s, sub, stride=PACK_ROWS), :] = word
            return carry

        lax.fori_loop(0, nsub, sub_body, 0)

        @pl.when(g == last_g)
        def _():
            nv = nvalid_ref[i]
            start_scatter(slot_ref, 0, jnp.where(i + 1 < n_used, jnp.minimum(sub, nv), nv))

    @pl.when(jnp.logical_and(i == last_i, g == last_g))
    def _():
        wait_scatter(jnp.minimum(i, n_used - 1))


def _expert_ffn(h_packed, slot_assign, blk_exp, blk_valid, n_used, w_up, b_up, w_down, b_down):
    rb, tf, th = FFN_ROWS, FFN_TF, FFN_TH
    nblk = slot_assign.shape[0] // rb
    n_up = D_FF // tf
    n_dn = D_MODEL // 2 // th
    n_assign = h_packed.shape[0] // PACK_ROWS * TOP_K
    n_tok = h_packed.shape[0] // PACK_ROWS
    slot_src = jnp.maximum(slot_assign, 0) % n_tok * PACK_ROWS

    def clamp(i, nu):
        return jnp.minimum(i, nu[0] - 1)

    def up_sel(i, g, nu):
        return jnp.where(i < nu[0], jnp.minimum(g, n_up - 1), n_up - 1)

    def dn_sel(i, g, nu):
        return jnp.where(i < nu[0], jnp.clip(g - n_up, 0, n_dn - 1), n_dn - 1)

    def dn_exp(i, g, be, nu):
        early = jnp.logical_and(g < n_up - 1, i < nu[0])
        return be[jnp.where(early, jnp.maximum(clamp(i, nu) - 1, 0), clamp(i, nu))]

    def dn_col(i, g, nu):
        early = jnp.logical_and(g < n_up - 1, i < nu[0])
        return jnp.where(early, n_dn - 1, dn_sel(i, g, nu))

    grid_spec = pltpu.PrefetchScalarGridSpec(
        num_scalar_prefetch=3,
        grid=(nblk, n_up + n_dn),
        in_specs=[
            pl.BlockSpec((rb,), lambda i, g, be, nu, *_: (clamp(i, nu),), memory_space=pltpu.SMEM),
            pl.BlockSpec((rb,), lambda i, g, be, nu, *_: (jnp.maximum(clamp(i, nu) - 1, 0),), memory_space=pltpu.SMEM),
            pl.BlockSpec((rb,), lambda i, g, be, nu, *_: (clamp(i, nu),), memory_space=pltpu.SMEM),
            pl.BlockSpec((rb,), lambda i, g, be, nu, *_: (clamp(i + 1, nu),), memory_space=pltpu.SMEM),
            pl.BlockSpec((1, D_MODEL, tf), lambda i, g, be, nu, *_: (be[clamp(i, nu)], 0, up_sel(i, g, nu))),
            pl.BlockSpec((1, D_MODEL, tf), lambda i, g, be, nu, *_: (be[clamp(i, nu)], 0, n_up + up_sel(i, g, nu))),
            pl.BlockSpec((1, D_FF, th), lambda i, g, be, nu, *_: (dn_exp(i, g, be, nu), 0, dn_col(i, g, nu))),
            pl.BlockSpec((1, D_FF, th), lambda i, g, be, nu, *_: (dn_exp(i, g, be, nu), 0, n_dn + dn_col(i, g, nu))),
            pl.BlockSpec((1, 1, tf), lambda i, g, be, nu, *_: (be[clamp(i, nu)], 0, up_sel(i, g, nu))),
            pl.BlockSpec((1, 1, tf), lambda i, g, be, nu, *_: (be[clamp(i, nu)], 0, n_up + up_sel(i, g, nu))),
            pl.BlockSpec((1, 1, th), lambda i, g, be, nu, *_: (be[clamp(i, nu)], 0, dn_sel(i, g, nu))),
            pl.BlockSpec((1, 1, th), lambda i, g, be, nu, *_: (be[clamp(i, nu)], 0, n_dn + dn_sel(i, g, nu))),
            pl.BlockSpec(memory_space=pl.ANY),
        ],
        out_specs=pl.BlockSpec(memory_space=pl.ANY),
        scratch_shapes=[pltpu.VMEM((rb * PACK_ROWS, LANES), jnp.int32),
                        pltpu.VMEM((rb, D_MODEL), BF16),
                        pltpu.VMEM((n_up, rb, tf), BF16),
                        pltpu.VMEM((rb * PACK_ROWS, LANES), jnp.int32),
                        pltpu.VMEM((D_MODEL, tf), BF16), pltpu.VMEM((D_MODEL, tf), BF16),
                        pltpu.VMEM((D_FF, th), BF16), pltpu.VMEM((D_FF, th), BF16),
                        pltpu.SemaphoreType.DMA, pltpu.SemaphoreType.DMA],
    )
    assert n_up >= rb // FFN_SUB - 1 and n_up + n_dn > rb // FFN_SUB, "row DMA bursts are spread over the grid steps"
    return pl.pallas_call(
        _ffn_kernel,
        grid_spec=grid_spec,
        out_shape=jax.ShapeDtypeStruct((n_assign * PACK_ROWS, LANES), jnp.int32),
        compiler_params=pltpu.CompilerParams(dimension_semantics=("arbitrary", "arbitrary"),
                                             vmem_limit_bytes=FFN_VMEM_LIMIT, disable_bounds_checks=True),
        name="expert_ffn",
    )(blk_exp, n_used, blk_valid, slot_assign, slot_assign, slot_src, slot_src, w_up, w_up, w_down, w_down,
      b_up.reshape(N_EXPERTS, 1, 2 * D_FF), b_up.reshape(N_EXPERTS, 1, 2 * D_FF),
      b_down.reshape(N_EXPERTS, 1, D_MODEL), b_down.reshape(N_EXPERTS, 1, D_MODEL), h_packed)


def _combine_kernel(gate_ref, h_ref, g_ref, b_ref, ys_ref, o_ref):
    tm = h_ref.shape[0]
    gates = gate_ref[...]
    cols = [None] * (2 * PACK_ROWS)
    for k in range(TOP_K):
        gk = gates[:, k:k + 1]
        for s in range(PACK_ROWS):
            word = ys_ref[k, pl.ds(s, tm, stride=PACK_ROWS), :]
            lo = pltpu.bitcast(lax.shift_left(word, 16), F32)
            hi = pltpu.bitcast(word & jnp.int32(-65536), F32)
            for c, blk in ((s, lo), (PACK_ROWS + s, hi)):
                cols[c] = gk * blk if cols[c] is None else cols[c] + gk * blk
    z = DN_ALPHA * h_ref[...] + jnp.concatenate(cols, axis=-1)
    o_ref[...] = _layer_norm(z, g_ref[...], b_ref[...])


def _combine_ln(gates, h1, ys, g, b):
    n = h1.shape[0]
    tm = min(COMBINE_LN_TM, n)
    row = lambda w: pl.BlockSpec((tm, w), lambda i: (i, 0))
    full = lambda a, b_: pl.BlockSpec((a, b_), lambda i: (0, 0))
    return pl.pallas_call(
        _combine_kernel,
        grid=(n // tm,),
        in_specs=[row(LANES), row(D_MODEL), full(1, D_MODEL), full(1, D_MODEL),
                  pl.BlockSpec((TOP_K, tm * PACK_ROWS, LANES), lambda i: (0, i, 0))],
        out_specs=row(D_MODEL),
        out_shape=jax.ShapeDtypeStruct((n, D_MODEL), F32),
        compiler_params=_cparams(("parallel",)),
        name="combine_ln",
    )(gates, h1, g.reshape(1, -1), b.reshape(1, -1), ys.reshape(TOP_K, n * PACK_ROWS, LANES))


def _layer(h, w_in, b_igate, b_fgate, conv_w, conv_b, rel_bias, beta_attn, beta_mlstm, w_out,
           ln1_g, ln1_b, w_router, b_router, w_up, b_up, w_down, b_down, ln2_g, ln2_b, batch, seq):
    n = batch * seq
    w_qkv = w_in[:, :3 * D_ATTN].astype(BF16)
    w_mix = w_in[:, 3 * D_ATTN:MAIN_COLS].astype(BF16)
    w_gate = jnp.zeros((D_MODEL, LANES), BF16).at[:, :2 * N_HEADS_M].set(w_in[:, MAIN_COLS:].astype(BF16))
    qkvs = _qkv_project(h, w_qkv, min(QKV_TM, n))
    proj_m, gates = _project(h, w_mix, w_gate, min(PROJ_TM, n), PROJ_TN)
    gate_bias = jnp.zeros((1, LANES), F32).at[0, :2 * N_HEADS_M].set(jnp.concatenate([b_igate, b_fgate]))

    outs, lses = [], []
    for (_, dil), qkv in zip(DILATED_CONFIGS, qkvs):
        o, l = _dilated_attention(qkv, _attn_bias_tables(rel_bias, dil), batch, seq, dil)
        outs.append(o)
        lses.append(l)
    y_attn = _attn_combine(outs, lses, beta_attn)
    y_mlstm = _mlstm(proj_m, gates, gate_bias, conv_w, conv_b, beta_mlstm, batch, seq)

    h1, h1_packed = _outproj_ln(y_attn, y_mlstm, h, w_out.astype(BF16), ln1_g, ln1_b)

    top_idx, top_gate, rank, counts = _router(h1, w_router, b_router)
    counts = counts[0, :N_EXPERTS]
    nb = (counts + FFN_ROWS - 1) // FFN_ROWS
    per = (counts + nb * FFN_SUB - 1) // jnp.maximum(nb * FFN_SUB, 1) * FFN_SUB
    blk_end = jnp.cumsum(nb)
    blk_start = blk_end - nb
    e_idx, r_idx = top_idx[:, :TOP_K], rank[:, :TOP_K]
    onehot = (e_idx.reshape(-1, 1) == jnp.arange(N_EXPERTS, dtype=jnp.int32)[None, :]).astype(F32)
    table = jnp.stack([jnp.maximum(per, 1), blk_start], axis=1).astype(F32)
    looked = jnp.dot(onehot, table, precision=lax.Precision.HIGHEST)
    per_a, start_a = looked[:, 0], looked[:, 1]
    r_f = r_idx.reshape(-1).astype(F32)
    blk_in_e = jnp.floor((r_f + 0.5) / per_a)
    dest = ((start_a + blk_in_e) * FFN_ROWS + (r_f - blk_in_e * per_a)).astype(jnp.int32)
    nblk = n * TOP_K // FFN_ROWS + N_EXPERTS
    blk_id = jnp.arange(nblk, dtype=jnp.int32)
    blk_exp = jnp.minimum(jnp.sum(blk_end[None, :] <= blk_id[:, None], axis=1), N_EXPERTS - 1).astype(jnp.int32)
    n_used = blk_end[-1:].astype(jnp.int32)
    blk_valid = jnp.clip(counts[blk_exp] - (blk_id - blk_start[blk_exp]) * per[blk_exp], 0,
                         per[blk_exp]).astype(jnp.int32)
    a_ids = jnp.arange(n * TOP_K, dtype=jnp.int32)
    slot_assign = jnp.full((nblk * FFN_ROWS,), -1, jnp.int32).at[dest].set(
        (a_ids % TOP_K) * n + a_ids // TOP_K, unique_indices=True)

    ys = _expert_ffn(h1_packed, slot_assign, blk_exp, blk_valid, n_used, w_up, b_up, w_down, b_down)
    return _combine_ln(top_gate, h1, ys, ln2_g, ln2_b)


def kernel(x, w_in, b_igate, b_fgate, conv_w, conv_b, rel_bias, beta_attn, beta_mlstm, w_out, ln1_g, ln1_b,
           w_router, b_router, w_up, b_up, w_down, b_down, ln2_g, ln2_b):
    batch, seq, d = x.shape
    h = x.reshape(batch * seq, d)
    for l in range(DEPTH):
        h = _layer(h, w_in[l], b_igate[l], b_fgate[l], conv_w[l], conv_b[l], rel_bias, beta_attn[l], beta_mlstm[l],
                   w_out[l], ln1_g[l], ln1_b[l], w_router[l], b_router[l], w_up[l], b_up[l], w_down[l], b_down[l],
                   ln2_g[l], ln2_b[l], batch, seq)
    return h.reshape(batch, seq, d)
```

```python
import functools
import math

import numpy as np
import jax
import jax.numpy as jnp
from jax import lax
from jax.experimental import pallas as pl
from jax.experimental.pallas import tpu as pltpu

F32 = jnp.float32
BF16 = jnp.bfloat16

D_MODEL = 2048
D_ATTN = 1024
HEAD_DIM_A = 64
N_HEADS_A = 16
DILATED_CONFIGS = ((128, 1), (512, 4), (2048, 16))
ATTN_BLOCK = 128
NUM_BUCKETS = 32
MAX_DISTANCE = 2048
D_MLSTM = 1024
N_HEADS_M = 4
HEAD_DIM_M = 256
CONV_K = 4
MLSTM_CHUNK = 128
MAIN_COLS = 3 * D_ATTN + 4 * D_MLSTM
N_EXPERTS = 32
TOP_K = 4
D_FF = 2048
SWIGLU_LIMIT = 7.0
SWIGLU_ALPHA = 1.702
DEPTH = 1
DN_ALPHA = (2 * DEPTH) ** 0.25
LN_EPS = 1e-5
HEAD_NORM_EPS = 1e-6
NEG_INF = -1e30

LANES = 128
VMEM_LIMIT = 48 * 1024 * 1024

PROJ_TM = 1024
PROJ_TN = 1024
COMBINE_TM = 512
OUTPROJ_TM = 512
ROUTER_TM = 512
QKV_TM = 512
ATTN_GROUP = 2
FFN_ROWS = 1024
FFN_SUB = 256
FFN_TF = 512
FFN_TH = 512
FFN_VMEM_LIMIT = 60 * 1024 * 1024
ROW_DMA_PRIORITY = 1
COMBINE_LN_TM = 256


def _cparams(sem, vmem=VMEM_LIMIT):
    return pltpu.CompilerParams(dimension_semantics=sem, vmem_limit_bytes=vmem)


def _dot(a, b):
    return jnp.dot(a, b, preferred_element_type=F32)


def _dot_f32_rhs(a_bf16, b_f32):
    hi = b_f32.astype(BF16)
    lo = (b_f32 - hi.astype(F32)).astype(BF16)
    return _dot(a_bf16, hi) + _dot(a_bf16, lo)


def _dot_f32_lhs(a_f32, b_bf16):
    hi = a_f32.astype(BF16)
    lo = (a_f32 - hi.astype(F32)).astype(BF16)
    return _dot(hi, b_bf16) + _dot(lo, b_bf16)


def _sigmoid(x):
    return 1.0 / (1.0 + jnp.exp(-x))


def _log_sigmoid(x):
    return jnp.minimum(x, 0.0) - jnp.log(1.0 + jnp.exp(-jnp.abs(x)))


def _proj_kernel(x_ref, w_ref, wg_ref, o_ref, g_ref):
    x = x_ref[...].astype(BF16)
    o_ref[...] = _dot(x, w_ref[...]).astype(o_ref.dtype)

    @pl.when(pl.program_id(1) == 0)
    def _():
        g_ref[...] = _dot(x, wg_ref[...])


def _project(x, w, w_gate, tm, tn):
    m, k = x.shape
    n = w.shape[1]
    return pl.pallas_call(
        _proj_kernel,
        grid=(m // tm, n // tn),
        in_specs=[pl.BlockSpec((tm, k), lambda i, j: (i, 0)),
                  pl.BlockSpec((k, tn), lambda i, j: (0, j)),
                  pl.BlockSpec((k, LANES), lambda i, j: (0, 0))],
        out_specs=[pl.BlockSpec((tm, tn), lambda i, j: (i, j)),
                   pl.BlockSpec((tm, LANES), lambda i, j: (i, 0))],
        out_shape=[jax.ShapeDtypeStruct((m, n), BF16), jax.ShapeDtypeStruct((m, LANES), F32)],
        compiler_params=_cparams(("parallel", "arbitrary")),
        name="in_proj",
    )(x, w, w_gate)


def _qkv_proj_kernel(x_ref, w_ref, *refs):
    n_out = len(DILATED_CONFIGS)
    o_refs, s_refs = refs[:n_out], refs[n_out:]
    res = _dot(x_ref[...].astype(BF16), w_ref[...])
    ntile, _, tm, _ = s_refs[0].shape
    wid = ntile * LANES
    for c in range(ntile):
        s_refs[0][c, 0] = res[:, c * LANES:(c + 1) * LANES]
    prev_dil = 1
    for level, ((_, dil), o_ref) in enumerate(zip(DILATED_CONFIGS, o_refs)):
        if dil == 1:
            o_ref[...] = res.astype(o_ref.dtype)
            continue
        step = dil // prev_dil
        rows = tm // dil
        keep = level + 1 < n_out
        for rp in range(prev_dil):
            for q in range(step):
                r = rp + prev_dil * q
                for c in range(ntile):
                    piece = s_refs[level - 1][c, rp, pl.ds(q, rows, stride=step), :]
                    col = r * wid + c * LANES
                    o_ref[:, col:col + LANES] = piece.astype(o_ref.dtype)
                    if keep:
                        s_refs[level][c, r] = piece
        prev_dil = dil


def _qkv_project(x, w, tm):
    m, k = x.shape
    wid = w.shape[1]
    dils = [d for _, d in DILATED_CONFIGS]
    return pl.pallas_call(
        _qkv_proj_kernel,
        grid=(m // tm,),
        in_specs=[pl.BlockSpec((tm, k), lambda i: (i, 0)),
                  pl.BlockSpec((k, wid), lambda i: (0, 0), pipeline_mode=pl.Buffered(1))],
        out_specs=[pl.BlockSpec((tm // d, d * wid), lambda i: (i, 0)) for d in dils],
        out_shape=[jax.ShapeDtypeStruct((m // d, d * wid), BF16) for d in dils],
        scratch_shapes=[pltpu.VMEM((wid // LANES, d, tm // d, LANES), F32) for d in dils[:-1]],
        compiler_params=_cparams(("parallel",), 56 * 1024 * 1024),
        name="qkv_proj",
    )(x, w)


def _attn_kernel(q_ref, kp_ref, kc_ref, vp_ref, vc_ref, bias_ref, sel_ref, o_ref, lse_ref):
    n = pl.program_id(2)
    tab = jnp.minimum(n, 1)
    grp, dh, nk = ATTN_GROUP, HEAD_DIM_A, 2 * ATTN_BLOCK
    wid = grp * dh
    lane_head = lax.broadcasted_iota(jnp.int32, (nk, wid), 1) // dh
    zero = jnp.zeros((nk, wid), BF16)
    ones_bd = jnp.concatenate([jnp.where(lane_head == j, 1.0, 0.0).astype(BF16) for j in range(grp)], axis=0)
    q_lane_head = lax.broadcasted_iota(jnp.int32, (ATTN_BLOCK, wid), 1) // dh
    lse_wide = []
    for g in range(N_HEADS_A // grp):
        cols = slice(g * wid, (g + 1) * wid)
        q = q_ref[0, :, cols] * (dh ** -0.5)
        kslab = jnp.concatenate([kp_ref[0, :, cols], kc_ref[0, :, cols]], axis=0)
        vslab = jnp.concatenate([vp_ref[0, :, cols], vc_ref[0, :, cols]], axis=0)
        k_bd = jnp.concatenate([jnp.where(lane_head == j, kslab, zero) for j in range(grp)], axis=0)
        v_bd = jnp.concatenate([jnp.where(lane_head == j, vslab, zero) for j in range(grp)], axis=0)
        s_all = lax.dot_general(q, k_bd, (((1,), (1,)), ((), ())), preferred_element_type=F32)
        ps, ms = [], []
        for j in range(grp):
            s = s_all[:, j * nk:(j + 1) * nk] + bias_ref[tab, g * grp + j]
            m = jnp.max(s, axis=-1, keepdims=True)
            ps.append(jnp.exp(s - m).astype(BF16))
            ms.append(m)
        p_all = jnp.concatenate(ps, axis=-1)
        res = _dot(p_all, jnp.concatenate([v_bd, ones_bd], axis=-1))
        den = res[:, wid:]
        o_ref[0, :, cols] = (res[:, :wid] / den).astype(o_ref.dtype)
        m_wide = ms[grp - 1]
        for j in range(grp - 1):
            m_wide = jnp.where(q_lane_head == j, ms[j], m_wide)
        lse_wide.append(m_wide + jnp.log(den))
    lse_ref[0] = _dot_f32_lhs(jnp.concatenate(lse_wide, axis=-1), sel_ref[...])


def _attn_bias_tables(rel_bias, dil):
    blk = ATTN_BLOCK
    period = 3 * blk
    k = np.arange(period)
    valid = k <= blk
    dist = np.where(valid, blk - k, 0) * dil
    max_exact = NUM_BUCKETS // 2
    d_f = np.maximum(dist, 1).astype(np.float32)
    large = max_exact + (np.log(d_f / np.float32(max_exact)) / np.float32(math.log(MAX_DISTANCE / max_exact))
                         * np.float32(NUM_BUCKETS - max_exact)).astype(np.int32)
    large = np.minimum(large, NUM_BUCKETS - 1)
    bucket = np.where(dist < max_exact, dist, large).astype(np.int32)
    w = jnp.where(jnp.asarray(valid)[None, :], rel_bias[jnp.asarray(bucket)].T.astype(F32), NEG_INF)
    t1 = jnp.tile(w, (1, blk))[:, :blk * (period - 1)].reshape(N_HEADS_A, blk, period - 1)[:, :, :2 * blk]
    has_prev = np.arange(2 * blk)[None, None, :] >= blk
    t0 = jnp.where(jnp.asarray(has_prev), t1, NEG_INF)
    return jnp.stack([t0, t1])


def _dilated_attention(qkv, bias_tab, batch, seq, dil):
    blk = ATTN_BLOCK
    l = seq // dil
    nb = l // blk
    ncb = 3
    pv = qkv.reshape(batch, l, dil * ncb * D_ATTN)
    head_sel = jnp.asarray((np.arange(D_ATTN)[:, None] == np.arange(LANES)[None, :] * HEAD_DIM_A), BF16)

    def cur(c):
        return pl.BlockSpec((1, blk, D_ATTN), lambda b, r, n: (b, n, r * ncb + c))

    def prev(c):
        return pl.BlockSpec((1, blk, D_ATTN), lambda b, r, n: (b, jnp.maximum(n - 1, 0), r * ncb + c))

    o, lse = pl.pallas_call(
        _attn_kernel,
        grid=(batch, dil, nb),
        in_specs=[cur(0), prev(1), cur(1), prev(2), cur(2),
                  pl.BlockSpec((2, N_HEADS_A, blk, 2 * blk), lambda b, r, n: (0, 0, 0, 0)),
                  pl.BlockSpec((D_ATTN, LANES), lambda b, r, n: (0, 0))],
        out_specs=[pl.BlockSpec((1, blk, D_ATTN), lambda b, r, n: (b, n, r)),
                   pl.BlockSpec((1, blk, LANES), lambda b, r, n: (b, n, r))],
        out_shape=[jax.ShapeDtypeStruct((batch, l, dil * D_ATTN), BF16),
                   jax.ShapeDtypeStruct((batch, l, dil * LANES), F32)],
        compiler_params=_cparams(("parallel", "parallel", "arbitrary")),
        name=f"dilated_attn_d{dil}",
    )(pv, pv, pv, pv, pv, bias_tab, head_sel)
    return o.reshape(batch * l, dil * D_ATTN), lse.reshape(batch * l, dil * LANES)


def _attn_combine_kernel(o1_ref, o2_ref, o3_ref, l1_ref, l2_ref, l3_ref, e_ref, et_ref, beta_ref, y_ref,
                         osc_ref, lsc_ref):
    tm = y_ref.shape[0]
    ntile = D_ATTN // LANES
    lses, outs = [], []
    for slot, ((_, dil), o_ref, l_ref) in enumerate(zip(DILATED_CONFIGS, (o1_ref, o2_ref, o3_ref),
                                                        (l1_ref, l2_ref, l3_ref))):
        if dil == 1:
            lses.append(l_ref[...])
            outs.append(lambda o_ref=o_ref: o_ref[...].astype(F32))
            continue
        rows = tm // dil
        for r in range(dil):
            lsc_ref[slot, pl.ds(r, rows, stride=dil), :] = l_ref[:, r * LANES:(r + 1) * LANES]
            for c in range(ntile):
                col = r * D_ATTN + c * LANES
                osc_ref[slot, c, pl.ds(r, rows, stride=dil), :] = o_ref[:, col:col + LANES].astype(F32)
        lses.append(lsc_ref[slot])
        outs.append(lambda slot=slot: jnp.concatenate([osc_ref[slot, c] for c in range(ntile)], axis=-1))
    mx = jnp.maximum(jnp.maximum(lses[0], lses[1]), lses[2])
    ws = [jnp.exp(l - mx) for l in lses]
    tot = ws[0] + ws[1] + ws[2]
    e = e_ref[...]
    acc = None
    for w, load_o in zip(ws, outs):
        term = _dot_f32_lhs(w / tot, e) * load_o()
        acc = term if acc is None else acc + term
    ss = _dot_f32_lhs(acc * acc, et_ref[...])
    inv = lax.rsqrt(ss * (1.0 / HEAD_DIM_A) + HEAD_NORM_EPS)
    y_ref[...] = (acc * _dot_f32_lhs(inv, e) * beta_ref[...]).astype(y_ref.dtype)


def _attn_combine(os_, lses, beta_attn):
    n = os_[0].shape[0]
    tm = min(COMBINE_TM, n)
    head_of_lane = np.arange(D_ATTN) // HEAD_DIM_A
    e = (np.arange(LANES)[:, None] == head_of_lane[None, :]).astype(np.float32)
    e_j = jnp.asarray(e, BF16)
    et_j = jnp.asarray(e.T, BF16)
    dils = [d for _, d in DILATED_CONFIGS]
    dilated = lambda w: [pl.BlockSpec((tm // d, d * w), lambda i: (i, 0)) for d in dils]
    full = lambda a, b: pl.BlockSpec((a, b), lambda i: (0, 0))
    return pl.pallas_call(
        _attn_combine_kernel,
        grid=(n // tm,),
        in_specs=dilated(D_ATTN) + dilated(LANES) + [full(LANES, D_ATTN), full(D_ATTN, LANES), full(1, D_ATTN)],
        out_specs=pl.BlockSpec((tm, D_ATTN), lambda i: (i, 0)),
        out_shape=jax.ShapeDtypeStruct((n, D_ATTN), BF16),
        scratch_shapes=[pltpu.VMEM((len(dils), D_ATTN // LANES, tm, LANES), F32),
                        pltpu.VMEM((len(dils), tm, LANES), F32)],
        compiler_params=_cparams(("parallel",)),
        name="attn_combine",
    )(*os_, *lses, e_j, et_j, beta_attn.reshape(1, D_ATTN).astype(F32))


def _mlstm_kernel(qp_ref, kp_ref, qprev_ref, kprev_ref, v_ref, og_ref, g_ref, gb_ref, cw_ref, cb_ref,
                  beta_ref, y_ref, c_ref, n_ref, m_ref):
    step = pl.program_id(1)
    ch = MLSTM_CHUNK
    dh = HEAD_DIM_M

    @pl.when(step == 0)
    def _():
        c_ref[...] = jnp.zeros(c_ref.shape, F32)
        n_ref[...] = jnp.zeros(n_ref.shape, F32)
        m_ref[...] = jnp.zeros(m_ref.shape, F32)

    def conv_silu(x_ref, prev_ref, coff, b):
        x = x_ref[b].astype(F32)
        p = jnp.where(step > 0, prev_ref[b].astype(F32), 0.0)
        xe = jnp.concatenate([p, x], axis=0)
        npad = p.shape[0]
        cols = slice(coff, coff + D_MLSTM)
        acc = cb_ref[:, cols] + cw_ref[CONV_K - 1:CONV_K, cols] * x
        for s in range(1, CONV_K):
            shifted = pltpu.roll(xe, s, 0)[npad:]
            acc = acc + cw_ref[CONV_K - 1 - s:CONV_K - s, cols] * shifted
        return acc * _sigmoid(acc)

    row_i = lax.broadcasted_iota(jnp.int32, (ch, ch), 0)
    col_i = lax.broadcasted_iota(jnp.int32, (ch, ch), 1)
    causal = row_i >= col_i
    tri = jnp.where(causal, 1.0, 0.0).astype(BF16)
    upp = jnp.where(row_i <= col_i, 1.0, 0.0).astype(BF16)
    b = 0
    qf = conv_silu(qp_ref, qprev_ref, 0, b)
    kf = conv_silu(kp_ref, kprev_ref, D_MLSTM, b) * (dh ** -0.5)
    qb = qf.astype(BF16)
    kb = kf.astype(BF16)
    g = g_ref[b] + gb_ref[...]
    gt = g.T
    b_cols = _dot_f32_rhs(tri, _log_sigmoid(g))
    b_rows = _dot_f32_lhs(_log_sigmoid(gt), upp)

    for h in range(N_HEADS_M):
        st = h
        hs = slice(h * dh, (h + 1) * dh)
        fi = N_HEADS_M + h
        i_row = gt[h:h + 1, :]
        i_col = g[:, h:h + 1]
        b_row = b_rows[fi:fi + 1, :]
        b_col = b_cols[:, fi:fi + 1]
        m_prev = m_ref[st][:, 0:1]
        q_h, k_h = qb[:, hs], kb[:, hs]
        v_h = v_ref[b, :, hs]

        dmat = jnp.where(causal, b_col - b_row + i_row, NEG_INF)
        m_inter = b_col + m_prev
        m_t = jnp.maximum(m_inter, jnp.max(dmat, axis=-1, keepdims=True))
        w = jnp.exp(dmat - m_t) * lax.dot_general(q_h, k_h, (((1,), (1,)), ((), ())),
                                                  preferred_element_type=F32)
        decay = jnp.exp(m_inter - m_t)
        c_old = c_ref[st]
        inter = lax.dot_general(q_h, c_old.astype(BF16), (((1,), (1,)), ((), ())), preferred_element_type=F32)
        num = _dot(w.astype(BF16), v_h) + decay * inter
        n_old = n_ref[st]
        den = jnp.sum(w, axis=-1, keepdims=True) + decay * jnp.sum(qf[:, hs] * n_old, axis=-1, keepdims=True)
        hh = num / jnp.maximum(jnp.abs(den), jnp.exp(-m_t))

        g_last = b_col[ch - 1:ch, :]
        a_row = g_last - b_row + i_row
        a_col = g_last - b_col + i_col
        m_new = jnp.maximum(g_last + m_prev, jnp.max(a_row, axis=-1, keepdims=True))
        carry = jnp.exp(g_last + m_prev - m_new)
        wa_col = jnp.exp(a_col - m_new)
        wv = (wa_col * v_h.astype(F32)).astype(BF16)
        c_ref[st] = carry * c_old + lax.dot_general(wv, k_h, (((0,), (0,)), ((), ())), preferred_element_type=F32)
        n_ref[st] = carry * n_old + jnp.sum(wa_col * kf[:, hs], axis=0, keepdims=True)
        m_ref[st] = jnp.broadcast_to(m_new, (1, LANES))

        gated = _sigmoid(og_ref[b, :, hs].astype(F32)) * hh
        ms = jnp.sum(gated * gated, axis=-1, keepdims=True) * (1.0 / dh)
        y_ref[b, :, hs] = (gated * lax.rsqrt(ms + HEAD_NORM_EPS) * beta_ref[:, hs]).astype(y_ref.dtype)


def _mlstm(proj, gates, gate_bias, conv_w, conv_b, beta_mlstm, batch, seq):
    ch = MLSTM_CHUNK
    nchunk = seq // ch
    pv = proj.reshape(batch, seq, 4 * D_MLSTM)
    gv = gates.reshape(batch, seq, LANES)
    prev_rows = 16
    cb0 = 0

    def cur(c):
        return pl.BlockSpec((1, ch, D_MLSTM), lambda b, n: (b, n, c))

    def prev(c):
        per = ch // prev_rows
        return pl.BlockSpec((1, prev_rows, D_MLSTM), lambda b, n: (b, jnp.maximum(n * per - 1, 0), c))

    const = lambda a, b_: pl.BlockSpec((a, b_), lambda b, n: (0, 0))
    y = pl.pallas_call(
        _mlstm_kernel,
        grid=(batch, nchunk),
        in_specs=[cur(cb0), cur(cb0 + 1), prev(cb0), prev(cb0 + 1), cur(cb0 + 2), cur(cb0 + 3),
                  pl.BlockSpec((1, ch, LANES), lambda b, n: (b, n, 0)),
                  const(1, LANES), const(CONV_K, 2 * D_MLSTM), const(1, 2 * D_MLSTM), const(1, D_MLSTM)],
        out_specs=pl.BlockSpec((1, ch, D_MLSTM), lambda b, n: (b, n, 0)),
        out_shape=jax.ShapeDtypeStruct((batch, seq, D_MLSTM), BF16),
        scratch_shapes=[pltpu.VMEM((N_HEADS_M, HEAD_DIM_M, HEAD_DIM_M), F32),
                        pltpu.VMEM((N_HEADS_M, 1, HEAD_DIM_M), F32),
                        pltpu.VMEM((N_HEADS_M, 1, LANES), F32)],
        compiler_params=_cparams(("parallel", "arbitrary")),
        name="mlstm",
    )(pv, pv, pv, pv, pv, pv, gv, gate_bias, conv_w.astype(F32), conv_b.reshape(1, -1).astype(F32),
      beta_mlstm.reshape(1, D_MLSTM).astype(F32))
    return y.reshape(batch * seq, D_MLSTM)


def _layer_norm(z, g, b):
    mu = jnp.mean(z, axis=-1, keepdims=True)
    zc = z - mu
    var = jnp.mean(zc * zc, axis=-1, keepdims=True)
    return zc * lax.rsqrt(var + LN_EPS) * g + b


PACK_ROWS = D_MODEL // (2 * LANES)


def _store_packed_rows(dst_ref, x, first=0):
    rows = x.shape[0]
    half = D_MODEL // 2
    for s in range(PACK_ROWS):
        lo = x[:, s * LANES:(s + 1) * LANES].astype(BF16).astype(F32)
        hi = x[:, half + s * LANES:half + (s + 1) * LANES].astype(BF16).astype(F32)
        word = pltpu.bitcast(hi, jnp.int32) | lax.shift_right_logical(pltpu.bitcast(lo, jnp.int32), 16)
        dst_ref[pl.ds(first * PACK_ROWS + s, rows, stride=PACK_ROWS), :] = word


def _load_packed_rows(src_ref, first, rows):
    los, his = [], []
    for s in range(PACK_ROWS):
        word = src_ref[pl.ds(first * PACK_ROWS + s, rows, stride=PACK_ROWS), :]
        los.append(pltpu.bitcast(lax.shift_left(word, 16), F32))
        his.append(pltpu.bitcast(word & jnp.int32(-65536), F32))
    return los, his


def _outproj_kernel(ya_ref, ym_ref, x_ref, w_ref, g_ref, b_ref, h_ref, hp_ref):
    y = _dot(ya_ref[...], w_ref[0:D_ATTN, :]) + _dot(ym_ref[...], w_ref[D_ATTN:D_MODEL, :])
    h = _layer_norm(DN_ALPHA * x_ref[...] + y, g_ref[...], b_ref[...])
    h_ref[...] = h
    _store_packed_rows(hp_ref, h)


def _outproj_ln(ya, ym, x, w_out, g, b):
    n = x.shape[0]
    tm = min(OUTPROJ_TM, n)
    row = lambda w: pl.BlockSpec((tm, w), lambda i: (i, 0))
    full = lambda a, b_: pl.BlockSpec((a, b_), lambda i: (0, 0))
    return pl.pallas_call(
        _outproj_kernel,
        grid=(n // tm,),
        in_specs=[row(D_ATTN), row(D_MLSTM), row(D_MODEL), full(D_MODEL, D_MODEL), full(1, D_MODEL), full(1, D_MODEL)],
        out_specs=[row(D_MODEL), pl.BlockSpec((tm * PACK_ROWS, LANES), lambda i: (i, 0))],
        out_shape=[jax.ShapeDtypeStruct((n, D_MODEL), F32),
                   jax.ShapeDtypeStruct((n * PACK_ROWS, LANES), jnp.int32)],
        compiler_params=_cparams(("parallel",)),
        name="out_proj_ln",
    )(ya, ym, x, w_out, g.reshape(1, -1), b.reshape(1, -1))


def _router_kernel(h_ref, whi_ref, wlo_ref, b_ref, tri_ref, idx_ref, gate_ref, rank_ref, cnt_ref, carry_ref):
    i = pl.program_id(0)

    @pl.when(i == 0)
    def _():
        carry_ref[...] = jnp.zeros(carry_ref.shape, F32)

    x = h_ref[...]
    xhi = x.astype(BF16)
    xlo = (x - xhi.astype(F32)).astype(BF16)
    logits = _dot(xhi, whi_ref[...]) + _dot(xhi, wlo_ref[...]) + _dot(xlo, whi_ref[...]) + b_ref[...]
    tm = logits.shape[0]
    lane = lax.broadcasted_iota(jnp.int32, (tm, LANES), 1)
    lane_f = lane.astype(F32)
    vals = jnp.where(lane < N_EXPERTS, logits, NEG_INF)

    sels, tops = [], []
    for _ in range(TOP_K):
        mx = jnp.max(vals, axis=-1, keepdims=True)
        first = jnp.min(jnp.where(vals == mx, lane_f, float(LANES)), axis=-1, keepdims=True)
        sel = lane_f == first
        sels.append(sel)
        tops.append((mx, first))
        vals = jnp.where(sel, 2.0 * NEG_INF, vals)

    exps = [jnp.exp(mx - tops[0][0]) for mx, _ in tops]
    tot = exps[0] + exps[1] + exps[2] + exps[3]

    onehot = jnp.zeros((tm, LANES), F32)
    for sel in sels:
        onehot = jnp.where(sel, 1.0, onehot)
    before = _dot(tri_ref[...], onehot.astype(BF16)) + carry_ref[...]

    idx_out = jnp.zeros((tm, LANES), F32)
    gate_out = jnp.zeros((tm, LANES), F32)
    rank_out = jnp.zeros((tm, LANES), F32)
    for k in range(TOP_K):
        rank_k = jnp.sum(jnp.where(sels[k], before, 0.0), axis=-1, keepdims=True)
        idx_out = jnp.where(lane == k, tops[k][1], idx_out)
        gate_out = jnp.where(lane == k, exps[k] / tot, gate_out)
        rank_out = jnp.where(lane == k, rank_k, rank_out)
    idx_ref[...] = idx_out.astype(jnp.int32)
    gate_ref[...] = gate_out
    rank_ref[...] = rank_out.astype(jnp.int32)

    carry = carry_ref[...] + jnp.sum(onehot, axis=0, keepdims=True)
    carry_ref[...] = carry
    cnt_ref[...] = carry.astype(jnp.int32)


def _router(h1, w_router, b_router):
    n = h1.shape[0]
    tm = min(ROUTER_TM, n)
    wpad = jnp.zeros((D_MODEL, LANES), F32).at[:, :N_EXPERTS].set(w_router)
    whi = wpad.astype(BF16)
    wlo = (wpad - whi.astype(F32)).astype(BF16)
    bpad = jnp.zeros((1, LANES), F32).at[0, :N_EXPERTS].set(b_router)
    tri = jnp.asarray(np.tril(np.ones((tm, tm), np.float32), -1), BF16)
    row = lambda w: pl.BlockSpec((tm, w), lambda i: (i, 0))
    full = lambda a, b_: pl.BlockSpec((a, b_), lambda i: (0, 0))
    return pl.pallas_call(
        _router_kernel,
        grid=(n // tm,),
        in_specs=[row(D_MODEL), full(D_MODEL, LANES), full(D_MODEL, LANES), full(1, LANES), full(tm, tm)],
        out_specs=[row(LANES), row(LANES), row(LANES), full(1, LANES)],
        out_shape=[jax.ShapeDtypeStruct((n, LANES), jnp.int32), jax.ShapeDtypeStruct((n, LANES), F32),
                   jax.ShapeDtypeStruct((n, LANES), jnp.int32), jax.ShapeDtypeStruct((1, LANES), jnp.int32)],
        scratch_shapes=[pltpu.VMEM((1, LANES), F32)],
        compiler_params=_cparams(("arbitrary",)),
        name="router",
    )(h1, whi, wlo, bpad, tri)


def _ffn_kernel(bexp_ref, nused_ref, nvalid_ref, slot_ref, slot_prv_ref, src_cur_ref, src_nxt_ref, wg_ref, wu_ref,
                wdl_ref, wdh_ref,
                bg_ref, bu_ref, bdl_ref, bdh_ref, xp_ref, ys_ref, xbuf_ref, x_ref, act_ref, obuf_ref, wgb_ref, wub_ref,
                wdlb_ref, wdhb_ref, in_sem, out_sem):
    del bexp_ref
    i = pl.program_id(0)
    g = pl.program_id(1)
    n_up = act_ref.shape[0]
    last_g = pl.num_programs(1) - 1
    last_i = pl.num_programs(0) - 1
    n_used = nused_ref[0]
    used = i < n_used
    rb = x_ref.shape[0]
    sub = FFN_SUB
    th = wdlb_ref.shape[1]
    unroll = 8

    def sub_blocks(blk):
        return (nvalid_ref[blk] + (sub - 1)) // sub

    nsub = sub_blocks(i)

    def start_gather(src_ref, lo, hi):
        def body(j, carry):
            for u in range(unroll):
                jj = j * unroll + u
                src = pl.multiple_of(src_ref[jj], PACK_ROWS)
                dst = pl.multiple_of(jj * PACK_ROWS, PACK_ROWS)
                pltpu.make_async_copy(xp_ref.at[pl.ds(src, PACK_ROWS)], xbuf_ref.at[pl.ds(dst, PACK_ROWS)],
                                      in_sem).start(priority=ROW_DMA_PRIORITY)
            return carry

        lax.fori_loop(lo // unroll, hi // unroll, body, 0)

    def wait_gather(nrows):
        npk = pl.multiple_of(nrows * PACK_ROWS, PACK_ROWS)
        pltpu.make_async_copy(xp_ref.at[pl.ds(0, npk)], xbuf_ref.at[pl.ds(0, npk)], in_sem).wait()

    def start_scatter(slots_ref, lo, hi):
        def one(jj):
            src = pl.multiple_of(jj * PACK_ROWS, PACK_ROWS)
            dst = pl.multiple_of(slots_ref[jj] * PACK_ROWS, PACK_ROWS)
            pltpu.make_async_copy(obuf_ref.at[pl.ds(src, PACK_ROWS)], ys_ref.at[pl.ds(dst, PACK_ROWS)],
                                  out_sem).start(priority=ROW_DMA_PRIORITY)

        def body(j, carry):
            for u in range(unroll):
                one(j * unroll + u)
            return carry

        nfull = jnp.maximum(hi - lo, 0) // unroll
        lax.fori_loop(lo // unroll, lo // unroll + nfull, body, 0)
        for u in range(unroll - 1):
            @pl.when(lo + nfull * unroll + u < hi)
            def _():
                one(lo + nfull * unroll + u)

    def wait_scatter(blk):
        npk = pl.multiple_of(nvalid_ref[blk] * PACK_ROWS, PACK_ROWS)
        pltpu.make_async_copy(obuf_ref.at[pl.ds(0, npk)], ys_ref.at[pl.ds(0, npk)], out_sem).wait()

    @pl.when(jnp.logical_and(used, g == 0))
    def _():
        @pl.when(i == 0)
        def _():
            start_gather(src_cur_ref, 0, nsub * sub)

        wait_gather(nsub * sub)
        for j in range(rb // sub):
            @pl.when(j < nsub)
            def _():
                los, his = _load_packed_rows(xbuf_ref, j * sub, sub)
                rows = slice(j * sub, (j + 1) * sub)
                for s in range(PACK_ROWS):
                    x_ref[rows, s * LANES:(s + 1) * LANES] = los[s].astype(BF16)
                    x_ref[rows, D_MODEL // 2 + s * LANES:D_MODEL // 2 + (s + 1) * LANES] = his[s].astype(BF16)

    @pl.when(jnp.logical_and(jnp.logical_and(g >= 1, g - 1 < sub_blocks(jnp.minimum(i + 1, last_i))), i + 1 < n_used))
    def _():
        start_gather(src_nxt_ref, (g - 1) * sub, g * sub)

    @pl.when(jnp.logical_and(jnp.logical_and(used, i > 0), g + 1 < rb // sub))
    def _():
        start_scatter(slot_prv_ref, (g + 1) * sub, jnp.minimum((g + 2) * sub, nvalid_ref[i - 1]))

    @pl.when(jnp.logical_and(used, g < n_up))
    def _():
        wgb_ref[...] = wg_ref[0].astype(BF16)
        wub_ref[...] = wu_ref[0].astype(BF16)

        def sub_body(j, carry):
            rows = pl.ds(pl.multiple_of(j * sub, sub), sub)
            x = x_ref[rows, :]
            hg = _dot(x, wgb_ref[...]) + bg_ref[0]
            hu = _dot(x, wub_ref[...]) + bu_ref[0]
            gate = jnp.minimum(hg, SWIGLU_LIMIT)
            up = jnp.clip(hu, -SWIGLU_LIMIT, SWIGLU_LIMIT)
            act_ref[g, rows, :] = ((up + 1.0) * (gate * _sigmoid(SWIGLU_ALPHA * gate))).astype(BF16)
            return carry

        lax.fori_loop(0, nsub, sub_body, 0)

    @pl.when(jnp.logical_and(used, g >= n_up))
    def _():
        @pl.when(jnp.logical_and(g == n_up, i > 0))
        def _():
            wait_scatter(i - 1)

        wdlb_ref[...] = wdl_ref[0].astype(BF16)
        wdhb_ref[...] = wdh_ref[0].astype(BF16)
        tile0 = (g - n_up) * (th // LANES)

        def sub_body(j, carry):
            row0 = pl.multiple_of(j * sub, sub)
            a = jnp.concatenate([act_ref[c, pl.ds(row0, sub), :] for c in range(n_up)], axis=-1)
            ylo = _dot(a, wdlb_ref[...]) + bdl_ref[0]
            yhi = _dot(a, wdhb_ref[...]) + bdh_ref[0]
            for s in range(th // LANES):
                lo = ylo[:, s * LANES:(s + 1) * LANES].astype(BF16).astype(F32)
                hi = yhi[:, s * LANES:(s + 1) * LANES].astype(BF16).astype(F32)
                word = pltpu.bitcast(hi, jnp.int32) | lax.shift_right_logical(pltpu.bitcast(lo, jnp.int32), 16)
                obuf_ref[pl.ds(row0 * PACK_ROWS + tile0 + s, sub, stride=PACK_ROWS), :] = word
            return carry

        lax.fori_loop(0, nsub, sub_body, 0)

        @pl.when(g == last_g)
        def _():
            nv = nvalid_ref[i]
            start_scatter(slot_ref, 0, jnp.where(i + 1 < n_used, jnp.minimum(sub, nv), nv))

    @pl.when(jnp.logical_and(i == last_i, g == last_g))
    def _():
        wait_scatter(jnp.minimum(i, n_used - 1))


def _expert_ffn(h_packed, slot_assign, blk_exp, blk_valid, n_used, w_up, b_up, w_down, b_down):
    rb, tf, th = FFN_ROWS, FFN_TF, FFN_TH
    nblk = slot_assign.shape[0] // rb
    n_up = D_FF // tf
    n_dn = D_MODEL // 2 // th
    n_assign = h_packed.shape[0] // PACK_ROWS * TOP_K
    n_tok = h_packed.shape[0] // PACK_ROWS
    slot_src = jnp.maximum(slot_assign, 0) % n_tok * PACK_ROWS

    def clamp(i, nu):
        return jnp.minimum(i, nu[0] - 1)

    def up_sel(i, g, nu):
        return jnp.where(i < nu[0], jnp.minimum(g, n_up - 1), n_up - 1)

    def dn_sel(i, g, nu):
        return jnp.where(i < nu[0], jnp.clip(g - n_up, 0, n_dn - 1), n_dn - 1)

    def dn_exp(i, g, be, nu):
        early = jnp.logical_and(g < n_up - 1, i < nu[0])
        return be[jnp.where(early, jnp.maximum(clamp(i, nu) - 1, 0), clamp(i, nu))]

    def dn_col(i, g, nu):
        early = jnp.logical_and(g < n_up - 1, i < nu[0])
        return jnp.where(early, n_dn - 1, dn_sel(i, g, nu))

    grid_spec = pltpu.PrefetchScalarGridSpec(
        num_scalar_prefetch=3,
        grid=(nblk, n_up + n_dn),
        in_specs=[
            pl.BlockSpec((rb,), lambda i, g, be, nu, *_: (clamp(i, nu),), memory_space=pltpu.SMEM),
            pl.BlockSpec((rb,), lambda i, g, be, nu, *_: (jnp.maximum(clamp(i, nu) - 1, 0),), memory_space=pltpu.SMEM),
            pl.BlockSpec((rb,), lambda i, g, be, nu, *_: (clamp(i, nu),), memory_space=pltpu.SMEM),
            pl.BlockSpec((rb,), lambda i, g, be, nu, *_: (clamp(i + 1, nu),), memory_space=pltpu.SMEM),
            pl.BlockSpec((1, D_MODEL, tf), lambda i, g, be, nu, *_: (be[clamp(i, nu)], 0, up_sel(i, g, nu))),
            pl.BlockSpec((1, D_MODEL, tf), lambda i, g, be, nu, *_: (be[clamp(i, nu)], 0, n_up + up_sel(i, g, nu))),
            pl.BlockSpec((1, D_FF, th), lambda i, g, be, nu, *_: (dn_exp(i, g, be, nu), 0, dn_col(i, g, nu))),
            pl.BlockSpec((1, D_FF, th), lambda i, g, be, nu, *_: (dn_exp(i, g, be, nu), 0, n_dn + dn_col(i, g, nu))),
            pl.BlockSpec((1, 1, tf), lambda i, g, be, nu, *_: (be[clamp(i, nu)], 0, up_sel(i, g, nu))),
            pl.BlockSpec((1, 1, tf), lambda i, g, be, nu, *_: (be[clamp(i, nu)], 0, n_up + up_sel(i, g, nu))),
            pl.BlockSpec((1, 1, th), lambda i, g, be, nu, *_: (be[clamp(i, nu)], 0, dn_sel(i, g, nu))),
            pl.BlockSpec((1, 1, th), lambda i, g, be, nu, *_: (be[clamp(i, nu)], 0, n_dn + dn_sel(i, g, nu))),
            pl.BlockSpec(memory_space=pl.ANY),
        ],
        out_specs=pl.BlockSpec(memory_space=pl.ANY),
        scratch_shapes=[pltpu.VMEM((rb * PACK_ROWS, LANES), jnp.int32),
                        pltpu.VMEM((rb, D_MODEL), BF16),
                        pltpu.VMEM((n_up, rb, tf), BF16),
                        pltpu.VMEM((rb * PACK_ROWS, LANES), jnp.int32),
                        pltpu.VMEM((D_MODEL, tf), BF16), pltpu.VMEM((D_MODEL, tf), BF16),
                        pltpu.VMEM((D_FF, th), BF16), pltpu.VMEM((D_FF, th), BF16),
                        pltpu.SemaphoreType.DMA, pltpu.SemaphoreType.DMA],
    )
    assert n_up >= rb // FFN_SUB - 1 and n_up + n_dn > rb // FFN_SUB, "row DMA bursts are spread over the grid steps"
    return pl.pallas_call(
        _ffn_kernel,
        grid_spec=grid_spec,
        out_shape=jax.ShapeDtypeStruct((n_assign * PACK_ROWS, LANES), jnp.int32),
        compiler_params=pltpu.CompilerParams(dimension_semantics=("arbitrary", "arbitrary"),
                                             vmem_limit_bytes=FFN_VMEM_LIMIT, disable_bounds_checks=True),
        name="expert_ffn",
    )(blk_exp, n_used, blk_valid, slot_assign, slot_assign, slot_src, slot_src, w_up, w_up, w_down, w_down,
      b_up.reshape(N_EXPERTS, 1, 2 * D_FF), b_up.reshape(N_EXPERTS, 1, 2 * D_FF),
      b_down.reshape(N_EXPERTS, 1, D_MODEL), b_down.reshape(N_EXPERTS, 1, D_MODEL), h_packed)


def _combine_kernel(gate_ref, h_ref, g_ref, b_ref, ys_ref, o_ref):
    tm = h_ref.shape[0]
    gates = gate_ref[...]
    cols = [None] * (2 * PACK_ROWS)
    for k in range(TOP_K):
        gk = gates[:, k:k + 1]
        for s in range(PACK_ROWS):
            word = ys_ref[k, pl.ds(s, tm, stride=PACK_ROWS), :]
            lo = pltpu.bitcast(lax.shift_left(word, 16), F32)
            hi = pltpu.bitcast(word & jnp.int32(-65536), F32)
            for c, blk in ((s, lo), (PACK_ROWS + s, hi)):
                cols[c] = gk * blk if cols[c] is None else cols[c] + gk * blk
    z = DN_ALPHA * h_ref[...] + jnp.concatenate(cols, axis=-1)
    o_ref[...] = _layer_norm(z, g_ref[...], b_ref[...])


def _combine_ln(gates, h1, ys, g, b):
    n = h1.shape[0]
    tm = min(COMBINE_LN_TM, n)
    row = lambda w: pl.BlockSpec((tm, w), lambda i: (i, 0))
    full = lambda a, b_: pl.BlockSpec((a, b_), lambda i: (0, 0))
    return pl.pallas_call(
        _combine_kernel,
        grid=(n // tm,),
        in_specs=[row(LANES), row(D_MODEL), full(1, D_MODEL), full(1, D_MODEL),
                  pl.BlockSpec((TOP_K, tm * PACK_ROWS, LANES), lambda i: (0, i, 0))],
        out_specs=row(D_MODEL),
        out_shape=jax.ShapeDtypeStruct((n, D_MODEL), F32),
        compiler_params=_cparams(("parallel",)),
        name="combine_ln",
    )(gates, h1, g.reshape(1, -1), b.reshape(1, -1), ys.reshape(TOP_K, n * PACK_ROWS, LANES))


def _layer(h, w_in, b_igate, b_fgate, conv_w, conv_b, rel_bias, beta_attn, beta_mlstm, w_out,
           ln1_g, ln1_b, w_router, b_router, w_up, b_up, w_down, b_down, ln2_g, ln2_b, batch, seq):
    n = batch * seq
    w_in, l = w_in
    w_qkv = w_in[l, :, :3 * D_ATTN].astype(BF16)
    w_mix = w_in[l, :, 3 * D_ATTN:MAIN_COLS].astype(BF16)
    w_gate = jnp.zeros((D_MODEL, LANES), BF16).at[:, :2 * N_HEADS_M].set(w_in[l, :, MAIN_COLS:].astype(BF16))
    qkvs = _qkv_project(h, w_qkv, min(QKV_TM, n))
    proj_m, gates = _project(h, w_mix, w_gate, min(PROJ_TM, n), PROJ_TN)
    gate_bias = jnp.zeros((1, LANES), F32).at[0, :2 * N_HEADS_M].set(jnp.concatenate([b_igate, b_fgate]))

    outs, lses = [], []
    for (_, dil), qkv in zip(DILATED_CONFIGS, qkvs):
        o, l = _dilated_attention(qkv, _attn_bias_tables(rel_bias, dil), batch, seq, dil)
        outs.append(o)
        lses.append(l)
    y_attn = _attn_combine(outs, lses, beta_attn)
    y_mlstm = _mlstm(proj_m, gates, gate_bias, conv_w, conv_b, beta_mlstm, batch, seq)

    h1, h1_packed = _outproj_ln(y_attn, y_mlstm, h, w_out.astype(BF16), ln1_g, ln1_b)

    top_idx, top_gate, rank, counts = _router(h1, w_router, b_router)
    counts = counts[0, :N_EXPERTS]
    nb = (counts + FFN_ROWS - 1) // FFN_ROWS
    per = (counts + nb * FFN_SUB - 1) // jnp.maximum(nb * FFN_SUB, 1) * FFN_SUB
    blk_end = jnp.cumsum(nb)
    blk_start = blk_end - nb
    e_idx, r_idx = top_idx[:, :TOP_K], rank[:, :TOP_K]
    onehot = (e_idx.reshape(-1, 1) == jnp.arange(N_EXPERTS, dtype=jnp.int32)[None, :]).astype(F32)
    table = jnp.stack([jnp.maximum(per, 1), blk_start], axis=1).astype(F32)
    looked = jnp.dot(onehot, table, precision=lax.Precision.HIGHEST)
    per_a, start_a = looked[:, 0], looked[:, 1]
    r_f = r_idx.reshape(-1).astype(F32)
    blk_in_e = jnp.floor((r_f + 0.5) / per_a)
    dest = ((start_a + blk_in_e) * FFN_ROWS + (r_f - blk_in_e * per_a)).astype(jnp.int32)
    nblk = n * TOP_K // FFN_ROWS + N_EXPERTS
    blk_id = jnp.arange(nblk, dtype=jnp.int32)
    blk_exp = jnp.minimum(jnp.sum(blk_end[None, :] <= blk_id[:, None], axis=1), N_EXPERTS - 1).astype(jnp.int32)
    n_used = blk_end[-1:].astype(jnp.int32)
    blk_valid = jnp.clip(counts[blk_exp] - (blk_id - blk_start[blk_exp]) * per[blk_exp], 0,
                         per[blk_exp]).astype(jnp.int32)
    order = jnp.argsort(dest).astype(jnp.int32)
    sorted_assign = (order % TOP_K) * n + order // TOP_K
    first = jnp.cumsum(blk_valid) - blk_valid
    row_in_blk = jnp.tile(jnp.arange(FFN_ROWS, dtype=jnp.int32), nblk)
    first_rep = jnp.repeat(first, FFN_ROWS)
    valid_rep = jnp.repeat(blk_valid, FFN_ROWS)
    picked = sorted_assign[jnp.minimum(first_rep + row_in_blk, n * TOP_K - 1)]
    slot_assign = jnp.where(row_in_blk < valid_rep, picked, -1)

    ys = _expert_ffn(h1_packed, slot_assign, blk_exp, blk_valid, n_used, w_up, b_up, w_down, b_down)
    return _combine_ln(top_gate, h1, ys, ln2_g, ln2_b)


def kernel(x, w_in, b_igate, b_fgate, conv_w, conv_b, rel_bias, beta_attn, beta_mlstm, w_out, ln1_g, ln1_b,
           w_router, b_router, w_up, b_up, w_down, b_down, ln2_g, ln2_b):
    batch, seq, d = x.shape
    h = x.reshape(batch * seq, d)
    for l in range(DEPTH):
        h = _layer(h, (w_in, l), b_igate[l], b_fgate[l], conv_w[l], conv_b[l], rel_bias, beta_attn[l], beta_mlstm[l],
                   w_out[l], ln1_g[l], ln1_b[l], w_router[l], b_router[l], w_up[l], b_up[l], w_down[l], b_down[l],
                   ln2_g[l], ln2_b[l], batch, seq)
    return h.reshape(batch, seq, d)
```

```python
import functools
import math

import numpy as np
import jax
import jax.numpy as jnp
from jax import lax
from jax.experimental import pallas as pl
from jax.experimental.pallas import tpu as pltpu

F32 = jnp.float32
BF16 = jnp.bfloat16

D_MODEL = 2048
D_ATTN = 1024
HEAD_DIM_A = 64
N_HEADS_A = 16
DILATED_CONFIGS = ((128, 1), (512, 4), (2048, 16))
ATTN_BLOCK = 128
NUM_BUCKETS = 32
MAX_DISTANCE = 2048
D_MLSTM = 1024
N_HEADS_M = 4
HEAD_DIM_M = 256
CONV_K = 4
MLSTM_CHUNK = 128
MAIN_COLS = 3 * D_ATTN + 4 * D_MLSTM
N_EXPERTS = 32
TOP_K = 4
D_FF = 2048
SWIGLU_LIMIT = 7.0
SWIGLU_ALPHA = 1.702
DEPTH = 1
DN_ALPHA = (2 * DEPTH) ** 0.25
LN_EPS = 1e-5
HEAD_NORM_EPS = 1e-6
NEG_INF = -1e30

LANES = 128
VMEM_LIMIT = 48 * 1024 * 1024

PROJ_TM = 1024
PROJ_TN = 1024
COMBINE_TM = 512
OUTPROJ_TM = 512
ROUTER_TM = 512
QKV_TM = 512
ATTN_GROUP = 2
FFN_ROWS = 1024
FFN_SUB = 256
FFN_TF = 512
FFN_TH = 512
FFN_VMEM_LIMIT = 60 * 1024 * 1024
ROW_DMA_PRIORITY = 1
COMBINE_LN_TM = 256


def _cparams(sem, vmem=VMEM_LIMIT):
    return pltpu.CompilerParams(dimension_semantics=sem, vmem_limit_bytes=vmem)


def _dot(a, b):
    return jnp.dot(a, b, preferred_element_type=F32)


def _dot_f32_rhs(a_bf16, b_f32):
    hi = b_f32.astype(BF16)
    lo = (b_f32 - hi.astype(F32)).astype(BF16)
    return _dot(a_bf16, hi) + _dot(a_bf16, lo)


def _dot_f32_lhs(a_f32, b_bf16):
    hi = a_f32.astype(BF16)
    lo = (a_f32 - hi.astype(F32)).astype(BF16)
    return _dot(hi, b_bf16) + _dot(lo, b_bf16)


def _sigmoid(x):
    return 1.0 / (1.0 + jnp.exp(-x))


def _log_sigmoid(x):
    return jnp.minimum(x, 0.0) - jnp.log(1.0 + jnp.exp(-jnp.abs(x)))


def _proj_kernel(x_ref, w_ref, wg_ref, o_ref, g_ref):
    x = x_ref[...].astype(BF16)
    o_ref[...] = _dot(x, w_ref[...]).astype(o_ref.dtype)

    @pl.when(pl.program_id(1) == 0)
    def _():
        g_ref[...] = _dot(x, wg_ref[...])


def _project(x, w, w_gate, tm, tn):
    m, k = x.shape
    n = w.shape[1]
    return pl.pallas_call(
        _proj_kernel,
        grid=(m // tm, n // tn),
        in_specs=[pl.BlockSpec((tm, k), lambda i, j: (i, 0)),
                  pl.BlockSpec((k, tn), lambda i, j: (0, j)),
                  pl.BlockSpec((k, LANES), lambda i, j: (0, 0))],
        out_specs=[pl.BlockSpec((tm, tn), lambda i, j: (i, j)),
                   pl.BlockSpec((tm, LANES), lambda i, j: (i, 0))],
        out_shape=[jax.ShapeDtypeStruct((m, n), BF16), jax.ShapeDtypeStruct((m, LANES), F32)],
        compiler_params=_cparams(("parallel", "arbitrary")),
        name="in_proj",
    )(x, w, w_gate)


def _qkv_proj_kernel(x_ref, w_ref, *refs):
    n_out = len(DILATED_CONFIGS)
    o_refs, s_refs = refs[:n_out], refs[n_out:]
    res = _dot(x_ref[...].astype(BF16), w_ref[...])
    ntile, _, tm, _ = s_refs[0].shape
    wid = ntile * LANES
    for c in range(ntile):
        s_refs[0][c, 0] = res[:, c * LANES:(c + 1) * LANES]
    prev_dil = 1
    for level, ((_, dil), o_ref) in enumerate(zip(DILATED_CONFIGS, o_refs)):
        if dil == 1:
            o_ref[...] = res.astype(o_ref.dtype)
            continue
        step = dil // prev_dil
        rows = tm // dil
        keep = level + 1 < n_out
        for rp in range(prev_dil):
            for q in range(step):
                r = rp + prev_dil * q
                for c in range(ntile):
                    piece = s_refs[level - 1][c, rp, pl.ds(q, rows, stride=step), :]
                    col = r * wid + c * LANES
                    o_ref[:, col:col + LANES] = piece.astype(o_ref.dtype)
                    if keep:
                        s_refs[level][c, r] = piece
        prev_dil = dil


def _qkv_project(x, w, tm):
    m, k = x.shape
    wid = w.shape[1]
    dils = [d for _, d in DILATED_CONFIGS]
    return pl.pallas_call(
        _qkv_proj_kernel,
        grid=(m // tm,),
        in_specs=[pl.BlockSpec((tm, k), lambda i: (i, 0)),
                  pl.BlockSpec((k, wid), lambda i: (0, 0), pipeline_mode=pl.Buffered(1))],
        out_specs=[pl.BlockSpec((tm // d, d * wid), lambda i: (i, 0)) for d in dils],
        out_shape=[jax.ShapeDtypeStruct((m // d, d * wid), BF16) for d in dils],
        scratch_shapes=[pltpu.VMEM((wid // LANES, d, tm // d, LANES), F32) for d in dils[:-1]],
        compiler_params=_cparams(("parallel",), 56 * 1024 * 1024),
        name="qkv_proj",
    )(x, w)


def _attn_kernel(q_ref, kp_ref, kc_ref, vp_ref, vc_ref, bias_ref, sel_ref, o_ref, lse_ref):
    n = pl.program_id(2)
    tab = jnp.minimum(n, 1)
    grp, dh, nk = ATTN_GROUP, HEAD_DIM_A, 2 * ATTN_BLOCK
    wid = grp * dh
    lane_head = lax.broadcasted_iota(jnp.int32, (nk, wid), 1) // dh
    zero = jnp.zeros((nk, wid), BF16)
    ones_bd = jnp.concatenate([jnp.where(lane_head == j, 1.0, 0.0).astype(BF16) for j in range(grp)], axis=0)
    q_lane_head = lax.broadcasted_iota(jnp.int32, (ATTN_BLOCK, wid), 1) // dh
    lse_wide = []
    for g in range(N_HEADS_A // grp):
        cols = slice(g * wid, (g + 1) * wid)
        q = q_ref[0, :, cols] * (dh ** -0.5)
        kslab = jnp.concatenate([kp_ref[0, :, cols], kc_ref[0, :, cols]], axis=0)
        vslab = jnp.concatenate([vp_ref[0, :, cols], vc_ref[0, :, cols]], axis=0)
        k_bd = jnp.concatenate([jnp.where(lane_head == j, kslab, zero) for j in range(grp)], axis=0)
        v_bd = jnp.concatenate([jnp.where(lane_head == j, vslab, zero) for j in range(grp)], axis=0)
        s_all = lax.dot_general(q, k_bd, (((1,), (1,)), ((), ())), preferred_element_type=F32)
        ps, ms = [], []
        for j in range(grp):
            s = s_all[:, j * nk:(j + 1) * nk] + bias_ref[tab, g * grp + j]
            m = jnp.max(s, axis=-1, keepdims=True)
            ps.append(jnp.exp(s - m).astype(BF16))
            ms.append(m)
        p_all = jnp.concatenate(ps, axis=-1)
        res = _dot(p_all, jnp.concatenate([v_bd, ones_bd], axis=-1))
        den = res[:, wid:]
        o_ref[0, :, cols] = (res[:, :wid] / den).astype(o_ref.dtype)
        m_wide = ms[grp - 1]
        for j in range(grp - 1):
            m_wide = jnp.where(q_lane_head == j, ms[j], m_wide)
        lse_wide.append(m_wide + jnp.log(den))
    lse_ref[0] = _dot_f32_lhs(jnp.concatenate(lse_wide, axis=-1), sel_ref[...])


def _attn_bias_tables(rel_bias, dil):
    blk = ATTN_BLOCK
    period = 3 * blk
    k = np.arange(period)
    valid = k <= blk
    dist = np.where(valid, blk - k, 0) * dil
    max_exact = NUM_BUCKETS // 2
    d_f = np.maximum(dist, 1).astype(np.float32)
    large = max_exact + (np.log(d_f / np.float32(max_exact)) / np.float32(math.log(MAX_DISTANCE / max_exact))
                         * np.float32(NUM_BUCKETS - max_exact)).astype(np.int32)
    large = np.minimum(large, NUM_BUCKETS - 1)
    bucket = np.where(dist < max_exact, dist, large).astype(np.int32)
    w = jnp.where(jnp.asarray(valid)[None, :], rel_bias[jnp.asarray(bucket)].T.astype(F32), NEG_INF)
    t1 = jnp.tile(w, (1, blk))[:, :blk * (period - 1)].reshape(N_HEADS_A, blk, period - 1)[:, :, :2 * blk]
    has_prev = np.arange(2 * blk)[None, None, :] >= blk
    t0 = jnp.where(jnp.asarray(has_prev), t1, NEG_INF)
    return jnp.stack([t0, t1])


def _dilated_attention(qkv, bias_tab, batch, seq, dil):
    blk = ATTN_BLOCK
    l = seq // dil
    nb = l // blk
    ncb = 3
    pv = qkv.reshape(batch, l, dil * ncb * D_ATTN)
    head_sel = jnp.asarray((np.arange(D_ATTN)[:, None] == np.arange(LANES)[None, :] * HEAD_DIM_A), BF16)

    def cur(c):
        return pl.BlockSpec((1, blk, D_ATTN), lambda b, r, n: (b, n, r * ncb + c))

    def prev(c):
        return pl.BlockSpec((1, blk, D_ATTN), lambda b, r, n: (b, jnp.maximum(n - 1, 0), r * ncb + c))

    o, lse = pl.pallas_call(
        _attn_kernel,
        grid=(batch, dil, nb),
        in_specs=[cur(0), prev(1), cur(1), prev(2), cur(2),
                  pl.BlockSpec((2, N_HEADS_A, blk, 2 * blk), lambda b, r, n: (0, 0, 0, 0)),
                  pl.BlockSpec((D_ATTN, LANES), lambda b, r, n: (0, 0))],
        out_specs=[pl.BlockSpec((1, blk, D_ATTN), lambda b, r, n: (b, n, r)),
                   pl.BlockSpec((1, blk, LANES), lambda b, r, n: (b, n, r))],
        out_shape=[jax.ShapeDtypeStruct((batch, l, dil * D_ATTN), BF16),
                   jax.ShapeDtypeStruct((batch, l, dil * LANES), F32)],
        compiler_params=_cparams(("parallel", "parallel", "arbitrary")),
        name=f"dilated_attn_d{dil}",
    )(pv, pv, pv, pv, pv, bias_tab, head_sel)
    return o.reshape(batch * l, dil * D_ATTN), lse.reshape(batch * l, dil * LANES)


def _attn_combine_kernel(o1_ref, o2_ref, o3_ref, l1_ref, l2_ref, l3_ref, e_ref, et_ref, beta_ref, y_ref,
                         osc_ref, lsc_ref):
    tm = y_ref.shape[0]
    ntile = D_ATTN // LANES
    lses, outs = [], []
    for slot, ((_, dil), o_ref, l_ref) in enumerate(zip(DILATED_CONFIGS, (o1_ref, o2_ref, o3_ref),
                                                        (l1_ref, l2_ref, l3_ref))):
        if dil == 1:
            lses.append(l_ref[...])
            outs.append(lambda o_ref=o_ref: o_ref[...].astype(F32))
            continue
        rows = tm // dil
        for r in range(dil):
            lsc_ref[slot, pl.ds(r, rows, stride=dil), :] = l_ref[:, r * LANES:(r + 1) * LANES]
            for c in range(ntile):
                col = r * D_ATTN + c * LANES
                osc_ref[slot, c, pl.ds(r, rows, stride=dil), :] = o_ref[:, col:col + LANES].astype(F32)
        lses.append(lsc_ref[slot])
        outs.append(lambda slot=slot: jnp.concatenate([osc_ref[slot, c] for c in range(ntile)], axis=-1))
    mx = jnp.maximum(jnp.maximum(lses[0], lses[1]), lses[2])
    ws = [jnp.exp(l - mx) for l in lses]
    tot = ws[0] + ws[1] + ws[2]
    e = e_ref[...]
    acc = None
    for w, load_o in zip(ws, outs):
        term = _dot_f32_lhs(w / tot, e) * load_o()
        acc = term if acc is None else acc + term
    ss = _dot_f32_lhs(acc * acc, et_ref[...])
    inv = lax.rsqrt(ss * (1.0 / HEAD_DIM_A) + HEAD_NORM_EPS)
    y_ref[...] = (acc * _dot_f32_lhs(inv, e) * beta_ref[...]).astype(y_ref.dtype)


def _attn_combine(os_, lses, beta_attn):
    n = os_[0].shape[0]
    tm = min(COMBINE_TM, n)
    head_of_lane = np.arange(D_ATTN) // HEAD_DIM_A
    e = (np.arange(LANES)[:, None] == head_of_lane[None, :]).astype(np.float32)
    e_j = jnp.asarray(e, BF16)
    et_j = jnp.asarray(e.T, BF16)
    dils = [d for _, d in DILATED_CONFIGS]
    dilated = lambda w: [pl.BlockSpec((tm // d, d * w), lambda i: (i, 0)) for d in dils]
    full = lambda a, b: pl.BlockSpec((a, b), lambda i: (0, 0))
    return pl.pallas_call(
        _attn_combine_kernel,
        grid=(n // tm,),
        in_specs=dilated(D_ATTN) + dilated(LANES) + [full(LANES, D_ATTN), full(D_ATTN, LANES), full(1, D_ATTN)],
        out_specs=pl.BlockSpec((tm, D_ATTN), lambda i: (i, 0)),
        out_shape=jax.ShapeDtypeStruct((n, D_ATTN), BF16),
        scratch_shapes=[pltpu.VMEM((len(dils), D_ATTN // LANES, tm, LANES), F32),
                        pltpu.VMEM((len(dils), tm, LANES), F32)],
        compiler_params=_cparams(("parallel",)),
        name="attn_combine",
    )(*os_, *lses, e_j, et_j, beta_attn.reshape(1, D_ATTN).astype(F32))


def _mlstm_kernel(qp_ref, kp_ref, qprev_ref, kprev_ref, v_ref, og_ref, g_ref, gb_ref, cw_ref, cb_ref,
                  beta_ref, y_ref, c_ref, n_ref, m_ref):
    step = pl.program_id(1)
    ch = MLSTM_CHUNK
    dh = HEAD_DIM_M

    @pl.when(step == 0)
    def _():
        c_ref[...] = jnp.zeros(c_ref.shape, F32)
        n_ref[...] = jnp.zeros(n_ref.shape, F32)
        m_ref[...] = jnp.zeros(m_ref.shape, F32)

    def conv_silu(x_ref, prev_ref, coff, b):
        x = x_ref[b].astype(F32)
        p = jnp.where(step > 0, prev_ref[b].astype(F32), 0.0)
        xe = jnp.concatenate([p, x], axis=0)
        npad = p.shape[0]
        cols = slice(coff, coff + D_MLSTM)
        acc = cb_ref[:, cols] + cw_ref[CONV_K - 1:CONV_K, cols] * x
        for s in range(1, CONV_K):
            shifted = pltpu.roll(xe, s, 0)[npad:]
            acc = acc + cw_ref[CONV_K - 1 - s:CONV_K - s, cols] * shifted
        return acc * _sigmoid(acc)

    row_i = lax.broadcasted_iota(jnp.int32, (ch, ch), 0)
    col_i = lax.broadcasted_iota(jnp.int32, (ch, ch), 1)
    causal = row_i >= col_i
    tri = jnp.where(causal, 1.0, 0.0).astype(BF16)
    upp = jnp.where(row_i <= col_i, 1.0, 0.0).astype(BF16)
    b = 0
    qf = conv_silu(qp_ref, qprev_ref, 0, b)
    kf = conv_silu(kp_ref, kprev_ref, D_MLSTM, b) * (dh ** -0.5)
    qb = qf.astype(BF16)
    kb = kf.astype(BF16)
    g = g_ref[b] + gb_ref[...]
    gt = g.T
    b_cols = _dot_f32_rhs(tri, _log_sigmoid(g))
    b_rows = _dot_f32_lhs(_log_sigmoid(gt), upp)

    for h in range(N_HEADS_M):
        st = h
        hs = slice(h * dh, (h + 1) * dh)
        fi = N_HEADS_M + h
        i_row = gt[h:h + 1, :]
        i_col = g[:, h:h + 1]
        b_row = b_rows[fi:fi + 1, :]
        b_col = b_cols[:, fi:fi + 1]
        m_prev = m_ref[st][:, 0:1]
        q_h, k_h = qb[:, hs], kb[:, hs]
        v_h = v_ref[b, :, hs]

        dmat = jnp.where(causal, b_col - b_row + i_row, NEG_INF)
        m_inter = b_col + m_prev
        m_t = jnp.maximum(m_inter, jnp.max(dmat, axis=-1, keepdims=True))
        w = jnp.exp(dmat - m_t) * lax.dot_general(q_h, k_h, (((1,), (1,)), ((), ())),
                                                  preferred_element_type=F32)
        decay = jnp.exp(m_inter - m_t)
        c_old = c_ref[st]
        inter = lax.dot_general(q_h, c_old.astype(BF16), (((1,), (1,)), ((), ())), preferred_element_type=F32)
        num = _dot(w.astype(BF16), v_h) + decay * inter
        n_old = n_ref[st]
        den = jnp.sum(w, axis=-1, keepdims=True) + decay * jnp.sum(qf[:, hs] * n_old, axis=-1, keepdims=True)
        hh = num / jnp.maximum(jnp.abs(den), jnp.exp(-m_t))

        g_last = b_col[ch - 1:ch, :]
        a_row = g_last - b_row + i_row
        a_col = g_last - b_col + i_col
        m_new = jnp.maximum(g_last + m_prev, jnp.max(a_row, axis=-1, keepdims=True))
        carry = jnp.exp(g_last + m_prev - m_new)
        wa_col = jnp.exp(a_col - m_new)
        wv = (wa_col * v_h.astype(F32)).astype(BF16)
        c_ref[st] = carry * c_old + lax.dot_general(wv, k_h, (((0,), (0,)), ((), ())), preferred_element_type=F32)
        n_ref[st] = carry * n_old + jnp.sum(wa_col * kf[:, hs], axis=0, keepdims=True)
        m_ref[st] = jnp.broadcast_to(m_new, (1, LANES))

        gated = _sigmoid(og_ref[b, :, hs].astype(F32)) * hh
        ms = jnp.sum(gated * gated, axis=-1, keepdims=True) * (1.0 / dh)
        y_ref[b, :, hs] = (gated * lax.rsqrt(ms + HEAD_NORM_EPS) * beta_ref[:, hs]).astype(y_ref.dtype)


def _mlstm(proj, gates, gate_bias, conv_w, conv_b, beta_mlstm, batch, seq):
    ch = MLSTM_CHUNK
    nchunk = seq // ch
    pv = proj.reshape(batch, seq, 4 * D_MLSTM)
    gv = gates.reshape(batch, seq, LANES)
    prev_rows = 16
    cb0 = 0

    def cur(c):
        return pl.BlockSpec((1, ch, D_MLSTM), lambda b, n: (b, n, c))

    def prev(c):
        per = ch // prev_rows
        return pl.BlockSpec((1, prev_rows, D_MLSTM), lambda b, n: (b, jnp.maximum(n * per - 1, 0), c))

    const = lambda a, b_: pl.BlockSpec((a, b_), lambda b, n: (0, 0))
    y = pl.pallas_call(
        _mlstm_kernel,
        grid=(batch, nchunk),
        in_specs=[cur(cb0), cur(cb0 + 1), prev(cb0), prev(cb0 + 1), cur(cb0 + 2), cur(cb0 + 3),
                  pl.BlockSpec((1, ch, LANES), lambda b, n: (b, n, 0)),
                  const(1, LANES), const(CONV_K, 2 * D_MLSTM), const(1, 2 * D_MLSTM), const(1, D_MLSTM)],
        out_specs=pl.BlockSpec((1, ch, D_MLSTM), lambda b, n: (b, n, 0)),
        out_shape=jax.ShapeDtypeStruct((batch, seq, D_MLSTM), BF16),
        scratch_shapes=[pltpu.VMEM((N_HEADS_M, HEAD_DIM_M, HEAD_DIM_M), F32),
                        pltpu.VMEM((N_HEADS_M, 1, HEAD_DIM_M), F32),
                        pltpu.VMEM((N_HEADS_M, 1, LANES), F32)],
        compiler_params=_cparams(("parallel", "arbitrary")),
        name="mlstm",
    )(pv, pv, pv, pv, pv, pv, gv, gate_bias, conv_w.astype(F32), conv_b.reshape(1, -1).astype(F32),
      beta_mlstm.reshape(1, D_MLSTM).astype(F32))
    return y.reshape(batch * seq, D_MLSTM)


def _layer_norm(z, g, b):
    mu = jnp.mean(z, axis=-1, keepdims=True)
    zc = z - mu
    var = jnp.mean(zc * zc, axis=-1, keepdims=True)
    return zc * lax.rsqrt(var + LN_EPS) * g + b


PACK_ROWS = D_MODEL // (2 * LANES)


def _store_packed_rows(dst_ref, x, first=0):
    rows = x.shape[0]
    half = D_MODEL // 2
    for s in range(PACK_ROWS):
        lo = x[:, s * LANES:(s + 1) * LANES].astype(BF16).astype(F32)
        hi = x[:, half + s * LANES:half + (s + 1) * LANES].astype(BF16).astype(F32)
        word = pltpu.bitcast(hi, jnp.int32) | lax.shift_right_logical(pltpu.bitcast(lo, jnp.int32), 16)
        dst_ref[pl.ds(first * PACK_ROWS + s, rows, stride=PACK_ROWS), :] = word


def _load_packed_rows(src_ref, first, rows):
    los, his = [], []
    for s in range(PACK_ROWS):
        word = src_ref[pl.ds(first * PACK_ROWS + s, rows, stride=PACK_ROWS), :]
        los.append(pltpu.bitcast(lax.shift_left(word, 16), F32))
        his.append(pltpu.bitcast(word & jnp.int32(-65536), F32))
    return los, his


def _outproj_kernel(ya_ref, ym_ref, x_ref, w_ref, g_ref, b_ref, h_ref, hp_ref):
    y = _dot(ya_ref[...], w_ref[0:D_ATTN, :]) + _dot(ym_ref[...], w_ref[D_ATTN:D_MODEL, :])
    h = _layer_norm(DN_ALPHA * x_ref[...] + y, g_ref[...], b_ref[...])
    h_ref[...] = h
    _store_packed_rows(hp_ref, h)


def _outproj_ln(ya, ym, x, w_out, g, b):
    n = x.shape[0]
    tm = min(OUTPROJ_TM, n)
    row = lambda w: pl.BlockSpec((tm, w), lambda i: (i, 0))
    full = lambda a, b_: pl.BlockSpec((a, b_), lambda i: (0, 0))
    return pl.pallas_call(
        _outproj_kernel,
        grid=(n // tm,),
        in_specs=[row(D_ATTN), row(D_MLSTM), row(D_MODEL), full(D_MODEL, D_MODEL), full(1, D_MODEL), full(1, D_MODEL)],
        out_specs=[row(D_MODEL), pl.BlockSpec((tm * PACK_ROWS, LANES), lambda i: (i, 0))],
        out_shape=[jax.ShapeDtypeStruct((n, D_MODEL), F32),
                   jax.ShapeDtypeStruct((n * PACK_ROWS, LANES), jnp.int32)],
        compiler_params=_cparams(("parallel",)),
        name="out_proj_ln",
    )(ya, ym, x, w_out, g.reshape(1, -1), b.reshape(1, -1))


def _router_kernel(h_ref, whi_ref, wlo_ref, b_ref, tri_ref, idx_ref, gate_ref, rank_ref, cnt_ref, carry_ref):
    i = pl.program_id(0)

    @pl.when(i == 0)
    def _():
        carry_ref[...] = jnp.zeros(carry_ref.shape, F32)

    x = h_ref[...]
    xhi = x.astype(BF16)
    xlo = (x - xhi.astype(F32)).astype(BF16)
    logits = _dot(xhi, whi_ref[...]) + _dot(xhi, wlo_ref[...]) + _dot(xlo, whi_ref[...]) + b_ref[...]
    tm = logits.shape[0]
    lane = lax.broadcasted_iota(jnp.int32, (tm, LANES), 1)
    lane_f = lane.astype(F32)
    vals = jnp.where(lane < N_EXPERTS, logits, NEG_INF)

    sels, tops = [], []
    for _ in range(TOP_K):
        mx = jnp.max(vals, axis=-1, keepdims=True)
        first = jnp.min(jnp.where(vals == mx, lane_f, float(LANES)), axis=-1, keepdims=True)
        sel = lane_f == first
        sels.append(sel)
        tops.append((mx, first))
        vals = jnp.where(sel, 2.0 * NEG_INF, vals)

    exps = [jnp.exp(mx - tops[0][0]) for mx, _ in tops]
    tot = exps[0] + exps[1] + exps[2] + exps[3]

    onehot = jnp.zeros((tm, LANES), F32)
    for sel in sels:
        onehot = jnp.where(sel, 1.0, onehot)
    before = _dot(tri_ref[...], onehot.astype(BF16)) + carry_ref[...]

    idx_out = jnp.zeros((tm, LANES), F32)
    gate_out = jnp.zeros((tm, LANES), F32)
    rank_out = jnp.zeros((tm, LANES), F32)
    for k in range(TOP_K):
        rank_k = jnp.sum(jnp.where(sels[k], before, 0.0), axis=-1, keepdims=True)
        idx_out = jnp.where(lane == k, tops[k][1], idx_out)
        gate_out = jnp.where(lane == k, exps[k] / tot, gate_out)
        rank_out = jnp.where(lane == k, rank_k, rank_out)
    idx_ref[...] = idx_out.astype(jnp.int32)
    gate_ref[...] = gate_out
    rank_ref[...] = rank_out.astype(jnp.int32)

    carry = carry_ref[...] + jnp.sum(onehot, axis=0, keepdims=True)
    carry_ref[...] = carry
    cnt_ref[...] = carry.astype(jnp.int32)


def _router(h1, w_router, b_router):
    n = h1.shape[0]
    tm = min(ROUTER_TM, n)
    wpad = jnp.zeros((D_MODEL, LANES), F32).at[:, :N_EXPERTS].set(w_router)
    whi = wpad.astype(BF16)
    wlo = (wpad - whi.astype(F32)).astype(BF16)
    bpad = jnp.zeros((1, LANES), F32).at[0, :N_EXPERTS].set(b_router)
    tri = jnp.asarray(np.tril(np.ones((tm, tm), np.float32), -1), BF16)
    row = lambda w: pl.BlockSpec((tm, w), lambda i: (i, 0))
    full = lambda a, b_: pl.BlockSpec((a, b_), lambda i: (0, 0))
    return pl.pallas_call(
        _router_kernel,
        grid=(n // tm,),
        in_specs=[row(D_MODEL), full(D_MODEL, LANES), full(D_MODEL, LANES), full(1, LANES), full(tm, tm)],
        out_specs=[row(LANES), row(LANES), row(LANES), full(1, LANES)],
        out_shape=[jax.ShapeDtypeStruct((n, LANES), jnp.int32), jax.ShapeDtypeStruct((n, LANES), F32),
                   jax.ShapeDtypeStruct((n, LANES), jnp.int32), jax.ShapeDtypeStruct((1, LANES), jnp.int32)],
        scratch_shapes=[pltpu.VMEM((1, LANES), F32)],
        compiler_params=_cparams(("arbitrary",)),
        name="router",
    )(h1, whi, wlo, bpad, tri)


def _ffn_kernel(bexp_ref, nused_ref, nvalid_ref, slot_ref, slot_prv_ref, src_cur_ref, src_nxt_ref, wg_ref, wu_ref,
                wdl_ref, wdh_ref, bup_ref, bdn_ref, xp_ref, ys_ref, xbuf_ref, x_ref, act_ref, obuf_ref, wgb_ref,
                wub_ref, wdlb_ref, wdhb_ref, in_sem, out_sem):
    i = pl.program_id(0)
    g = pl.program_id(1)
    n_up = act_ref.shape[0]
    expert = bexp_ref[i]
    n_dn = bdn_ref.shape[1] // 2
    last_g = pl.num_programs(1) - 1
    last_i = pl.num_programs(0) - 1
    n_used = nused_ref[0]
    used = i < n_used
    rb = x_ref.shape[0]
    sub = FFN_SUB
    th = wdlb_ref.shape[1]
    unroll = 8

    def sub_blocks(blk):
        return (nvalid_ref[blk] + (sub - 1)) // sub

    nsub = sub_blocks(i)

    def start_gather(src_ref, lo, hi):
        def body(j, carry):
            for u in range(unroll):
                jj = j * unroll + u
                src = pl.multiple_of(src_ref[jj], PACK_ROWS)
                dst = pl.multiple_of(jj * PACK_ROWS, PACK_ROWS)
                pltpu.make_async_copy(xp_ref.at[pl.ds(src, PACK_ROWS)], xbuf_ref.at[pl.ds(dst, PACK_ROWS)],
                                      in_sem).start(priority=ROW_DMA_PRIORITY)
            return carry

        lax.fori_loop(lo // unroll, hi // unroll, body, 0)

    def wait_gather(nrows):
        npk = pl.multiple_of(nrows * PACK_ROWS, PACK_ROWS)
        pltpu.make_async_copy(xp_ref.at[pl.ds(0, npk)], xbuf_ref.at[pl.ds(0, npk)], in_sem).wait()

    def start_scatter(slots_ref, lo, hi):
        def one(jj):
            src = pl.multiple_of(jj * PACK_ROWS, PACK_ROWS)
            dst = pl.multiple_of(slots_ref[jj] * PACK_ROWS, PACK_ROWS)
            pltpu.make_async_copy(obuf_ref.at[pl.ds(src, PACK_ROWS)], ys_ref.at[pl.ds(dst, PACK_ROWS)],
                                  out_sem).start(priority=ROW_DMA_PRIORITY)

        def body(j, carry):
            for u in range(unroll):
                one(j * unroll + u)
            return carry

        nfull = jnp.maximum(hi - lo, 0) // unroll
        lax.fori_loop(lo // unroll, lo // unroll + nfull, body, 0)
        for u in range(unroll - 1):
            @pl.when(lo + nfull * unroll + u < hi)
            def _():
                one(lo + nfull * unroll + u)

    def wait_scatter(blk):
        npk = pl.multiple_of(nvalid_ref[blk] * PACK_ROWS, PACK_ROWS)
        pltpu.make_async_copy(obuf_ref.at[pl.ds(0, npk)], ys_ref.at[pl.ds(0, npk)], out_sem).wait()

    @pl.when(jnp.logical_and(used, g == 0))
    def _():
        @pl.when(i == 0)
        def _():
            start_gather(src_cur_ref, 0, nsub * sub)

        wait_gather(nsub * sub)
        for j in range(rb // sub):
            @pl.when(j < nsub)
            def _():
                los, his = _load_packed_rows(xbuf_ref, j * sub, sub)
                rows = slice(j * sub, (j + 1) * sub)
                for s in range(PACK_ROWS):
                    x_ref[rows, s * LANES:(s + 1) * LANES] = los[s].astype(BF16)
                    x_ref[rows, D_MODEL // 2 + s * LANES:D_MODEL // 2 + (s + 1) * LANES] = his[s].astype(BF16)

    @pl.when(jnp.logical_and(jnp.logical_and(g >= 1, g - 1 < sub_blocks(jnp.minimum(i + 1, last_i))), i + 1 < n_used))
    def _():
        start_gather(src_nxt_ref, (g - 1) * sub, g * sub)

    @pl.when(jnp.logical_and(jnp.logical_and(used, i > 0), g + 1 < rb // sub))
    def _():
        start_scatter(slot_prv_ref, (g + 1) * sub, jnp.minimum((g + 2) * sub, nvalid_ref[i - 1]))

    @pl.when(jnp.logical_and(used, g < n_up))
    def _():
        wgb_ref[...] = wg_ref[0].astype(BF16)
        wub_ref[...] = wu_ref[0].astype(BF16)

        def sub_body(j, carry):
            rows = pl.ds(pl.multiple_of(j * sub, sub), sub)
            x = x_ref[rows, :]
            hg = _dot(x, wgb_ref[...]) + bup_ref[expert, pl.ds(g, 1), :]
            hu = _dot(x, wub_ref[...]) + bup_ref[expert, pl.ds(n_up + g, 1), :]
            gate = jnp.minimum(hg, SWIGLU_LIMIT)
            up = jnp.clip(hu, -SWIGLU_LIMIT, SWIGLU_LIMIT)
            act_ref[g, rows, :] = ((up + 1.0) * (gate * _sigmoid(SWIGLU_ALPHA * gate))).astype(BF16)
            return carry

        lax.fori_loop(0, nsub, sub_body, 0)

    @pl.when(jnp.logical_and(used, g >= n_up))
    def _():
        @pl.when(jnp.logical_and(g == n_up, i > 0))
        def _():
            wait_scatter(i - 1)

        wdlb_ref[...] = wdl_ref[0].astype(BF16)
        wdhb_ref[...] = wdh_ref[0].astype(BF16)
        tile0 = (g - n_up) * (th // LANES)

        def sub_body(j, carry):
            row0 = pl.multiple_of(j * sub, sub)
            a = jnp.concatenate([act_ref[c, pl.ds(row0, sub), :] for c in range(n_up)], axis=-1)
            ylo = _dot(a, wdlb_ref[...]) + bdn_ref[expert, pl.ds(g - n_up, 1), :]
            yhi = _dot(a, wdhb_ref[...]) + bdn_ref[expert, pl.ds(n_dn + g - n_up, 1), :]
            for s in range(th // LANES):
                lo = ylo[:, s * LANES:(s + 1) * LANES].astype(BF16).astype(F32)
                hi = yhi[:, s * LANES:(s + 1) * LANES].astype(BF16).astype(F32)
                word = pltpu.bitcast(hi, jnp.int32) | lax.shift_right_logical(pltpu.bitcast(lo, jnp.int32), 16)
                obuf_ref[pl.ds(row0 * PACK_ROWS + tile0 + s, sub, stride=PACK_ROWS), :] = word
            return carry

        lax.fori_loop(0, nsub, sub_body, 0)

        @pl.when(g == last_g)
        def _():
            nv = nvalid_ref[i]
            start_scatter(slot_ref, 0, jnp.where(i + 1 < n_used, jnp.minimum(sub, nv), nv))

    @pl.when(jnp.logical_and(i == last_i, g == last_g))
    def _():
        wait_scatter(jnp.minimum(i, n_used - 1))


def _expert_ffn(h_packed, slot_assign, blk_exp, blk_valid, n_used, w_up, b_up, w_down, b_down):
    rb, tf, th = FFN_ROWS, FFN_TF, FFN_TH
    nblk = slot_assign.shape[0] // rb
    n_up = D_FF // tf
    n_dn = D_MODEL // 2 // th
    n_assign = h_packed.shape[0] // PACK_ROWS * TOP_K
    n_tok = h_packed.shape[0] // PACK_ROWS
    slot_src = jnp.maximum(slot_assign, 0) % n_tok * PACK_ROWS

    def clamp(i, nu):
        return jnp.minimum(i, nu[0] - 1)

    def up_sel(i, g, nu):
        return jnp.where(i < nu[0], jnp.minimum(g, n_up - 1), n_up - 1)

    def dn_sel(i, g, nu):
        return jnp.where(i < nu[0], jnp.clip(g - n_up, 0, n_dn - 1), n_dn - 1)

    def dn_exp(i, g, be, nu):
        early = jnp.logical_and(g < n_up - 1, i < nu[0])
        return be[jnp.where(early, jnp.maximum(clamp(i, nu) - 1, 0), clamp(i, nu))]

    def dn_col(i, g, nu):
        early = jnp.logical_and(g < n_up - 1, i < nu[0])
        return jnp.where(early, n_dn - 1, dn_sel(i, g, nu))

    grid_spec = pltpu.PrefetchScalarGridSpec(
        num_scalar_prefetch=3,
        grid=(nblk, n_up + n_dn),
        in_specs=[
            pl.BlockSpec((rb,), lambda i, g, be, nu, *_: (clamp(i, nu),), memory_space=pltpu.SMEM),
            pl.BlockSpec((rb,), lambda i, g, be, nu, *_: (jnp.maximum(clamp(i, nu) - 1, 0),), memory_space=pltpu.SMEM),
            pl.BlockSpec((rb,), lambda i, g, be, nu, *_: (clamp(i, nu),), memory_space=pltpu.SMEM),
            pl.BlockSpec((rb,), lambda i, g, be, nu, *_: (clamp(i + 1, nu),), memory_space=pltpu.SMEM),
            pl.BlockSpec((1, D_MODEL, tf), lambda i, g, be, nu, *_: (be[clamp(i, nu)], 0, up_sel(i, g, nu))),
            pl.BlockSpec((1, D_MODEL, tf), lambda i, g, be, nu, *_: (be[clamp(i, nu)], 0, n_up + up_sel(i, g, nu))),
            pl.BlockSpec((1, D_FF, th), lambda i, g, be, nu, *_: (dn_exp(i, g, be, nu), 0, dn_col(i, g, nu))),
            pl.BlockSpec((1, D_FF, th), lambda i, g, be, nu, *_: (dn_exp(i, g, be, nu), 0, n_dn + dn_col(i, g, nu))),
            pl.BlockSpec((N_EXPERTS, 2 * n_up, tf), lambda i, g, *_: (0, 0, 0)),
            pl.BlockSpec((N_EXPERTS, 2 * n_dn, th), lambda i, g, *_: (0, 0, 0)),
            pl.BlockSpec(memory_space=pl.ANY),
        ],
        out_specs=pl.BlockSpec(memory_space=pl.ANY),
        scratch_shapes=[pltpu.VMEM((rb * PACK_ROWS, LANES), jnp.int32),
                        pltpu.VMEM((rb, D_MODEL), BF16),
                        pltpu.VMEM((n_up, rb, tf), BF16),
                        pltpu.VMEM((rb * PACK_ROWS, LANES), jnp.int32),
                        pltpu.VMEM((D_MODEL, tf), BF16), pltpu.VMEM((D_MODEL, tf), BF16),
                        pltpu.VMEM((D_FF, th), BF16), pltpu.VMEM((D_FF, th), BF16),
                        pltpu.SemaphoreType.DMA, pltpu.SemaphoreType.DMA],
    )
    assert n_up >= rb // FFN_SUB - 1 and n_up + n_dn > rb // FFN_SUB, "row DMA bursts are spread over the grid steps"
    return pl.pallas_call(
        _ffn_kernel,
        grid_spec=grid_spec,
        out_shape=jax.ShapeDtypeStruct((n_assign * PACK_ROWS, LANES), jnp.int32),
        compiler_params=pltpu.CompilerParams(dimension_semantics=("arbitrary", "arbitrary"),
                                             vmem_limit_bytes=FFN_VMEM_LIMIT, disable_bounds_checks=True),
        name="expert_ffn",
    )(blk_exp, n_used, blk_valid, slot_assign, slot_assign, slot_src, slot_src, w_up, w_up, w_down, w_down,
      b_up.reshape(N_EXPERTS, 2 * n_up, tf), b_down.reshape(N_EXPERTS, 2 * n_dn, th), h_packed)


def _combine_kernel(gate_ref, h_ref, g_ref, b_ref, ys_ref, o_ref):
    tm = h_ref.shape[0]
    gates = gate_ref[...]
    cols = [None] * (2 * PACK_ROWS)
    for k in range(TOP_K):
        gk = gates[:, k:k + 1]
        for s in range(PACK_ROWS):
            word = ys_ref[k, pl.ds(s, tm, stride=PACK_ROWS), :]
            lo = pltpu.bitcast(lax.shift_left(word, 16), F32)
            hi = pltpu.bitcast(word & jnp.int32(-65536), F32)
            for c, blk in ((s, lo), (PACK_ROWS + s, hi)):
                cols[c] = gk * blk if cols[c] is None else cols[c] + gk * blk
    z = DN_ALPHA * h_ref[...] + jnp.concatenate(cols, axis=-1)
    o_ref[...] = _layer_norm(z, g_ref[...], b_ref[...])


def _combine_ln(gates, h1, ys, g, b):
    n = h1.shape[0]
    tm = min(COMBINE_LN_TM, n)
    row = lambda w: pl.BlockSpec((tm, w), lambda i: (i, 0))
    full = lambda a, b_: pl.BlockSpec((a, b_), lambda i: (0, 0))
    return pl.pallas_call(
        _combine_kernel,
        grid=(n // tm,),
        in_specs=[row(LANES), row(D_MODEL), full(1, D_MODEL), full(1, D_MODEL),
                  pl.BlockSpec((TOP_K, tm * PACK_ROWS, LANES), lambda i: (0, i, 0))],
        out_specs=row(D_MODEL),
        out_shape=jax.ShapeDtypeStruct((n, D_MODEL), F32),
        compiler_params=_cparams(("parallel",)),
        name="combine_ln",
    )(gates, h1, g.reshape(1, -1), b.reshape(1, -1), ys.reshape(TOP_K, n * PACK_ROWS, LANES))


def _layer(h, w_in, b_igate, b_fgate, conv_w, conv_b, rel_bias, beta_attn, beta_mlstm, w_out,
           ln1_g, ln1_b, w_router, b_router, w_up, b_up, w_down, b_down, ln2_g, ln2_b, batch, seq):
    n = batch * seq
    w_in, l = w_in
    w_qkv = w_in[l, :, :3 * D_ATTN].astype(BF16)
    w_mix = w_in[l, :, 3 * D_ATTN:MAIN_COLS].astype(BF16)
    w_gate = jnp.zeros((D_MODEL, LANES), BF16).at[:, :2 * N_HEADS_M].set(w_in[l, :, MAIN_COLS:].astype(BF16))
    qkvs = _qkv_project(h, w_qkv, min(QKV_TM, n))
    proj_m, gates = _project(h, w_mix, w_gate, min(PROJ_TM, n), PROJ_TN)
    gate_bias = jnp.zeros((1, LANES), F32).at[0, :2 * N_HEADS_M].set(jnp.concatenate([b_igate, b_fgate]))

    outs, lses = [], []
    for (_, dil), qkv in zip(DILATED_CONFIGS, qkvs):
        o, l = _dilated_attention(qkv, _attn_bias_tables(rel_bias, dil), batch, seq, dil)
        outs.append(o)
        lses.append(l)
    y_attn = _attn_combine(outs, lses, beta_attn)
    y_mlstm = _mlstm(proj_m, gates, gate_bias, conv_w, conv_b, beta_mlstm, batch, seq)

    h1, h1_packed = _outproj_ln(y_attn, y_mlstm, h, w_out.astype(BF16), ln1_g, ln1_b)

    top_idx, top_gate, rank, counts = _router(h1, w_router, b_router)
    counts = counts[0, :N_EXPERTS]
    nb = (counts + FFN_ROWS - 1) // FFN_ROWS
    per = (counts + nb * FFN_SUB - 1) // jnp.maximum(nb * FFN_SUB, 1) * FFN_SUB
    blk_end = jnp.cumsum(nb)
    blk_start = blk_end - nb
    e_idx, r_idx = top_idx[:, :TOP_K], rank[:, :TOP_K]
    onehot = (e_idx.reshape(-1, 1) == jnp.arange(N_EXPERTS, dtype=jnp.int32)[None, :]).astype(F32)
    table = jnp.stack([jnp.maximum(per, 1), blk_start], axis=1).astype(F32)
    looked = jnp.dot(onehot, table, precision=lax.Precision.HIGHEST)
    per_a, start_a = looked[:, 0], looked[:, 1]
    r_f = r_idx.reshape(-1).astype(F32)
    blk_in_e = jnp.floor((r_f + 0.5) / per_a)
    dest = ((start_a + blk_in_e) * FFN_ROWS + (r_f - blk_in_e * per_a)).astype(jnp.int32)
    nblk = n * TOP_K // FFN_ROWS + N_EXPERTS
    blk_id = jnp.arange(nblk, dtype=jnp.int32)
    blk_exp = jnp.minimum(jnp.sum(blk_end[None, :] <= blk_id[:, None], axis=1), N_EXPERTS - 1).astype(jnp.int32)
    n_used = blk_end[-1:].astype(jnp.int32)
    blk_valid = jnp.clip(counts[blk_exp] - (blk_id - blk_start[blk_exp]) * per[blk_exp], 0,
                         per[blk_exp]).astype(jnp.int32)
    order = jnp.argsort(dest).astype(jnp.int32)
    sorted_assign = (order % TOP_K) * n + order // TOP_K
    first = jnp.cumsum(blk_valid) - blk_valid
    row_in_blk = jnp.tile(jnp.arange(FFN_ROWS, dtype=jnp.int32), nblk)
    first_rep = jnp.repeat(first, FFN_ROWS)
    valid_rep = jnp.repeat(blk_valid, FFN_ROWS)
    picked = sorted_assign[jnp.minimum(first_rep + row_in_blk, n * TOP_K - 1)]
    slot_assign = jnp.where(row_in_blk < valid_rep, picked, -1)

    ys = _expert_ffn(h1_packed, slot_assign, blk_exp, blk_valid, n_used, w_up, b_up, w_down, b_down)
    return _combine_ln(top_gate, h1, ys, ln2_g, ln2_b)


def kernel(x, w_in, b_igate, b_fgate, conv_w, conv_b, rel_bias, beta_attn, beta_mlstm, w_out, ln1_g, ln1_b,
           w_router, b_router, w_up, b_up, w_down, b_down, ln2_g, ln2_b):
    batch, seq, d = x.shape
    h = x.reshape(batch * seq, d)
    for l in range(DEPTH):
        h = _layer(h, (w_in, l), b_igate[l], b_fgate[l], conv_w[l], conv_b[l], rel_bias, beta_attn[l], beta_mlstm[l],
                   w_out[l], ln1_g[l], ln1_b[l], w_router[l], b_router[l], w_up[l], b_up[l], w_down[l], b_down[l],
                   ln2_g[l], ln2_b[l], batch, seq)
    return h.reshape(batch, seq, d)
```

```python
import functools
import math

import numpy as np
import jax
import jax.numpy as jnp
from jax import lax
from jax.experimental import pallas as pl
from jax.experimental.pallas import tpu as pltpu

F32 = jnp.float32
BF16 = jnp.bfloat16

D_MODEL = 2048
D_ATTN = 1024
HEAD_DIM_A = 64
N_HEADS_A = 16
DILATED_CONFIGS = ((128, 1), (512, 4), (2048, 16))
ATTN_BLOCK = 128
NUM_BUCKETS = 32
MAX_DISTANCE = 2048
D_MLSTM = 1024
N_HEADS_M = 4
HEAD_DIM_M = 256
CONV_K = 4
MLSTM_CHUNK = 128
MAIN_COLS = 3 * D_ATTN + 4 * D_MLSTM
N_EXPERTS = 32
TOP_K = 4
D_FF = 2048
SWIGLU_LIMIT = 7.0
SWIGLU_ALPHA = 1.702
DEPTH = 1
DN_ALPHA = (2 * DEPTH) ** 0.25
LN_EPS = 1e-5
HEAD_NORM_EPS = 1e-6
NEG_INF = -1e30

LANES = 128
VMEM_LIMIT = 48 * 1024 * 1024

PROJ_TM = 1024
PROJ_TN = 1024
COMBINE_TM = 512
OUTPROJ_TM = 512
ROUTER_TM = 512
QKV_TM = 512
ATTN_GROUP = 2
FFN_ROWS = 1024
FFN_SUB = 256
FFN_TF = 512
FFN_TH = 512
FFN_VMEM_LIMIT = 60 * 1024 * 1024
ROW_DMA_PRIORITY = 1
COMBINE_LN_TM = 256


def _cparams(sem, vmem=VMEM_LIMIT):
    return pltpu.CompilerParams(dimension_semantics=sem, vmem_limit_bytes=vmem)


def _dot(a, b):
    return jnp.dot(a, b, preferred_element_type=F32)


def _dot_f32_rhs(a_bf16, b_f32):
    hi = b_f32.astype(BF16)
    lo = (b_f32 - hi.astype(F32)).astype(BF16)
    return _dot(a_bf16, hi) + _dot(a_bf16, lo)


def _dot_f32_lhs(a_f32, b_bf16):
    hi = a_f32.astype(BF16)
    lo = (a_f32 - hi.astype(F32)).astype(BF16)
    return _dot(hi, b_bf16) + _dot(lo, b_bf16)


def _sigmoid(x):
    return 1.0 / (1.0 + jnp.exp(-x))


def _log_sigmoid(x):
    return jnp.minimum(x, 0.0) - jnp.log(1.0 + jnp.exp(-jnp.abs(x)))


def _proj_kernel(x_ref, w_ref, wg_ref, o_ref, g_ref):
    x = x_ref[...].astype(BF16)
    o_ref[...] = _dot(x, w_ref[...]).astype(o_ref.dtype)

    @pl.when(pl.program_id(1) == 0)
    def _():
        g_ref[...] = _dot(x, wg_ref[...])


def _project(x, w, w_gate, tm, tn):
    m, k = x.shape
    n = w.shape[1]
    return pl.pallas_call(
        _proj_kernel,
        grid=(m // tm, n // tn),
        in_specs=[pl.BlockSpec((tm, k), lambda i, j: (i, 0)),
                  pl.BlockSpec((k, tn), lambda i, j: (0, j)),
                  pl.BlockSpec((k, LANES), lambda i, j: (0, 0))],
        out_specs=[pl.BlockSpec((tm, tn), lambda i, j: (i, j)),
                   pl.BlockSpec((tm, LANES), lambda i, j: (i, 0))],
        out_shape=[jax.ShapeDtypeStruct((m, n), BF16), jax.ShapeDtypeStruct((m, LANES), F32)],
        compiler_params=_cparams(("parallel", "arbitrary")),
        name="in_proj",
    )(x, w, w_gate)


def _qkv_proj_kernel(x_ref, w_ref, *refs):
    n_out = len(DILATED_CONFIGS)
    o_refs, s_refs = refs[:n_out], refs[n_out:]
    res = _dot(x_ref[...].astype(BF16), w_ref[...])
    ntile, _, tm, _ = s_refs[0].shape
    wid = ntile * LANES
    for c in range(ntile):
        s_refs[0][c, 0] = res[:, c * LANES:(c + 1) * LANES]
    prev_dil = 1
    for level, ((_, dil), o_ref) in enumerate(zip(DILATED_CONFIGS, o_refs)):
        if dil == 1:
            o_ref[...] = res.astype(o_ref.dtype)
            continue
        step = dil // prev_dil
        rows = tm // dil
        keep = level + 1 < n_out
        for rp in range(prev_dil):
            for q in range(step):
                r = rp + prev_dil * q
                for c in range(ntile):
                    piece = s_refs[level - 1][c, rp, pl.ds(q, rows, stride=step), :]
                    col = r * wid + c * LANES
                    o_ref[:, col:col + LANES] = piece.astype(o_ref.dtype)
                    if keep:
                        s_refs[level][c, r] = piece
        prev_dil = dil


def _qkv_project(x, w, tm):
    m, k = x.shape
    wid = w.shape[1]
    dils = [d for _, d in DILATED_CONFIGS]
    return pl.pallas_call(
        _qkv_proj_kernel,
        grid=(m // tm,),
        in_specs=[pl.BlockSpec((tm, k), lambda i: (i, 0)),
                  pl.BlockSpec((k, wid), lambda i: (0, 0), pipeline_mode=pl.Buffered(1))],
        out_specs=[pl.BlockSpec((tm // d, d * wid), lambda i: (i, 0)) for d in dils],
        out_shape=[jax.ShapeDtypeStruct((m // d, d * wid), BF16) for d in dils],
        scratch_shapes=[pltpu.VMEM((wid // LANES, d, tm // d, LANES), F32) for d in dils[:-1]],
        compiler_params=_cparams(("parallel",), 56 * 1024 * 1024),
        name="qkv_proj",
    )(x, w)


def _attn_kernel(q_ref, kp_ref, kc_ref, vp_ref, vc_ref, bias_ref, sel_ref, o_ref, lse_ref):
    n = pl.program_id(2)
    tab = jnp.minimum(n, 1)
    grp, dh, nk = ATTN_GROUP, HEAD_DIM_A, 2 * ATTN_BLOCK
    wid = grp * dh
    lane_head = lax.broadcasted_iota(jnp.int32, (nk, wid), 1) // dh
    zero = jnp.zeros((nk, wid), BF16)
    ones_bd = jnp.concatenate([jnp.where(lane_head == j, 1.0, 0.0).astype(BF16) for j in range(grp)], axis=0)
    q_lane_head = lax.broadcasted_iota(jnp.int32, (ATTN_BLOCK, wid), 1) // dh
    lse_wide = []
    for g in range(N_HEADS_A // grp):
        cols = slice(g * wid, (g + 1) * wid)
        q = q_ref[0, :, cols] * (dh ** -0.5)
        kslab = jnp.concatenate([kp_ref[0, :, cols], kc_ref[0, :, cols]], axis=0)
        vslab = jnp.concatenate([vp_ref[0, :, cols], vc_ref[0, :, cols]], axis=0)
        k_bd = jnp.concatenate([jnp.where(lane_head == j, kslab, zero) for j in range(grp)], axis=0)
        v_bd = jnp.concatenate([jnp.where(lane_head == j, vslab, zero) for j in range(grp)], axis=0)
        s_all = lax.dot_general(q, k_bd, (((1,), (1,)), ((), ())), preferred_element_type=F32)
        ps, ms = [], []
        for j in range(grp):
            s = s_all[:, j * nk:(j + 1) * nk] + bias_ref[tab, g * grp + j]
            m = jnp.max(s, axis=-1, keepdims=True)
            ps.append(jnp.exp(s - m).astype(BF16))
            ms.append(m)
        p_all = jnp.concatenate(ps, axis=-1)
        res = _dot(p_all, jnp.concatenate([v_bd, ones_bd], axis=-1))
        den = res[:, wid:]
        o_ref[0, :, cols] = (res[:, :wid] / den).astype(o_ref.dtype)
        m_wide = ms[grp - 1]
        for j in range(grp - 1):
            m_wide = jnp.where(q_lane_head == j, ms[j], m_wide)
        lse_wide.append(m_wide + jnp.log(den))
    lse_ref[0] = _dot_f32_lhs(jnp.concatenate(lse_wide, axis=-1), sel_ref[...])


def _attn_bias_tables(rel_bias, dil):
    blk = ATTN_BLOCK
    period = 3 * blk
    k = np.arange(period)
    valid = k <= blk
    dist = np.where(valid, blk - k, 0) * dil
    max_exact = NUM_BUCKETS // 2
    d_f = np.maximum(dist, 1).astype(np.float32)
    large = max_exact + (np.log(d_f / np.float32(max_exact)) / np.float32(math.log(MAX_DISTANCE / max_exact))
                         * np.float32(NUM_BUCKETS - max_exact)).astype(np.int32)
    large = np.minimum(large, NUM_BUCKETS - 1)
    bucket = np.where(dist < max_exact, dist, large).astype(np.int32)
    w = jnp.where(jnp.asarray(valid)[None, :], rel_bias[jnp.asarray(bucket)].T.astype(F32), NEG_INF)
    t1 = jnp.tile(w, (1, blk))[:, :blk * (period - 1)].reshape(N_HEADS_A, blk, period - 1)[:, :, :2 * blk]
    has_prev = np.arange(2 * blk)[None, None, :] >= blk
    t0 = jnp.where(jnp.asarray(has_prev), t1, NEG_INF)
    return jnp.stack([t0, t1])


def _dilated_attention(qkv, bias_tab, batch, seq, dil):
    blk = ATTN_BLOCK
    l = seq // dil
    nb = l // blk
    ncb = 3
    pv = qkv.reshape(batch, l, dil * ncb * D_ATTN)
    head_sel = jnp.asarray((np.arange(D_ATTN)[:, None] == np.arange(LANES)[None, :] * HEAD_DIM_A), BF16)

    def cur(c):
        return pl.BlockSpec((1, blk, D_ATTN), lambda b, r, n: (b, n, r * ncb + c))

    def prev(c):
        return pl.BlockSpec((1, blk, D_ATTN), lambda b, r, n: (b, jnp.maximum(n - 1, 0), r * ncb + c))

    o, lse = pl.pallas_call(
        _attn_kernel,
        grid=(batch, dil, nb),
        in_specs=[cur(0), prev(1), cur(1), prev(2), cur(2),
                  pl.BlockSpec((2, N_HEADS_A, blk, 2 * blk), lambda b, r, n: (0, 0, 0, 0)),
                  pl.BlockSpec((D_ATTN, LANES), lambda b, r, n: (0, 0))],
        out_specs=[pl.BlockSpec((1, blk, D_ATTN), lambda b, r, n: (b, n, r)),
                   pl.BlockSpec((1, blk, LANES), lambda b, r, n: (b, n, r))],
        out_shape=[jax.ShapeDtypeStruct((batch, l, dil * D_ATTN), BF16),
                   jax.ShapeDtypeStruct((batch, l, dil * LANES), F32)],
        compiler_params=_cparams(("parallel", "parallel", "arbitrary")),
        name=f"dilated_attn_d{dil}",
    )(pv, pv, pv, pv, pv, bias_tab, head_sel)
    return o.reshape(batch * l, dil * D_ATTN), lse.reshape(batch * l, dil * LANES)


def _attn_combine_kernel(o1_ref, o2_ref, o3_ref, l1_ref, l2_ref, l3_ref, e_ref, et_ref, beta_ref, y_ref,
                         osc_ref, lsc_ref):
    tm = y_ref.shape[0]
    ntile = D_ATTN // LANES
    lses, outs = [], []
    for slot, ((_, dil), o_ref, l_ref) in enumerate(zip(DILATED_CONFIGS, (o1_ref, o2_ref, o3_ref),
                                                        (l1_ref, l2_ref, l3_ref))):
        if dil == 1:
            lses.append(l_ref[...])
            outs.append(lambda o_ref=o_ref: o_ref[...].astype(F32))
            continue
        rows = tm // dil
        for r in range(dil):
            lsc_ref[slot, pl.ds(r, rows, stride=dil), :] = l_ref[:, r * LANES:(r + 1) * LANES]
            for c in range(ntile):
                col = r * D_ATTN + c * LANES
                osc_ref[slot, c, pl.ds(r, rows, stride=dil), :] = o_ref[:, col:col + LANES].astype(F32)
        lses.append(lsc_ref[slot])
        outs.append(lambda slot=slot: jnp.concatenate([osc_ref[slot, c] for c in range(ntile)], axis=-1))
    mx = jnp.maximum(jnp.maximum(lses[0], lses[1]), lses[2])
    ws = [jnp.exp(l - mx) for l in lses]
    tot = ws[0] + ws[1] + ws[2]
    e = e_ref[...]
    acc = None
    for w, load_o in zip(ws, outs):
        term = _dot_f32_lhs(w / tot, e) * load_o()
        acc = term if acc is None else acc + term
    ss = _dot_f32_lhs(acc * acc, et_ref[...])
    inv = lax.rsqrt(ss * (1.0 / HEAD_DIM_A) + HEAD_NORM_EPS)
    y_ref[...] = (acc * _dot_f32_lhs(inv, e) * beta_ref[...]).astype(y_ref.dtype)


def _attn_combine(os_, lses, beta_attn):
    n = os_[0].shape[0]
    tm = min(COMBINE_TM, n)
    head_of_lane = np.arange(D_ATTN) // HEAD_DIM_A
    e = (np.arange(LANES)[:, None] == head_of_lane[None, :]).astype(np.float32)
    e_j = jnp.asarray(e, BF16)
    et_j = jnp.asarray(e.T, BF16)
    dils = [d for _, d in DILATED_CONFIGS]
    dilated = lambda w: [pl.BlockSpec((tm // d, d * w), lambda i: (i, 0)) for d in dils]
    full = lambda a, b: pl.BlockSpec((a, b), lambda i: (0, 0))
    return pl.pallas_call(
        _attn_combine_kernel,
        grid=(n // tm,),
        in_specs=dilated(D_ATTN) + dilated(LANES) + [full(LANES, D_ATTN), full(D_ATTN, LANES), full(1, D_ATTN)],
        out_specs=pl.BlockSpec((tm, D_ATTN), lambda i: (i, 0)),
        out_shape=jax.ShapeDtypeStruct((n, D_ATTN), BF16),
        scratch_shapes=[pltpu.VMEM((len(dils), D_ATTN // LANES, tm, LANES), F32),
                        pltpu.VMEM((len(dils), tm, LANES), F32)],
        compiler_params=_cparams(("parallel",)),
        name="attn_combine",
    )(*os_, *lses, e_j, et_j, beta_attn.reshape(1, D_ATTN).astype(F32))


def _mlstm_kernel(qp_ref, kp_ref, qprev_ref, kprev_ref, v_ref, og_ref, g_ref, gb_ref, cw_ref, cb_ref,
                  beta_ref, y_ref, c_ref, n_ref, m_ref):
    step = pl.program_id(1)
    ch = MLSTM_CHUNK
    dh = HEAD_DIM_M

    @pl.when(step == 0)
    def _():
        c_ref[...] = jnp.zeros(c_ref.shape, F32)
        n_ref[...] = jnp.zeros(n_ref.shape, F32)
        m_ref[...] = jnp.zeros(m_ref.shape, F32)

    npad = qprev_ref.shape[1]
    shift_r = lax.broadcasted_iota(jnp.int32, ((CONV_K - 1) * ch, npad + ch), 0)
    shift_c = lax.broadcasted_iota(jnp.int32, ((CONV_K - 1) * ch, npad + ch), 1)
    shift_mat = jnp.where(shift_c == shift_r % ch + npad - 1 - shift_r // ch, 1.0, 0.0).astype(BF16)

    def conv_silu(x_ref, prev_ref, coff, b):
        xb = x_ref[b]
        pb = jnp.where(step > 0, prev_ref[b], jnp.zeros_like(prev_ref[b]))
        shifted = _dot(shift_mat, jnp.concatenate([pb, xb], axis=0))
        cols = slice(coff, coff + D_MLSTM)
        acc = cb_ref[:, cols] + cw_ref[CONV_K - 1:CONV_K, cols] * xb.astype(F32)
        for s in range(1, CONV_K):
            acc = acc + cw_ref[CONV_K - 1 - s:CONV_K - s, cols] * shifted[(s - 1) * ch:s * ch]
        return acc * _sigmoid(acc)

    row_i = lax.broadcasted_iota(jnp.int32, (ch, ch), 0)
    col_i = lax.broadcasted_iota(jnp.int32, (ch, ch), 1)
    causal = row_i >= col_i
    tri = jnp.where(causal, 1.0, 0.0).astype(BF16)
    upp = jnp.where(row_i <= col_i, 1.0, 0.0).astype(BF16)
    b = 0
    qf = conv_silu(qp_ref, qprev_ref, 0, b)
    kf = conv_silu(kp_ref, kprev_ref, D_MLSTM, b) * (dh ** -0.5)
    qb = qf.astype(BF16)
    kb = kf.astype(BF16)
    g = g_ref[b] + gb_ref[...]
    gt = g.T
    b_cols = _dot_f32_rhs(tri, _log_sigmoid(g))
    b_rows = _dot_f32_lhs(_log_sigmoid(gt), upp)

    for h in range(N_HEADS_M):
        st = h
        hs = slice(h * dh, (h + 1) * dh)
        fi = N_HEADS_M + h
        i_row = gt[h:h + 1, :]
        i_col = g[:, h:h + 1]
        b_row = b_rows[fi:fi + 1, :]
        b_col = b_cols[:, fi:fi + 1]
        m_prev = m_ref[st][:, 0:1]
        q_h, k_h = qb[:, hs], kb[:, hs]
        v_h = v_ref[b, :, hs]

        dmat = jnp.where(causal, b_col - b_row + i_row, NEG_INF)
        m_inter = b_col + m_prev
        m_t = jnp.maximum(m_inter, jnp.max(dmat, axis=-1, keepdims=True))
        w = jnp.exp(dmat - m_t) * lax.dot_general(q_h, k_h, (((1,), (1,)), ((), ())),
                                                  preferred_element_type=F32)
        decay = jnp.exp(m_inter - m_t)
        c_old = c_ref[st]
        inter = lax.dot_general(q_h, c_old.astype(BF16), (((1,), (1,)), ((), ())), preferred_element_type=F32)
        num = _dot(w.astype(BF16), v_h) + decay * inter
        n_old = n_ref[st]
        den = jnp.sum(w, axis=-1, keepdims=True) + decay * jnp.sum(qf[:, hs] * n_old, axis=-1, keepdims=True)
        hh = num / jnp.maximum(jnp.abs(den), jnp.exp(-m_t))

        g_last = b_col[ch - 1:ch, :]
        a_row = g_last - b_row + i_row
        a_col = g_last - b_col + i_col
        m_new = jnp.maximum(g_last + m_prev, jnp.max(a_row, axis=-1, keepdims=True))
        carry = jnp.exp(g_last + m_prev - m_new)
        wa_col = jnp.exp(a_col - m_new)
        wv = (wa_col * v_h.astype(F32)).astype(BF16)
        c_ref[st] = carry * c_old + lax.dot_general(wv, k_h, (((0,), (0,)), ((), ())), preferred_element_type=F32)
        n_ref[st] = carry * n_old + jnp.sum(wa_col * kf[:, hs], axis=0, keepdims=True)
        m_ref[st] = jnp.broadcast_to(m_new, (1, LANES))

        gated = _sigmoid(og_ref[b, :, hs].astype(F32)) * hh
        ms = jnp.sum(gated * gated, axis=-1, keepdims=True) * (1.0 / dh)
        y_ref[b, :, hs] = (gated * lax.rsqrt(ms + HEAD_NORM_EPS) * beta_ref[:, hs]).astype(y_ref.dtype)


def _mlstm(proj, gates, gate_bias, conv_w, conv_b, beta_mlstm, batch, seq):
    ch = MLSTM_CHUNK
    nchunk = seq // ch
    pv = proj.reshape(batch, seq, 4 * D_MLSTM)
    gv = gates.reshape(batch, seq, LANES)
    prev_rows = 16
    cb0 = 0

    def cur(c):
        return pl.BlockSpec((1, ch, D_MLSTM), lambda b, n: (b, n, c))

    def prev(c):
        per = ch // prev_rows
        return pl.BlockSpec((1, prev_rows, D_MLSTM), lambda b, n: (b, jnp.maximum(n * per - 1, 0), c))

    const = lambda a, b_: pl.BlockSpec((a, b_), lambda b, n: (0, 0))
    y = pl.pallas_call(
        _mlstm_kernel,
        grid=(batch, nchunk),
        in_specs=[cur(cb0), cur(cb0 + 1), prev(cb0), prev(cb0 + 1), cur(cb0 + 2), cur(cb0 + 3),
                  pl.BlockSpec((1, ch, LANES), lambda b, n: (b, n, 0)),
                  const(1, LANES), const(CONV_K, 2 * D_MLSTM), const(1, 2 * D_MLSTM), const(1, D_MLSTM)],
        out_specs=pl.BlockSpec((1, ch, D_MLSTM), lambda b, n: (b, n, 0)),
        out_shape=jax.ShapeDtypeStruct((batch, seq, D_MLSTM), BF16),
        scratch_shapes=[pltpu.VMEM((N_HEADS_M, HEAD_DIM_M, HEAD_DIM_M), F32),
                        pltpu.VMEM((N_HEADS_M, 1, HEAD_DIM_M), F32),
                        pltpu.VMEM((N_HEADS_M, 1, LANES), F32)],
        compiler_params=_cparams(("parallel", "arbitrary")),
        name="mlstm",
    )(pv, pv, pv, pv, pv, pv, gv, gate_bias, conv_w.astype(F32), conv_b.reshape(1, -1).astype(F32),
      beta_mlstm.reshape(1, D_MLSTM).astype(F32))
    return y.reshape(batch * seq, D_MLSTM)


def _layer_norm(z, g, b):
    mu = jnp.mean(z, axis=-1, keepdims=True)
    zc = z - mu
    var = jnp.mean(zc * zc, axis=-1, keepdims=True)
    return zc * lax.rsqrt(var + LN_EPS) * g + b


PACK_ROWS = D_MODEL // (2 * LANES)


def _store_packed_rows(dst_ref, x, first=0):
    rows = x.shape[0]
    half = D_MODEL // 2
    for s in range(PACK_ROWS):
        lo = x[:, s * LANES:(s + 1) * LANES].astype(BF16).astype(F32)
        hi = x[:, half + s * LANES:half + (s + 1) * LANES].astype(BF16).astype(F32)
        word = pltpu.bitcast(hi, jnp.int32) | lax.shift_right_logical(pltpu.bitcast(lo, jnp.int32), 16)
        dst_ref[pl.ds(first * PACK_ROWS + s, rows, stride=PACK_ROWS), :] = word


def _load_packed_rows(src_ref, first, rows):
    los, his = [], []
    for s in range(PACK_ROWS):
        word = src_ref[pl.ds(first * PACK_ROWS + s, rows, stride=PACK_ROWS), :]
        los.append(pltpu.bitcast(lax.shift_left(word, 16), F32))
        his.append(pltpu.bitcast(word & jnp.int32(-65536), F32))
    return los, his


def _outproj_kernel(ya_ref, ym_ref, x_ref, w_ref, g_ref, b_ref, h_ref, hp_ref):
    y = _dot(ya_ref[...], w_ref[0:D_ATTN, :]) + _dot(ym_ref[...], w_ref[D_ATTN:D_MODEL, :])
    h = _layer_norm(DN_ALPHA * x_ref[...] + y, g_ref[...], b_ref[...])
    h_ref[...] = h
    _store_packed_rows(hp_ref, h)


def _outproj_ln(ya, ym, x, w_out, g, b):
    n = x.shape[0]
    tm = min(OUTPROJ_TM, n)
    row = lambda w: pl.BlockSpec((tm, w), lambda i: (i, 0))
    full = lambda a, b_: pl.BlockSpec((a, b_), lambda i: (0, 0))
    return pl.pallas_call(
        _outproj_kernel,
        grid=(n // tm,),
        in_specs=[row(D_ATTN), row(D_MLSTM), row(D_MODEL), full(D_MODEL, D_MODEL), full(1, D_MODEL), full(1, D_MODEL)],
        out_specs=[row(D_MODEL), pl.BlockSpec((tm * PACK_ROWS, LANES), lambda i: (i, 0))],
        out_shape=[jax.ShapeDtypeStruct((n, D_MODEL), F32),
                   jax.ShapeDtypeStruct((n * PACK_ROWS, LANES), jnp.int32)],
        compiler_params=_cparams(("parallel",)),
        name="out_proj_ln",
    )(ya, ym, x, w_out, g.reshape(1, -1), b.reshape(1, -1))


def _router_kernel(h_ref, whi_ref, wlo_ref, b_ref, tri_ref, idx_ref, gate_ref, rank_ref, cnt_ref, carry_ref):
    i = pl.program_id(0)

    @pl.when(i == 0)
    def _():
        carry_ref[...] = jnp.zeros(carry_ref.shape, F32)

    x = h_ref[...]
    xhi = x.astype(BF16)
    xlo = (x - xhi.astype(F32)).astype(BF16)
    logits = _dot(xhi, whi_ref[...]) + _dot(xhi, wlo_ref[...]) + _dot(xlo, whi_ref[...]) + b_ref[...]
    tm = logits.shape[0]
    lane = lax.broadcasted_iota(jnp.int32, (tm, LANES), 1)
    lane_f = lane.astype(F32)
    vals = jnp.where(lane < N_EXPERTS, logits, NEG_INF)

    sels, tops = [], []
    for _ in range(TOP_K):
        mx = jnp.max(vals, axis=-1, keepdims=True)
        first = jnp.min(jnp.where(vals == mx, lane_f, float(LANES)), axis=-1, keepdims=True)
        sel = lane_f == first
        sels.append(sel)
        tops.append((mx, first))
        vals = jnp.where(sel, 2.0 * NEG_INF, vals)

    exps = [jnp.exp(mx - tops[0][0]) for mx, _ in tops]
    tot = exps[0] + exps[1] + exps[2] + exps[3]

    onehot = jnp.zeros((tm, LANES), F32)
    for sel in sels:
        onehot = jnp.where(sel, 1.0, onehot)
    before = _dot(tri_ref[...], onehot.astype(BF16)) + carry_ref[...]

    idx_out = jnp.zeros((tm, LANES), F32)
    gate_out = jnp.zeros((tm, LANES), F32)
    rank_out = jnp.zeros((tm, LANES), F32)
    for k in range(TOP_K):
        rank_k = jnp.sum(jnp.where(sels[k], before, 0.0), axis=-1, keepdims=True)
        idx_out = jnp.where(lane == k, tops[k][1], idx_out)
        gate_out = jnp.where(lane == k, exps[k] / tot, gate_out)
        rank_out = jnp.where(lane == k, rank_k, rank_out)
    idx_ref[...] = idx_out.astype(jnp.int32)
    gate_ref[...] = gate_out
    rank_ref[...] = rank_out.astype(jnp.int32)

    carry = carry_ref[...] + jnp.sum(onehot, axis=0, keepdims=True)
    carry_ref[...] = carry
    cnt_ref[...] = carry.astype(jnp.int32)


def _router(h1, w_router, b_router):
    n = h1.shape[0]
    tm = min(ROUTER_TM, n)
    wpad = jnp.zeros((D_MODEL, LANES), F32).at[:, :N_EXPERTS].set(w_router)
    whi = wpad.astype(BF16)
    wlo = (wpad - whi.astype(F32)).astype(BF16)
    bpad = jnp.zeros((1, LANES), F32).at[0, :N_EXPERTS].set(b_router)
    tri = jnp.asarray(np.tril(np.ones((tm, tm), np.float32), -1), BF16)
    row = lambda w: pl.BlockSpec((tm, w), lambda i: (i, 0))
    full = lambda a, b_: pl.BlockSpec((a, b_), lambda i: (0, 0))
    return pl.pallas_call(
        _router_kernel,
        grid=(n // tm,),
        in_specs=[row(D_MODEL), full(D_MODEL, LANES), full(D_MODEL, LANES), full(1, LANES), full(tm, tm)],
        out_specs=[row(LANES), row(LANES), row(LANES), full(1, LANES)],
        out_shape=[jax.ShapeDtypeStruct((n, LANES), jnp.int32), jax.ShapeDtypeStruct((n, LANES), F32),
                   jax.ShapeDtypeStruct((n, LANES), jnp.int32), jax.ShapeDtypeStruct((1, LANES), jnp.int32)],
        scratch_shapes=[pltpu.VMEM((1, LANES), F32)],
        compiler_params=_cparams(("arbitrary",)),
        name="router",
    )(h1, whi, wlo, bpad, tri)


def _ffn_kernel(bexp_ref, nused_ref, nvalid_ref, slot_ref, slot_prv_ref, src_cur_ref, src_nxt_ref, wg_ref, wu_ref,
                wdl_ref, wdh_ref, bup_ref, bdn_ref, xp_ref, ys_ref, xbuf_ref, x_ref, act_ref, obuf_ref, wgb_ref,
                wub_ref, wdlb_ref, wdhb_ref, in_sem, out_sem):
    i = pl.program_id(0)
    g = pl.program_id(1)
    n_up = act_ref.shape[0]
    expert = bexp_ref[i]
    n_dn = bdn_ref.shape[1] // 2
    last_g = pl.num_programs(1) - 1
    last_i = pl.num_programs(0) - 1
    n_used = nused_ref[0]
    used = i < n_used
    rb = x_ref.shape[0]
    sub = FFN_SUB
    th = wdlb_ref.shape[1]
    unroll = 8

    def sub_blocks(blk):
        return (nvalid_ref[blk] + (sub - 1)) // sub

    nsub = sub_blocks(i)

    def start_gather(src_ref, lo, hi):
        def body(j, carry):
            for u in range(unroll):
                jj = j * unroll + u
                src = pl.multiple_of(src_ref[jj], PACK_ROWS)
                dst = pl.multiple_of(jj * PACK_ROWS, PACK_ROWS)
                pltpu.make_async_copy(xp_ref.at[pl.ds(src, PACK_ROWS)], xbuf_ref.at[pl.ds(dst, PACK_ROWS)],
                                      in_sem).start(priority=ROW_DMA_PRIORITY)
            return carry

        lax.fori_loop(lo // unroll, hi // unroll, body, 0)

    def wait_gather(nrows):
        npk = pl.multiple_of(nrows * PACK_ROWS, PACK_ROWS)
        pltpu.make_async_copy(xp_ref.at[pl.ds(0, npk)], xbuf_ref.at[pl.ds(0, npk)], in_sem).wait()

    def start_scatter(slots_ref, lo, hi):
        def one(jj):
            src = pl.multiple_of(jj * PACK_ROWS, PACK_ROWS)
            dst = pl.multiple_of(slots_ref[jj] * PACK_ROWS, PACK_ROWS)
            pltpu.make_async_copy(obuf_ref.at[pl.ds(src, PACK_ROWS)], ys_ref.at[pl.ds(dst, PACK_ROWS)],
                                  out_sem).start(priority=ROW_DMA_PRIORITY)

        def body(j, carry):
            for u in range(unroll):
                one(j * unroll + u)
            return carry

        nfull = jnp.maximum(hi - lo, 0) // unroll
        lax.fori_loop(lo // unroll, lo // unroll + nfull, body, 0)
        for u in range(unroll - 1):
            @pl.when(lo + nfull * unroll + u < hi)
            def _():
                one(lo + nfull * unroll + u)

    def wait_scatter(blk):
        npk = pl.multiple_of(nvalid_ref[blk] * PACK_ROWS, PACK_ROWS)
        pltpu.make_async_copy(obuf_ref.at[pl.ds(0, npk)], ys_ref.at[pl.ds(0, npk)], out_sem).wait()

    @pl.when(jnp.logical_and(used, g == 0))
    def _():
        @pl.when(i == 0)
        def _():
            start_gather(src_cur_ref, 0, nsub * sub)

        wait_gather(nsub * sub)
        for j in range(rb // sub):
            @pl.when(j < nsub)
            def _():
                los, his = _load_packed_rows(xbuf_ref, j * sub, sub)
                rows = slice(j * sub, (j + 1) * sub)
                for s in range(PACK_ROWS):
                    x_ref[rows, s * LANES:(s + 1) * LANES] = los[s].astype(BF16)
                    x_ref[rows, D_MODEL // 2 + s * LANES:D_MODEL // 2 + (s + 1) * LANES] = his[s].astype(BF16)

    @pl.when(jnp.logical_and(jnp.logical_and(g >= 1, g - 1 < sub_blocks(jnp.minimum(i + 1, last_i))), i + 1 < n_used))
    def _():
        start_gather(src_nxt_ref, (g - 1) * sub, g * sub)

    @pl.when(jnp.logical_and(jnp.logical_and(used, i > 0), g + 1 < rb // sub))
    def _():
        start_scatter(slot_prv_ref, (g + 1) * sub, jnp.minimum((g + 2) * sub, nvalid_ref[i - 1]))

    @pl.when(jnp.logical_and(used, g < n_up))
    def _():
        wgb_ref[...] = wg_ref[0].astype(BF16)
        wub_ref[...] = wu_ref[0].astype(BF16)

        def sub_body(j, carry):
            rows = pl.ds(pl.multiple_of(j * sub, sub), sub)
            x = x_ref[rows, :]
            hg = _dot(x, wgb_ref[...]) + bup_ref[expert, pl.ds(g, 1), :]
            hu = _dot(x, wub_ref[...]) + bup_ref[expert, pl.ds(n_up + g, 1), :]
            gate = jnp.minimum(hg, SWIGLU_LIMIT)
            up = jnp.clip(hu, -SWIGLU_LIMIT, SWIGLU_LIMIT)
            act_ref[g, rows, :] = ((up + 1.0) * (gate * _sigmoid(SWIGLU_ALPHA * gate))).astype(BF16)
            return carry

        lax.fori_loop(0, nsub, sub_body, 0)

    @pl.when(jnp.logical_and(used, g >= n_up))
    def _():
        @pl.when(jnp.logical_and(g == n_up, i > 0))
        def _():
            wait_scatter(i - 1)

        wdlb_ref[...] = wdl_ref[0].astype(BF16)
        wdhb_ref[...] = wdh_ref[0].astype(BF16)
        tile0 = (g - n_up) * (th // LANES)

        def sub_body(j, carry):
            row0 = pl.multiple_of(j * sub, sub)
            a = jnp.concatenate([act_ref[c, pl.ds(row0, sub), :] for c in range(n_up)], axis=-1)
            ylo = _dot(a, wdlb_ref[...]) + bdn_ref[expert, pl.ds(g - n_up, 1), :]
            yhi = _dot(a, wdhb_ref[...]) + bdn_ref[expert, pl.ds(n_dn + g - n_up, 1), :]
            for s in range(th // LANES):
                lo = ylo[:, s * LANES:(s + 1) * LANES].astype(BF16).astype(F32)
                hi = yhi[:, s * LANES:(s + 1) * LANES].astype(BF16).astype(F32)
                word = pltpu.bitcast(hi, jnp.int32) | lax.shift_right_logical(pltpu.bitcast(lo, jnp.int32), 16)
                obuf_ref[pl.ds(row0 * PACK_ROWS + tile0 + s, sub, stride=PACK_ROWS), :] = word
            return carry

        lax.fori_loop(0, nsub, sub_body, 0)

        @pl.when(g == last_g)
        def _():
            nv = nvalid_ref[i]
            start_scatter(slot_ref, 0, jnp.where(i + 1 < n_used, jnp.minimum(sub, nv), nv))

    @pl.when(jnp.logical_and(i == last_i, g == last_g))
    def _():
        wait_scatter(jnp.minimum(i, n_used - 1))


def _expert_ffn(h_packed, slot_assign, blk_exp, blk_valid, n_used, w_up, b_up, w_down, b_down):
    rb, tf, th = FFN_ROWS, FFN_TF, FFN_TH
    nblk = slot_assign.shape[0] // rb
    n_up = D_FF // tf
    n_dn = D_MODEL // 2 // th
    n_assign = h_packed.shape[0] // PACK_ROWS * TOP_K
    n_tok = h_packed.shape[0] // PACK_ROWS
    slot_src = jnp.maximum(slot_assign, 0) % n_tok * PACK_ROWS

    def clamp(i, nu):
        return jnp.minimum(i, nu[0] - 1)

    def up_sel(i, g, nu):
        return jnp.where(i < nu[0], jnp.minimum(g, n_up - 1), n_up - 1)

    def dn_sel(i, g, nu):
        return jnp.where(i < nu[0], jnp.clip(g - n_up, 0, n_dn - 1), n_dn - 1)

    def dn_exp(i, g, be, nu):
        early = jnp.logical_and(g < n_up - 1, i < nu[0])
        return be[jnp.where(early, jnp.maximum(clamp(i, nu) - 1, 0), clamp(i, nu))]

    def dn_col(i, g, nu):
        early = jnp.logical_and(g < n_up - 1, i < nu[0])
        return jnp.where(early, n_dn - 1, dn_sel(i, g, nu))

    grid_spec = pltpu.PrefetchScalarGridSpec(
        num_scalar_prefetch=3,
        grid=(nblk, n_up + n_dn),
        in_specs=[
            pl.BlockSpec((rb,), lambda i, g, be, nu, *_: (clamp(i, nu),), memory_space=pltpu.SMEM),
            pl.BlockSpec((rb,), lambda i, g, be, nu, *_: (jnp.maximum(clamp(i, nu) - 1, 0),), memory_space=pltpu.SMEM),
            pl.BlockSpec((rb,), lambda i, g, be, nu, *_: (clamp(i, nu),), memory_space=pltpu.SMEM),
            pl.BlockSpec((rb,), lambda i, g, be, nu, *_: (clamp(i + 1, nu),), memory_space=pltpu.SMEM),
            pl.BlockSpec((1, D_MODEL, tf), lambda i, g, be, nu, *_: (be[clamp(i, nu)], 0, up_sel(i, g, nu))),
            pl.BlockSpec((1, D_MODEL, tf), lambda i, g, be, nu, *_: (be[clamp(i, nu)], 0, n_up + up_sel(i, g, nu))),
            pl.BlockSpec((1, D_FF, th), lambda i, g, be, nu, *_: (dn_exp(i, g, be, nu), 0, dn_col(i, g, nu))),
            pl.BlockSpec((1, D_FF, th), lambda i, g, be, nu, *_: (dn_exp(i, g, be, nu), 0, n_dn + dn_col(i, g, nu))),
            pl.BlockSpec((N_EXPERTS, 2 * n_up, tf), lambda i, g, *_: (0, 0, 0)),
            pl.BlockSpec((N_EXPERTS, 2 * n_dn, th), lambda i, g, *_: (0, 0, 0)),
            pl.BlockSpec(memory_space=pl.ANY),
        ],
        out_specs=pl.BlockSpec(memory_space=pl.ANY),
        scratch_shapes=[pltpu.VMEM((rb * PACK_ROWS, LANES), jnp.int32),
                        pltpu.VMEM((rb, D_MODEL), BF16),
                        pltpu.VMEM((n_up, rb, tf), BF16),
                        pltpu.VMEM((rb * PACK_ROWS, LANES), jnp.int32),
                        pltpu.VMEM((D_MODEL, tf), BF16), pltpu.VMEM((D_MODEL, tf), BF16),
                        pltpu.VMEM((D_FF, th), BF16), pltpu.VMEM((D_FF, th), BF16),
                        pltpu.SemaphoreType.DMA, pltpu.SemaphoreType.DMA],
    )
    assert n_up >= rb // FFN_SUB - 1 and n_up + n_dn > rb // FFN_SUB, "row DMA bursts are spread over the grid steps"
    return pl.pallas_call(
        _ffn_kernel,
        grid_spec=grid_spec,
        out_shape=jax.ShapeDtypeStruct((n_assign * PACK_ROWS, LANES), jnp.int32),
        compiler_params=pltpu.CompilerParams(dimension_semantics=("arbitrary", "arbitrary"),
                                             vmem_limit_bytes=FFN_VMEM_LIMIT, disable_bounds_checks=True),
        name="expert_ffn",
    )(blk_exp, n_used, blk_valid, slot_assign, slot_assign, slot_src, slot_src, w_up, w_up, w_down, w_down,
      b_up.reshape(N_EXPERTS, 2 * n_up, tf), b_down.reshape(N_EXPERTS, 2 * n_dn, th), h_packed)


def _combine_kernel(gate_ref, h_ref, g_ref, b_ref, ys_ref, o_ref):
    tm = h_ref.shape[0]
    gates = gate_ref[...]
    cols = [None] * (2 * PACK_ROWS)
    for k in range(TOP_K):
        gk = gates[:, k:k + 1]
        for s in range(PACK_ROWS):
            word = ys_ref[k, pl.ds(s, tm, stride=PACK_ROWS), :]
            lo = pltpu.bitcast(lax.shift_left(word, 16), F32)
            hi = pltpu.bitcast(word & jnp.int32(-65536), F32)
            for c, blk in ((s, lo), (PACK_ROWS + s, hi)):
                cols[c] = gk * blk if cols[c] is None else cols[c] + gk * blk
    z = DN_ALPHA * h_ref[...] + jnp.concatenate(cols, axis=-1)
    o_ref[...] = _layer_norm(z, g_ref[...], b_ref[...])


def _combine_ln(gates, h1, ys, g, b):
    n = h1.shape[0]
    tm = min(COMBINE_LN_TM, n)
    row = lambda w: pl.BlockSpec((tm, w), lambda i: (i, 0))
    full = lambda a, b_: pl.BlockSpec((a, b_), lambda i: (0, 0))
    return pl.pallas_call(
        _combine_kernel,
        grid=(n // tm,),
        in_specs=[row(LANES), row(D_MODEL), full(1, D_MODEL), full(1, D_MODEL),
                  pl.BlockSpec((TOP_K, tm * PACK_ROWS, LANES), lambda i: (0, i, 0))],
        out_specs=row(D_MODEL),
        out_shape=jax.ShapeDtypeStruct((n, D_MODEL), F32),
        compiler_params=_cparams(("parallel",)),
        name="combine_ln",
    )(gates, h1, g.reshape(1, -1), b.reshape(1, -1), ys.reshape(TOP_K, n * PACK_ROWS, LANES))


def _layer(h, w_in, b_igate, b_fgate, conv_w, conv_b, rel_bias, beta_attn, beta_mlstm, w_out,
           ln1_g, ln1_b, w_router, b_router, w_up, b_up, w_down, b_down, ln2_g, ln2_b, batch, seq):
    n = batch * seq
    w_in, l = w_in
    w_qkv = w_in[l, :, :3 * D_ATTN].astype(BF16)
    w_mix = w_in[l, :, 3 * D_ATTN:MAIN_COLS].astype(BF16)
    w_gate = jnp.zeros((D_MODEL, LANES), BF16).at[:, :2 * N_HEADS_M].set(w_in[l, :, MAIN_COLS:].astype(BF16))
    qkvs = _qkv_project(h, w_qkv, min(QKV_TM, n))
    proj_m, gates = _project(h, w_mix, w_gate, min(PROJ_TM, n), PROJ_TN)
    gate_bias = jnp.zeros((1, LANES), F32).at[0, :2 * N_HEADS_M].set(jnp.concatenate([b_igate, b_fgate]))

    outs, lses = [], []
    for (_, dil), qkv in zip(DILATED_CONFIGS, qkvs):
        o, l = _dilated_attention(qkv, _attn_bias_tables(rel_bias, dil), batch, seq, dil)
        outs.append(o)
        lses.append(l)
    y_attn = _attn_combine(outs, lses, beta_attn)
    y_mlstm = _mlstm(proj_m, gates, gate_bias, conv_w, conv_b, beta_mlstm, batch, seq)

    h1, h1_packed = _outproj_ln(y_attn, y_mlstm, h, w_out.astype(BF16), ln1_g, ln1_b)

    top_idx, top_gate, rank, counts = _router(h1, w_router, b_router)
    counts = counts[0, :N_EXPERTS]
    nb = (counts + FFN_ROWS - 1) // FFN_ROWS
    per = (counts + nb * FFN_SUB - 1) // jnp.maximum(nb * FFN_SUB, 1) * FFN_SUB
    blk_end = jnp.cumsum(nb)
    blk_start = blk_end - nb
    e_idx, r_idx = top_idx[:, :TOP_K], rank[:, :TOP_K]
    onehot = (e_idx.reshape(-1, 1) == jnp.arange(N_EXPERTS, dtype=jnp.int32)[None, :]).astype(F32)
    table = jnp.stack([jnp.maximum(per, 1), blk_start], axis=1).astype(F32)
    looked = jnp.dot(onehot, table, precision=lax.Precision.HIGHEST)
    per_a, start_a = looked[:, 0], looked[:, 1]
    r_f = r_idx.reshape(-1).astype(F32)
    blk_in_e = jnp.floor((r_f + 0.5) / per_a)
    dest = ((start_a + blk_in_e) * FFN_ROWS + (r_f - blk_in_e * per_a)).astype(jnp.int32)
    nblk = n * TOP_K // FFN_ROWS + N_EXPERTS
    blk_id = jnp.arange(nblk, dtype=jnp.int32)
    blk_exp = jnp.minimum(jnp.sum(blk_end[None, :] <= blk_id[:, None], axis=1), N_EXPERTS - 1).astype(jnp.int32)
    n_used = blk_end[-1:].astype(jnp.int32)
    blk_valid = jnp.clip(counts[blk_exp] - (blk_id - blk_start[blk_exp]) * per[blk_exp], 0,
                         per[blk_exp]).astype(jnp.int32)
    order = jnp.argsort(dest).astype(jnp.int32)
    sorted_assign = (order % TOP_K) * n + order // TOP_K
    first = jnp.cumsum(blk_valid) - blk_valid
    row_in_blk = jnp.tile(jnp.arange(FFN_ROWS, dtype=jnp.int32), nblk)
    first_rep = jnp.repeat(first, FFN_ROWS)
    valid_rep = jnp.repeat(blk_valid, FFN_ROWS)
    picked = sorted_assign[jnp.minimum(first_rep + row_in_blk, n * TOP_K - 1)]
    slot_assign = jnp.where(row_in_blk < valid_rep, picked, -1)

    ys = _expert_ffn(h1_packed, slot_assign, blk_exp, blk_valid, n_used, w_up, b_up, w_down, b_down)
    return _combine_ln(top_gate, h1, ys, ln2_g, ln2_b)


def kernel(x, w_in, b_igate, b_fgate, conv_w, conv_b, rel_bias, beta_attn, beta_mlstm, w_out, ln1_g, ln1_b,
           w_router, b_router, w_up, b_up, w_down, b_down, ln2_g, ln2_b):
    batch, seq, d = x.shape
    h = x.reshape(batch * seq, d)
    for l in range(DEPTH):
        h = _layer(h, (w_in, l), b_igate[l], b_fgate[l], conv_w[l], conv_b[l], rel_bias, beta_attn[l], beta_mlstm[l],
                   w_out[l], ln1_g[l], ln1_b[l], w_router[l], b_router[l], w_up[l], b_up[l], w_down[l], b_down[l],
                   ln2_g[l], ln2_b[l], batch, seq)
    return h.reshape(batch, seq, d)
```

```python
import functools
import math

import numpy as np
import jax
import jax.numpy as jnp
from jax import lax
from jax.experimental import pallas as pl
from jax.experimental.pallas import tpu as pltpu

F32 = jnp.float32
BF16 = jnp.bfloat16

D_MODEL = 2048
D_ATTN = 1024
HEAD_DIM_A = 64
N_HEADS_A = 16
DILATED_CONFIGS = ((128, 1), (512, 4), (2048, 16))
ATTN_BLOCK = 128
NUM_BUCKETS = 32
MAX_DISTANCE = 2048
D_MLSTM = 1024
N_HEADS_M = 4
HEAD_DIM_M = 256
CONV_K = 4
MLSTM_CHUNK = 128
MAIN_COLS = 3 * D_ATTN + 4 * D_MLSTM
N_EXPERTS = 32
TOP_K = 4
D_FF = 2048
SWIGLU_LIMIT = 7.0
SWIGLU_ALPHA = 1.702
DEPTH = 1
DN_ALPHA = (2 * DEPTH) ** 0.25
LN_EPS = 1e-5
HEAD_NORM_EPS = 1e-6
NEG_INF = -1e30

LANES = 128
VMEM_LIMIT = 48 * 1024 * 1024

PROJ_TM = 1024
PROJ_TN = 1024
COMBINE_TM = 512
OUTPROJ_TM = 512
ROUTER_TM = 512
QKV_TM = 512
ATTN_GROUP = 2
FFN_ROWS = 1024
FFN_SUB = 256
FFN_TF = 512
FFN_TH = 512
FFN_VMEM_LIMIT = 60 * 1024 * 1024
ROW_DMA_PRIORITY = 1
COMBINE_LN_TM = 512


def _cparams(sem, vmem=VMEM_LIMIT):
    return pltpu.CompilerParams(dimension_semantics=sem, vmem_limit_bytes=vmem)


def _dot(a, b):
    return jnp.dot(a, b, preferred_element_type=F32)


def _dot_f32_rhs(a_bf16, b_f32):
    hi = b_f32.astype(BF16)
    lo = (b_f32 - hi.astype(F32)).astype(BF16)
    return _dot(a_bf16, hi) + _dot(a_bf16, lo)


def _dot_f32_lhs(a_f32, b_bf16):
    hi = a_f32.astype(BF16)
    lo = (a_f32 - hi.astype(F32)).astype(BF16)
    return _dot(hi, b_bf16) + _dot(lo, b_bf16)


def _sigmoid(x):
    return 1.0 / (1.0 + jnp.exp(-x))


def _log_sigmoid(x):
    return jnp.minimum(x, 0.0) - jnp.log(1.0 + jnp.exp(-jnp.abs(x)))


def _proj_kernel(x_ref, w_ref, wg_ref, o_ref, g_ref):
    x = x_ref[...].astype(BF16)
    o_ref[...] = _dot(x, w_ref[...]).astype(o_ref.dtype)

    @pl.when(pl.program_id(1) == 0)
    def _():
        g_ref[...] = _dot(x, wg_ref[...])


def _project(x, w, w_gate, tm, tn):
    m, k = x.shape
    n = w.shape[1]
    return pl.pallas_call(
        _proj_kernel,
        grid=(m // tm, n // tn),
        in_specs=[pl.BlockSpec((tm, k), lambda i, j: (i, 0)),
                  pl.BlockSpec((k, tn), lambda i, j: (0, j)),
                  pl.BlockSpec((k, LANES), lambda i, j: (0, 0))],
        out_specs=[pl.BlockSpec((tm, tn), lambda i, j: (i, j)),
                   pl.BlockSpec((tm, LANES), lambda i, j: (i, 0))],
        out_shape=[jax.ShapeDtypeStruct((m, n), BF16), jax.ShapeDtypeStruct((m, LANES), F32)],
        compiler_params=_cparams(("parallel", "arbitrary")),
        name="in_proj",
    )(x, w, w_gate)


def _qkv_proj_kernel(x_ref, w_ref, *refs):
    n_out = len(DILATED_CONFIGS)
    o_refs, s_refs = refs[:n_out], refs[n_out:]
    res = _dot(x_ref[...].astype(BF16), w_ref[...])
    ntile, _, tm, _ = s_refs[0].shape
    wid = ntile * LANES
    for c in range(ntile):
        s_refs[0][c, 0] = res[:, c * LANES:(c + 1) * LANES]
    prev_dil = 1
    for level, ((_, dil), o_ref) in enumerate(zip(DILATED_CONFIGS, o_refs)):
        if dil == 1:
            o_ref[...] = res.astype(o_ref.dtype)
            continue
        step = dil // prev_dil
        rows = tm // dil
        keep = level + 1 < n_out
        for rp in range(prev_dil):
            for q in range(step):
                r = rp + prev_dil * q
                for c in range(ntile):
                    piece = s_refs[level - 1][c, rp, pl.ds(q, rows, stride=step), :]
                    col = r * wid + c * LANES
                    o_ref[:, col:col + LANES] = piece.astype(o_ref.dtype)
                    if keep:
                        s_refs[level][c, r] = piece
        prev_dil = dil


def _qkv_project(x, w, tm):
    m, k = x.shape
    wid = w.shape[1]
    dils = [d for _, d in DILATED_CONFIGS]
    return pl.pallas_call(
        _qkv_proj_kernel,
        grid=(m // tm,),
        in_specs=[pl.BlockSpec((tm, k), lambda i: (i, 0)),
                  pl.BlockSpec((k, wid), lambda i: (0, 0), pipeline_mode=pl.Buffered(1))],
        out_specs=[pl.BlockSpec((tm // d, d * wid), lambda i: (i, 0)) for d in dils],
        out_shape=[jax.ShapeDtypeStruct((m // d, d * wid), BF16) for d in dils],
        scratch_shapes=[pltpu.VMEM((wid // LANES, d, tm // d, LANES), F32) for d in dils[:-1]],
        compiler_params=_cparams(("parallel",), 56 * 1024 * 1024),
        name="qkv_proj",
    )(x, w)


def _attn_kernel(q_ref, kp_ref, kc_ref, vp_ref, vc_ref, bias_ref, sel_ref, o_ref, lse_ref):
    n = pl.program_id(2)
    tab = jnp.minimum(n, 1)
    grp, dh, nk = ATTN_GROUP, HEAD_DIM_A, 2 * ATTN_BLOCK
    wid = grp * dh
    lane_head = lax.broadcasted_iota(jnp.int32, (nk, wid), 1) // dh
    zero = jnp.zeros((nk, wid), BF16)
    ones_bd = jnp.concatenate([jnp.where(lane_head == j, 1.0, 0.0).astype(BF16) for j in range(grp)], axis=0)
    q_lane_head = lax.broadcasted_iota(jnp.int32, (ATTN_BLOCK, wid), 1) // dh
    lse_wide = []
    for g in range(N_HEADS_A // grp):
        cols = slice(g * wid, (g + 1) * wid)
        q = q_ref[0, :, cols] * (dh ** -0.5)
        kslab = jnp.concatenate([kp_ref[0, :, cols], kc_ref[0, :, cols]], axis=0)
        vslab = jnp.concatenate([vp_ref[0, :, cols], vc_ref[0, :, cols]], axis=0)
        k_bd = jnp.concatenate([jnp.where(lane_head == j, kslab, zero) for j in range(grp)], axis=0)
        v_bd = jnp.concatenate([jnp.where(lane_head == j, vslab, zero) for j in range(grp)], axis=0)
        s_all = lax.dot_general(q, k_bd, (((1,), (1,)), ((), ())), preferred_element_type=F32)
        ps, ms = [], []
        for j in range(grp):
            s = s_all[:, j * nk:(j + 1) * nk] + bias_ref[tab, g * grp + j]
            m = jnp.max(s, axis=-1, keepdims=True)
            ps.append(jnp.exp(s - m).astype(BF16))
            ms.append(m)
        p_all = jnp.concatenate(ps, axis=-1)
        res = _dot(p_all, jnp.concatenate([v_bd, ones_bd], axis=-1))
        den = res[:, wid:]
        o_ref[0, :, cols] = (res[:, :wid] / den).astype(o_ref.dtype)
        m_wide = ms[grp - 1]
        for j in range(grp - 1):
            m_wide = jnp.where(q_lane_head == j, ms[j], m_wide)
        lse_wide.append(m_wide + jnp.log(den))
    lse_ref[0] = _dot_f32_lhs(jnp.concatenate(lse_wide, axis=-1), sel_ref[...])


def _attn_bias_tables(rel_bias, dil):
    blk = ATTN_BLOCK
    period = 3 * blk
    k = np.arange(period)
    valid = k <= blk
    dist = np.where(valid, blk - k, 0) * dil
    max_exact = NUM_BUCKETS // 2
    d_f = np.maximum(dist, 1).astype(np.float32)
    large = max_exact + (np.log(d_f / np.float32(max_exact)) / np.float32(math.log(MAX_DISTANCE / max_exact))
                         * np.float32(NUM_BUCKETS - max_exact)).astype(np.int32)
    large = np.minimum(large, NUM_BUCKETS - 1)
    bucket = np.where(dist < max_exact, dist, large).astype(np.int32)
    w = jnp.where(jnp.asarray(valid)[None, :], rel_bias[jnp.asarray(bucket)].T.astype(F32), NEG_INF)
    t1 = jnp.tile(w, (1, blk))[:, :blk * (period - 1)].reshape(N_HEADS_A, blk, period - 1)[:, :, :2 * blk]
    has_prev = np.arange(2 * blk)[None, None, :] >= blk
    t0 = jnp.where(jnp.asarray(has_prev), t1, NEG_INF)
    return jnp.stack([t0, t1])


def _dilated_attention(qkv, bias_tab, batch, seq, dil):
    blk = ATTN_BLOCK
    l = seq // dil
    nb = l // blk
    ncb = 3
    pv = qkv.reshape(batch, l, dil * ncb * D_ATTN)
    head_sel = jnp.asarray((np.arange(D_ATTN)[:, None] == np.arange(LANES)[None, :] * HEAD_DIM_A), BF16)

    def cur(c):
        return pl.BlockSpec((1, blk, D_ATTN), lambda b, r, n: (b, n, r * ncb + c))

    def prev(c):
        return pl.BlockSpec((1, blk, D_ATTN), lambda b, r, n: (b, jnp.maximum(n - 1, 0), r * ncb + c))

    o, lse = pl.pallas_call(
        _attn_kernel,
        grid=(batch, dil, nb),
        in_specs=[cur(0), prev(1), cur(1), prev(2), cur(2),
                  pl.BlockSpec((2, N_HEADS_A, blk, 2 * blk), lambda b, r, n: (0, 0, 0, 0)),
                  pl.BlockSpec((D_ATTN, LANES), lambda b, r, n: (0, 0))],
        out_specs=[pl.BlockSpec((1, blk, D_ATTN), lambda b, r, n: (b, n, r)),
                   pl.BlockSpec((1, blk, LANES), lambda b, r, n: (b, n, r))],
        out_shape=[jax.ShapeDtypeStruct((batch, l, dil * D_ATTN), BF16),
                   jax.ShapeDtypeStruct((batch, l, dil * LANES), F32)],
        compiler_params=_cparams(("parallel", "parallel", "arbitrary")),
        name=f"dilated_attn_d{dil}",
    )(pv, pv, pv, pv, pv, bias_tab, head_sel)
    return o.reshape(batch * l, dil * D_ATTN), lse.reshape(batch * l, dil * LANES)


def _attn_combine_kernel(o1_ref, o2_ref, o3_ref, l1_ref, l2_ref, l3_ref, e_ref, et_ref, beta_ref, y_ref,
                         osc_ref, lsc_ref):
    tm = y_ref.shape[0]
    ntile = D_ATTN // LANES
    lses, outs = [], []
    for slot, ((_, dil), o_ref, l_ref) in enumerate(zip(DILATED_CONFIGS, (o1_ref, o2_ref, o3_ref),
                                                        (l1_ref, l2_ref, l3_ref))):
        if dil == 1:
            lses.append(l_ref[...])
            outs.append(lambda o_ref=o_ref: o_ref[...].astype(F32))
            continue
        rows = tm // dil
        for r in range(dil):
            lsc_ref[slot, pl.ds(r, rows, stride=dil), :] = l_ref[:, r * LANES:(r + 1) * LANES]
            for c in range(ntile):
                col = r * D_ATTN + c * LANES
                osc_ref[slot, c, pl.ds(r, rows, stride=dil), :] = o_ref[:, col:col + LANES].astype(F32)
        lses.append(lsc_ref[slot])
        outs.append(lambda slot=slot: jnp.concatenate([osc_ref[slot, c] for c in range(ntile)], axis=-1))
    mx = jnp.maximum(jnp.maximum(lses[0], lses[1]), lses[2])
    ws = [jnp.exp(l - mx) for l in lses]
    tot = ws[0] + ws[1] + ws[2]
    e = e_ref[...]
    acc = None
    for w, load_o in zip(ws, outs):
        term = _dot_f32_lhs(w / tot, e) * load_o()
        acc = term if acc is None else acc + term
    ss = _dot_f32_lhs(acc * acc, et_ref[...])
    inv = lax.rsqrt(ss * (1.0 / HEAD_DIM_A) + HEAD_NORM_EPS)
    y_ref[...] = (acc * _dot_f32_lhs(inv, e) * beta_ref[...]).astype(y_ref.dtype)


def _attn_combine(os_, lses, beta_attn):
    n = os_[0].shape[0]
    tm = min(COMBINE_TM, n)
    head_of_lane = np.arange(D_ATTN) // HEAD_DIM_A
    e = (np.arange(LANES)[:, None] == head_of_lane[None, :]).astype(np.float32)
    e_j = jnp.asarray(e, BF16)
    et_j = jnp.asarray(e.T, BF16)
    dils = [d for _, d in DILATED_CONFIGS]
    dilated = lambda w: [pl.BlockSpec((tm // d, d * w), lambda i: (i, 0)) for d in dils]
    full = lambda a, b: pl.BlockSpec((a, b), lambda i: (0, 0))
    return pl.pallas_call(
        _attn_combine_kernel,
        grid=(n // tm,),
        in_specs=dilated(D_ATTN) + dilated(LANES) + [full(LANES, D_ATTN), full(D_ATTN, LANES), full(1, D_ATTN)],
        out_specs=pl.BlockSpec((tm, D_ATTN), lambda i: (i, 0)),
        out_shape=jax.ShapeDtypeStruct((n, D_ATTN), BF16),
        scratch_shapes=[pltpu.VMEM((len(dils), D_ATTN // LANES, tm, LANES), F32),
                        pltpu.VMEM((len(dils), tm, LANES), F32)],
        compiler_params=_cparams(("parallel",)),
        name="attn_combine",
    )(*os_, *lses, e_j, et_j, beta_attn.reshape(1, D_ATTN).astype(F32))


def _mlstm_kernel(qp_ref, kp_ref, qprev_ref, kprev_ref, v_ref, og_ref, g_ref, gb_ref, cw_ref, cb_ref,
                  beta_ref, y_ref, c_ref, n_ref, m_ref):
    step = pl.program_id(1)
    ch = MLSTM_CHUNK
    dh = HEAD_DIM_M

    @pl.when(step == 0)
    def _():
        c_ref[...] = jnp.zeros(c_ref.shape, F32)
        n_ref[...] = jnp.zeros(n_ref.shape, F32)
        m_ref[...] = jnp.zeros(m_ref.shape, F32)

    npad = qprev_ref.shape[1]
    shift_r = lax.broadcasted_iota(jnp.int32, ((CONV_K - 1) * ch, npad + ch), 0)
    shift_c = lax.broadcasted_iota(jnp.int32, ((CONV_K - 1) * ch, npad + ch), 1)
    shift_mat = jnp.where(shift_c == shift_r % ch + npad - 1 - shift_r // ch, 1.0, 0.0).astype(BF16)

    def conv_silu(x_ref, prev_ref, coff, b):
        xb = x_ref[b]
        pb = jnp.where(step > 0, prev_ref[b], jnp.zeros_like(prev_ref[b]))
        shifted = _dot(shift_mat, jnp.concatenate([pb, xb], axis=0))
        cols = slice(coff, coff + D_MLSTM)
        acc = cb_ref[:, cols] + cw_ref[CONV_K - 1:CONV_K, cols] * xb.astype(F32)
        for s in range(1, CONV_K):
            acc = acc + cw_ref[CONV_K - 1 - s:CONV_K - s, cols] * shifted[(s - 1) * ch:s * ch]
        return acc * _sigmoid(acc)

    row_i = lax.broadcasted_iota(jnp.int32, (ch, ch), 0)
    col_i = lax.broadcasted_iota(jnp.int32, (ch, ch), 1)
    causal = row_i >= col_i
    tri = jnp.where(causal, 1.0, 0.0).astype(BF16)
    upp = jnp.where(row_i <= col_i, 1.0, 0.0).astype(BF16)
    b = 0
    qf = conv_silu(qp_ref, qprev_ref, 0, b)
    kf = conv_silu(kp_ref, kprev_ref, D_MLSTM, b) * (dh ** -0.5)
    qb = qf.astype(BF16)
    kb = kf.astype(BF16)
    g = g_ref[b] + gb_ref[...]
    gt = g.T
    b_cols = _dot_f32_rhs(tri, _log_sigmoid(g))
    b_rows = _dot_f32_lhs(_log_sigmoid(gt), upp)

    for h in range(N_HEADS_M):
        st = h
        hs = slice(h * dh, (h + 1) * dh)
        fi = N_HEADS_M + h
        i_row = gt[h:h + 1, :]
        i_col = g[:, h:h + 1]
        b_row = b_rows[fi:fi + 1, :]
        b_col = b_cols[:, fi:fi + 1]
        m_prev = m_ref[st][:, 0:1]
        q_h, k_h = qb[:, hs], kb[:, hs]
        v_h = v_ref[b, :, hs]

        dmat = jnp.where(causal, b_col - b_row + i_row, NEG_INF)
        m_inter = b_col + m_prev
        m_t = jnp.maximum(m_inter, jnp.max(dmat, axis=-1, keepdims=True))
        w = jnp.exp(dmat - m_t) * lax.dot_general(q_h, k_h, (((1,), (1,)), ((), ())),
                                                  preferred_element_type=F32)
        decay = jnp.exp(m_inter - m_t)
        c_old = c_ref[st]
        inter = lax.dot_general(q_h, c_old.astype(BF16), (((1,), (1,)), ((), ())), preferred_element_type=F32)
        num = _dot(w.astype(BF16), v_h) + decay * inter
        n_old = n_ref[st]
        den = jnp.sum(w, axis=-1, keepdims=True) + decay * jnp.sum(qf[:, hs] * n_old, axis=-1, keepdims=True)
        hh = num / jnp.maximum(jnp.abs(den), jnp.exp(-m_t))

        g_last = b_col[ch - 1:ch, :]
        a_row = g_last - b_row + i_row
        a_col = g_last - b_col + i_col
        m_new = jnp.maximum(g_last + m_prev, jnp.max(a_row, axis=-1, keepdims=True))
        carry = jnp.exp(g_last + m_prev - m_new)
        wa_col = jnp.exp(a_col - m_new)
        wv = (wa_col * v_h.astype(F32)).astype(BF16)
        c_ref[st] = carry * c_old + lax.dot_general(wv, k_h, (((0,), (0,)), ((), ())), preferred_element_type=F32)
        n_ref[st] = carry * n_old + jnp.sum(wa_col * kf[:, hs], axis=0, keepdims=True)
        m_ref[st] = jnp.broadcast_to(m_new, (1, LANES))

        gated = _sigmoid(og_ref[b, :, hs].astype(F32)) * hh
        ms = jnp.sum(gated * gated, axis=-1, keepdims=True) * (1.0 / dh)
        y_ref[b, :, hs] = (gated * lax.rsqrt(ms + HEAD_NORM_EPS) * beta_ref[:, hs]).astype(y_ref.dtype)


def _mlstm(proj, gates, gate_bias, conv_w, conv_b, beta_mlstm, batch, seq):
    ch = MLSTM_CHUNK
    nchunk = seq // ch
    pv = proj.reshape(batch, seq, 4 * D_MLSTM)
    gv = gates.reshape(batch, seq, LANES)
    prev_rows = 16
    cb0 = 0

    def cur(c):
        return pl.BlockSpec((1, ch, D_MLSTM), lambda b, n: (b, n, c))

    def prev(c):
        per = ch // prev_rows
        return pl.BlockSpec((1, prev_rows, D_MLSTM), lambda b, n: (b, jnp.maximum(n * per - 1, 0), c))

    const = lambda a, b_: pl.BlockSpec((a, b_), lambda b, n: (0, 0))
    y = pl.pallas_call(
        _mlstm_kernel,
        grid=(batch, nchunk),
        in_specs=[cur(cb0), cur(cb0 + 1), prev(cb0), prev(cb0 + 1), cur(cb0 + 2), cur(cb0 + 3),
                  pl.BlockSpec((1, ch, LANES), lambda b, n: (b, n, 0)),
                  const(1, LANES), const(CONV_K, 2 * D_MLSTM), const(1, 2 * D_MLSTM), const(1, D_MLSTM)],
        out_specs=pl.BlockSpec((1, ch, D_MLSTM), lambda b, n: (b, n, 0)),
        out_shape=jax.ShapeDtypeStruct((batch, seq, D_MLSTM), BF16),
        scratch_shapes=[pltpu.VMEM((N_HEADS_M, HEAD_DIM_M, HEAD_DIM_M), F32),
                        pltpu.VMEM((N_HEADS_M, 1, HEAD_DIM_M), F32),
                        pltpu.VMEM((N_HEADS_M, 1, LANES), F32)],
        compiler_params=_cparams(("parallel", "arbitrary")),
        name="mlstm",
    )(pv, pv, pv, pv, pv, pv, gv, gate_bias, conv_w.astype(F32), conv_b.reshape(1, -1).astype(F32),
      beta_mlstm.reshape(1, D_MLSTM).astype(F32))
    return y.reshape(batch * seq, D_MLSTM)


def _layer_norm(z, g, b):
    mu = jnp.mean(z, axis=-1, keepdims=True)
    zc = z - mu
    var = jnp.mean(zc * zc, axis=-1, keepdims=True)
    return zc * lax.rsqrt(var + LN_EPS) * g + b


PACK_ROWS = D_MODEL // (2 * LANES)


def _store_packed_rows(dst_ref, x, first=0):
    rows = x.shape[0]
    half = D_MODEL // 2
    for s in range(PACK_ROWS):
        lo = x[:, s * LANES:(s + 1) * LANES].astype(BF16).astype(F32)
        hi = x[:, half + s * LANES:half + (s + 1) * LANES].astype(BF16).astype(F32)
        word = pltpu.bitcast(hi, jnp.int32) | lax.shift_right_logical(pltpu.bitcast(lo, jnp.int32), 16)
        dst_ref[pl.ds(first * PACK_ROWS + s, rows, stride=PACK_ROWS), :] = word


def _load_packed_rows(src_ref, first, rows):
    los, his = [], []
    for s in range(PACK_ROWS):
        word = src_ref[pl.ds(first * PACK_ROWS + s, rows, stride=PACK_ROWS), :]
        los.append(pltpu.bitcast(lax.shift_left(word, 16), F32))
        his.append(pltpu.bitcast(word & jnp.int32(-65536), F32))
    return los, his


def _outproj_kernel(ya_ref, ym_ref, x_ref, w_ref, g_ref, b_ref, h_ref, hp_ref):
    y = _dot(ya_ref[...], w_ref[0:D_ATTN, :]) + _dot(ym_ref[...], w_ref[D_ATTN:D_MODEL, :])
    h = _layer_norm(DN_ALPHA * x_ref[...] + y, g_ref[...], b_ref[...])
    h_ref[...] = h
    _store_packed_rows(hp_ref, h)


def _outproj_ln(ya, ym, x, w_out, g, b):
    n = x.shape[0]
    tm = min(OUTPROJ_TM, n)
    row = lambda w: pl.BlockSpec((tm, w), lambda i: (i, 0))
    full = lambda a, b_: pl.BlockSpec((a, b_), lambda i: (0, 0))
    return pl.pallas_call(
        _outproj_kernel,
        grid=(n // tm,),
        in_specs=[row(D_ATTN), row(D_MLSTM), row(D_MODEL), full(D_MODEL, D_MODEL), full(1, D_MODEL), full(1, D_MODEL)],
        out_specs=[row(D_MODEL), pl.BlockSpec((tm * PACK_ROWS, LANES), lambda i: (i, 0))],
        out_shape=[jax.ShapeDtypeStruct((n, D_MODEL), F32),
                   jax.ShapeDtypeStruct((n * PACK_ROWS, LANES), jnp.int32)],
        compiler_params=_cparams(("parallel",)),
        name="out_proj_ln",
    )(ya, ym, x, w_out, g.reshape(1, -1), b.reshape(1, -1))


def _router_kernel(h_ref, whi_ref, wlo_ref, b_ref, tri_ref, idx_ref, gate_ref, rank_ref, cnt_ref, carry_ref):
    i = pl.program_id(0)

    @pl.when(i == 0)
    def _():
        carry_ref[...] = jnp.zeros(carry_ref.shape, F32)

    x = h_ref[...]
    xhi = x.astype(BF16)
    xlo = (x - xhi.astype(F32)).astype(BF16)
    logits = _dot(xhi, whi_ref[...]) + _dot(xhi, wlo_ref[...]) + _dot(xlo, whi_ref[...]) + b_ref[...]
    tm = logits.shape[0]
    lane = lax.broadcasted_iota(jnp.int32, (tm, LANES), 1)
    lane_f = lane.astype(F32)
    vals = jnp.where(lane < N_EXPERTS, logits, NEG_INF)

    sels, tops = [], []
    for _ in range(TOP_K):
        mx = jnp.max(vals, axis=-1, keepdims=True)
        first = jnp.min(jnp.where(vals == mx, lane_f, float(LANES)), axis=-1, keepdims=True)
        sel = lane_f == first
        sels.append(sel)
        tops.append((mx, first))
        vals = jnp.where(sel, 2.0 * NEG_INF, vals)

    exps = [jnp.exp(mx - tops[0][0]) for mx, _ in tops]
    tot = exps[0] + exps[1] + exps[2] + exps[3]

    onehot = jnp.zeros((tm, LANES), F32)
    for sel in sels:
        onehot = jnp.where(sel, 1.0, onehot)
    before = _dot(tri_ref[...], onehot.astype(BF16)) + carry_ref[...]

    idx_out = jnp.zeros((tm, LANES), F32)
    gate_out = jnp.zeros((tm, LANES), F32)
    rank_out = jnp.zeros((tm, LANES), F32)
    for k in range(TOP_K):
        rank_k = jnp.sum(jnp.where(sels[k], before, 0.0), axis=-1, keepdims=True)
        idx_out = jnp.where(lane == k, tops[k][1], idx_out)
        gate_out = jnp.where(lane == k, exps[k] / tot, gate_out)
        rank_out = jnp.where(lane == k, rank_k, rank_out)
    idx_ref[...] = idx_out.astype(jnp.int32)
    gate_ref[...] = gate_out
    rank_ref[...] = rank_out.astype(jnp.int32)

    carry = carry_ref[...] + jnp.sum(onehot, axis=0, keepdims=True)
    carry_ref[...] = carry
    cnt_ref[...] = carry.astype(jnp.int32)


def _router(h1, w_router, b_router):
    n = h1.shape[0]
    tm = min(ROUTER_TM, n)
    wpad = jnp.zeros((D_MODEL, LANES), F32).at[:, :N_EXPERTS].set(w_router)
    whi = wpad.astype(BF16)
    wlo = (wpad - whi.astype(F32)).astype(BF16)
    bpad = jnp.zeros((1, LANES), F32).at[0, :N_EXPERTS].set(b_router)
    tri = jnp.asarray(np.tril(np.ones((tm, tm), np.float32), -1), BF16)
    row = lambda w: pl.BlockSpec((tm, w), lambda i: (i, 0))
    full = lambda a, b_: pl.BlockSpec((a, b_), lambda i: (0, 0))
    return pl.pallas_call(
        _router_kernel,
        grid=(n // tm,),
        in_specs=[row(D_MODEL), full(D_MODEL, LANES), full(D_MODEL, LANES), full(1, LANES), full(tm, tm)],
        out_specs=[row(LANES), row(LANES), row(LANES), full(1, LANES)],
        out_shape=[jax.ShapeDtypeStruct((n, LANES), jnp.int32), jax.ShapeDtypeStruct((n, LANES), F32),
                   jax.ShapeDtypeStruct((n, LANES), jnp.int32), jax.ShapeDtypeStruct((1, LANES), jnp.int32)],
        scratch_shapes=[pltpu.VMEM((1, LANES), F32)],
        compiler_params=_cparams(("arbitrary",)),
        name="router",
    )(h1, whi, wlo, bpad, tri)


def _ffn_kernel(bexp_ref, nused_ref, nvalid_ref, slot_ref, slot_prv_ref, src_cur_ref, src_nxt_ref, wg_ref, wu_ref,
                wdl_ref, wdh_ref, bup_ref, bdn_ref, xp_ref, ys_ref, xbuf_ref, x_ref, act_ref, obuf_ref, wgb_ref,
                wub_ref, wdlb_ref, wdhb_ref, in_sem, out_sem):
    i = pl.program_id(0)
    g = pl.program_id(1)
    n_up = act_ref.shape[0]
    expert = bexp_ref[i]
    n_dn = bdn_ref.shape[1] // 2
    last_g = pl.num_programs(1) - 1
    last_i = pl.num_programs(0) - 1
    n_used = nused_ref[0]
    used = i < n_used
    rb = x_ref.shape[0]
    sub = FFN_SUB
    th = wdlb_ref.shape[1]
    unroll = 8

    def sub_blocks(blk):
        return (nvalid_ref[blk] + (sub - 1)) // sub

    nsub = sub_blocks(i)

    def start_gather(src_ref, lo, hi):
        def body(j, carry):
            for u in range(unroll):
                jj = j * unroll + u
                src = pl.multiple_of(src_ref[jj], PACK_ROWS)
                dst = pl.multiple_of(jj * PACK_ROWS, PACK_ROWS)
                pltpu.make_async_copy(xp_ref.at[pl.ds(src, PACK_ROWS)], xbuf_ref.at[pl.ds(dst, PACK_ROWS)],
                                      in_sem).start(priority=ROW_DMA_PRIORITY)
            return carry

        lax.fori_loop(lo // unroll, hi // unroll, body, 0)

    def wait_gather(nrows):
        npk = pl.multiple_of(nrows * PACK_ROWS, PACK_ROWS)
        pltpu.make_async_copy(xp_ref.at[pl.ds(0, npk)], xbuf_ref.at[pl.ds(0, npk)], in_sem).wait()

    def start_scatter(slots_ref, lo, hi):
        def one(jj):
            src = pl.multiple_of(jj * PACK_ROWS, PACK_ROWS)
            dst = pl.multiple_of(slots_ref[jj] * PACK_ROWS, PACK_ROWS)
            pltpu.make_async_copy(obuf_ref.at[pl.ds(src, PACK_ROWS)], ys_ref.at[pl.ds(dst, PACK_ROWS)],
                                  out_sem).start(priority=ROW_DMA_PRIORITY)

        def body(j, carry):
            for u in range(unroll):
                one(j * unroll + u)
            return carry

        nfull = jnp.maximum(hi - lo, 0) // unroll
        lax.fori_loop(lo // unroll, lo // unroll + nfull, body, 0)
        for u in range(unroll - 1):
            @pl.when(lo + nfull * unroll + u < hi)
            def _():
                one(lo + nfull * unroll + u)

    def wait_scatter(blk):
        npk = pl.multiple_of(nvalid_ref[blk] * PACK_ROWS, PACK_ROWS)
        pltpu.make_async_copy(obuf_ref.at[pl.ds(0, npk)], ys_ref.at[pl.ds(0, npk)], out_sem).wait()

    @pl.when(jnp.logical_and(used, g == 0))
    def _():
        @pl.when(i == 0)
        def _():
            start_gather(src_cur_ref, 0, nsub * sub)

        wait_gather(nsub * sub)
        for j in range(rb // sub):
            @pl.when(j < nsub)
            def _():
                los, his = _load_packed_rows(xbuf_ref, j * sub, sub)
                rows = slice(j * sub, (j + 1) * sub)
                for s in range(PACK_ROWS):
                    x_ref[rows, s * LANES:(s + 1) * LANES] = los[s].astype(BF16)
                    x_ref[rows, D_MODEL // 2 + s * LANES:D_MODEL // 2 + (s + 1) * LANES] = his[s].astype(BF16)

    @pl.when(jnp.logical_and(jnp.logical_and(g >= 1, g - 1 < sub_blocks(jnp.minimum(i + 1, last_i))), i + 1 < n_used))
    def _():
        start_gather(src_nxt_ref, (g - 1) * sub, g * sub)

    @pl.when(jnp.logical_and(jnp.logical_and(used, i > 0), g + 1 < rb // sub))
    def _():
        start_scatter(slot_prv_ref, (g + 1) * sub, jnp.minimum((g + 2) * sub, nvalid_ref[i - 1]))

    @pl.when(jnp.logical_and(used, g < n_up))
    def _():
        wgb_ref[...] = wg_ref[0].astype(BF16)
        wub_ref[...] = wu_ref[0].astype(BF16)

        def sub_body(j, carry):
            rows = pl.ds(pl.multiple_of(j * sub, sub), sub)
            x = x_ref[rows, :]
            hg = _dot(x, wgb_ref[...]) + bup_ref[expert, pl.ds(g, 1), :]
            hu = _dot(x, wub_ref[...]) + bup_ref[expert, pl.ds(n_up + g, 1), :]
            gate = jnp.minimum(hg, SWIGLU_LIMIT)
            up = jnp.clip(hu, -SWIGLU_LIMIT, SWIGLU_LIMIT)
            act_ref[g, rows, :] = ((up + 1.0) * (gate * _sigmoid(SWIGLU_ALPHA * gate))).astype(BF16)
            return carry

        lax.fori_loop(0, nsub, sub_body, 0)

    @pl.when(jnp.logical_and(used, g >= n_up))
    def _():
        @pl.when(jnp.logical_and(g == n_up, i > 0))
        def _():
            wait_scatter(i - 1)

        wdlb_ref[...] = wdl_ref[0].astype(BF16)
        wdhb_ref[...] = wdh_ref[0].astype(BF16)
        tile0 = (g - n_up) * (th // LANES)

        def sub_body(j, carry):
            row0 = pl.multiple_of(j * sub, sub)
            a = jnp.concatenate([act_ref[c, pl.ds(row0, sub), :] for c in range(n_up)], axis=-1)
            ylo = _dot(a, wdlb_ref[...]) + bdn_ref[expert, pl.ds(g - n_up, 1), :]
            yhi = _dot(a, wdhb_ref[...]) + bdn_ref[expert, pl.ds(n_dn + g - n_up, 1), :]
            for s in range(th // LANES):
                lo = ylo[:, s * LANES:(s + 1) * LANES].astype(BF16).astype(F32)
                hi = yhi[:, s * LANES:(s + 1) * LANES].astype(BF16).astype(F32)
                word = pltpu.bitcast(hi, jnp.int32) | lax.shift_right_logical(pltpu.bitcast(lo, jnp.int32), 16)
                obuf_ref[pl.ds(row0 * PACK_ROWS + tile0 + s, sub, stride=PACK_ROWS), :] = word
            return carry

        lax.fori_loop(0, nsub, sub_body, 0)

        @pl.when(g == last_g)
        def _():
            nv = nvalid_ref[i]
            start_scatter(slot_ref, 0, jnp.where(i + 1 < n_used, jnp.minimum(sub, nv), nv))

    @pl.when(jnp.logical_and(i == last_i, g == last_g))
    def _():
        wait_scatter(jnp.minimum(i, n_used - 1))


def _expert_ffn(h_packed, slot_assign, blk_exp, blk_valid, n_used, w_up, b_up, w_down, b_down):
    rb, tf, th = FFN_ROWS, FFN_TF, FFN_TH
    nblk = slot_assign.shape[0] // rb
    n_up = D_FF // tf
    n_dn = D_MODEL // 2 // th
    n_assign = h_packed.shape[0] // PACK_ROWS * TOP_K
    n_tok = h_packed.shape[0] // PACK_ROWS
    slot_src = jnp.maximum(slot_assign, 0) % n_tok * PACK_ROWS

    def clamp(i, nu):
        return jnp.minimum(i, nu[0] - 1)

    def up_sel(i, g, nu):
        return jnp.where(i < nu[0], jnp.minimum(g, n_up - 1), n_up - 1)

    def dn_sel(i, g, nu):
        return jnp.where(i < nu[0], jnp.clip(g - n_up, 0, n_dn - 1), n_dn - 1)

    def dn_exp(i, g, be, nu):
        early = jnp.logical_and(g < n_up - 1, i < nu[0])
        return be[jnp.where(early, jnp.maximum(clamp(i, nu) - 1, 0), clamp(i, nu))]

    def dn_col(i, g, nu):
        early = jnp.logical_and(g < n_up - 1, i < nu[0])
        return jnp.where(early, n_dn - 1, dn_sel(i, g, nu))

    grid_spec = pltpu.PrefetchScalarGridSpec(
        num_scalar_prefetch=3,
        grid=(nblk, n_up + n_dn),
        in_specs=[
            pl.BlockSpec((rb,), lambda i, g, be, nu, *_: (clamp(i, nu),), memory_space=pltpu.SMEM),
            pl.BlockSpec((rb,), lambda i, g, be, nu, *_: (jnp.maximum(clamp(i, nu) - 1, 0),), memory_space=pltpu.SMEM),
            pl.BlockSpec((rb,), lambda i, g, be, nu, *_: (clamp(i, nu),), memory_space=pltpu.SMEM),
            pl.BlockSpec((rb,), lambda i, g, be, nu, *_: (clamp(i + 1, nu),), memory_space=pltpu.SMEM),
            pl.BlockSpec((1, D_MODEL, tf), lambda i, g, be, nu, *_: (be[clamp(i, nu)], 0, up_sel(i, g, nu))),
            pl.BlockSpec((1, D_MODEL, tf), lambda i, g, be, nu, *_: (be[clamp(i, nu)], 0, n_up + up_sel(i, g, nu))),
            pl.BlockSpec((1, D_FF, th), lambda i, g, be, nu, *_: (dn_exp(i, g, be, nu), 0, dn_col(i, g, nu))),
            pl.BlockSpec((1, D_FF, th), lambda i, g, be, nu, *_: (dn_exp(i, g, be, nu), 0, n_dn + dn_col(i, g, nu))),
            pl.BlockSpec((N_EXPERTS, 2 * n_up, tf), lambda i, g, *_: (0, 0, 0)),
            pl.BlockSpec((N_EXPERTS, 2 * n_dn, th), lambda i, g, *_: (0, 0, 0)),
            pl.BlockSpec(memory_space=pl.ANY),
        ],
        out_specs=pl.BlockSpec(memory_space=pl.ANY),
        scratch_shapes=[pltpu.VMEM((rb * PACK_ROWS, LANES), jnp.int32),
                        pltpu.VMEM((rb, D_MODEL), BF16),
                        pltpu.VMEM((n_up, rb, tf), BF16),
                        pltpu.VMEM((rb * PACK_ROWS, LANES), jnp.int32),
                        pltpu.VMEM((D_MODEL, tf), BF16), pltpu.VMEM((D_MODEL, tf), BF16),
                        pltpu.VMEM((D_FF, th), BF16), pltpu.VMEM((D_FF, th), BF16),
                        pltpu.SemaphoreType.DMA, pltpu.SemaphoreType.DMA],
    )
    assert n_up >= rb // FFN_SUB - 1 and n_up + n_dn > rb // FFN_SUB, "row DMA bursts are spread over the grid steps"
    return pl.pallas_call(
        _ffn_kernel,
        grid_spec=grid_spec,
        out_shape=jax.ShapeDtypeStruct((n_assign * PACK_ROWS, LANES), jnp.int32),
        compiler_params=pltpu.CompilerParams(dimension_semantics=("arbitrary", "arbitrary"),
                                             vmem_limit_bytes=FFN_VMEM_LIMIT, disable_bounds_checks=True),
        name="expert_ffn",
    )(blk_exp, n_used, blk_valid, slot_assign, slot_assign, slot_src, slot_src, w_up, w_up, w_down, w_down,
      b_up.reshape(N_EXPERTS, 2 * n_up, tf), b_down.reshape(N_EXPERTS, 2 * n_dn, th), h_packed)


def _combine_kernel(gate_ref, h_ref, g_ref, b_ref, ys_ref, o_ref):
    tm = h_ref.shape[0]
    gates = gate_ref[...]
    cols = [None] * (2 * PACK_ROWS)
    for k in range(TOP_K):
        gk = gates[:, k:k + 1]
        for s in range(PACK_ROWS):
            word = ys_ref[k, pl.ds(s, tm, stride=PACK_ROWS), :]
            lo = pltpu.bitcast(lax.shift_left(word, 16), F32)
            hi = pltpu.bitcast(word & jnp.int32(-65536), F32)
            for c, blk in ((s, lo), (PACK_ROWS + s, hi)):
                cols[c] = gk * blk if cols[c] is None else cols[c] + gk * blk
    z = DN_ALPHA * h_ref[...] + jnp.concatenate(cols, axis=-1)
    o_ref[...] = _layer_norm(z, g_ref[...], b_ref[...])


def _combine_ln(gates, h1, ys, g, b):
    n = h1.shape[0]
    tm = min(COMBINE_LN_TM, n)
    row = lambda w: pl.BlockSpec((tm, w), lambda i: (i, 0))
    full = lambda a, b_: pl.BlockSpec((a, b_), lambda i: (0, 0))
    return pl.pallas_call(
        _combine_kernel,
        grid=(n // tm,),
        in_specs=[row(LANES), row(D_MODEL), full(1, D_MODEL), full(1, D_MODEL),
                  pl.BlockSpec((TOP_K, tm * PACK_ROWS, LANES), lambda i: (0, i, 0))],
        out_specs=row(D_MODEL),
        out_shape=jax.ShapeDtypeStruct((n, D_MODEL), F32),
        compiler_params=_cparams(("parallel",)),
        name="combine_ln",
    )(gates, h1, g.reshape(1, -1), b.reshape(1, -1), ys.reshape(TOP_K, n * PACK_ROWS, LANES))


def _layer(h, w_in, b_igate, b_fgate, conv_w, conv_b, rel_bias, beta_attn, beta_mlstm, w_out,
           ln1_g, ln1_b, w_router, b_router, w_up, b_up, w_down, b_down, ln2_g, ln2_b, batch, seq):
    n = batch * seq
    w_in, l = w_in
    w_qkv = w_in[l, :, :3 * D_ATTN].astype(BF16)
    w_mix = w_in[l, :, 3 * D_ATTN:MAIN_COLS].astype(BF16)
    w_gate = jnp.zeros((D_MODEL, LANES), BF16).at[:, :2 * N_HEADS_M].set(w_in[l, :, MAIN_COLS:].astype(BF16))
    qkvs = _qkv_project(h, w_qkv, min(QKV_TM, n))
    proj_m, gates = _project(h, w_mix, w_gate, min(PROJ_TM, n), PROJ_TN)
    gate_bias = jnp.zeros((1, LANES), F32).at[0, :2 * N_HEADS_M].set(jnp.concatenate([b_igate, b_fgate]))

    outs, lses = [], []
    for (_, dil), qkv in zip(DILATED_CONFIGS, qkvs):
        o, l = _dilated_attention(qkv, _attn_bias_tables(rel_bias, dil), batch, seq, dil)
        outs.append(o)
        lses.append(l)
    y_attn = _attn_combine(outs, lses, beta_attn)
    y_mlstm = _mlstm(proj_m, gates, gate_bias, conv_w, conv_b, beta_mlstm, batch, seq)

    h1, h1_packed = _outproj_ln(y_attn, y_mlstm, h, w_out.astype(BF16), ln1_g, ln1_b)

    top_idx, top_gate, rank, counts = _router(h1, w_router, b_router)
    counts = counts[0, :N_EXPERTS]
    nb = (counts + FFN_ROWS - 1) // FFN_ROWS
    per = (counts + nb * FFN_SUB - 1) // jnp.maximum(nb * FFN_SUB, 1) * FFN_SUB
    blk_end = jnp.cumsum(nb)
    blk_start = blk_end - nb
    e_idx, r_idx = top_idx[:, :TOP_K], rank[:, :TOP_K]
    onehot = (e_idx.reshape(-1, 1) == jnp.arange(N_EXPERTS, dtype=jnp.int32)[None, :]).astype(F32)
    table = jnp.stack([jnp.maximum(per, 1), blk_start], axis=1).astype(F32)
    looked = jnp.dot(onehot, table, precision=lax.Precision.HIGHEST)
    per_a, start_a = looked[:, 0], looked[:, 1]
    r_f = r_idx.reshape(-1).astype(F32)
    blk_in_e = jnp.floor((r_f + 0.5) / per_a)
    dest = ((start_a + blk_in_e) * FFN_ROWS + (r_f - blk_in_e * per_a)).astype(jnp.int32)
    nblk = n * TOP_K // FFN_ROWS + N_EXPERTS
    blk_id = jnp.arange(nblk, dtype=jnp.int32)
    blk_exp = jnp.minimum(jnp.sum(blk_end[None, :] <= blk_id[:, None], axis=1), N_EXPERTS - 1).astype(jnp.int32)
    n_used = blk_end[-1:].astype(jnp.int32)
    blk_valid = jnp.clip(counts[blk_exp] - (blk_id - blk_start[blk_exp]) * per[blk_exp], 0,
                         per[blk_exp]).astype(jnp.int32)
    order = jnp.argsort(dest).astype(jnp.int32)
    sorted_assign = (order % TOP_K) * n + order // TOP_K
    first = jnp.cumsum(blk_valid) - blk_valid
    row_in_blk = jnp.tile(jnp.arange(FFN_ROWS, dtype=jnp.int32), nblk)
    first_rep = jnp.repeat(first, FFN_ROWS)
    valid_rep = jnp.repeat(blk_valid, FFN_ROWS)
    picked = sorted_assign[jnp.minimum(first_rep + row_in_blk, n * TOP_K - 1)]
    slot_assign = jnp.where(row_in_blk < valid_rep, picked, -1)

    ys = _expert_ffn(h1_packed, slot_assign, blk_exp, blk_valid, n_used, w_up, b_up, w_down, b_down)
    return _combine_ln(top_gate, h1, ys, ln2_g, ln2_b)


def kernel(x, w_in, b_igate, b_fgate, conv_w, conv_b, rel_bias, beta_attn, beta_mlstm, w_out, ln1_g, ln1_b,
           w_router, b_router, w_up, b_up, w_down, b_down, ln2_g, ln2_b):
    batch, seq, d = x.shape
    h = x.reshape(batch * seq, d)
    for l in range(DEPTH):
        h = _layer(h, (w_in, l), b_igate[l], b_fgate[l], conv_w[l], conv_b[l], rel_bias, beta_attn[l], beta_mlstm[l],
                   w_out[l], ln1_g[l], ln1_b[l], w_router[l], b_router[l], w_up[l], b_up[l], w_down[l], b_down[l],
                   ln2_g[l], ln2_b[l], batch, seq)
    return h.reshape(batch, seq, d)
```
